```python
import math
import jax
import jax.numpy as jnp
from jax import lax
import numpy as np

D_MODEL = 2048
BATCH = 2
SEQ = 16384
DEPTH = 4
DEC_BATCH = 4
DEC_SEQ = 2048
PAST_LEN = 128

N_MIXERS = 2
N_HYENA = (DEPTH + 1) // 2
N_ATTN = DEPTH // 2
HYENA_ORDER = 2
N_DIRS = 2
SHORT_WIDTH = 3
FILTER_BANDS = 16
FILTER_EMB = 1 + 2 * FILTER_BANDS
FILTER_HIDDEN = 64
DECAY_TARGET = 1e-2
FAST_DECAY_PCT = 0.3
SLOW_DECAY_PCT = 1.5
ATTN_PATTERNS = ((128, 1), (512, 4), (2048, 16))
N_GROUPS = len(ATTN_PATTERNS)
HEADS_PER_GROUP = 8
HEAD_DIM = 128
GROUP_WIDTH = HEADS_PER_GROUP * HEAD_DIM
ROT_DIM = HEAD_DIM // 4
ROPE_THETA = 500000.0
D_FF = -(-8 * D_MODEL // (3 * 256)) * 256
EPS = 1e-6

kernel_name = 'hybrid_hyena_dilated_attn_encoder'

F32 = jnp.float32


def rms_norm(x, g):
    x32 = x.astype(F32)
    y = x32 * lax.rsqrt(jnp.mean(x32 * x32, axis=-1, keepdims=True) + EPS)
    return (y * g.astype(F32)).astype(x.dtype)


def modulate(h, shift, scale):
    return h * (1 + scale[:, None, :]) + shift[:, None, :]


def short_conv(z, w, b):
    zp = jnp.pad(z, ((0, 0), (1, 1), (0, 0)))
    return zp[:, :-2] * w[0] + zp[:, 1:-1] * w[1] + zp[:, 2:] * w[2] + b


def hyena_filters(L, w1, b1, freq, w2, b2, w3, decay):
    pos = jnp.arange(L, dtype=F32)
    t = pos / max(L - 1, 1)
    bands = jnp.linspace(1e-4, FILTER_BANDS - 1, FILTER_BANDS, dtype=F32)
    ang = 2.0 * math.pi * pos[:, None] * bands[None, :] / L
    z = jnp.concatenate([t[:, None], jnp.cos(ang), -jnp.sin(ang)], axis=-1)
    a = jnp.sin(freq[0].astype(F32) * (z @ w1.astype(F32) + b1.astype(F32)))
    a = jnp.sin(freq[1].astype(F32) * (a @ w2.astype(F32) + b2.astype(F32)))
    h = (a @ w3.astype(F32)).reshape(L, HYENA_ORDER, N_DIRS, D_MODEL)
    window = jnp.exp(-t[:, None, None, None] * jnp.abs(decay.astype(F32))[None])
    return h * window


def long_conv(u, filt_fwd, filt_bwd, skip):
    L = u.shape[1]
    filt = jnp.concatenate(
        [filt_fwd, jnp.zeros((1, filt_fwd.shape[1]), F32), filt_bwd[:0:-1]], axis=0)
    filt = filt * lax.rsqrt(jnp.sum(filt * filt, axis=0, keepdims=True) + EPS)
    u32 = u.astype(F32)
    spec = jnp.fft.rfft(u32, n=2 * L, axis=1) * jnp.fft.rfft(filt, axis=0)[None]
    y = jnp.fft.irfft(spec, n=2 * L, axis=1)[:, :L]
    return (y + u32 * skip.astype(F32)).astype(u.dtype)


def hyena_mixer(h, w_in, b_in, conv_w, conv_b, fw1, fb1, ffreq, fw2, fb2, fw3, decay, skip,
                w_out, b_out):
    L = h.shape[1]
    z = short_conv(h @ w_in + b_in, conv_w, conv_b)
    x1, x2, v = jnp.split(z, 3, axis=-1)
    filt = hyena_filters(L, fw1, fb1, ffreq, fw2, fb2, fw3, decay)
    out = v
    for o, gate in enumerate((x1, x2)):
        out = gate * long_conv(out, filt[:, o, 0], filt[:, o, 1], skip[o])
    return out @ w_out + b_out


def rope_partial(x, pos):
    half = ROT_DIM // 2
    inv = ROPE_THETA ** (-jnp.arange(0, ROT_DIM, 2, dtype=F32) / ROT_DIM)
    ang = pos[:, None] * inv[None, :]
    cos = jnp.cos(ang)[None, :, None, :]
    sin = jnp.sin(ang)[None, :, None, :]
    a, b, rest = x[..., :half], x[..., half:ROT_DIM], x[..., ROT_DIM:]
    return jnp.concatenate([a * cos - b * sin, b * cos + a * sin, rest], axis=-1)


def dilated_group(q, k, v, window, dilation):
    B, L, H, dh = q.shape
    r = dilation
    radius = window // (2 * r)
    blk = radius
    n = L // r
    nb = -(-n // blk)
    n_pad = nb * blk

    def sub(a, lo, hi):
        a = a.reshape(B, n, r, H, dh)
        return jnp.pad(a, ((0, 0), (lo, hi), (0, 0), (0, 0), (0, 0)))

    qb = sub(q, 0, n_pad - n).reshape(B, nb, blk, r, H, dh)

    def windows(a):
        ab = sub(a, blk, blk + n_pad - n).reshape(B, nb + 2, blk, r, H, dh)
        return jnp.concatenate([ab[:, :-2], ab[:, 1:-1], ab[:, 2:]], axis=2)

    kw, vw = windows(k), windows(v)
    s = jnp.einsum('bnqrhd,bnkrhd->bnrhqk', qb, kw) * (HEAD_DIM ** -0.5)
    qi = (jnp.arange(nb)[:, None] * blk + jnp.arange(blk)[None, :])[:, :, None]
    ki = (jnp.arange(nb)[:, None] * blk - blk + jnp.arange(3 * blk)[None, :])[:, None, :]
    valid = ((jnp.abs(ki - qi) <= radius) & (ki >= 0) & (ki < n)) | (ki == qi)
    s = jnp.where(valid[None, :, None, None], s, -jnp.inf)
    m = jnp.max(s, axis=-1, keepdims=True)
    p = jnp.exp(s - m)
    den = jnp.sum(p, axis=-1)
    o = jnp.einsum('bnrhqk,bnkrhd->bnqrhd', p, vw) / jnp.moveaxis(den, -1, 2)[..., None]
    lse = jnp.moveaxis(m[..., 0] + jnp.log(den), -1, 2)
    o = o.reshape(B, n_pad, r, H, dh)[:, :n].reshape(B, L, H, dh)
    lse = lse.reshape(B, n_pad, r, H)[:, :n].reshape(B, L, H)
    return o, lse


def dilated_attention(h, w_in, w_out):
    B, L, _ = h.shape
    qkv = (h @ w_in).astype(F32).reshape(B, L, N_GROUPS, 3, HEADS_PER_GROUP, HEAD_DIM)
    pos = jnp.arange(L, dtype=F32)
    outs, lses = [], []
    for g, (window, dilation) in enumerate(ATTN_PATTERNS):
        q = rope_partial(qkv[:, :, g, 0], pos)
        k = rope_partial(qkv[:, :, g, 1], pos)
        o, lse = dilated_group(q, k, qkv[:, :, g, 2], window, dilation)
        outs.append(o)
        lses.append(lse)
    alpha = jax.nn.softmax(jnp.stack(lses, axis=-1), axis=-1)
    o = jnp.einsum('gblhd,blhg->blhd', jnp.stack(outs, axis=0), alpha)
    return o.reshape(B, L, GROUP_WIDTH).astype(h.dtype) @ w_out


def swiglu(h, w_gate, w_up, w_down):
    return (jax.nn.silu(h @ w_gate) * (h @ w_up)) @ w_down


def encoder(x, c, p):
    cs = jax.nn.silu(c)
    for i in range(DEPTH):
        mod = cs @ p['ada_w'][i] + p['ada_b'][i]
        sh_m, sc_m, g_m, sh_f, sc_f, g_f = jnp.split(mod, 6, axis=-1)
        h = modulate(rms_norm(x, p['norm_mix'][i]), sh_m, sc_m)
        j = i // N_MIXERS
        if i % N_MIXERS == 0:
            y = hyena_mixer(h, p['hy_w_in'][j], p['hy_b_in'][j], p['hy_conv_w'][j], p['hy_conv_b'][j],
                            p['hy_fw1'][j], p['hy_fb1'][j], p['hy_ffreq'][j], p['hy_fw2'][j],
                            p['hy_fb2'][j], p['hy_fw3'][j], p['hy_decay'][j], p['hy_skip'][j],
                            p['hy_w_out'][j], p['hy_b_out'][j])
        else:
            y = dilated_attention(h, p['at_w_in'][j], p['at_w_out'][j])
        x = x + g_m[:, None, :] * y
        h = modulate(rms_norm(x, p['norm_ffn'][i]), sh_f, sc_f)
        x = x + g_f[:, None, :] * swiglu(h, p['ffn_w_gate'][i], p['ffn_w_up'][i], p['ffn_w_down'][i])
    sh, sc = jnp.split(cs @ p['final_ada_w'] + p['final_ada_b'], 2, axis=-1)
    return modulate(rms_norm(x, p['final_norm']), sh, sc)


def setup_inputs(seed: int = 0) -> dict:
    key = jax.random.key(seed)
    ks = iter(jax.random.split(key, 40))

    def nrm(shape, scale):
        return scale * jax.random.normal(next(ks), shape, F32)

    D = D_MODEL
    decay_base = jnp.linspace(-math.log(DECAY_TARGET) / SLOW_DECAY_PCT,
                              -math.log(DECAY_TARGET) / FAST_DECAY_PCT, D, dtype=F32)
    return {
        'x_prompt': nrm((BATCH, SEQ, D), 1.0),
        'x_sample': nrm((DEC_BATCH, DEC_SEQ, D), 1.0),
        'c_prompt': nrm((BATCH, D), 1.0),
        'c_sample': nrm((DEC_BATCH, D), 1.0),
        'ada_w': nrm((DEPTH, D, 6 * D), 0.5 * D ** -0.5),
        'ada_b': nrm((DEPTH, 6 * D), 0.02),
        'norm_mix': 1.0 + nrm((DEPTH, D), 0.05),
        'norm_ffn': 1.0 + nrm((DEPTH, D), 0.05),
        'hy_w_in': nrm((N_HYENA, D, 3 * D), D ** -0.5),
        'hy_b_in': nrm((N_HYENA, 3 * D), 0.02),
        'hy_conv_w': nrm((N_HYENA, SHORT_WIDTH, 3 * D), SHORT_WIDTH ** -0.5),
        'hy_conv_b': nrm((N_HYENA, 3 * D), 0.02),
        'hy_fw1': nrm((N_HYENA, FILTER_EMB, FILTER_HIDDEN), FILTER_EMB ** -0.5),
        'hy_fb1': nrm((N_HYENA, FILTER_HIDDEN), 0.1),
        'hy_ffreq': 1.0 + nrm((N_HYENA, 2, FILTER_HIDDEN), 0.05),
        'hy_fw2': nrm((N_HYENA, FILTER_HIDDEN, FILTER_HIDDEN), FILTER_HIDDEN ** -0.5),
        'hy_fb2': nrm((N_HYENA, FILTER_HIDDEN), 0.1),
        'hy_fw3': nrm((N_HYENA, FILTER_HIDDEN, HYENA_ORDER * N_DIRS * D), FILTER_HIDDEN ** -0.5),
        'hy_decay': decay_base + nrm((N_HYENA, HYENA_ORDER, N_DIRS, D), 0.1),
        'hy_skip': nrm((N_HYENA, HYENA_ORDER, D), 1.0),
        'hy_w_out': nrm((N_HYENA, D, D), D ** -0.5),
        'hy_b_out': nrm((N_HYENA, D), 0.02),
        'at_w_in': nrm((N_ATTN, D, N_GROUPS * 3 * GROUP_WIDTH), D ** -0.5),
        'at_w_out': nrm((N_ATTN, GROUP_WIDTH, D), GROUP_WIDTH ** -0.5),
        'ffn_w_gate': nrm((DEPTH, D, D_FF), D ** -0.5),
        'ffn_w_up': nrm((DEPTH, D, D_FF), D ** -0.5),
        'ffn_w_down': nrm((DEPTH, D_FF, D), D_FF ** -0.5),
        'final_norm': 1.0 + nrm((D,), 0.05),
        'final_ada_w': nrm((D, 2 * D), 0.5 * D ** -0.5),
        'final_ada_b': nrm((2 * D,), 0.02),
    }


def reference(x_prompt, x_sample, c_prompt, c_sample, ada_w, ada_b, norm_mix, norm_ffn,
              hy_w_in, hy_b_in, hy_conv_w, hy_conv_b, hy_fw1, hy_fb1, hy_ffreq, hy_fw2, hy_fb2,
              hy_fw3, hy_decay, hy_skip, hy_w_out, hy_b_out, at_w_in, at_w_out,
              ffn_w_gate, ffn_w_up, ffn_w_down, final_norm, final_ada_w, final_ada_b):
    params = dict(ada_w=ada_w, ada_b=ada_b, norm_mix=norm_mix, norm_ffn=norm_ffn,
                  hy_w_in=hy_w_in, hy_b_in=hy_b_in, hy_conv_w=hy_conv_w, hy_conv_b=hy_conv_b,
                  hy_fw1=hy_fw1, hy_fb1=hy_fb1, hy_ffreq=hy_ffreq, hy_fw2=hy_fw2, hy_fb2=hy_fb2,
                  hy_fw3=hy_fw3, hy_decay=hy_decay, hy_skip=hy_skip, hy_w_out=hy_w_out,
                  hy_b_out=hy_b_out, at_w_in=at_w_in, at_w_out=at_w_out,
                  ffn_w_gate=ffn_w_gate, ffn_w_up=ffn_w_up, ffn_w_down=ffn_w_down,
                  final_norm=final_norm, final_ada_w=final_ada_w, final_ada_b=final_ada_b)
    y_prompt = encoder(x_prompt, c_prompt, params)
    y_sample = encoder(x_sample, c_sample, params)
    return (y_prompt, y_sample)
```

```python
import functools
import math

import numpy as np
import jax
import jax.numpy as jnp
from jax import lax
from jax.experimental import pallas as pl
from jax.experimental.pallas import tpu as pltpu

F32 = jnp.float32
BF16 = jnp.bfloat16
EPS = 1e-6

DEPTH = 4
HYENA_ORDER = 2
N_DIRS = 2
FILTER_BANDS = 16
FILTER_EMB = 1 + 2 * FILTER_BANDS
ATTN_PATTERNS = ((128, 1), (512, 4), (2048, 16))
N_GROUPS = len(ATTN_PATTERNS)
HEADS_PER_GROUP = 8
HEAD_DIM = 128
GROUP_WIDTH = HEADS_PER_GROUP * HEAD_DIM
ROT_DIM = HEAD_DIM // 4
ROPE_THETA = 500000.0

LANES = 128
SUBLANES = 8
VMEM_LIMIT_BYTES = 56 * 1024 * 1024

DFT_N2 = 256
ATT_BLK = 64
NEG_BIG = -1e30


def _params(sem):
    return pltpu.CompilerParams(dimension_semantics=sem, vmem_limit_bytes=VMEM_LIMIT_BYTES)


def _dot(a, b):
    return jnp.dot(a, b, preferred_element_type=F32)


def _split(a):
    hi = a.astype(BF16)
    lo = (a - hi.astype(F32)).astype(BF16)
    return hi, lo


def _dot3(a, b):
    ah, al = _split(a)
    bh, bl = _split(b)
    return _dot(ah, bh) + _dot(al, bh) + _dot(ah, bl)


def _modnorm(x, g, shift, scale):
    ms = jnp.mean(x * x, axis=-1, keepdims=True)
    y = x * lax.rsqrt(ms + EPS) * g
    return y * (1.0 + scale) + shift


def _ada_kernel(c_ref, w_ref, b_ref, o_ref):
    c = c_ref[...]
    cs = c * jax.nn.sigmoid(c)
    o_ref[...] = _dot3(cs, w_ref[...]) + b_ref[...]


def _ada(c_all, w, b, tn=1024):
    nl, d, no = w.shape
    r = c_all.shape[0]
    return pl.pallas_call(
        _ada_kernel,
        grid=(nl, no // tn),
        in_specs=[
            pl.BlockSpec((r, d), lambda l, j: (0, 0)),
            pl.BlockSpec((None, d, tn), lambda l, j: (l, 0, j)),
            pl.BlockSpec((None, 1, tn), lambda l, j: (l, 0, j)),
        ],
        out_specs=pl.BlockSpec((None, r, tn), lambda l, j: (l, 0, j)),
        out_shape=jax.ShapeDtypeStruct((nl, r, no), F32),
        compiler_params=_params(("parallel", "parallel")),
        name="ada_mod",
    )(c_all, w, b.reshape(nl, 1, no))


def _norm_mm_kernel(*refs, mode, tn):
    if mode == "swiglu":
        x_ref, g_ref, sh_ref, sc_ref, wg_ref, wu_ref, o_ref, h_ref = refs
    elif mode == "rope":
        x_ref, g_ref, sh_ref, sc_ref, w_ref, c_ref, s1_ref, s2_ref, o_ref, h_ref = refs
    else:
        x_ref, g_ref, sh_ref, sc_ref, w_ref, b_ref, o_ref, h_ref = refs
    j = pl.program_id(1)

    @pl.when(j == 0)
    def _():
        h_ref[...] = _modnorm(x_ref[...], g_ref[...], sh_ref[...], sc_ref[...]).astype(BF16)

    h = h_ref[...]
    if mode == "swiglu":
        a = _dot(h, wg_ref[...])
        u = _dot(h, wu_ref[...])
        o_ref[...] = (a * jax.nn.sigmoid(a) * u).astype(o_ref.dtype)
    elif mode == "rope":
        acc = _dot(h, w_ref[...])
        part = (j // (GROUP_WIDTH // tn)) % 3

        @pl.when(part == 2)
        def _():
            o_ref[...] = acc.astype(o_ref.dtype)

        @pl.when(part != 2)
        def _():
            reps = tn // HEAD_DIM
            c = jnp.concatenate([c_ref[...]] * reps, axis=1)
            s1 = jnp.concatenate([s1_ref[...]] * reps, axis=1)
            s2 = jnp.concatenate([s2_ref[...]] * reps, axis=1)
            half = ROT_DIM // 2
            r = acc * c + pltpu.roll(acc, tn - half, 1) * s1 + pltpu.roll(acc, half, 1) * s2
            qs = jnp.where(part == 0, HEAD_DIM ** -0.5, 1.0).astype(F32)
            o_ref[...] = (r * qs).astype(o_ref.dtype)
    else:
        o_ref[...] = (_dot(h, w_ref[...]) + b_ref[...]).astype(o_ref.dtype)


def _norm_mm(x2d, seq_len, g, shift, scale, ws, *, mode, out_dtype, bias=None, rope_tabs=None,
             tm=512, tn=512):
    m, d = x2d.shape
    nout = ws[0].shape[1]
    tm = min(tm, seq_len)
    per_seq = seq_len // tm
    vec = pl.BlockSpec((None, 1, d), lambda i, j: (i // per_seq, 0, 0))
    in_specs = [pl.BlockSpec((tm, d), lambda i, j: (i, 0)),
                pl.BlockSpec((1, d), lambda i, j: (0, 0)), vec, vec]
    in_specs += [pl.BlockSpec((d, tn), lambda i, j: (0, j)) for _ in ws]
    args = [x2d, g.reshape(1, d), shift, scale, *ws]
    if mode == "rope":
        tab = pl.BlockSpec((tm, HEAD_DIM), lambda i, j: (i % per_seq, 0))
        in_specs += [tab, tab, tab]
        args += list(rope_tabs)
    elif mode == "bias":
        in_specs.append(pl.BlockSpec((1, tn), lambda i, j: (0, j)))
        args.append(bias.reshape(1, nout))
    return pl.pallas_call(
        functools.partial(_norm_mm_kernel, mode=mode, tn=tn),
        grid=(m // tm, nout // tn),
        in_specs=in_specs,
        out_specs=pl.BlockSpec((tm, tn), lambda i, j: (i, j)),
        out_shape=jax.ShapeDtypeStruct((m, nout), out_dtype),
        scratch_shapes=[pltpu.VMEM((tm, d), BF16)],
        compiler_params=_params(("parallel", "arbitrary")),
        name="norm_mm_" + mode,
    )(*args)


def _mm_res_kernel(*refs, merge):
    if merge:
        (o0, o1, o2, l0, l1, l2, w_ref, b_ref, x_ref, gt_ref, out_ref, a_ref) = refs
        j = pl.program_id(1)

        @pl.when(j == 0)
        def _():
            ls = (l0[...], l1[...], l2[...])
            mx = jnp.maximum(jnp.maximum(ls[0], ls[1]), ls[2])
            ws = [jnp.exp(l - mx) for l in ls]
            num = ws[0] * o0[...] + ws[1] * o1[...] + ws[2] * o2[...]
            a_ref[...] = (num / (ws[0] + ws[1] + ws[2])).astype(BF16)

        a = a_ref[...]
    else:
        a_in, w_ref, b_ref, x_ref, gt_ref, out_ref = refs
        a = a_in[...]
    y = _dot(a, w_ref[...]) + b_ref[...]
    out_ref[...] = x_ref[...] + gt_ref[...] * y


def _mm_res(a_list, w, bias, x2d, gate, seq_len, *, merge, tm=512, tn=512):
    m, d = x2d.shape
    k = w.shape[0]
    tm = min(tm, seq_len)
    per_seq = seq_len // tm
    a_spec = pl.BlockSpec((tm, k), lambda i, j: (i, 0))
    in_specs = [a_spec] * len(a_list) + [
        pl.BlockSpec((k, tn), lambda i, j: (0, j)),
        pl.BlockSpec((1, tn), lambda i, j: (0, j)),
        pl.BlockSpec((tm, tn), lambda i, j: (i, j)),
        pl.BlockSpec((None, 1, tn), lambda i, j: (i // per_seq, 0, j)),
    ]
    return pl.pallas_call(
        functools.partial(_mm_res_kernel, merge=merge),
        grid=(m // tm, d // tn),
        in_specs=in_specs,
        out_specs=pl.BlockSpec((tm, tn), lambda i, j: (i, j)),
        out_shape=jax.ShapeDtypeStruct((m, d), F32),
        scratch_shapes=[pltpu.VMEM((tm, k), BF16)] if merge else [],
        compiler_params=_params(("parallel", "arbitrary")),
        name="mm_res_merge" if merge else "mm_res",
    )(*a_list, w, bias.reshape(1, d), x2d, gate)


def _final_kernel(x_ref, g_ref, sh_ref, sc_ref, o_ref):
    o_ref[...] = _modnorm(x_ref[...], g_ref[...], sh_ref[...], sc_ref[...])


def _final(x2d, seq_len, g, shift, scale, tm=512):
    m, d = x2d.shape
    tm = min(tm, seq_len)
    per_seq = seq_len // tm
    vec = pl.BlockSpec((None, 1, d), lambda i: (i // per_seq, 0, 0))
    return pl.pallas_call(
        _final_kernel,
        grid=(m // tm,),
        in_specs=[pl.BlockSpec((tm, d), lambda i: (i, 0)), pl.BlockSpec((1, d), lambda i: (0, 0)), vec, vec],
        out_specs=pl.BlockSpec((tm, d), lambda i: (i, 0)),
        out_shape=jax.ShapeDtypeStruct((m, d), F32),
        compiler_params=_params(("parallel",)),
        name="final_norm",
    )(x2d, g.reshape(1, d), shift, scale)


def _short_conv_kernel(zp_ref, zm_ref, zn_ref, w_ref, b_ref, o_ref, *, tl):
    i = pl.program_id(1)
    last = pl.num_programs(1) - 1
    z = zm_ref[...]
    prev_row = jnp.where(i == 0, 0.0, zp_ref[SUBLANES - 1:SUBLANES, :])
    next_row = jnp.where(i == last, 0.0, zn_ref[0:1, :])
    row = lax.broadcasted_iota(jnp.int32, z.shape, 0)
    up = jnp.where(row == 0, prev_row, pltpu.roll(z, 1, 0))
    dn = jnp.where(row == tl - 1, next_row, pltpu.roll(z, tl - 1, 0))
    w = w_ref[...]
    o_ref[...] = up * w[0:1, :] + z * w[1:2, :] + dn * w[2:3, :] + b_ref[...]


def _short_conv(z, w, b, tl=512, tc=512):
    bsz, seq_len, c = z.shape
    tl = min(tl, seq_len)
    nsub = tl // SUBLANES
    nrow8 = seq_len // SUBLANES
    return pl.pallas_call(
        functools.partial(_short_conv_kernel, tl=tl),
        grid=(bsz, seq_len // tl, c // tc),
        in_specs=[
            pl.BlockSpec((None, SUBLANES, tc), lambda bb, i, j: (bb, jnp.maximum(i * nsub - 1, 0), j)),
            pl.BlockSpec((None, tl, tc), lambda bb, i, j: (bb, i, j)),
            pl.BlockSpec((None, SUBLANES, tc), lambda bb, i, j: (bb, jnp.minimum((i + 1) * nsub, nrow8 - 1), j)),
            pl.BlockSpec((3, tc), lambda bb, i, j: (0, j)),
            pl.BlockSpec((1, tc), lambda bb, i, j: (0, j)),
        ],
        out_specs=pl.BlockSpec((None, tl, tc), lambda bb, i, j: (bb, i, j)),
        out_shape=jax.ShapeDtypeStruct(z.shape, F32),
        compiler_params=_params(("parallel", "parallel", "parallel")),
        name="short_conv",
    )(z, z, z, w, b.reshape(1, c))


def _filter_positions(seq_len):
    n = 2 * seq_len
    idx = jnp.arange(n)
    pos = jnp.where(idx < seq_len, idx, n - idx).astype(F32)
    t = pos / max(seq_len - 1, 1)
    bands = jnp.linspace(1e-4, FILTER_BANDS - 1, FILTER_BANDS, dtype=F32)
    ang = 2.0 * math.pi * pos[:, None] * bands[None, :] / seq_len
    z = jnp.concatenate([t[:, None], jnp.cos(ang), -jnp.sin(ang)], axis=-1)
    return jnp.pad(z, ((0, 0), (0, LANES - FILTER_EMB)))


def _filter_mlp_kernel(z_ref, w1_ref, b1_ref, f_ref, w2_ref, b2_ref, o_ref):
    f = f_ref[...]
    a = jnp.sin(f[0:1, :] * (_dot3(z_ref[...], w1_ref[...]) + b1_ref[...]))
    o_ref[...] = jnp.sin(f[1:2, :] * (_dot3(a, w2_ref[...]) + b2_ref[...]))


def _filter_mlp(zf, w1, b1, freq, w2, b2, tr=512):
    n = zf.shape[0]
    hid = w1.shape[1]
    w1p = jnp.pad(w1, ((0, LANES - w1.shape[0]), (0, 0)))
    full = lambda shape: pl.BlockSpec(shape, lambda i: (0,) * len(shape))
    return pl.pallas_call(
        _filter_mlp_kernel,
        grid=(n // tr,),
        in_specs=[pl.BlockSpec((tr, LANES), lambda i: (i, 0)), full((LANES, hid)), full((1, hid)),
                  full((2, hid)), full((hid, hid)), full((1, hid))],
        out_specs=pl.BlockSpec((tr, hid), lambda i: (i, 0)),
        out_shape=jax.ShapeDtypeStruct((n, hid), F32),
        compiler_params=_params(("parallel",)),
        name="filter_mlp",
    )(zf, w1p, b1.reshape(1, hid), freq, w2, b2.reshape(1, hid))


def _filter_taps_kernel(a_ref, t_ref, w3_ref, dec_ref, o_ref, ss_ref, *, tr, seq_len):
    i = pl.program_id(2)
    h = _dot3(a_ref[...], w3_ref[...])
    t = t_ref[...][:, 0:1]
    h = h * jnp.exp(-t * jnp.abs(dec_ref[...]))
    row = i * tr + lax.broadcasted_iota(jnp.int32, h.shape, 0)
    sign = jnp.where(row < seq_len, 1.0, -1.0)
    h = jnp.where(row == seq_len, 0.0, h * sign)
    o_ref[...] = h

    @pl.when(i == 0)
    def _():
        ss_ref[...] = jnp.zeros_like(ss_ref)

    ss_ref[...] += jnp.sum(h * h, axis=0, keepdims=True)


def _filter_taps(a2, zf, w3, decay, seq_len, tr=512, tc=512):
    n, hid = a2.shape
    d = decay.shape[-1]
    nct = d // tc
    half_tiles = seq_len // tr
    dec = decay.reshape(1, HYENA_ORDER * N_DIRS * d)

    def col(o, jc, i):
        return (o * N_DIRS + (i >= half_tiles).astype(jnp.int32)) * nct + jc

    return pl.pallas_call(
        functools.partial(_filter_taps_kernel, tr=tr, seq_len=seq_len),
        grid=(HYENA_ORDER, nct, n // tr),
        in_specs=[
            pl.BlockSpec((tr, hid), lambda o, jc, i: (i, 0)),
            pl.BlockSpec((tr, LANES), lambda o, jc, i: (i, 0)),
            pl.BlockSpec((hid, tc), lambda o, jc, i: (0, col(o, jc, i))),
            pl.BlockSpec((1, tc), lambda o, jc, i: (0, col(o, jc, i))),
        ],
        out_specs=[
            pl.BlockSpec((None, tr, tc), lambda o, jc, i: (o, i, jc)),
            pl.BlockSpec((None, 1, tc), lambda o, jc, i: (o, 0, jc)),
        ],
        out_shape=[jax.ShapeDtypeStruct((HYENA_ORDER, n, d), F32),
                   jax.ShapeDtypeStruct((HYENA_ORDER, 1, d), F32)],
        compiler_params=_params(("parallel", "parallel", "arbitrary")),
        name="filter_taps",
    )(a2, zf, w3, dec)


def _dft_consts(n1, n2):
    n = n1 * n2
    k2 = np.arange(n2 // 2)[:, None]
    nn2 = np.arange(n2)[None, :]
    ph = 2.0 * np.pi * nn2 * (k2 + 0.5) / n2
    f_s1 = np.concatenate([np.cos(ph), -np.sin(ph)], axis=0)
    m = np.arange(n2 // 2)[:, None]
    kk = np.arange(n2 // 2)[None, :]
    ph3 = 2.0 * np.pi * m * (kk + 0.5) / n2
    f_s3 = (2.0 / n) * np.concatenate([np.cos(ph3), -np.sin(ph3)], axis=1)
    a = 2.0 * np.pi * np.outer(np.arange(n1), np.arange(n1)) / n1
    c, s = np.cos(a), -np.sin(a)
    f_fwd = np.block([[c, -s], [s, c]])
    f_inv = np.block([[c, s], [-s, c]])
    th = 2.0 * np.pi * (np.arange(n2 // 2)[:, None] + 0.5) * np.arange(n1)[None, :] / n
    as_bf = lambda x: jnp.asarray(x, F32).astype(BF16)
    tw_c = jnp.broadcast_to(jnp.asarray(np.cos(th), F32)[:, :, None], (n2 // 2, n1, LANES))
    tw_s = jnp.broadcast_to(jnp.asarray(np.sin(th), F32)[:, :, None], (n2 // 2, n1, LANES))
    return dict(f_s1=as_bf(f_s1), f_s3=as_bf(f_s3), f_fwd=as_bf(f_fwd), f_inv=as_bf(f_inv),
                f_s1_f32=jnp.asarray(f_s1, F32), f_fwd_f32=jnp.asarray(f_fwd, F32),
                tw_c=tw_c, tw_s=tw_s)


def _fft_s1_kernel(f_ref, u_ref, o_ref, *, precise):
    half = o_ref.shape[1]
    if precise:
        r = _dot3(f_ref[...], u_ref[...])
    else:
        r = _dot(f_ref[...], u_ref[...].astype(BF16))
    o_ref[0] = r[:half]
    o_ref[1] = r[half:]


def _fft_s1(f_s1, u_view, rows, width, n_cols, col_index):
    bsz = u_view.shape[0]
    half = f_s1.shape[0] // 2
    f = f_s1[:, :rows]
    return pl.pallas_call(
        functools.partial(_fft_s1_kernel, precise=f_s1.dtype == F32),
        grid=(bsz, n_cols),
        in_specs=[pl.BlockSpec(f.shape, lambda bb, j: (0, 0)),
                  pl.BlockSpec((None, rows, width), lambda bb, j: (bb, 0, col_index(j)))],
        out_specs=pl.BlockSpec((None, 2, half, width), lambda bb, j: (bb, 0, 0, j)),
        out_shape=jax.ShapeDtypeStruct((bsz, 2, half, n_cols * width), F32),
        compiler_params=_params(("parallel", "parallel")),
        name="fft_s1",
    )(f, u_view)


def _tile_lanes(x, ct):
    return jnp.concatenate([x] * (ct // LANES), axis=-1) if ct > LANES else x


def _fft_s2f_kernel(a_ref, c_ref, s_ref, ff_ref, ss_ref, o_ref, *, kb, ct):
    n1 = a_ref.shape[2]
    scale = lax.rsqrt(ss_ref[...] + EPS)

    def body(kk, carry):
        ar, ai = a_ref[0, kk], a_ref[1, kk]
        c = _tile_lanes(c_ref[kk], ct)
        s = _tile_lanes(s_ref[kk], ct)
        br = ar * c + ai * s
        bi = ai * c - ar * s
        x = _dot3(ff_ref[...], jnp.concatenate([br, bi], axis=0))
        o_ref[0, kk] = x[:n1] * scale
        o_ref[1, kk] = x[n1:] * scale
        return carry

    lax.fori_loop(0, kb, body, 0)


def _fft_s2_kernel(a_ref, k_ref, c_ref, s_ref, ff_ref, fi_ref, o_ref, *, kb, ct):
    n1 = a_ref.shape[2]

    def body(kk, carry):
        ar, ai = a_ref[0, kk], a_ref[1, kk]
        c = _tile_lanes(c_ref[kk], ct)
        s = _tile_lanes(s_ref[kk], ct)
        br = ar * c + ai * s
        bi = ai * c - ar * s
        x = _dot(ff_ref[...], jnp.concatenate([br, bi], axis=0).astype(BF16))
        xr, xi = x[:n1], x[n1:]
        kr, ki = k_ref[0, kk], k_ref[1, kk]
        zr = xr * kr - xi * ki
        zi = xr * ki + xi * kr
        y = _dot(fi_ref[...], jnp.concatenate([zr, zi], axis=0).astype(BF16))
        yr, yi = y[:n1], y[n1:]
        o_ref[0, kk] = yr * c - yi * s
        o_ref[1, kk] = yr * s + yi * c
        return carry

    lax.fori_loop(0, kb, body, 0)


def _s2_tiles(n1, d):
    kb = max(1, 512 // n1)
    ct = min(d, 512)
    return kb, ct


def _fft_s2f(a, sumsq, consts):
    n_o, _, k2n, nd = a.shape
    n1 = consts["f_fwd"].shape[0] // 2
    d = nd // n1
    kb, ct = _s2_tiles(n1, d)
    a5 = a.reshape(n_o, 2, k2n, n1, d)
    return pl.pallas_call(
        functools.partial(_fft_s2f_kernel, kb=kb, ct=ct),
        grid=(k2n // kb, d // ct, n_o),
        in_specs=[
            pl.BlockSpec((None, 2, kb, n1, ct), lambda k, j, o: (o, 0, k, 0, j)),
            pl.BlockSpec((kb, n1, LANES), lambda k, j, o: (k, 0, 0)),
            pl.BlockSpec((kb, n1, LANES), lambda k, j, o: (k, 0, 0)),
            pl.BlockSpec((2 * n1, 2 * n1), lambda k, j, o: (0, 0)),
            pl.BlockSpec((None, 1, ct), lambda k, j, o: (o, 0, j)),
        ],
        out_specs=pl.BlockSpec((None, 2, kb, n1, ct), lambda k, j, o: (o, 0, k, 0, j)),
        out_shape=jax.ShapeDtypeStruct(a5.shape, F32),
        compiler_params=_params(("parallel", "parallel", "parallel")),
        name="fft_s2_filter",
    )(a5, consts["tw_c"], consts["tw_s"], consts["f_fwd_f32"], sumsq)


def _fft_s2(a, kf, consts):
    bsz, _, k2n, nd = a.shape
    n1 = consts["f_fwd"].shape[0] // 2
    d = nd // n1
    kb, ct = _s2_tiles(n1, d)
    a5 = a.reshape(bsz, 2, k2n, n1, d)
    out = pl.pallas_call(
        functools.partial(_fft_s2_kernel, kb=kb, ct=ct),
        grid=(k2n // kb, d // ct, bsz),
        in_specs=[
            pl.BlockSpec((None, 2, kb, n1, ct), lambda k, j, bb: (bb, 0, k, 0, j)),
            pl.BlockSpec((2, kb, n1, ct), lambda k, j, bb: (0, k, 0, j)),
            pl.BlockSpec((kb, n1, LANES), lambda k, j, bb: (k, 0, 0)),
            pl.BlockSpec((kb, n1, LANES), lambda k, j, bb: (k, 0, 0)),
            pl.BlockSpec((2 * n1, 2 * n1), lambda k, j, bb: (0, 0)),
            pl.BlockSpec((2 * n1, 2 * n1), lambda k, j, bb: (0, 0)),
        ],
        out_specs=pl.BlockSpec((None, 2, kb, n1, ct), lambda k, j, bb: (bb, 0, k, 0, j)),
        out_shape=jax.ShapeDtypeStruct(a5.shape, F32),
        compiler_params=_params(("parallel", "parallel", "parallel")),
        name="fft_s2",
    )(a5, kf, consts["tw_c"], consts["tw_s"], consts["f_fwd"], consts["f_inv"])
    return out.reshape(a.shape)


def _fft_s3_kernel(f_ref, t_ref, u_ref, g_ref, sk_ref, o_ref):
    t = jnp.concatenate([t_ref[0], t_ref[1]], axis=0).astype(BF16)
    y = _dot(f_ref[...], t)
    u = u_ref[...].astype(F32)
    o_ref[...] = (g_ref[...].astype(F32) * (y + u * sk_ref[...])).astype(o_ref.dtype)


def _fft_s3(f_s3, t, u_view, u_index, g_view, g_index, skip, n1, out_dtype):
    bsz, _, half, nd = t.shape
    d = nd // n1
    rows = half
    return pl.pallas_call(
        _fft_s3_kernel,
        grid=(bsz, n1),
        in_specs=[
            pl.BlockSpec(f_s3.shape, lambda bb, j: (0, 0)),
            pl.BlockSpec((None, 2, half, d), lambda bb, j: (bb, 0, 0, j)),
            pl.BlockSpec((None, rows, d), lambda bb, j: (bb, 0, u_index(j))),
            pl.BlockSpec((None, rows, d), lambda bb, j: (bb, 0, g_index(j))),
            pl.BlockSpec((1, d), lambda bb, j: (0, 0)),
        ],
        out_specs=pl.BlockSpec((None, rows, d), lambda bb, j: (bb, 0, j)),
        out_shape=jax.ShapeDtypeStruct((bsz, rows, nd), out_dtype),
        compiler_params=_params(("parallel", "parallel")),
        name="fft_s3",
    )(f_s3, t, u_view, g_view, skip.reshape(1, d))


def _hyena_filter_spectra(seq_len, consts, fw1, fb1, ffreq, fw2, fb2, fw3, decay):
    d = decay.shape[-1]
    n1 = consts["f_fwd"].shape[0] // 2
    zf = _filter_positions(seq_len)
    a2 = _filter_mlp(zf, fw1, fb1, ffreq, fw2, fb2)
    filt, sumsq = _filter_taps(a2, zf, fw3, decay, seq_len)
    fview = filt.reshape(HYENA_ORDER, DFT_N2, n1 * d)
    a = _fft_s1(consts["f_s1_f32"], fview, DFT_N2, d, n1, lambda j: j)
    return _fft_s2f(a, sumsq, consts)


def _hyena_mixer(x2d, bsz, seq_len, g, shift, scale, gate, w_in, b_in, conv_w, conv_b, kf, skip,
                 w_out, b_out, consts):
    d = x2d.shape[1]
    n1 = consts["f_fwd"].shape[0] // 2
    half = DFT_N2 // 2
    z = _norm_mm(x2d, seq_len, g, shift, scale, [w_in], mode="bias", bias=b_in, out_dtype=F32)
    z = _short_conv(z.reshape(bsz, seq_len, 3 * d), conv_w, conv_b)
    zview = z.reshape(bsz, half, n1 * 3 * d)
    a = _fft_s1(consts["f_s1"], zview, half, d, n1, lambda j: 3 * j + 2)
    t = _fft_s2(a, kf[0], consts)
    y1 = _fft_s3(consts["f_s3"], t, zview, lambda j: 3 * j + 2, zview, lambda j: 3 * j, skip[0], n1, F32)
    a = _fft_s1(consts["f_s1"], y1, half, d, n1, lambda j: j)
    t = _fft_s2(a, kf[1], consts)
    y2 = _fft_s3(consts["f_s3"], t, y1, lambda j: j, zview, lambda j: 3 * j + 1, skip[1], n1, BF16)
    return _mm_res([y2.reshape(bsz * seq_len, d)], w_out, b_out, x2d, gate, seq_len, merge=False)


def _attn_kernel(q_ref, kp_ref, km_ref, kn_ref, vp_ref, vm_ref, vn_ref, o_ref, l_ref, kx_ref, vx_ref,
                 *, tq, n):
    i = pl.program_id(2)
    kx_ref[0:ATT_BLK] = kp_ref[...]
    kx_ref[ATT_BLK:ATT_BLK + tq] = km_ref[...]
    kx_ref[ATT_BLK + tq:] = kn_ref[...]
    vx_ref[0:ATT_BLK] = vp_ref[...]
    vx_ref[ATT_BLK:ATT_BLK + tq] = vm_ref[...]
    vx_ref[ATT_BLK + tq:] = vn_ref[...]
    qb = 2 * ATT_BLK
    kw = qb + 2 * ATT_BLK
    row = lax.broadcasted_iota(jnp.int32, (qb, kw), 0)
    col = lax.broadcasted_iota(jnp.int32, (qb, kw), 1)
    band = (col >= row) & (col <= row + 2 * ATT_BLK)
    for s in range(tq // qb):
        ki = i * tq + s * qb - ATT_BLK + col
        valid = band & (ki >= 0) & (ki < n)
        for h in range(HEADS_PER_GROUP):
            cs = slice(h * HEAD_DIM, (h + 1) * HEAD_DIM)
            q = q_ref[s * qb:(s + 1) * qb, cs]
            k = kx_ref[s * qb:s * qb + kw, cs]
            v = vx_ref[s * qb:s * qb + kw, cs]
            sc = lax.dot_general(q, k, (((1,), (1,)), ((), ())), preferred_element_type=F32)
            sc = jnp.where(valid, sc, NEG_BIG)
            m = jnp.max(sc, axis=-1, keepdims=True)
            p = jnp.exp(sc - m)
            den = jnp.sum(p, axis=-1, keepdims=True)
            o = _dot(p.astype(BF16), v) / den
            o_ref[s * qb:(s + 1) * qb, cs] = o
            l_ref[s * qb:(s + 1) * qb, cs] = jnp.broadcast_to(m + jnp.log(den), (qb, HEAD_DIM))


def _attn_group(qkv, g, dilation, tq=256):
    bsz, seq_len, width = qkv.shape
    r = dilation
    n = seq_len // r
    tq = min(tq, n)
    gw = GROUP_WIDTH
    view = qkv.reshape(bsz, n, r * width)
    nblk = n // ATT_BLK
    per = tq // ATT_BLK
    cpr = width // gw

    def cidx(rho, part):
        return rho * cpr + g * 3 + part

    def main(part):
        return pl.BlockSpec((None, tq, gw), lambda bb, rho, i: (bb, i, cidx(rho, part)))

    def prev(part):
        return pl.BlockSpec((None, ATT_BLK, gw),
                            lambda bb, rho, i: (bb, jnp.maximum(i * per - 1, 0), cidx(rho, part)))

    def nxt(part):
        return pl.BlockSpec((None, ATT_BLK, gw),
                            lambda bb, rho, i: (bb, jnp.minimum((i + 1) * per, nblk - 1), cidx(rho, part)))

    out_spec = pl.BlockSpec((None, tq, gw), lambda bb, rho, i: (bb, i, rho))
    o, lse = pl.pallas_call(
        functools.partial(_attn_kernel, tq=tq, n=n),
        grid=(bsz, r, n // tq),
        in_specs=[main(0), prev(1), main(1), nxt(1), prev(2), main(2), nxt(2)],
        out_specs=[out_spec, out_spec],
        out_shape=[jax.ShapeDtypeStruct((bsz, n, r * gw), F32)] * 2,
        scratch_shapes=[pltpu.VMEM((tq + 2 * ATT_BLK, gw), BF16)] * 2,
        compiler_params=_params(("parallel", "parallel", "parallel")),
        name="attn_group%d" % g,
    )(view, view, view, view, view, view, view)
    return o.reshape(bsz * seq_len, gw), lse.reshape(bsz * seq_len, gw)


def _rope_tables(seq_len):
    half = ROT_DIM // 2
    pos = jnp.arange(seq_len, dtype=F32)
    inv = ROPE_THETA ** (-jnp.arange(0, ROT_DIM, 2, dtype=F32) / ROT_DIM)
    ang = pos[:, None] * inv[None, :]
    cos, sin = jnp.cos(ang), jnp.sin(ang)
    rest = HEAD_DIM - ROT_DIM
    c = jnp.concatenate([cos, cos, jnp.ones((seq_len, rest), F32)], axis=1)
    s1 = jnp.concatenate([-sin, jnp.zeros((seq_len, half + rest), F32)], axis=1)
    s2 = jnp.concatenate([jnp.zeros((seq_len, half), F32), sin, jnp.zeros((seq_len, rest), F32)], axis=1)
    return c, s1, s2


def _attn_mixer(x2d, bsz, seq_len, g, shift, scale, gate, w_in, w_out, rope_tabs):
    d = x2d.shape[1]
    qkv = _norm_mm(x2d, seq_len, g, shift, scale, [w_in], mode="rope", rope_tabs=rope_tabs, out_dtype=BF16)
    qkv = qkv.reshape(bsz, seq_len, w_in.shape[1])
    outs, lses = [], []
    for gi, (_, dilation) in enumerate(ATTN_PATTERNS):
        o, lse = _attn_group(qkv, gi, dilation)
        outs.append(o)
        lses.append(lse)
    return _mm_res(outs + lses, w_out, jnp.zeros((d,), F32), x2d, gate, seq_len, merge=True, tm=256)


def _encoder(x, mods, final_mod, p):
    bsz, seq_len, d = x.shape
    x2d = x.reshape(bsz * seq_len, d)
    n1 = 2 * seq_len // DFT_N2
    consts = _dft_consts(n1, DFT_N2)
    rope_tabs = _rope_tables(seq_len)
    for i in range(DEPTH):
        sh_m, sc_m, g_m, sh_f, sc_f, g_f = [mods[i][:, None, k * d:(k + 1) * d] for k in range(6)]
        j = i // 2
        if i % 2 == 0:
            kf = _hyena_filter_spectra(seq_len, consts, p["hy_fw1"][j], p["hy_fb1"][j], p["hy_ffreq"][j],
                                       p["hy_fw2"][j], p["hy_fb2"][j], p["hy_fw3"][j], p["hy_decay"][j])
            x2d = _hyena_mixer(x2d, bsz, seq_len, p["norm_mix"][i], sh_m, sc_m, g_m, p["hy_w_in"][j],
                               p["hy_b_in"][j], p["hy_conv_w"][j], p["hy_conv_b"][j], kf, p["hy_skip"][j],
                               p["hy_w_out"][j], p["hy_b_out"][j], consts)
        else:
            x2d = _attn_mixer(x2d, bsz, seq_len, p["norm_mix"][i], sh_m, sc_m, g_m, p["at_w_in"][j],
                              p["at_w_out"][j], rope_tabs)
        hmid = _norm_mm(x2d, seq_len, p["norm_ffn"][i], sh_f, sc_f, [p["ffn_w_gate"][i], p["ffn_w_up"][i]],
                        mode="swiglu", out_dtype=BF16)
        x2d = _mm_res([hmid], p["ffn_w_down"][i], jnp.zeros((d,), F32), x2d, g_f, seq_len, merge=False)
    sh, sc = final_mod[:, None, :d], final_mod[:, None, d:]
    return _final(x2d, seq_len, p["final_norm"], sh, sc).reshape(bsz, seq_len, d)


def kernel(x_prompt, x_sample, c_prompt, c_sample, ada_w, ada_b, norm_mix, norm_ffn, hy_w_in, hy_b_in, hy_conv_w, hy_conv_b, hy_fw1, hy_fb1, hy_ffreq, hy_fw2, hy_fb2, hy_fw3, hy_decay, hy_skip, hy_w_out, hy_b_out, at_w_in, at_w_out, ffn_w_gate, ffn_w_up, ffn_w_down, final_norm, final_ada_w, final_ada_b):
    d = x_prompt.shape[-1]
    bp, bs = c_prompt.shape[0], c_sample.shape[0]
    pad = -(bp + bs) % (2 * SUBLANES)
    c_all = jnp.concatenate([c_prompt, c_sample, jnp.zeros((pad, d), F32)], axis=0)
    mods = _ada(c_all, ada_w, ada_b)
    fmod = _ada(c_all, final_ada_w[None], final_ada_b[None])[0]
    p = dict(norm_mix=norm_mix, norm_ffn=norm_ffn,
             hy_w_in=hy_w_in.astype(BF16), hy_b_in=hy_b_in, hy_conv_w=hy_conv_w, hy_conv_b=hy_conv_b,
             hy_fw1=hy_fw1, hy_fb1=hy_fb1, hy_ffreq=hy_ffreq, hy_fw2=hy_fw2, hy_fb2=hy_fb2, hy_fw3=hy_fw3,
             hy_decay=hy_decay, hy_skip=hy_skip, hy_w_out=hy_w_out.astype(BF16), hy_b_out=hy_b_out,
             at_w_in=at_w_in.astype(BF16), at_w_out=at_w_out.astype(BF16),
             ffn_w_gate=ffn_w_gate.astype(BF16), ffn_w_up=ffn_w_up.astype(BF16),
             ffn_w_down=ffn_w_down.astype(BF16), final_norm=final_norm)
    y_prompt = _encoder(x_prompt, mods[:, :bp], fmod[:bp], p)
    y_sample = _encoder(x_sample, mods[:, bp:bp + bs], fmod[bp:bp + bs], p)
    return (y_prompt, y_sample)
```

```python
import functools
import math

import numpy as np
import jax
import jax.numpy as jnp
from jax import lax
from jax.experimental import pallas as pl
from jax.experimental.pallas import tpu as pltpu

F32 = jnp.float32
BF16 = jnp.bfloat16
EPS = 1e-6

DEPTH = 4
HYENA_ORDER = 2
N_DIRS = 2
FILTER_BANDS = 16
FILTER_EMB = 1 + 2 * FILTER_BANDS
ATTN_PATTERNS = ((128, 1), (512, 4), (2048, 16))
HEADS_PER_GROUP = 8
HEAD_DIM = 128
GROUP_WIDTH = HEADS_PER_GROUP * HEAD_DIM
ROT_DIM = HEAD_DIM // 4
ROPE_THETA = 500000.0

LANES = 128
SUBLANES = 8
VMEM_LIMIT_BYTES = 56 * 1024 * 1024

DFT_N2 = 256
SLAB_ROWS = DFT_N2 // 2
ATT_BAND = 64
ATT_Q = 2 * ATT_BAND
CLASSES = 16
ROW_TILE = 1024
NEG_BIG = -1e30


def _params(sem):
    return pltpu.CompilerParams(dimension_semantics=sem, vmem_limit_bytes=VMEM_LIMIT_BYTES)


def _dot(a, b):
    return jnp.dot(a, b, preferred_element_type=F32)


def _split(a):
    hi = a.astype(BF16)
    lo = (a - hi.astype(F32)).astype(BF16)
    return hi, lo


def _dot3(a, b):
    ah, al = _split(a)
    bh, bl = _split(b)
    return _dot(ah, bh) + _dot(al, bh) + _dot(ah, bl)


def _modnorm(x, g, shift, scale):
    ms = jnp.mean(x * x, axis=-1, keepdims=True)
    y = x * lax.rsqrt(ms + EPS) * g
    return y * (1.0 + scale) + shift


def _pieces(ref, count):
    return [ref[:, r, :] for r in range(count)]


def _ada_kernel(c_ref, w_ref, b_ref, o_ref):
    c = c_ref[...]
    cs = c * jax.nn.sigmoid(c)
    o_ref[...] = _dot3(cs, w_ref[...]) + b_ref[...]


def _ada(c_all, w, b, tn=1024):
    nl, d, no = w.shape
    r = c_all.shape[0]
    return pl.pallas_call(
        _ada_kernel,
        grid=(nl, no // tn),
        in_specs=[
            pl.BlockSpec((r, d), lambda l, j: (0, 0)),
            pl.BlockSpec((None, d, tn), lambda l, j: (l, 0, j)),
            pl.BlockSpec((None, 1, tn), lambda l, j: (l, 0, j)),
        ],
        out_specs=pl.BlockSpec((None, r, tn), lambda l, j: (l, 0, j)),
        out_shape=jax.ShapeDtypeStruct((nl, r, no), F32),
        compiler_params=_params(("parallel", "parallel")),
        name="ada_mod",
    )(c_all, w, b.reshape(nl, 1, no))


def _norm_mm_kernel(*refs, mode, tn, n_pieces):
    if mode == "swiglu":
        x_ref, g_ref, sh_ref, sc_ref, wg_ref, wu_ref, o_ref, h_ref = refs
    elif mode == "rope":
        x_ref, g_ref, sh_ref, sc_ref, w_ref, c_ref, s1_ref, s2_ref, o_ref, h_ref = refs
    else:
        x_ref, g_ref, sh_ref, sc_ref, w_ref, b_ref, o_ref, h_ref = refs
    j = pl.program_id(2)

    @pl.when(j == 0)
    def _():
        g, sh, sc = g_ref[...], sh_ref[...], sc_ref[...]
        if n_pieces:
            rows = x_ref.shape[0]
            for r, piece in enumerate(_pieces(x_ref, n_pieces)):
                h_ref[r * rows:(r + 1) * rows, :] = _modnorm(piece, g, sh, sc).astype(BF16)
        else:
            h_ref[...] = _modnorm(x_ref[...].reshape(h_ref.shape), g, sh, sc).astype(BF16)

    h = h_ref[...]
    if mode == "swiglu":
        a = _dot(h, wg_ref[...])
        u = _dot(h, wu_ref[...])
        o_ref[...] = (a * jax.nn.sigmoid(a) * u).astype(o_ref.dtype).reshape(o_ref.shape)
    elif mode == "rope":
        acc = _dot(h, w_ref[...])
        part = (j // (GROUP_WIDTH // tn)) % 3

        @pl.when(part == 2)
        def _():
            o_ref[...] = acc.astype(o_ref.dtype).reshape(o_ref.shape)

        @pl.when(part != 2)
        def _():
            reps = tn // HEAD_DIM
            tabs = [t[...].reshape(acc.shape[0], HEAD_DIM) for t in (c_ref, s1_ref, s2_ref)]
            c, s1, s2 = [jnp.concatenate([t] * reps, axis=1) for t in tabs]
            half = ROT_DIM // 2
            r = acc * c + pltpu.roll(acc, tn - half, 1) * s1 + pltpu.roll(acc, half, 1) * s2
            qs = jnp.where(part == 0, HEAD_DIM ** -0.5, 1.0).astype(F32)
            o_ref[...] = (r * qs).astype(o_ref.dtype).reshape(o_ref.shape)
    else:
        o_ref[...] = (_dot(h, w_ref[...]) + b_ref[...]).astype(o_ref.dtype).reshape(o_ref.shape)


def _norm_mm(x, x_spec, n_pieces, rows, grid_rows, g, shift, scale, ws, w_layer, col0, nout, out_shape,
             out_spec, *, mode, bias=None, rope=None, tn=512):
    d = x.shape[-1]
    bsz = x.shape[0]
    cb = col0 // tn
    vec = pl.BlockSpec((None, 1, d), lambda b, i, j: (b, 0, 0))
    in_specs = [x_spec, pl.BlockSpec((1, d), lambda b, i, j: (0, 0)), vec, vec]
    in_specs += [pl.BlockSpec((None, d, tn), lambda b, i, j: (w_layer, 0, cb + j)) for _ in ws]
    args = [x, g.reshape(1, d), shift, scale, *ws]
    if mode == "rope":
        tabs, tab_spec = rope
        in_specs += [tab_spec] * 3
        args += list(tabs)
    elif mode == "bias":
        in_specs.append(pl.BlockSpec((1, tn), lambda b, i, j: (0, cb + j)))
        args.append(bias)
    return pl.pallas_call(
        functools.partial(_norm_mm_kernel, mode=mode, tn=tn, n_pieces=n_pieces),
        grid=(bsz, grid_rows, nout // tn),
        in_specs=in_specs,
        out_specs=out_spec,
        out_shape=out_shape,
        scratch_shapes=[pltpu.VMEM((rows, d), BF16)],
        compiler_params=_params(("parallel", "parallel", "arbitrary")),
        name="norm_mm_" + mode,
    )(*args)


def _mm_res_kernel(*refs, a_mode, res_mode, out_mode, n_a, n_res):
    if a_mode == "merge":
        (o0, o1, o2, l0, l1, l2, w_ref, b_ref, x_ref, gt_ref, out_ref, a_ref) = refs
    elif a_mode == "pieces":
        a_in, w_ref, b_ref, x_ref, gt_ref, out_ref, a_ref = refs
    else:
        a_in, w_ref, b_ref, x_ref, gt_ref, out_ref = refs
    j = pl.program_id(2)

    if a_mode == "merge":
        @pl.when(j == 0)
        def _():
            k = a_ref.shape[1]
            ls = [l[...].reshape(-1, k) for l in (l0, l1, l2)]
            os_ = [o[...].reshape(-1, k) for o in (o0, o1, o2)]
            mx = jnp.maximum(jnp.maximum(ls[0], ls[1]), ls[2])
            ws = [jnp.exp(l - mx) for l in ls]
            num = ws[0] * os_[0] + ws[1] * os_[1] + ws[2] * os_[2]
            a_ref[...] = (num / (ws[0] + ws[1] + ws[2])).astype(BF16)

        a = a_ref[...]
    elif a_mode == "pieces":
        @pl.when(j == 0)
        def _():
            rows = a_in.shape[0]
            for r, piece in enumerate(_pieces(a_in, n_a)):
                a_ref[r * rows:(r + 1) * rows, :] = piece.astype(BF16)

        a = a_ref[...]
    else:
        a = a_in[...].reshape(-1, a_in.shape[-1])
    y = gt_ref[...] * (_dot(a, w_ref[...]) + b_ref[...])
    if res_mode == "pieces":
        res = jnp.concatenate(_pieces(x_ref, n_res), axis=0)
    else:
        res = x_ref[...].reshape(y.shape)
    val = res + y
    if out_mode == "pieces":
        rows = out_ref.shape[0]
        for r in range(out_ref.shape[1]):
            out_ref[:, r, :] = val[r * rows:(r + 1) * rows, :]
    else:
        out_ref[...] = val.reshape(out_ref.shape)


def _mm_res(a_list, a_specs, w, w_layer, bias, res, res_spec, gate, out_shape, out_spec, grid_rows, rows,
            *, a_mode="plain", res_mode="plain", out_mode="plain", n_a=0, n_res=0, tn=512):
    bsz = gate.shape[0]
    k, d = w.shape[1], w.shape[2]
    in_specs = list(a_specs) + [
        pl.BlockSpec((None, k, tn), lambda b, i, j: (w_layer, 0, j)),
        pl.BlockSpec((1, tn), lambda b, i, j: (0, j)),
        res_spec,
        pl.BlockSpec((None, 1, tn), lambda b, i, j: (b, 0, j)),
    ]
    scratch = [pltpu.VMEM((rows, k), BF16)] if a_mode != "plain" else []
    return pl.pallas_call(
        functools.partial(_mm_res_kernel, a_mode=a_mode, res_mode=res_mode, out_mode=out_mode, n_a=n_a,
                          n_res=n_res),
        grid=(bsz, grid_rows, d // tn),
        in_specs=in_specs,
        out_specs=out_spec,
        out_shape=out_shape,
        scratch_shapes=scratch,
        compiler_params=_params(("parallel", "parallel", "arbitrary")),
        name="mm_res_" + a_mode,
    )(*a_list, w, bias, res, gate)


def _final_kernel(x_ref, g_ref, sh_ref, sc_ref, o_ref):
    o_ref[...] = _modnorm(x_ref[...], g_ref[...], sh_ref[...], sc_ref[...])


def _final(x, g, shift, scale, tm=512):
    bsz, seq_len, d = x.shape
    vec = pl.BlockSpec((None, 1, d), lambda b, i: (b, 0, 0))
    blk = pl.BlockSpec((None, tm, d), lambda b, i: (b, i, 0))
    return pl.pallas_call(
        _final_kernel,
        grid=(bsz, seq_len // tm),
        in_specs=[blk, pl.BlockSpec((1, d), lambda b, i: (0, 0)), vec, vec],
        out_specs=blk,
        out_shape=jax.ShapeDtypeStruct(x.shape, F32),
        compiler_params=_params(("parallel", "parallel")),
        name="final_norm",
    )(x, g.reshape(1, d), shift, scale)


def _filter_positions(seq_len, n1):
    n = 2 * seq_len
    h = SLAB_ROWS
    half = jnp.arange(2)[:, None, None]
    s = jnp.arange(n1)[None, :, None]
    r = jnp.arange(h)[None, None, :]
    idx = ((half * h + r) * n1 + s).reshape(n)
    pos = jnp.where(idx < seq_len, idx, n - idx).astype(F32)
    sign = jnp.where(idx < seq_len, 1.0, jnp.where(idx == seq_len, 0.0, -1.0)).astype(F32)
    t = pos / max(seq_len - 1, 1)
    bands = jnp.linspace(1e-4, FILTER_BANDS - 1, FILTER_BANDS, dtype=F32)
    ang = 2.0 * math.pi * pos[:, None] * bands[None, :] / seq_len
    z = jnp.concatenate([t[:, None], jnp.cos(ang), -jnp.sin(ang)], axis=-1)
    z = jnp.pad(z, ((0, 0), (0, LANES - FILTER_EMB - 1)))
    return jnp.concatenate([z, sign[:, None]], axis=-1)


def _filter_mlp_kernel(z_ref, w1_ref, b1_ref, f_ref, w2_ref, b2_ref, o_ref):
    f = f_ref[...]
    a = jnp.sin(f[0:1, :] * (_dot3(z_ref[...], w1_ref[...]) + b1_ref[...]))
    o_ref[...] = jnp.sin(f[1:2, :] * (_dot3(a, w2_ref[...]) + b2_ref[...]))


def _filter_mlp(zf, w1, b1, freq, w2, b2, tr=512):
    n = zf.shape[0]
    hid = w1.shape[1]
    w1p = jnp.pad(w1, ((0, LANES - w1.shape[0]), (0, 0)))
    full = lambda shape: pl.BlockSpec(shape, lambda i: (0,) * len(shape))
    return pl.pallas_call(
        _filter_mlp_kernel,
        grid=(n // tr,),
        in_specs=[pl.BlockSpec((tr, LANES), lambda i: (i, 0)), full((LANES, hid)), full((1, hid)),
                  full((2, hid)), full((hid, hid)), full((1, hid))],
        out_specs=pl.BlockSpec((tr, hid), lambda i: (i, 0)),
        out_shape=jax.ShapeDtypeStruct((n, hid), F32),
        compiler_params=_params(("parallel",)),
        name="filter_mlp",
    )(zf, w1p, b1.reshape(1, hid), freq, w2, b2.reshape(1, hid))


def _filter_taps_kernel(a_ref, z_ref, w3_ref, dec_ref, o_ref, ss_ref):
    i = pl.program_id(2)
    z = z_ref[...]
    t, sign = z[:, 0:1], z[:, LANES - 1:LANES]
    h = _dot3(a_ref[...], w3_ref[...]) * jnp.exp(-t * jnp.abs(dec_ref[...])) * sign
    o_ref[...] = h

    @pl.when(i == 0)
    def _():
        ss_ref[...] = jnp.zeros_like(ss_ref)

    ss_ref[...] += jnp.sum(h * h, axis=0, keepdims=True)


def _filter_taps(a2, zf, w3, decay, tr=512, tc=512):
    n, hid = a2.shape
    d = decay.shape[-1]
    nct = d // tc
    half_tiles = n // (2 * tr)
    dec = decay.reshape(1, HYENA_ORDER * N_DIRS * d)

    def col(o, jc, i):
        return (o * N_DIRS + (i >= half_tiles).astype(jnp.int32)) * nct + jc

    return pl.pallas_call(
        _filter_taps_kernel,
        grid=(HYENA_ORDER, nct, n // tr),
        in_specs=[
            pl.BlockSpec((tr, hid), lambda o, jc, i: (i, 0)),
            pl.BlockSpec((tr, LANES), lambda o, jc, i: (i, 0)),
            pl.BlockSpec((hid, tc), lambda o, jc, i: (0, col(o, jc, i))),
            pl.BlockSpec((1, tc), lambda o, jc, i: (0, col(o, jc, i))),
        ],
        out_specs=[
            pl.BlockSpec((None, tr, tc), lambda o, jc, i: (o, i, jc)),
            pl.BlockSpec((None, 1, tc), lambda o, jc, i: (o, 0, jc)),
        ],
        out_shape=[jax.ShapeDtypeStruct((HYENA_ORDER, n, d), F32),
                   jax.ShapeDtypeStruct((HYENA_ORDER, 1, d), F32)],
        compiler_params=_params(("parallel", "parallel", "arbitrary")),
        name="filter_taps",
    )(a2, zf, w3, dec)


def _dft_consts(n1, n2):
    n = n1 * n2
    k2 = np.arange(n2 // 2)[:, None]
    nn2 = np.arange(n2)[None, :]
    ph = 2.0 * np.pi * nn2 * (k2 + 0.5) / n2
    f_s1 = np.concatenate([np.cos(ph), -np.sin(ph)], axis=0)
    m = np.arange(n2 // 2)[:, None]
    kk = np.arange(n2 // 2)[None, :]
    ph3 = 2.0 * np.pi * m * (kk + 0.5) / n2
    f_s3 = (2.0 / n) * np.concatenate([np.cos(ph3), -np.sin(ph3)], axis=1)
    a = 2.0 * np.pi * np.outer(np.arange(n1), np.arange(n1)) / n1
    c, s = np.cos(a), -np.sin(a)
    f_fwd = np.block([[c, -s], [s, c]])
    f_inv = np.block([[c, s], [-s, c]])
    th = 2.0 * np.pi * (np.arange(n2 // 2)[:, None] + 0.5) * np.arange(n1)[None, :] / n
    as_bf = lambda x: jnp.asarray(x, F32).astype(BF16)
    tw_c = jnp.broadcast_to(jnp.asarray(np.cos(th), F32)[:, :, None], (n2 // 2, n1, LANES))
    tw_s = jnp.broadcast_to(jnp.asarray(np.sin(th), F32)[:, :, None], (n2 // 2, n1, LANES))
    return dict(f_s1=as_bf(f_s1[:, :n2 // 2]), f_s3=as_bf(f_s3), f_fwd=as_bf(f_fwd), f_inv=as_bf(f_inv),
                f_s1_f32=jnp.asarray(f_s1, F32), f_fwd_f32=jnp.asarray(f_fwd, F32),
                tw_c=tw_c, tw_s=tw_s, n1=n1)


def _slab_tile(n1):
    return min(n1, 8)


def _shift_rows(x, down):
    rows = x.shape[0]
    row = lax.broadcasted_iota(jnp.int32, x.shape, 0)
    if down:
        return jnp.where(row == 0, 0.0, pltpu.roll(x, 1, 0))
    return jnp.where(row == rows - 1, 0.0, pltpu.roll(x, rows - 1, 0))


def _short_conv_slabs(main_ref, prev_ref, next_ref, w_ref, b_ref, first, last):
    n_slabs = main_ref.shape[0]
    prev = prev_ref[0]
    prev = jnp.where(first, _shift_rows(prev, True), prev)
    nxt = next_ref[0]
    nxt = jnp.where(last, _shift_rows(nxt, False), nxt)
    w = w_ref[...]
    out = []
    for s in range(n_slabs):
        up = prev if s == 0 else main_ref[s - 1]
        dn = nxt if s == n_slabs - 1 else main_ref[s + 1]
        out.append(up * w[0:1, :] + main_ref[s] * w[1:2, :] + dn * w[2:3, :] + b_ref[...])
    return out


def _fft_s1_kernel(*refs, short_conv):
    if short_conv:
        f_ref, m_ref, p_ref, n_ref, w_ref, b_ref, o_ref, u_ref = refs
        t = pl.program_id(1)
        slabs = _short_conv_slabs(m_ref, p_ref, n_ref, w_ref, b_ref, t == 0, t == pl.num_programs(1) - 1)
    else:
        f_ref, m_ref, o_ref = refs
        slabs = [m_ref[s] for s in range(m_ref.shape[0])]
    half = o_ref.shape[1]
    for s, u in enumerate(slabs):
        if short_conv:
            u_ref[s] = u
        r = _dot(f_ref[...], u.astype(BF16))
        o_ref[0, :, s, :] = r[:half]
        o_ref[1, :, s, :] = r[half:]


def _fft_s1(consts, src, col_block, conv=None, ct=256):
    bsz, n1, h, c = src.shape
    d = consts["d"]
    st = _slab_tile(n1)
    cpb = d // ct
    f = consts["f_s1"]
    main = pl.BlockSpec((None, st, h, ct), lambda b, t, j: (b, t, 0, col_block * cpb + j))
    a_spec = pl.BlockSpec((None, 2, h, st, ct), lambda b, t, j: (b, 0, 0, t, j))
    a_shape = jax.ShapeDtypeStruct((bsz, 2, h, n1, d), F32)
    fspec = pl.BlockSpec(f.shape, lambda b, t, j: (0, 0))
    if conv is None:
        return pl.pallas_call(
            functools.partial(_fft_s1_kernel, short_conv=False),
            grid=(bsz, n1 // st, cpb),
            in_specs=[fspec, main], out_specs=a_spec, out_shape=a_shape,
            compiler_params=_params(("parallel", "parallel", "parallel")), name="fft_s1",
        )(f, src)
    w, b = conv
    prev = pl.BlockSpec((None, 1, h, ct), lambda b, t, j: (b, (t * st + n1 - 1) % n1, 0, col_block * cpb + j))
    nxt = pl.BlockSpec((None, 1, h, ct), lambda b, t, j: (b, ((t + 1) * st) % n1, 0, col_block * cpb + j))
    wspec = pl.BlockSpec((3, ct), lambda b, t, j: (0, col_block * cpb + j))
    bspec = pl.BlockSpec((1, ct), lambda b, t, j: (0, col_block * cpb + j))
    u_spec = pl.BlockSpec((None, st, h, ct), lambda b, t, j: (b, t, 0, j))
    return pl.pallas_call(
        functools.partial(_fft_s1_kernel, short_conv=True),
        grid=(bsz, n1 // st, cpb),
        in_specs=[fspec, main, prev, nxt, wspec, bspec],
        out_specs=[a_spec, u_spec],
        out_shape=[a_shape, jax.ShapeDtypeStruct((bsz, n1, h, d), F32)],
        compiler_params=_params(("parallel", "parallel", "parallel")), name="fft_s1_conv",
    )(f, src, src, src, w, b)


def _fft_s1f_kernel(flo_ref, fhi_ref, lo_ref, hi_ref, o_ref):
    half = o_ref.shape[1]
    for s in range(lo_ref.shape[0]):
        r = _dot3(flo_ref[...], lo_ref[s]) + _dot3(fhi_ref[...], hi_ref[s])
        o_ref[0, :, s, :] = r[:half]
        o_ref[1, :, s, :] = r[half:]


def _fft_s1f(consts, taps, ct=256):
    n_o, _, n1, h, d = taps.shape
    st = _slab_tile(n1)
    f = consts["f_s1_f32"]
    flo, fhi = f[:, :h], f[:, h:]
    fspec = pl.BlockSpec(flo.shape, lambda o, t, j: (0, 0))
    return pl.pallas_call(
        _fft_s1f_kernel,
        grid=(n_o, n1 // st, d // ct),
        in_specs=[fspec, fspec,
                  pl.BlockSpec((None, None, st, h, ct), lambda o, t, j: (o, 0, t, 0, j)),
                  pl.BlockSpec((None, None, st, h, ct), lambda o, t, j: (o, 1, t, 0, j))],
        out_specs=pl.BlockSpec((None, 2, h, st, ct), lambda o, t, j: (o, 0, 0, t, j)),
        out_shape=jax.ShapeDtypeStruct((n_o, 2, h, n1, d), F32),
        compiler_params=_params(("parallel", "parallel", "parallel")), name="fft_s1_filter",
    )(flo, fhi, taps, taps)


def _tile_lanes(x, ct):
    return jnp.concatenate([x] * (ct // LANES), axis=-1) if ct > LANES else x


def _fft_s2f_kernel(a_ref, c_ref, s_ref, ff_ref, ss_ref, o_ref, *, kb, ct):
    n1 = a_ref.shape[2]
    scale = lax.rsqrt(ss_ref[...] + EPS)

    def body(kk, carry):
        ar, ai = a_ref[0, kk], a_ref[1, kk]
        c = _tile_lanes(c_ref[kk], ct)
        s = _tile_lanes(s_ref[kk], ct)
        br = ar * c + ai * s
        bi = ai * c - ar * s
        x = _dot3(ff_ref[...], jnp.concatenate([br, bi], axis=0))
        o_ref[0, kk] = x[:n1] * scale
        o_ref[1, kk] = x[n1:] * scale
        return carry

    lax.fori_loop(0, kb, body, 0)


def _fft_s2_kernel(a_ref, k_ref, c_ref, s_ref, ff_ref, fi_ref, o_ref, *, kb, ct):
    n1 = a_ref.shape[2]

    def body(kk, carry):
        ar, ai = a_ref[0, kk], a_ref[1, kk]
        c = _tile_lanes(c_ref[kk], ct)
        s = _tile_lanes(s_ref[kk], ct)
        br = ar * c + ai * s
        bi = ai * c - ar * s
        x = _dot(ff_ref[...], jnp.concatenate([br, bi], axis=0).astype(BF16))
        xr, xi = x[:n1], x[n1:]
        kr, ki = k_ref[0, kk], k_ref[1, kk]
        zr = xr * kr - xi * ki
        zi = xr * ki + xi * kr
        y = _dot(fi_ref[...], jnp.concatenate([zr, zi], axis=0).astype(BF16))
        yr, yi = y[:n1], y[n1:]
        o_ref[0, kk] = yr * c - yi * s
        o_ref[1, kk] = yr * s + yi * c
        return carry

    lax.fori_loop(0, kb, body, 0)


def _s2_tiles(n1, d):
    kb = max(1, 512 // n1)
    ct = min(d, 512)
    return kb, ct


def _fft_s2f(a, sumsq, consts):
    n_o, _, k2n, n1, d = a.shape
    kb, ct = _s2_tiles(n1, d)
    blk = pl.BlockSpec((None, 2, kb, n1, ct), lambda k, j, o: (o, 0, k, 0, j))
    tw = pl.BlockSpec((kb, n1, LANES), lambda k, j, o: (k, 0, 0))
    return pl.pallas_call(
        functools.partial(_fft_s2f_kernel, kb=kb, ct=ct),
        grid=(k2n // kb, d // ct, n_o),
        in_specs=[blk, tw, tw, pl.BlockSpec((2 * n1, 2 * n1), lambda k, j, o: (0, 0)),
                  pl.BlockSpec((None, 1, ct), lambda k, j, o: (o, 0, j))],
        out_specs=blk,
        out_shape=jax.ShapeDtypeStruct(a.shape, F32),
        compiler_params=_params(("parallel", "parallel", "parallel")),
        name="fft_s2_filter",
    )(a, consts["tw_c"], consts["tw_s"], consts["f_fwd_f32"], sumsq)


def _fft_s2(a, kf, order, consts):
    bsz, _, k2n, n1, d = a.shape
    kb, ct = _s2_tiles(n1, d)
    blk = pl.BlockSpec((None, 2, kb, n1, ct), lambda k, j, b: (b, 0, k, 0, j))
    tw = pl.BlockSpec((kb, n1, LANES), lambda k, j, b: (k, 0, 0))
    mat = pl.BlockSpec((2 * n1, 2 * n1), lambda k, j, b: (0, 0))
    return pl.pallas_call(
        functools.partial(_fft_s2_kernel, kb=kb, ct=ct),
        grid=(k2n // kb, d // ct, bsz),
        in_specs=[blk, pl.BlockSpec((None, 2, kb, n1, ct), lambda k, j, b: (order, 0, k, 0, j)),
                  tw, tw, mat, mat],
        out_specs=blk,
        out_shape=jax.ShapeDtypeStruct(a.shape, F32),
        compiler_params=_params(("parallel", "parallel", "parallel")),
        name="fft_s2",
    )(a, kf, consts["tw_c"], consts["tw_s"], consts["f_fwd"], consts["f_inv"])


def _fft_s3_kernel(f_ref, t_ref, u_ref, gm_ref, gp_ref, gn_ref, w_ref, b_ref, sk_ref, o_ref):
    t_id = pl.program_id(1)
    gates = _short_conv_slabs(gm_ref, gp_ref, gn_ref, w_ref, b_ref, t_id == 0, t_id == pl.num_programs(1) - 1)
    for s, gate in enumerate(gates):
        t = jnp.concatenate([t_ref[0, :, s, :], t_ref[1, :, s, :]], axis=0).astype(BF16)
        y = _dot(f_ref[...], t)
        o_ref[s] = gate * (y + u_ref[s] * sk_ref[...])


def _fft_s3(consts, t, u, z, gate_block, conv_w, conv_b, skip, order, ct=256):
    bsz, _, h, n1, d = t.shape
    st = _slab_tile(n1)
    cpb = d // ct
    f = consts["f_s3"]
    gcol = lambda j: gate_block * cpb + j
    slab = lambda idx: pl.BlockSpec((None, 1, h, ct), lambda b, tt, j: (b, idx(tt), 0, gcol(j)))
    return pl.pallas_call(
        _fft_s3_kernel,
        grid=(bsz, n1 // st, cpb),
        in_specs=[
            pl.BlockSpec(f.shape, lambda b, tt, j: (0, 0)),
            pl.BlockSpec((None, 2, h, st, ct), lambda b, tt, j: (b, 0, 0, tt, j)),
            pl.BlockSpec((None, st, h, ct), lambda b, tt, j: (b, tt, 0, j)),
            pl.BlockSpec((None, st, h, ct), lambda b, tt, j: (b, tt, 0, gcol(j))),
            slab(lambda tt: (tt * st + n1 - 1) % n1),
            slab(lambda tt: ((tt + 1) * st) % n1),
            pl.BlockSpec((3, ct), lambda b, tt, j: (0, gcol(j))),
            pl.BlockSpec((1, ct), lambda b, tt, j: (0, gcol(j))),
            pl.BlockSpec((None, 1, ct), lambda b, tt, j: (order, 0, j)),
        ],
        out_specs=pl.BlockSpec((None, st, h, ct), lambda b, tt, j: (b, tt, 0, j)),
        out_shape=jax.ShapeDtypeStruct((bsz, n1, h, d), F32),
        compiler_params=_params(("parallel", "parallel", "parallel")),
        name="fft_s3",
    )(f, t, u, z, z, z, conv_w, conv_b, skip)


def _hyena_filter_spectra(seq_len, consts, fw1, fb1, ffreq, fw2, fb2, fw3, decay):
    d = decay.shape[-1]
    n1 = consts["n1"]
    zf = _filter_positions(seq_len, n1)
    a2 = _filter_mlp(zf, fw1, fb1, ffreq, fw2, fb2)
    taps, sumsq = _filter_taps(a2, zf, fw3, decay)
    a = _fft_s1f(consts, taps.reshape(HYENA_ORDER, 2, n1, SLAB_ROWS, d))
    return _fft_s2f(a, sumsq, consts)


def _hyena_mixer(x, g, shift, scale, gate, p, layer, kf, consts):
    bsz, seq_len, d = x.shape
    n1 = consts["n1"]
    h = SLAB_ROWS
    st = _slab_tile(n1)
    xv = x.reshape(bsz, h, n1, d)
    z = _norm_mm(
        xv, pl.BlockSpec((None, h, st, d), lambda b, i, j: (b, 0, i, 0)), st, st * h, n1 // st, g, shift, scale,
        [p["hy_w_in"]], layer, 0, 3 * d, jax.ShapeDtypeStruct((bsz, n1, h, 3 * d), F32),
        pl.BlockSpec((None, st, h, 512), lambda b, i, j: (b, i, 0, j)),
        mode="bias", bias=p["hy_b_in"][layer].reshape(1, 3 * d))
    cw, cb = p["hy_conv_w"][layer], p["hy_conv_b"][layer].reshape(1, 3 * d)
    skip = p["hy_skip"][layer].reshape(HYENA_ORDER, 1, d)
    a, u = _fft_s1(consts, z, 2, conv=(cw, cb))
    t = _fft_s2(a, kf, 0, consts)
    y1 = _fft_s3(consts, t, u, z, 0, cw, cb, skip, 0)
    a = _fft_s1(consts, y1, 0)
    t = _fft_s2(a, kf, 1, consts)
    y2 = _fft_s3(consts, t, y1, z, 1, cw, cb, skip, 1)
    q = ROW_TILE // n1
    nat = pl.BlockSpec((None, ROW_TILE, 512), lambda b, i, j: (b, i, j))
    return _mm_res(
        [y2], [pl.BlockSpec((None, n1, q, d), lambda b, i, j: (b, 0, i, 0))], p["hy_w_out"], layer,
        p["hy_b_out"][layer].reshape(1, d), x, nat, gate, jax.ShapeDtypeStruct(x.shape, F32), nat,
        seq_len // ROW_TILE, ROW_TILE, a_mode="pieces", n_a=q)


def _attn_kernel(q_ref, kp_ref, km_ref, kn_ref, vp_ref, vm_ref, vn_ref, o_ref, l_ref, kx_ref, vx_ref, *scr,
                 phases, ta, n_sub, to_classes):
    i = pl.program_id(2)
    halo = ATT_BAND // phases
    qa = ATT_Q // phases
    ka = 2 * qa
    kx_ref[:, 0:halo] = kp_ref[...]
    kx_ref[:, halo:halo + ta] = km_ref[...]
    kx_ref[:, halo + ta:] = kn_ref[...]
    vx_ref[:, 0:halo] = vp_ref[...]
    vx_ref[:, halo:halo + ta] = vm_ref[...]
    vx_ref[:, halo + ta:] = vn_ref[...]
    row = lax.broadcasted_iota(jnp.int32, (ATT_Q, 2 * ATT_Q), 0)
    col = lax.broadcasted_iota(jnp.int32, (ATT_Q, 2 * ATT_Q), 1)
    cq, aq = row >> (qa.bit_length() - 1), row & (qa - 1)
    ck, ak = col >> (ka.bit_length() - 1), col & (ka - 1)
    delta = phases * (ak - aq) - ATT_BAND + ck - cq
    band = (delta >= -ATT_BAND) & (delta <= ATT_BAND)
    for s in range(ta // qa):
        key_idx = phases * (i * ta + s * qa - halo + ak) + ck
        valid = band & (key_idx >= 0) & (key_idx < n_sub)
        for h in range(HEADS_PER_GROUP):
            cs = slice(h * HEAD_DIM, (h + 1) * HEAD_DIM)
            q = jnp.concatenate([q_ref[c, s * qa:(s + 1) * qa, cs] for c in range(phases)], axis=0)
            k = jnp.concatenate([kx_ref[c, s * qa:s * qa + ka, cs] for c in range(phases)], axis=0)
            v = jnp.concatenate([vx_ref[c, s * qa:s * qa + ka, cs] for c in range(phases)], axis=0)
            sc = lax.dot_general(q, k, (((1,), (1,)), ((), ())), preferred_element_type=F32)
            sc = jnp.where(valid, sc, NEG_BIG)
            m = jnp.max(sc, axis=-1, keepdims=True)
            pr = jnp.exp(sc - m)
            den = jnp.sum(pr, axis=-1, keepdims=True)
            o = _dot(pr.astype(BF16), v) / den
            lse = jnp.broadcast_to(m + jnp.log(den), (ATT_Q, HEAD_DIM))
            if to_classes:
                t_ref = scr[0]
                na = ATT_Q // CLASSES
                for val, dst in ((o, o_ref), (lse, l_ref)):
                    t_ref[...] = val.reshape(na, CLASSES, HEAD_DIM)
                    for rho in range(CLASSES):
                        dst[rho, s * na:(s + 1) * na, cs] = t_ref[:, rho, :]
            else:
                for c in range(phases):
                    o_ref[c, s * qa:(s + 1) * qa, cs] = o[c * qa:(c + 1) * qa]
                    l_ref[c, s * qa:(s + 1) * qa, cs] = lse[c * qa:(c + 1) * qa]


def _attn_group(qkv, col0, phases, n_sub, lead_grid, lead_block, lead_index, ta, *, to_classes=False,
                out_arr_shape=None, out_block=None, out_index=None):
    gw = GROUP_WIDTH
    halo = ATT_BAND // phases
    rows = qkv.shape[-2]
    per = ta // halo
    nblk = rows // halo
    cb = col0 // gw

    def spec(nrows, ridx, part):
        return pl.BlockSpec(tuple(lead_block) + (nrows, gw),
                            lambda b, rho, i: tuple(lead_index(b, rho)) + (ridx(i), cb + part))

    main = lambda part: spec(ta, lambda i: i, part)
    prev = lambda part: spec(halo, lambda i: jnp.maximum(i * per - 1, 0), part)
    nxt = lambda part: spec(halo, lambda i: jnp.minimum((i + 1) * per, nblk - 1), part)
    bsz = qkv.shape[0]
    if to_classes:
        out_spec = pl.BlockSpec(out_block, out_index)
        scratch_extra = [pltpu.VMEM((ATT_Q // CLASSES, CLASSES, HEAD_DIM), F32)]
    else:
        out_arr_shape = qkv.shape[:-1] + (gw,)
        out_spec = pl.BlockSpec(tuple(lead_block) + (ta, gw),
                                lambda b, rho, i: tuple(lead_index(b, rho)) + (i, 0))
        scratch_extra = []
    kx_shape = (phases, ta + 2 * halo, gw)
    return pl.pallas_call(
        functools.partial(_attn_kernel, phases=phases, ta=ta, n_sub=n_sub, to_classes=to_classes),
        grid=(bsz, lead_grid, rows // ta),
        in_specs=[main(0), prev(1), main(1), nxt(1), prev(2), main(2), nxt(2)],
        out_specs=[out_spec, out_spec],
        out_shape=[jax.ShapeDtypeStruct(out_arr_shape, F32)] * 2,
        scratch_shapes=[pltpu.VMEM(kx_shape, BF16)] * 2 + scratch_extra,
        compiler_params=_params(("parallel", "parallel", "parallel")),
        name="attn_p%d" % phases + ("_cls" if to_classes else ""),
    )(qkv, qkv, qkv, qkv, qkv, qkv, qkv)


def _rope_tables(pos):
    half = ROT_DIM // 2
    inv = ROPE_THETA ** (-jnp.arange(0, ROT_DIM, 2, dtype=F32) / ROT_DIM)
    ang = pos.astype(F32)[:, None] * inv[None, :]
    cos, sin = jnp.cos(ang), jnp.sin(ang)
    n = pos.shape[0]
    rest = HEAD_DIM - ROT_DIM
    c = jnp.concatenate([cos, cos, jnp.ones((n, rest), F32)], axis=1)
    s1 = jnp.concatenate([-sin, jnp.zeros((n, half + rest), F32)], axis=1)
    s2 = jnp.concatenate([jnp.zeros((n, half), F32), sin, jnp.zeros((n, rest), F32)], axis=1)
    return c, s1, s2


def _attn_mixer(x, g, shift, scale, gate, p, layer):
    bsz, seq_len, d = x.shape
    gw = GROUP_WIDTH
    nc = seq_len // CLASSES
    ca = ROW_TILE // CLASSES
    w_in = p["at_w_in"]
    tabs = _rope_tables(jnp.arange(seq_len))
    qkv0 = _norm_mm(
        x, pl.BlockSpec((None, ROW_TILE, d), lambda b, i, j: (b, i, 0)), 0, ROW_TILE, seq_len // ROW_TILE,
        g, shift, scale, [w_in], layer, 0, 3 * gw, jax.ShapeDtypeStruct((bsz, seq_len, 3 * gw), BF16),
        pl.BlockSpec((None, ROW_TILE, 512), lambda b, i, j: (b, i, j)), mode="rope",
        rope=(tabs, pl.BlockSpec((ROW_TILE, HEAD_DIM), lambda b, i, j: (i, 0))))
    pos_c = (jnp.arange(nc)[None, :] * CLASSES + jnp.arange(CLASSES)[:, None]).reshape(-1)
    tabs_c = [t.reshape(CLASSES, nc, HEAD_DIM) for t in _rope_tables(pos_c)]
    qkv12 = _norm_mm(
        x.reshape(bsz, nc, CLASSES, d), pl.BlockSpec((None, ca, CLASSES, d), lambda b, i, j: (b, i, 0, 0)),
        CLASSES, ROW_TILE, nc // ca, g, shift, scale, [w_in], layer, 3 * gw, 6 * gw,
        jax.ShapeDtypeStruct((bsz, CLASSES, nc, 6 * gw), BF16),
        pl.BlockSpec((None, CLASSES, ca, 512), lambda b, i, j: (b, 0, i, j)), mode="rope",
        rope=(tabs_c, pl.BlockSpec((CLASSES, ca, HEAD_DIM), lambda b, i, j: (0, i, 0))))
    cls_shape = (bsz, CLASSES, nc, gw)
    ta0 = min(256, seq_len)
    o0, l0 = _attn_group(
        qkv0.reshape(bsz, 1, seq_len, 3 * gw), 0, 1, seq_len, 1, (None, 1), lambda b, rho: (b, 0), ta0,
        to_classes=True, out_arr_shape=cls_shape, out_block=(None, CLASSES, ta0 // CLASSES, gw),
        out_index=lambda b, rho, i: (b, 0, i, 0))
    dil1 = ATTN_PATTERNS[1][1]
    ph = CLASSES // dil1
    ta1 = min(64, nc)
    o1, l1 = _attn_group(
        qkv12.reshape(bsz, ph, dil1, nc, 6 * gw), 0, ph, seq_len // dil1, dil1, (None, ph, None),
        lambda b, rho: (b, 0, rho), ta1)
    o1, l1 = o1.reshape(cls_shape), l1.reshape(cls_shape)
    ta2 = min(256, nc)
    o2, l2 = _attn_group(
        qkv12.reshape(bsz, CLASSES, 1, nc, 6 * gw), 3 * gw, 1, nc, CLASSES, (None, None, 1),
        lambda b, rho: (b, rho, 0), ta2)
    o2, l2 = o2.reshape(cls_shape), l2.reshape(cls_shape)
    cm = ca // 2
    cls_blk = lambda width: pl.BlockSpec((None, CLASSES, cm, width), lambda b, i, j: (b, 0, i, 0))
    out_blk = pl.BlockSpec((None, CLASSES, cm, 512), lambda b, i, j: (b, 0, i, j))
    return _mm_res(
        [o0, o1, o2, l0, l1, l2], [cls_blk(gw)] * 6, p["at_w_out"], layer, jnp.zeros((1, d), F32),
        x.reshape(bsz, nc, CLASSES, d), pl.BlockSpec((None, cm, CLASSES, 512), lambda b, i, j: (b, i, 0, j)),
        gate, jax.ShapeDtypeStruct((bsz, CLASSES, nc, d), F32), out_blk, nc // cm, CLASSES * cm,
        a_mode="merge", res_mode="pieces", n_res=CLASSES)


def _ffn(x, g, shift, scale, gate, p, layer, class_major):
    d = x.shape[-1]
    dff = p["ffn_w_gate"].shape[-1]
    bsz = x.shape[0]
    zero_b = jnp.zeros((1, d), F32)
    tn_down = 256
    if not class_major:
        seq_len = x.shape[1]
        tiles = seq_len // ROW_TILE
        xs = pl.BlockSpec((None, ROW_TILE, d), lambda b, i, j: (b, i, 0))
        hs = pl.BlockSpec((None, ROW_TILE, 512), lambda b, i, j: (b, i, j))
        hmid = _norm_mm(x, xs, 0, ROW_TILE, tiles, g, shift, scale, [p["ffn_w_gate"], p["ffn_w_up"]], layer, 0,
                        dff, jax.ShapeDtypeStruct((bsz, seq_len, dff), BF16), hs, mode="swiglu")
        ds = pl.BlockSpec((None, ROW_TILE, tn_down), lambda b, i, j: (b, i, j))
        return _mm_res([hmid], [pl.BlockSpec((None, ROW_TILE, dff), lambda b, i, j: (b, i, 0))],
                       p["ffn_w_down"], layer, zero_b, x, ds, gate, jax.ShapeDtypeStruct(x.shape, F32), ds,
                       tiles, ROW_TILE, tn=tn_down)
    nc = x.shape[2]
    seq_len = CLASSES * nc
    ca = ROW_TILE // CLASSES
    tiles = nc // ca
    cls = lambda width, jj: pl.BlockSpec((None, CLASSES, ca, width), (lambda b, i, j: (b, 0, i, j)) if jj else
                                         (lambda b, i, j: (b, 0, i, 0)))
    hmid = _norm_mm(x, cls(d, False), 0, ROW_TILE, tiles, g, shift, scale, [p["ffn_w_gate"], p["ffn_w_up"]],
                    layer, 0, dff, jax.ShapeDtypeStruct((bsz, CLASSES, nc, dff), BF16), cls(512, True),
                    mode="swiglu")
    out = _mm_res([hmid], [cls(dff, False)], p["ffn_w_down"], layer, zero_b, x, cls(tn_down, True), gate,
                  jax.ShapeDtypeStruct((bsz, nc, CLASSES, d), F32),
                  pl.BlockSpec((None, ca, CLASSES, tn_down), lambda b, i, j: (b, i, 0, j)), tiles, ROW_TILE,
                  out_mode="pieces", tn=tn_down)
    return out.reshape(bsz, seq_len, d)


def _encoder(x, mods, final_mod, p):
    bsz, seq_len, d = x.shape
    n1 = 2 * seq_len // DFT_N2
    consts = _dft_consts(n1, DFT_N2)
    consts["d"] = d
    for i in range(DEPTH):
        sh_m, sc_m, g_m, sh_f, sc_f, g_f = [mods[i][:, None, k * d:(k + 1) * d] for k in range(6)]
        j = i // 2
        if i % 2 == 0:
            kf = _hyena_filter_spectra(seq_len, consts, p["hy_fw1"][j], p["hy_fb1"][j], p["hy_ffreq"][j],
                                       p["hy_fw2"][j], p["hy_fb2"][j], p["hy_fw3"][j], p["hy_decay"][j])
            x = _hyena_mixer(x, p["norm_mix"][i], sh_m, sc_m, g_m, p, j, kf, consts)
        else:
            x = _attn_mixer(x, p["norm_mix"][i], sh_m, sc_m, g_m, p, j)
        x = _ffn(x, p["norm_ffn"][i], sh_f, sc_f, g_f, p, i, class_major=(i % 2 == 1))
    sh, sc = final_mod[:, None, :d], final_mod[:, None, d:]
    return _final(x, p["final_norm"], sh, sc)


def kernel(x_prompt, x_sample, c_prompt, c_sample, ada_w, ada_b, norm_mix, norm_ffn, hy_w_in, hy_b_in, hy_conv_w, hy_conv_b, hy_fw1, hy_fb1, hy_ffreq, hy_fw2, hy_fb2, hy_fw3, hy_decay, hy_skip, hy_w_out, hy_b_out, at_w_in, at_w_out, ffn_w_gate, ffn_w_up, ffn_w_down, final_norm, final_ada_w, final_ada_b):
    d = x_prompt.shape[-1]
    bp, bs = c_prompt.shape[0], c_sample.shape[0]
    pad = -(bp + bs) % (2 * SUBLANES)
    c_all = jnp.concatenate([c_prompt, c_sample, jnp.zeros((pad, d), F32)], axis=0)
    mods = _ada(c_all, ada_w, ada_b)
    fmod = _ada(c_all, final_ada_w[None], final_ada_b[None])[0]
    p = dict(norm_mix=norm_mix, norm_ffn=norm_ffn,
             hy_w_in=hy_w_in.astype(BF16), hy_b_in=hy_b_in, hy_conv_w=hy_conv_w, hy_conv_b=hy_conv_b,
             hy_fw1=hy_fw1, hy_fb1=hy_fb1, hy_ffreq=hy_ffreq, hy_fw2=hy_fw2, hy_fb2=hy_fb2, hy_fw3=hy_fw3,
             hy_decay=hy_decay, hy_skip=hy_skip, hy_w_out=hy_w_out.astype(BF16), hy_b_out=hy_b_out,
             at_w_in=at_w_in.astype(BF16), at_w_out=at_w_out.astype(BF16),
             ffn_w_gate=ffn_w_gate.astype(BF16), ffn_w_up=ffn_w_up.astype(BF16),
             ffn_w_down=ffn_w_down.astype(BF16), final_norm=final_norm)
    y_prompt = _encoder(x_prompt, mods[:, :bp], fmod[:bp], p)
    y_sample = _encoder(x_sample, mods[:, bp:bp + bs], fmod[bp:bp + bs], p)
    return (y_prompt, y_sample)
```

```python
import functools
import math

import numpy as np
import jax
import jax.numpy as jnp
from jax import lax
from jax.experimental import pallas as pl
from jax.experimental.pallas import tpu as pltpu

F32 = jnp.float32
BF16 = jnp.bfloat16
EPS = 1e-6

DEPTH = 4
HYENA_ORDER = 2
N_DIRS = 2
FILTER_BANDS = 16
FILTER_EMB = 1 + 2 * FILTER_BANDS
ATTN_PATTERNS = ((128, 1), (512, 4), (2048, 16))
HEADS_PER_GROUP = 8
HEAD_DIM = 128
GROUP_WIDTH = HEADS_PER_GROUP * HEAD_DIM
ROT_DIM = HEAD_DIM // 4
ROPE_THETA = 500000.0

LANES = 128
SUBLANES = 8
VMEM_LIMIT_BYTES = 56 * 1024 * 1024

DFT_N2 = 256
SLAB_ROWS = DFT_N2 // 2
ATT_BAND = 64
ATT_Q = 2 * ATT_BAND
CLASSES = 16
ROW_TILE = 1024
NEG_BIG = -1e30


def _params(sem):
    return pltpu.CompilerParams(dimension_semantics=sem, vmem_limit_bytes=VMEM_LIMIT_BYTES)


def _dot(a, b):
    return jnp.dot(a, b, preferred_element_type=F32)


def _split(a):
    hi = a.astype(BF16)
    lo = (a - hi.astype(F32)).astype(BF16)
    return hi, lo


def _dot3(a, b):
    ah, al = _split(a)
    bh, bl = _split(b)
    return _dot(ah, bh) + _dot(al, bh) + _dot(ah, bl)


def _modnorm(x, g, shift, scale):
    ms = jnp.mean(x * x, axis=-1, keepdims=True)
    y = x * lax.rsqrt(ms + EPS) * g
    return y * (1.0 + scale) + shift


PERM_ROWS = 256


def _perm_matrix(p, q):
    m = np.zeros((PERM_ROWS, PERM_ROWS), np.float32)
    pi, qi = np.meshgrid(np.arange(p), np.arange(q), indexing="ij")
    m[(qi * p + pi).ravel(), (pi * q + qi).ravel()] = 1.0
    return jnp.asarray(m, BF16)


def _perm_for(p, q):
    assert (q <= 16 and p % (PERM_ROWS // q) == 0) or (p <= 16 and q % (PERM_ROWS // p) == 0), (p, q)
    return _perm_matrix(PERM_ROWS // q, q) if q <= 16 else _perm_matrix(p, PERM_ROWS // p)


def _transpose_rows(dst_ref, src_ref, pm_ref, p, q):
    if q <= 16:
        pg = PERM_ROWS // q
        for grp in range(p // pg):
            t = _dot(pm_ref[...], src_ref[grp * PERM_ROWS:(grp + 1) * PERM_ROWS, :]).astype(BF16)
            for qi in range(q):
                dst_ref[qi * p + grp * pg:qi * p + (grp + 1) * pg, :] = t[qi * pg:(qi + 1) * pg]
    else:
        qg = PERM_ROWS // p
        for grp in range(q // qg):
            blk = jnp.concatenate([src_ref[pi * q + grp * qg:pi * q + (grp + 1) * qg, :] for pi in range(p)],
                                  axis=0)
            dst_ref[grp * PERM_ROWS:(grp + 1) * PERM_ROWS, :] = _dot(pm_ref[...], blk).astype(BF16)


def _ada_kernel(c_ref, w_ref, b_ref, o_ref):
    c = c_ref[...]
    cs = c * jax.nn.sigmoid(c)
    o_ref[...] = _dot3(cs, w_ref[...]) + b_ref[...]


def _ada(c_all, w, b, tn=1024):
    nl, d, no = w.shape
    r = c_all.shape[0]
    return pl.pallas_call(
        _ada_kernel,
        grid=(nl, no // tn),
        in_specs=[
            pl.BlockSpec((r, d), lambda l, j: (0, 0)),
            pl.BlockSpec((None, d, tn), lambda l, j: (l, 0, j)),
            pl.BlockSpec((None, 1, tn), lambda l, j: (l, 0, j)),
        ],
        out_specs=pl.BlockSpec((None, r, tn), lambda l, j: (l, 0, j)),
        out_shape=jax.ShapeDtypeStruct((nl, r, no), F32),
        compiler_params=_params(("parallel", "parallel")),
        name="ada_mod",
    )(c_all, w, b.reshape(nl, 1, no))


def _norm_mm_kernel(*refs, mode, tn, perm):
    if perm:
        pm_ref, refs = refs[0], refs[1:]
        h0_ref, refs = refs[-1], refs[:-1]
    if mode == "swiglu":
        x_ref, g_ref, sh_ref, sc_ref, wg_ref, wu_ref, o_ref, h_ref = refs
    elif mode == "rope":
        x_ref, g_ref, sh_ref, sc_ref, w_ref, c_ref, s1_ref, s2_ref, o_ref, h_ref = refs
    else:
        x_ref, g_ref, sh_ref, sc_ref, w_ref, b_ref, o_ref, h_ref = refs
    j = pl.program_id(2)

    @pl.when(j == 0)
    def _():
        h = _modnorm(x_ref[...].reshape(h_ref.shape), g_ref[...], sh_ref[...], sc_ref[...]).astype(BF16)
        if perm:
            h0_ref[...] = h
            _transpose_rows(h_ref, h0_ref, pm_ref, *perm)
        else:
            h_ref[...] = h

    h = h_ref[...]
    if mode == "swiglu":
        a = _dot(h, wg_ref[...])
        u = _dot(h, wu_ref[...])
        o_ref[...] = (a * jax.nn.sigmoid(a) * u).astype(o_ref.dtype).reshape(o_ref.shape)
    elif mode == "rope":
        acc = _dot(h, w_ref[...])
        part = (j // (GROUP_WIDTH // tn)) % 3

        @pl.when(part == 2)
        def _():
            o_ref[...] = acc.astype(o_ref.dtype).reshape(o_ref.shape)

        @pl.when(part != 2)
        def _():
            reps = tn // HEAD_DIM
            tabs = [t[...].reshape(acc.shape[0], HEAD_DIM) for t in (c_ref, s1_ref, s2_ref)]
            c, s1, s2 = [jnp.concatenate([t] * reps, axis=1) for t in tabs]
            half = ROT_DIM // 2
            r = acc * c + pltpu.roll(acc, tn - half, 1) * s1 + pltpu.roll(acc, half, 1) * s2
            qs = jnp.where(part == 0, HEAD_DIM ** -0.5, 1.0).astype(F32)
            o_ref[...] = (r * qs).astype(o_ref.dtype).reshape(o_ref.shape)
    else:
        o_ref[...] = (_dot(h, w_ref[...]) + b_ref[...]).astype(o_ref.dtype).reshape(o_ref.shape)


def _norm_mm(x, x_spec, perm, rows, grid_rows, g, shift, scale, ws, w_layer, col0, nout, out_shape,
             out_spec, *, mode, bias=None, rope=None, tn=512):
    d = x.shape[-1]
    bsz = x.shape[0]
    cb = col0 // tn
    vec = pl.BlockSpec((None, 1, d), lambda b, i, j: (b, 0, 0))
    in_specs = [x_spec, pl.BlockSpec((1, d), lambda b, i, j: (0, 0)), vec, vec]
    in_specs += [pl.BlockSpec((None, d, tn), lambda b, i, j: (w_layer, 0, cb + j)) for _ in ws]
    args = [x, g.reshape(1, d), shift, scale, *ws]
    scratch = [pltpu.VMEM((rows, d), BF16)]
    if perm:
        in_specs.insert(0, pl.BlockSpec((PERM_ROWS, PERM_ROWS), lambda b, i, j: (0, 0)))
        args.insert(0, _perm_for(*perm))
        scratch.append(pltpu.VMEM((rows, d), BF16))
    if mode == "rope":
        tabs, tab_spec = rope
        in_specs += [tab_spec] * 3
        args += list(tabs)
    elif mode == "bias":
        in_specs.append(pl.BlockSpec((1, tn), lambda b, i, j: (0, cb + j)))
        args.append(bias)
    return pl.pallas_call(
        functools.partial(_norm_mm_kernel, mode=mode, tn=tn, perm=perm),
        grid=(bsz, grid_rows, nout // tn),
        in_specs=in_specs,
        out_specs=out_spec,
        out_shape=out_shape,
        scratch_shapes=scratch,
        compiler_params=_params(("parallel", "parallel", "arbitrary")),
        name="norm_mm_" + mode,
    )(*args)


def _mm_res_kernel(*refs, a_mode, perm):
    if a_mode == "merge":
        (pm_ref, o0, o1, o2, l0, l1, l2, w_ref, b_ref, x_ref, gt_ref, out_ref, a_ref, a0_ref) = refs
    elif a_mode == "perm":
        pm_ref, a_in, w_ref, b_ref, x_ref, gt_ref, out_ref, a_ref, a0_ref = refs
    else:
        a_in, w_ref, b_ref, x_ref, gt_ref, out_ref = refs
    j = pl.program_id(2)

    if a_mode != "plain":
        @pl.when(j == 0)
        def _():
            k = a_ref.shape[1]
            if a_mode == "merge":
                ls = [l[...].reshape(-1, k) for l in (l0, l1, l2)]
                os_ = [o[...].reshape(-1, k) for o in (o0, o1, o2)]
                mx = jnp.maximum(jnp.maximum(ls[0], ls[1]), ls[2])
                ws = [jnp.exp(l - mx) for l in ls]
                num = ws[0] * os_[0] + ws[1] * os_[1] + ws[2] * os_[2]
                a0_ref[...] = (num / (ws[0] + ws[1] + ws[2])).astype(BF16)
            else:
                a0_ref[...] = a_in[...].reshape(-1, k).astype(BF16)
            _transpose_rows(a_ref, a0_ref, pm_ref, *perm)

        a = a_ref[...]
    else:
        a = a_in[...]
    out_ref[...] = x_ref[...] + gt_ref[...] * (_dot(a, w_ref[...]) + b_ref[...])


def _mm_res(a_list, a_specs, w, w_layer, bias, x, gate, rows, *, a_mode="plain", perm=None, tn=512):
    bsz, seq_len, d = x.shape
    k = w.shape[1]
    blk = pl.BlockSpec((None, rows, tn), lambda b, i, j: (b, i, j))
    in_specs = list(a_specs) + [
        pl.BlockSpec((None, k, tn), lambda b, i, j: (w_layer, 0, j)),
        pl.BlockSpec((1, tn), lambda b, i, j: (0, j)),
        blk,
        pl.BlockSpec((None, 1, tn), lambda b, i, j: (b, 0, j)),
    ]
    args = [*a_list, w, bias, x, gate]
    scratch = []
    if a_mode != "plain":
        in_specs.insert(0, pl.BlockSpec((PERM_ROWS, PERM_ROWS), lambda b, i, j: (0, 0)))
        args.insert(0, _perm_for(*perm))
        scratch = [pltpu.VMEM((rows, k), BF16)] * 2
    return pl.pallas_call(
        functools.partial(_mm_res_kernel, a_mode=a_mode, perm=perm),
        grid=(bsz, seq_len // rows, d // tn),
        in_specs=in_specs,
        out_specs=blk,
        out_shape=jax.ShapeDtypeStruct(x.shape, F32),
        scratch_shapes=scratch,
        compiler_params=_params(("parallel", "parallel", "arbitrary")),
        name="mm_res_" + a_mode,
    )(*args)


def _final_kernel(x_ref, g_ref, sh_ref, sc_ref, o_ref):
    o_ref[...] = _modnorm(x_ref[...], g_ref[...], sh_ref[...], sc_ref[...])


def _final(x, g, shift, scale, tm=512):
    bsz, seq_len, d = x.shape
    vec = pl.BlockSpec((None, 1, d), lambda b, i: (b, 0, 0))
    blk = pl.BlockSpec((None, tm, d), lambda b, i: (b, i, 0))
    return pl.pallas_call(
        _final_kernel,
        grid=(bsz, seq_len // tm),
        in_specs=[blk, pl.BlockSpec((1, d), lambda b, i: (0, 0)), vec, vec],
        out_specs=blk,
        out_shape=jax.ShapeDtypeStruct(x.shape, F32),
        compiler_params=_params(("parallel", "parallel")),
        name="final_norm",
    )(x, g.reshape(1, d), shift, scale)


def _filter_positions(seq_len, n1):
    n = 2 * seq_len
    h = SLAB_ROWS
    half = jnp.arange(2)[:, None, None]
    s = jnp.arange(n1)[None, :, None]
    r = jnp.arange(h)[None, None, :]
    idx = ((half * h + r) * n1 + s).reshape(n)
    pos = jnp.where(idx < seq_len, idx, n - idx).astype(F32)
    sign = jnp.where(idx < seq_len, 1.0, jnp.where(idx == seq_len, 0.0, -1.0)).astype(F32)
    t = pos / max(seq_len - 1, 1)
    bands = jnp.linspace(1e-4, FILTER_BANDS - 1, FILTER_BANDS, dtype=F32)
    ang = 2.0 * math.pi * pos[:, None] * bands[None, :] / seq_len
    z = jnp.concatenate([t[:, None], jnp.cos(ang), -jnp.sin(ang)], axis=-1)
    z = jnp.pad(z, ((0, 0), (0, LANES - FILTER_EMB - 1)))
    return jnp.concatenate([z, sign[:, None]], axis=-1)


def _filter_mlp_kernel(z_ref, w1_ref, b1_ref, f_ref, w2_ref, b2_ref, o_ref):
    f = f_ref[...]
    a = jnp.sin(f[0:1, :] * (_dot3(z_ref[...], w1_ref[...]) + b1_ref[...]))
    o_ref[...] = jnp.sin(f[1:2, :] * (_dot3(a, w2_ref[...]) + b2_ref[...]))


def _filter_mlp(zf, w1, b1, freq, w2, b2, tr=512):
    n = zf.shape[0]
    hid = w1.shape[1]
    w1p = jnp.pad(w1, ((0, LANES - w1.shape[0]), (0, 0)))
    full = lambda shape: pl.BlockSpec(shape, lambda i: (0,) * len(shape))
    return pl.pallas_call(
        _filter_mlp_kernel,
        grid=(n // tr,),
        in_specs=[pl.BlockSpec((tr, LANES), lambda i: (i, 0)), full((LANES, hid)), full((1, hid)),
                  full((2, hid)), full((hid, hid)), full((1, hid))],
        out_specs=pl.BlockSpec((tr, hid), lambda i: (i, 0)),
        out_shape=jax.ShapeDtypeStruct((n, hid), F32),
        compiler_params=_params(("parallel",)),
        name="filter_mlp",
    )(zf, w1p, b1.reshape(1, hid), freq, w2, b2.reshape(1, hid))


def _filter_taps_kernel(a_ref, z_ref, w3_ref, dec_ref, o_ref, ss_ref):
    i = pl.program_id(2)
    z = z_ref[...]
    t, sign = z[:, 0:1], z[:, LANES - 1:LANES]
    h = _dot3(a_ref[...], w3_ref[...]) * jnp.exp(-t * jnp.abs(dec_ref[...])) * sign
    o_ref[...] = h

    @pl.when(i == 0)
    def _():
        ss_ref[...] = jnp.zeros_like(ss_ref)

    ss_ref[...] += jnp.sum(h * h, axis=0, keepdims=True)


def _filter_taps(a2, zf, w3, decay, tr=512, tc=512):
    n, hid = a2.shape
    d = decay.shape[-1]
    nct = d // tc
    half_tiles = n // (2 * tr)
    dec = decay.reshape(1, HYENA_ORDER * N_DIRS * d)

    def col(o, jc, i):
        return (o * N_DIRS + (i >= half_tiles).astype(jnp.int32)) * nct + jc

    return pl.pallas_call(
        _filter_taps_kernel,
        grid=(HYENA_ORDER, nct, n // tr),
        in_specs=[
            pl.BlockSpec((tr, hid), lambda o, jc, i: (i, 0)),
            pl.BlockSpec((tr, LANES), lambda o, jc, i: (i, 0)),
            pl.BlockSpec((hid, tc), lambda o, jc, i: (0, col(o, jc, i))),
            pl.BlockSpec((1, tc), lambda o, jc, i: (0, col(o, jc, i))),
        ],
        out_specs=[
            pl.BlockSpec((None, tr, tc), lambda o, jc, i: (o, i, jc)),
            pl.BlockSpec((None, 1, tc), lambda o, jc, i: (o, 0, jc)),
        ],
        out_shape=[jax.ShapeDtypeStruct((HYENA_ORDER, n, d), F32),
                   jax.ShapeDtypeStruct((HYENA_ORDER, 1, d), F32)],
        compiler_params=_params(("parallel", "parallel", "arbitrary")),
        name="filter_taps",
    )(a2, zf, w3, dec)


def _dft_consts(n1, n2):
    n = n1 * n2
    k2 = np.arange(n2 // 2)[:, None]
    nn2 = np.arange(n2)[None, :]
    ph = 2.0 * np.pi * nn2 * (k2 + 0.5) / n2
    f_s1 = np.concatenate([np.cos(ph), -np.sin(ph)], axis=0)
    m = np.arange(n2 // 2)[:, None]
    kk = np.arange(n2 // 2)[None, :]
    ph3 = 2.0 * np.pi * m * (kk + 0.5) / n2
    f_s3 = (2.0 / n) * np.concatenate([np.cos(ph3), -np.sin(ph3)], axis=1)
    a = 2.0 * np.pi * np.outer(np.arange(n1), np.arange(n1)) / n1
    c, s = np.cos(a), -np.sin(a)
    f_fwd = np.block([[c, -s], [s, c]])
    f_inv = np.block([[c, s], [-s, c]])
    th = 2.0 * np.pi * (np.arange(n2 // 2)[:, None] + 0.5) * np.arange(n1)[None, :] / n
    as_bf = lambda x: jnp.asarray(x, F32).astype(BF16)
    tw_c = jnp.broadcast_to(jnp.asarray(np.cos(th), F32)[:, :, None], (n2 // 2, n1, LANES))
    tw_s = jnp.broadcast_to(jnp.asarray(np.sin(th), F32)[:, :, None], (n2 // 2, n1, LANES))
    return dict(f_s1=as_bf(f_s1[:, :n2 // 2]), f_s3=as_bf(f_s3), f_fwd=as_bf(f_fwd), f_inv=as_bf(f_inv),
                f_s1_f32=jnp.asarray(f_s1, F32), f_fwd_f32=jnp.asarray(f_fwd, F32),
                tw_c=tw_c, tw_s=tw_s, n1=n1)


def _slab_tile(n1):
    return min(n1, 8)


def _shift_rows(x, down):
    rows = x.shape[0]
    row = lax.broadcasted_iota(jnp.int32, x.shape, 0)
    if down:
        return jnp.where(row == 0, 0.0, pltpu.roll(x, 1, 0))
    return jnp.where(row == rows - 1, 0.0, pltpu.roll(x, rows - 1, 0))


def _short_conv_slabs(main_ref, prev_ref, next_ref, w_ref, b_ref, first, last):
    n_slabs = main_ref.shape[0]
    prev = prev_ref[0]
    prev = jnp.where(first, _shift_rows(prev, True), prev)
    nxt = next_ref[0]
    nxt = jnp.where(last, _shift_rows(nxt, False), nxt)
    w = w_ref[...]
    out = []
    for s in range(n_slabs):
        up = prev if s == 0 else main_ref[s - 1]
        dn = nxt if s == n_slabs - 1 else main_ref[s + 1]
        out.append(up * w[0:1, :] + main_ref[s] * w[1:2, :] + dn * w[2:3, :] + b_ref[...])
    return out


def _fft_s1_kernel(*refs, short_conv):
    if short_conv:
        f_ref, m_ref, p_ref, n_ref, w_ref, b_ref, o_ref, u_ref = refs
        t = pl.program_id(1)
        slabs = _short_conv_slabs(m_ref, p_ref, n_ref, w_ref, b_ref, t == 0, t == pl.num_programs(1) - 1)
    else:
        f_ref, m_ref, o_ref = refs
        slabs = [m_ref[s] for s in range(m_ref.shape[0])]
    half = o_ref.shape[1]
    for s, u in enumerate(slabs):
        if short_conv:
            u_ref[s] = u
        r = _dot(f_ref[...], u.astype(BF16))
        o_ref[0, :, s, :] = r[:half]
        o_ref[1, :, s, :] = r[half:]


def _fft_s1(consts, src, col_block, conv=None, ct=256):
    bsz, n1, h, c = src.shape
    d = consts["d"]
    st = _slab_tile(n1)
    cpb = d // ct
    f = consts["f_s1"]
    main = pl.BlockSpec((None, st, h, ct), lambda b, t, j: (b, t, 0, col_block * cpb + j))
    a_spec = pl.BlockSpec((None, 2, h, st, ct), lambda b, t, j: (b, 0, 0, t, j))
    a_shape = jax.ShapeDtypeStruct((bsz, 2, h, n1, d), F32)
    fspec = pl.BlockSpec(f.shape, lambda b, t, j: (0, 0))
    if conv is None:
        return pl.pallas_call(
            functools.partial(_fft_s1_kernel, short_conv=False),
            grid=(bsz, n1 // st, cpb),
            in_specs=[fspec, main], out_specs=a_spec, out_shape=a_shape,
            compiler_params=_params(("parallel", "parallel", "parallel")), name="fft_s1",
        )(f, src)
    w, b = conv
    prev = pl.BlockSpec((None, 1, h, ct), lambda b, t, j: (b, (t * st + n1 - 1) % n1, 0, col_block * cpb + j))
    nxt = pl.BlockSpec((None, 1, h, ct), lambda b, t, j: (b, ((t + 1) * st) % n1, 0, col_block * cpb + j))
    wspec = pl.BlockSpec((3, ct), lambda b, t, j: (0, col_block * cpb + j))
    bspec = pl.BlockSpec((1, ct), lambda b, t, j: (0, col_block * cpb + j))
    u_spec = pl.BlockSpec((None, st, h, ct), lambda b, t, j: (b, t, 0, j))
    return pl.pallas_call(
        functools.partial(_fft_s1_kernel, short_conv=True),
        grid=(bsz, n1 // st, cpb),
        in_specs=[fspec, main, prev, nxt, wspec, bspec],
        out_specs=[a_spec, u_spec],
        out_shape=[a_shape, jax.ShapeDtypeStruct((bsz, n1, h, d), F32)],
        compiler_params=_params(("parallel", "parallel", "parallel")), name="fft_s1_conv",
    )(f, src, src, src, w, b)


def _fft_s1f_kernel(flo_ref, fhi_ref, lo_ref, hi_ref, o_ref):
    half = o_ref.shape[1]
    for s in range(lo_ref.shape[0]):
        r = _dot3(flo_ref[...], lo_ref[s]) + _dot3(fhi_ref[...], hi_ref[s])
        o_ref[0, :, s, :] = r[:half]
        o_ref[1, :, s, :] = r[half:]


def _fft_s1f(consts, taps, ct=256):
    n_o, _, n1, h, d = taps.shape
    st = _slab_tile(n1)
    f = consts["f_s1_f32"]
    flo, fhi = f[:, :h], f[:, h:]
    fspec = pl.BlockSpec(flo.shape, lambda o, t, j: (0, 0))
    return pl.pallas_call(
        _fft_s1f_kernel,
        grid=(n_o, n1 // st, d // ct),
        in_specs=[fspec, fspec,
                  pl.BlockSpec((None, None, st, h, ct), lambda o, t, j: (o, 0, t, 0, j)),
                  pl.BlockSpec((None, None, st, h, ct), lambda o, t, j: (o, 1, t, 0, j))],
        out_specs=pl.BlockSpec((None, 2, h, st, ct), lambda o, t, j: (o, 0, 0, t, j)),
        out_shape=jax.ShapeDtypeStruct((n_o, 2, h, n1, d), F32),
        compiler_params=_params(("parallel", "parallel", "parallel")), name="fft_s1_filter",
    )(flo, fhi, taps, taps)


def _tile_lanes(x, ct):
    return jnp.concatenate([x] * (ct // LANES), axis=-1) if ct > LANES else x


def _fft_s2f_kernel(a_ref, c_ref, s_ref, ff_ref, ss_ref, o_ref, *, kb, ct):
    n1 = a_ref.shape[2]
    scale = lax.rsqrt(ss_ref[...] + EPS)

    def body(kk, carry):
        ar, ai = a_ref[0, kk], a_ref[1, kk]
        c = _tile_lanes(c_ref[kk], ct)
        s = _tile_lanes(s_ref[kk], ct)
        br = ar * c + ai * s
        bi = ai * c - ar * s
        x = _dot3(ff_ref[...], jnp.concatenate([br, bi], axis=0))
        o_ref[0, kk] = x[:n1] * scale
        o_ref[1, kk] = x[n1:] * scale
        return carry

    lax.fori_loop(0, kb, body, 0)


def _fft_s2_kernel(a_ref, k_ref, c_ref, s_ref, ff_ref, fi_ref, o_ref, *, kb, ct):
    n1 = a_ref.shape[2]

    def body(kk, carry):
        ar, ai = a_ref[0, kk], a_ref[1, kk]
        c = _tile_lanes(c_ref[kk], ct)
        s = _tile_lanes(s_ref[kk], ct)
        br = ar * c + ai * s
        bi = ai * c - ar * s
        x = _dot(ff_ref[...], jnp.concatenate([br, bi], axis=0).astype(BF16))
        xr, xi = x[:n1], x[n1:]
        kr, ki = k_ref[0, kk], k_ref[1, kk]
        zr = xr * kr - xi * ki
        zi = xr * ki + xi * kr
        y = _dot(fi_ref[...], jnp.concatenate([zr, zi], axis=0).astype(BF16))
        yr, yi = y[:n1], y[n1:]
        o_ref[0, kk] = yr * c - yi * s
        o_ref[1, kk] = yr * s + yi * c
        return carry

    lax.fori_loop(0, kb, body, 0)


def _s2_tiles(n1, d):
    kb = max(1, 512 // n1)
    ct = min(d, 512)
    return kb, ct


def _fft_s2f(a, sumsq, consts):
    n_o, _, k2n, n1, d = a.shape
    kb, ct = _s2_tiles(n1, d)
    blk = pl.BlockSpec((None, 2, kb, n1, ct), lambda k, j, o: (o, 0, k, 0, j))
    tw = pl.BlockSpec((kb, n1, LANES), lambda k, j, o: (k, 0, 0))
    return pl.pallas_call(
        functools.partial(_fft_s2f_kernel, kb=kb, ct=ct),
        grid=(k2n // kb, d // ct, n_o),
        in_specs=[blk, tw, tw, pl.BlockSpec((2 * n1, 2 * n1), lambda k, j, o: (0, 0)),
                  pl.BlockSpec((None, 1, ct), lambda k, j, o: (o, 0, j))],
        out_specs=blk,
        out_shape=jax.ShapeDtypeStruct(a.shape, F32),
        compiler_params=_params(("parallel", "parallel", "parallel")),
        name="fft_s2_filter",
    )(a, consts["tw_c"], consts["tw_s"], consts["f_fwd_f32"], sumsq)


def _fft_s2(a, kf, order, consts):
    bsz, _, k2n, n1, d = a.shape
    kb, ct = _s2_tiles(n1, d)
    blk = pl.BlockSpec((None, 2, kb, n1, ct), lambda k, j, b: (b, 0, k, 0, j))
    tw = pl.BlockSpec((kb, n1, LANES), lambda k, j, b: (k, 0, 0))
    mat = pl.BlockSpec((2 * n1, 2 * n1), lambda k, j, b: (0, 0))
    return pl.pallas_call(
        functools.partial(_fft_s2_kernel, kb=kb, ct=ct),
        grid=(k2n // kb, d // ct, bsz),
        in_specs=[blk, pl.BlockSpec((None, 2, kb, n1, ct), lambda k, j, b: (order, 0, k, 0, j)),
                  tw, tw, mat, mat],
        out_specs=blk,
        out_shape=jax.ShapeDtypeStruct(a.shape, F32),
        compiler_params=_params(("parallel", "parallel", "parallel")),
        name="fft_s2",
    )(a, kf, consts["tw_c"], consts["tw_s"], consts["f_fwd"], consts["f_inv"])


def _fft_s3_kernel(f_ref, t_ref, u_ref, gm_ref, gp_ref, gn_ref, w_ref, b_ref, sk_ref, o_ref):
    t_id = pl.program_id(1)
    gates = _short_conv_slabs(gm_ref, gp_ref, gn_ref, w_ref, b_ref, t_id == 0, t_id == pl.num_programs(1) - 1)
    for s, gate in enumerate(gates):
        t = jnp.concatenate([t_ref[0, :, s, :], t_ref[1, :, s, :]], axis=0).astype(BF16)
        y = _dot(f_ref[...], t)
        o_ref[s] = gate * (y + u_ref[s] * sk_ref[...])


def _fft_s3(consts, t, u, z, gate_block, conv_w, conv_b, skip, order, ct=256):
    bsz, _, h, n1, d = t.shape
    st = _slab_tile(n1)
    cpb = d // ct
    f = consts["f_s3"]
    gcol = lambda j: gate_block * cpb + j
    slab = lambda idx: pl.BlockSpec((None, 1, h, ct), lambda b, tt, j: (b, idx(tt), 0, gcol(j)))
    return pl.pallas_call(
        _fft_s3_kernel,
        grid=(bsz, n1 // st, cpb),
        in_specs=[
            pl.BlockSpec(f.shape, lambda b, tt, j: (0, 0)),
            pl.BlockSpec((None, 2, h, st, ct), lambda b, tt, j: (b, 0, 0, tt, j)),
            pl.BlockSpec((None, st, h, ct), lambda b, tt, j: (b, tt, 0, j)),
            pl.BlockSpec((None, st, h, ct), lambda b, tt, j: (b, tt, 0, gcol(j))),
            slab(lambda tt: (tt * st + n1 - 1) % n1),
            slab(lambda tt: ((tt + 1) * st) % n1),
            pl.BlockSpec((3, ct), lambda b, tt, j: (0, gcol(j))),
            pl.BlockSpec((1, ct), lambda b, tt, j: (0, gcol(j))),
            pl.BlockSpec((None, 1, ct), lambda b, tt, j: (order, 0, j)),
        ],
        out_specs=pl.BlockSpec((None, st, h, ct), lambda b, tt, j: (b, tt, 0, j)),
        out_shape=jax.ShapeDtypeStruct((bsz, n1, h, d), F32),
        compiler_params=_params(("parallel", "parallel", "parallel")),
        name="fft_s3",
    )(f, t, u, z, z, z, conv_w, conv_b, skip)


def _hyena_filter_spectra(seq_len, consts, fw1, fb1, ffreq, fw2, fb2, fw3, decay):
    d = decay.shape[-1]
    n1 = consts["n1"]
    zf = _filter_positions(seq_len, n1)
    a2 = _filter_mlp(zf, fw1, fb1, ffreq, fw2, fb2)
    taps, sumsq = _filter_taps(a2, zf, fw3, decay)
    a = _fft_s1f(consts, taps.reshape(HYENA_ORDER, 2, n1, SLAB_ROWS, d))
    return _fft_s2f(a, sumsq, consts)


def _hyena_mixer(x, g, shift, scale, gate, p, layer, kf, consts):
    bsz, seq_len, d = x.shape
    n1 = consts["n1"]
    h = SLAB_ROWS
    st = _slab_tile(n1)
    xv = x.reshape(bsz, h, n1, d)
    z = _norm_mm(
        xv, pl.BlockSpec((None, h, st, d), lambda b, i, j: (b, 0, i, 0)), (h, st), st * h, n1 // st, g, shift, scale,
        [p["hy_w_in"]], layer, 0, 3 * d, jax.ShapeDtypeStruct((bsz, n1, h, 3 * d), F32),
        pl.BlockSpec((None, st, h, 512), lambda b, i, j: (b, i, 0, j)),
        mode="bias", bias=p["hy_b_in"][layer].reshape(1, 3 * d))
    cw, cb = p["hy_conv_w"][layer], p["hy_conv_b"][layer].reshape(1, 3 * d)
    skip = p["hy_skip"][layer].reshape(HYENA_ORDER, 1, d)
    a, u = _fft_s1(consts, z, 2, conv=(cw, cb))
    t = _fft_s2(a, kf, 0, consts)
    y1 = _fft_s3(consts, t, u, z, 0, cw, cb, skip, 0)
    a = _fft_s1(consts, y1, 0)
    t = _fft_s2(a, kf, 1, consts)
    y2 = _fft_s3(consts, t, y1, z, 1, cw, cb, skip, 1)
    q = ROW_TILE // n1
    return _mm_res(
        [y2], [pl.BlockSpec((None, n1, q, d), lambda b, i, j: (b, 0, i, 0))], p["hy_w_out"], layer,
        p["hy_b_out"][layer].reshape(1, d), x, gate, ROW_TILE, a_mode="perm", perm=(n1, q))


def _attn_kernel(q_ref, kp_ref, km_ref, kn_ref, vp_ref, vm_ref, vn_ref, o_ref, l_ref, kx_ref, vx_ref, *scr,
                 phases, ta, n_sub, to_classes):
    i = pl.program_id(2)
    halo = ATT_BAND // phases
    qa = ATT_Q // phases
    ka = 2 * qa
    kx_ref[:, 0:halo] = kp_ref[...]
    kx_ref[:, halo:halo + ta] = km_ref[...]
    kx_ref[:, halo + ta:] = kn_ref[...]
    vx_ref[:, 0:halo] = vp_ref[...]
    vx_ref[:, halo:halo + ta] = vm_ref[...]
    vx_ref[:, halo + ta:] = vn_ref[...]
    row = lax.broadcasted_iota(jnp.int32, (ATT_Q, 2 * ATT_Q), 0)
    col = lax.broadcasted_iota(jnp.int32, (ATT_Q, 2 * ATT_Q), 1)
    cq, aq = row >> (qa.bit_length() - 1), row & (qa - 1)
    ck, ak = col >> (ka.bit_length() - 1), col & (ka - 1)
    delta = phases * (ak - aq) - ATT_BAND + ck - cq
    band = (delta >= -ATT_BAND) & (delta <= ATT_BAND)
    for s in range(ta // qa):
        key_idx = phases * (i * ta + s * qa - halo + ak) + ck
        valid = band & (key_idx >= 0) & (key_idx < n_sub)
        for h in range(HEADS_PER_GROUP):
            cs = slice(h * HEAD_DIM, (h + 1) * HEAD_DIM)
            q = jnp.concatenate([q_ref[c, s * qa:(s + 1) * qa, cs] for c in range(phases)], axis=0)
            k = jnp.concatenate([kx_ref[c, s * qa:s * qa + ka, cs] for c in range(phases)], axis=0)
            v = jnp.concatenate([vx_ref[c, s * qa:s * qa + ka, cs] for c in range(phases)], axis=0)
            sc = lax.dot_general(q, k, (((1,), (1,)), ((), ())), preferred_element_type=F32)
            sc = jnp.where(valid, sc, NEG_BIG)
            m = jnp.max(sc, axis=-1, keepdims=True)
            pr = jnp.exp(sc - m)
            den = jnp.sum(pr, axis=-1, keepdims=True)
            o = _dot(pr.astype(BF16), v) / den
            lse = jnp.broadcast_to(m + jnp.log(den), (ATT_Q, HEAD_DIM))
            if to_classes:
                t_ref = scr[0]
                na = ATT_Q // CLASSES
                for val, dst in ((o, o_ref), (lse, l_ref)):
                    t_ref[...] = val.reshape(na, CLASSES, HEAD_DIM)
                    for rho in range(CLASSES):
                        dst[rho, s * na:(s + 1) * na, cs] = t_ref[:, rho, :]
            else:
                for c in range(phases):
                    o_ref[c, s * qa:(s + 1) * qa, cs] = o[c * qa:(c + 1) * qa]
                    l_ref[c, s * qa:(s + 1) * qa, cs] = lse[c * qa:(c + 1) * qa]


def _attn_group(qkv, col0, phases, n_sub, lead_grid, lead_block, lead_index, ta, *, to_classes=False,
                out_arr_shape=None, out_block=None, out_index=None):
    gw = GROUP_WIDTH
    halo = ATT_BAND // phases
    rows = qkv.shape[-2]
    per = ta // halo
    nblk = rows // halo
    cb = col0 // gw

    def spec(nrows, ridx, part):
        return pl.BlockSpec(tuple(lead_block) + (nrows, gw),
                            lambda b, rho, i: tuple(lead_index(b, rho)) + (ridx(i), cb + part))

    main = lambda part: spec(ta, lambda i: i, part)
    prev = lambda part: spec(halo, lambda i: jnp.maximum(i * per - 1, 0), part)
    nxt = lambda part: spec(halo, lambda i: jnp.minimum((i + 1) * per, nblk - 1), part)
    bsz = qkv.shape[0]
    if to_classes:
        out_spec = pl.BlockSpec(out_block, out_index)
        scratch_extra = [pltpu.VMEM((ATT_Q // CLASSES, CLASSES, HEAD_DIM), F32)]
    else:
        out_arr_shape = qkv.shape[:-1] + (gw,)
        out_spec = pl.BlockSpec(tuple(lead_block) + (ta, gw),
                                lambda b, rho, i: tuple(lead_index(b, rho)) + (i, 0))
        scratch_extra = []
    kx_shape = (phases, ta + 2 * halo, gw)
    return pl.pallas_call(
        functools.partial(_attn_kernel, phases=phases, ta=ta, n_sub=n_sub, to_classes=to_classes),
        grid=(bsz, lead_grid, rows // ta),
        in_specs=[main(0), prev(1), main(1), nxt(1), prev(2), main(2), nxt(2)],
        out_specs=[out_spec, out_spec],
        out_shape=[jax.ShapeDtypeStruct(out_arr_shape, F32)] * 2,
        scratch_shapes=[pltpu.VMEM(kx_shape, BF16)] * 2 + scratch_extra,
        compiler_params=_params(("parallel", "parallel", "parallel")),
        name="attn_p%d" % phases + ("_cls" if to_classes else ""),
    )(qkv, qkv, qkv, qkv, qkv, qkv, qkv)


def _rope_tables(pos):
    half = ROT_DIM // 2
    inv = ROPE_THETA ** (-jnp.arange(0, ROT_DIM, 2, dtype=F32) / ROT_DIM)
    ang = pos.astype(F32)[:, None] * inv[None, :]
    cos, sin = jnp.cos(ang), jnp.sin(ang)
    n = pos.shape[0]
    rest = HEAD_DIM - ROT_DIM
    c = jnp.concatenate([cos, cos, jnp.ones((n, rest), F32)], axis=1)
    s1 = jnp.concatenate([-sin, jnp.zeros((n, half + rest), F32)], axis=1)
    s2 = jnp.concatenate([jnp.zeros((n, half), F32), sin, jnp.zeros((n, rest), F32)], axis=1)
    return c, s1, s2


def _attn_mixer(x, g, shift, scale, gate, p, layer):
    bsz, seq_len, d = x.shape
    gw = GROUP_WIDTH
    nc = seq_len // CLASSES
    ca = ROW_TILE // CLASSES
    w_in = p["at_w_in"]
    tabs = _rope_tables(jnp.arange(seq_len))
    qkv0 = _norm_mm(
        x, pl.BlockSpec((None, ROW_TILE, d), lambda b, i, j: (b, i, 0)), None, ROW_TILE, seq_len // ROW_TILE,
        g, shift, scale, [w_in], layer, 0, 3 * gw, jax.ShapeDtypeStruct((bsz, seq_len, 3 * gw), BF16),
        pl.BlockSpec((None, ROW_TILE, 512), lambda b, i, j: (b, i, j)), mode="rope",
        rope=(tabs, pl.BlockSpec((ROW_TILE, HEAD_DIM), lambda b, i, j: (i, 0))))
    pos_c = (jnp.arange(nc)[None, :] * CLASSES + jnp.arange(CLASSES)[:, None]).reshape(-1)
    tabs_c = [t.reshape(CLASSES, nc, HEAD_DIM) for t in _rope_tables(pos_c)]
    qkv12 = _norm_mm(
        x.reshape(bsz, nc, CLASSES, d), pl.BlockSpec((None, ca, CLASSES, d), lambda b, i, j: (b, i, 0, 0)),
        (ca, CLASSES), ROW_TILE, nc // ca, g, shift, scale, [w_in], layer, 3 * gw, 6 * gw,
        jax.ShapeDtypeStruct((bsz, CLASSES, nc, 6 * gw), BF16),
        pl.BlockSpec((None, CLASSES, ca, 512), lambda b, i, j: (b, 0, i, j)), mode="rope",
        rope=(tabs_c, pl.BlockSpec((CLASSES, ca, HEAD_DIM), lambda b, i, j: (0, i, 0))))
    cls_shape = (bsz, CLASSES, nc, gw)
    ta0 = min(256, seq_len)
    o0, l0 = _attn_group(
        qkv0.reshape(bsz, 1, seq_len, 3 * gw), 0, 1, seq_len, 1, (None, 1), lambda b, rho: (b, 0), ta0,
        to_classes=True, out_arr_shape=cls_shape, out_block=(None, CLASSES, ta0 // CLASSES, gw),
        out_index=lambda b, rho, i: (b, 0, i, 0))
    dil1 = ATTN_PATTERNS[1][1]
    ph = CLASSES // dil1
    ta1 = min(64, nc)
    o1, l1 = _attn_group(
        qkv12.reshape(bsz, ph, dil1, nc, 6 * gw), 0, ph, seq_len // dil1, dil1, (None, ph, None),
        lambda b, rho: (b, 0, rho), ta1)
    o1, l1 = o1.reshape(cls_shape), l1.reshape(cls_shape)
    ta2 = min(256, nc)
    o2, l2 = _attn_group(
        qkv12.reshape(bsz, CLASSES, 1, nc, 6 * gw), 3 * gw, 1, nc, CLASSES, (None, None, 1),
        lambda b, rho: (b, rho, 0), ta2)
    o2, l2 = o2.reshape(cls_shape), l2.reshape(cls_shape)
    cm = ca // 2
    cls_blk = pl.BlockSpec((None, CLASSES, cm, gw), lambda b, i, j: (b, 0, i, 0))
    return _mm_res([o0, o1, o2, l0, l1, l2], [cls_blk] * 6, p["at_w_out"], layer, jnp.zeros((1, d), F32),
                   x, gate, CLASSES * cm, a_mode="merge", perm=(CLASSES, cm))


def _ffn(x, g, shift, scale, gate, p, layer):
    bsz, seq_len, d = x.shape
    dff = p["ffn_w_gate"].shape[-1]
    tiles = seq_len // ROW_TILE
    xs = pl.BlockSpec((None, ROW_TILE, d), lambda b, i, j: (b, i, 0))
    hs = pl.BlockSpec((None, ROW_TILE, 512), lambda b, i, j: (b, i, j))
    hmid = _norm_mm(x, xs, None, ROW_TILE, tiles, g, shift, scale, [p["ffn_w_gate"], p["ffn_w_up"]], layer, 0,
                    dff, jax.ShapeDtypeStruct((bsz, seq_len, dff), BF16), hs, mode="swiglu")
    return _mm_res([hmid], [pl.BlockSpec((None, ROW_TILE, dff), lambda b, i, j: (b, i, 0))],
                   p["ffn_w_down"], layer, jnp.zeros((1, d), F32), x, gate, ROW_TILE, tn=256)


def _encoder(x, mods, final_mod, p):
    bsz, seq_len, d = x.shape
    n1 = 2 * seq_len // DFT_N2
    consts = _dft_consts(n1, DFT_N2)
    consts["d"] = d
    for i in range(DEPTH):
        sh_m, sc_m, g_m, sh_f, sc_f, g_f = [mods[i][:, None, k * d:(k + 1) * d] for k in range(6)]
        j = i // 2
        if i % 2 == 0:
            kf = _hyena_filter_spectra(seq_len, consts, p["hy_fw1"][j], p["hy_fb1"][j], p["hy_ffreq"][j],
                                       p["hy_fw2"][j], p["hy_fb2"][j], p["hy_fw3"][j], p["hy_decay"][j])
            x = _hyena_mixer(x, p["norm_mix"][i], sh_m, sc_m, g_m, p, j, kf, consts)
        else:
            x = _attn_mixer(x, p["norm_mix"][i], sh_m, sc_m, g_m, p, j)
        x = _ffn(x, p["norm_ffn"][i], sh_f, sc_f, g_f, p, i)
    sh, sc = final_mod[:, None, :d], final_mod[:, None, d:]
    return _final(x, p["final_norm"], sh, sc)


def kernel(x_prompt, x_sample, c_prompt, c_sample, ada_w, ada_b, norm_mix, norm_ffn, hy_w_in, hy_b_in, hy_conv_w, hy_conv_b, hy_fw1, hy_fb1, hy_ffreq, hy_fw2, hy_fb2, hy_fw3, hy_decay, hy_skip, hy_w_out, hy_b_out, at_w_in, at_w_out, ffn_w_gate, ffn_w_up, ffn_w_down, final_norm, final_ada_w, final_ada_b):
    d = x_prompt.shape[-1]
    bp, bs = c_prompt.shape[0], c_sample.shape[0]
    pad = -(bp + bs) % (2 * SUBLANES)
    c_all = jnp.concatenate([c_prompt, c_sample, jnp.zeros((pad, d), F32)], axis=0)
    mods = _ada(c_all, ada_w, ada_b)
    fmod = _ada(c_all, final_ada_w[None], final_ada_b[None])[0]
    p = dict(norm_mix=norm_mix, norm_ffn=norm_ffn,
             hy_w_in=hy_w_in.astype(BF16), hy_b_in=hy_b_in, hy_conv_w=hy_conv_w, hy_conv_b=hy_conv_b,
             hy_fw1=hy_fw1, hy_fb1=hy_fb1, hy_ffreq=hy_ffreq, hy_fw2=hy_fw2, hy_fb2=hy_fb2, hy_fw3=hy_fw3,
             hy_decay=hy_decay, hy_skip=hy_skip, hy_w_out=hy_w_out.astype(BF16), hy_b_out=hy_b_out,
             at_w_in=at_w_in.astype(BF16), at_w_out=at_w_out.astype(BF16),
             ffn_w_gate=ffn_w_gate.astype(BF16), ffn_w_up=ffn_w_up.astype(BF16),
             ffn_w_down=ffn_w_down.astype(BF16), final_norm=final_norm)
    y_prompt = _encoder(x_prompt, mods[:, :bp], fmod[:bp], p)
    y_sample = _encoder(x_sample, mods[:, bp:bp + bs], fmod[bp:bp + bs], p)
    return (y_prompt, y_sample)
```

```python
import functools
import math

import numpy as np
import jax
import jax.numpy as jnp
from jax import lax
from jax.experimental import pallas as pl
from jax.experimental.pallas import tpu as pltpu

F32 = jnp.float32
BF16 = jnp.bfloat16
EPS = 1e-6

DEPTH = 4
HYENA_ORDER = 2
N_DIRS = 2
FILTER_BANDS = 16
FILTER_EMB = 1 + 2 * FILTER_BANDS
ATTN_PATTERNS = ((128, 1), (512, 4), (2048, 16))
HEADS_PER_GROUP = 8
HEAD_DIM = 128
GROUP_WIDTH = HEADS_PER_GROUP * HEAD_DIM
ROT_DIM = HEAD_DIM // 4
ROPE_THETA = 500000.0

LANES = 128
SUBLANES = 8
VMEM_LIMIT_BYTES = 56 * 1024 * 1024

DFT_N2 = 256
SLAB_ROWS = DFT_N2 // 2
ATT_BAND = 64
ATT_Q = 2 * ATT_BAND
CLASSES = 16
ROW_TILE = 1024
NEG_BIG = -1e30


def _params(sem):
    return pltpu.CompilerParams(dimension_semantics=sem, vmem_limit_bytes=VMEM_LIMIT_BYTES)


def _dot(a, b):
    return jnp.dot(a, b, preferred_element_type=F32)


def _split(a):
    hi = a.astype(BF16)
    lo = (a - hi.astype(F32)).astype(BF16)
    return hi, lo


def _dot3(a, b):
    ah, al = _split(a)
    bh, bl = _split(b)
    return _dot(ah, bh) + _dot(al, bh) + _dot(ah, bl)


def _modnorm(x, g, shift, scale):
    ms = jnp.mean(x * x, axis=-1, keepdims=True)
    y = x * lax.rsqrt(ms + EPS) * g
    return y * (1.0 + scale) + shift


PERM_ROWS = 256


def _perm_matrix(p, q):
    m = np.zeros((PERM_ROWS, PERM_ROWS), np.float32)
    pi, qi = np.meshgrid(np.arange(p), np.arange(q), indexing="ij")
    m[(qi * p + pi).ravel(), (pi * q + qi).ravel()] = 1.0
    return jnp.asarray(m, BF16)


def _perm_for(p, q):
    assert (q <= 16 and p % (PERM_ROWS // q) == 0) or (p <= 16 and q % (PERM_ROWS // p) == 0), (p, q)
    return _perm_matrix(PERM_ROWS // q, q) if q <= 16 else _perm_matrix(p, PERM_ROWS // p)


def _transpose_rows(dst_ref, src_ref, pm_ref, p, q):
    if q <= 16:
        pg = PERM_ROWS // q
        for grp in range(p // pg):
            t = _dot(pm_ref[...], src_ref[grp * PERM_ROWS:(grp + 1) * PERM_ROWS, :]).astype(BF16)
            for qi in range(q):
                dst_ref[qi * p + grp * pg:qi * p + (grp + 1) * pg, :] = t[qi * pg:(qi + 1) * pg]
    else:
        qg = PERM_ROWS // p
        for grp in range(q // qg):
            blk = jnp.concatenate([src_ref[pi * q + grp * qg:pi * q + (grp + 1) * qg, :] for pi in range(p)],
                                  axis=0)
            dst_ref[grp * PERM_ROWS:(grp + 1) * PERM_ROWS, :] = _dot(pm_ref[...], blk).astype(BF16)


def _ada_kernel(c_ref, w_ref, b_ref, o_ref):
    c = c_ref[...]
    cs = c * jax.nn.sigmoid(c)
    o_ref[...] = _dot3(cs, w_ref[...]) + b_ref[...]


def _ada(c_all, w, b, tn=1024):
    nl, d, no = w.shape
    r = c_all.shape[0]
    return pl.pallas_call(
        _ada_kernel,
        grid=(nl, no // tn),
        in_specs=[
            pl.BlockSpec((r, d), lambda l, j: (0, 0)),
            pl.BlockSpec((None, d, tn), lambda l, j: (l, 0, j)),
            pl.BlockSpec((None, 1, tn), lambda l, j: (l, 0, j)),
        ],
        out_specs=pl.BlockSpec((None, r, tn), lambda l, j: (l, 0, j)),
        out_shape=jax.ShapeDtypeStruct((nl, r, no), F32),
        compiler_params=_params(("parallel", "parallel")),
        name="ada_mod",
    )(c_all, w, b.reshape(nl, 1, no))


def _norm_mm_kernel(*refs, mode, tn, perm):
    if perm:
        pm_ref, refs = refs[0], refs[1:]
        h0_ref, refs = refs[-1], refs[:-1]
    if mode == "swiglu":
        x_ref, g_ref, sh_ref, sc_ref, wg_ref, wu_ref, o_ref, h_ref = refs
    elif mode == "rope":
        x_ref, g_ref, sh_ref, sc_ref, w_ref, c_ref, s_ref, rot_ref, o_ref, h_ref = refs
    else:
        x_ref, g_ref, sh_ref, sc_ref, w_ref, b_ref, o_ref, h_ref = refs
    j = pl.program_id(2)

    @pl.when(j == 0)
    def _():
        h = _modnorm(x_ref[...].reshape(h_ref.shape), g_ref[...], sh_ref[...], sc_ref[...]).astype(BF16)
        if perm:
            h0_ref[...] = h
            _transpose_rows(h_ref, h0_ref, pm_ref, *perm)
        else:
            h_ref[...] = h

    h = h_ref[...]
    if mode == "swiglu":
        a = _dot(h, wg_ref[...])
        u = _dot(h, wu_ref[...])
        o_ref[...] = (a * jax.nn.sigmoid(a) * u).astype(o_ref.dtype).reshape(o_ref.shape)
    elif mode == "rope":
        acc = _dot(h, w_ref[...])
        part = (j // (GROUP_WIDTH // tn)) % 3

        @pl.when(part == 2)
        def _():
            o_ref[...] = acc.astype(o_ref.dtype).reshape(o_ref.shape)

        @pl.when(part != 2)
        def _():
            reps = tn // HEAD_DIM
            tabs = [t[...].reshape(acc.shape[0], HEAD_DIM) for t in (c_ref, s_ref)]
            c, s = [jnp.concatenate([t] * reps, axis=1) for t in tabs]
            partner = _dot(acc.astype(BF16), rot_ref[...])
            qs = jnp.where(part == 0, HEAD_DIM ** -0.5, 1.0).astype(F32)
            o_ref[...] = ((acc * c + partner * s) * qs).astype(o_ref.dtype).reshape(o_ref.shape)
    else:
        o_ref[...] = (_dot(h, w_ref[...]) + b_ref[...]).astype(o_ref.dtype).reshape(o_ref.shape)


def _norm_mm(x, x_spec, perm, rows, grid_rows, g, shift, scale, ws, w_layer, col0, nout, out_shape,
             out_spec, *, mode, bias=None, rope=None, tn=512):
    d = x.shape[-1]
    bsz = x.shape[0]
    cb = col0 // tn
    vec = pl.BlockSpec((None, 1, d), lambda b, i, j: (b, 0, 0))
    in_specs = [x_spec, pl.BlockSpec((1, d), lambda b, i, j: (0, 0)), vec, vec]
    in_specs += [pl.BlockSpec((None, d, tn), lambda b, i, j: (w_layer, 0, cb + j)) for _ in ws]
    args = [x, g.reshape(1, d), shift, scale, *ws]
    scratch = [pltpu.VMEM((rows, d), BF16)]
    if perm:
        in_specs.insert(0, pl.BlockSpec((PERM_ROWS, PERM_ROWS), lambda b, i, j: (0, 0)))
        args.insert(0, _perm_for(*perm))
        scratch.append(pltpu.VMEM((rows, d), BF16))
    if mode == "rope":
        tabs, tab_spec = rope
        in_specs += [tab_spec] * 2 + [pl.BlockSpec((tn, tn), lambda b, i, j: (0, 0))]
        args += list(tabs) + [_rope_partner_matrix(tn)]
    elif mode == "bias":
        in_specs.append(pl.BlockSpec((1, tn), lambda b, i, j: (0, cb + j)))
        args.append(bias)
    return pl.pallas_call(
        functools.partial(_norm_mm_kernel, mode=mode, tn=tn, perm=perm),
        grid=(bsz, grid_rows, nout // tn),
        in_specs=in_specs,
        out_specs=out_spec,
        out_shape=out_shape,
        scratch_shapes=scratch,
        compiler_params=_params(("parallel", "parallel", "arbitrary")),
        name="norm_mm_" + mode,
    )(*args)


def _mm_res_kernel(*refs, a_mode, perm):
    if a_mode == "merge":
        (pm_ref, o0, o1, o2, l0, l1, l2, w_ref, b_ref, x_ref, gt_ref, out_ref, a_ref, a0_ref) = refs
    elif a_mode == "perm":
        pm_ref, a_in, w_ref, b_ref, x_ref, gt_ref, out_ref, a_ref, a0_ref = refs
    else:
        a_in, w_ref, b_ref, x_ref, gt_ref, out_ref = refs
    j = pl.program_id(2)

    if a_mode != "plain":
        @pl.when(j == 0)
        def _():
            k = a_ref.shape[1]
            if a_mode == "merge":
                ls = [l[...].reshape(-1, k) for l in (l0, l1, l2)]
                os_ = [o[...].reshape(-1, k) for o in (o0, o1, o2)]
                mx = jnp.maximum(jnp.maximum(ls[0], ls[1]), ls[2])
                ws = [jnp.exp(l - mx) for l in ls]
                num = ws[0] * os_[0] + ws[1] * os_[1] + ws[2] * os_[2]
                a0_ref[...] = (num / (ws[0] + ws[1] + ws[2])).astype(BF16)
            else:
                a0_ref[...] = a_in[...].reshape(-1, k).astype(BF16)
            _transpose_rows(a_ref, a0_ref, pm_ref, *perm)

        a = a_ref[...]
    else:
        a = a_in[...]
    out_ref[...] = x_ref[...] + gt_ref[...] * (_dot(a, w_ref[...]) + b_ref[...])


def _mm_res(a_list, a_specs, w, w_layer, bias, x, gate, rows, *, a_mode="plain", perm=None, tn=512):
    bsz, seq_len, d = x.shape
    k = w.shape[1]
    blk = pl.BlockSpec((None, rows, tn), lambda b, i, j: (b, i, j))
    in_specs = list(a_specs) + [
        pl.BlockSpec((None, k, tn), lambda b, i, j: (w_layer, 0, j)),
        pl.BlockSpec((1, tn), lambda b, i, j: (0, j)),
        blk,
        pl.BlockSpec((None, 1, tn), lambda b, i, j: (b, 0, j)),
    ]
    args = [*a_list, w, bias, x, gate]
    scratch = []
    if a_mode != "plain":
        in_specs.insert(0, pl.BlockSpec((PERM_ROWS, PERM_ROWS), lambda b, i, j: (0, 0)))
        args.insert(0, _perm_for(*perm))
        scratch = [pltpu.VMEM((rows, k), BF16)] * 2
    return pl.pallas_call(
        functools.partial(_mm_res_kernel, a_mode=a_mode, perm=perm),
        grid=(bsz, seq_len // rows, d // tn),
        in_specs=in_specs,
        out_specs=blk,
        out_shape=jax.ShapeDtypeStruct(x.shape, F32),
        scratch_shapes=scratch,
        compiler_params=_params(("parallel", "parallel", "arbitrary")),
        name="mm_res_" + a_mode,
    )(*args)


def _final_kernel(x_ref, g_ref, sh_ref, sc_ref, o_ref):
    o_ref[...] = _modnorm(x_ref[...], g_ref[...], sh_ref[...], sc_ref[...])


def _final(x, g, shift, scale, tm=512):
    bsz, seq_len, d = x.shape
    vec = pl.BlockSpec((None, 1, d), lambda b, i: (b, 0, 0))
    blk = pl.BlockSpec((None, tm, d), lambda b, i: (b, i, 0))
    return pl.pallas_call(
        _final_kernel,
        grid=(bsz, seq_len // tm),
        in_specs=[blk, pl.BlockSpec((1, d), lambda b, i: (0, 0)), vec, vec],
        out_specs=blk,
        out_shape=jax.ShapeDtypeStruct(x.shape, F32),
        compiler_params=_params(("parallel", "parallel")),
        name="final_norm",
    )(x, g.reshape(1, d), shift, scale)


def _filter_positions(seq_len, n1):
    n = 2 * seq_len
    h = SLAB_ROWS
    half = jnp.arange(2)[:, None, None]
    s = jnp.arange(n1)[None, :, None]
    r = jnp.arange(h)[None, None, :]
    idx = ((half * h + r) * n1 + s).reshape(n)
    pos = jnp.where(idx < seq_len, idx, n - idx).astype(F32)
    sign = jnp.where(idx < seq_len, 1.0, jnp.where(idx == seq_len, 0.0, -1.0)).astype(F32)
    t = pos / max(seq_len - 1, 1)
    bands = jnp.linspace(1e-4, FILTER_BANDS - 1, FILTER_BANDS, dtype=F32)
    ang = 2.0 * math.pi * pos[:, None] * bands[None, :] / seq_len
    z = jnp.concatenate([t[:, None], jnp.cos(ang), -jnp.sin(ang)], axis=-1)
    z = jnp.pad(z, ((0, 0), (0, LANES - FILTER_EMB - 1)))
    return jnp.concatenate([z, sign[:, None]], axis=-1)


def _filter_mlp_kernel(z_ref, w1_ref, b1_ref, f_ref, w2_ref, b2_ref, o_ref):
    f = f_ref[...]
    a = jnp.sin(f[0:1, :] * (_dot3(z_ref[...], w1_ref[...]) + b1_ref[...]))
    o_ref[...] = jnp.sin(f[1:2, :] * (_dot3(a, w2_ref[...]) + b2_ref[...]))


def _filter_mlp(zf, w1, b1, freq, w2, b2, tr=512):
    n = zf.shape[0]
    hid = w1.shape[1]
    w1p = jnp.pad(w1, ((0, LANES - w1.shape[0]), (0, 0)))
    full = lambda shape: pl.BlockSpec(shape, lambda i: (0,) * len(shape))
    return pl.pallas_call(
        _filter_mlp_kernel,
        grid=(n // tr,),
        in_specs=[pl.BlockSpec((tr, LANES), lambda i: (i, 0)), full((LANES, hid)), full((1, hid)),
                  full((2, hid)), full((hid, hid)), full((1, hid))],
        out_specs=pl.BlockSpec((tr, hid), lambda i: (i, 0)),
        out_shape=jax.ShapeDtypeStruct((n, hid), F32),
        compiler_params=_params(("parallel",)),
        name="filter_mlp",
    )(zf, w1p, b1.reshape(1, hid), freq, w2, b2.reshape(1, hid))


def _dft_consts(n1, n2):
    n = n1 * n2
    k2 = np.arange(n2 // 2)[:, None]
    nn2 = np.arange(n2)[None, :]
    ph = 2.0 * np.pi * nn2 * (k2 + 0.5) / n2
    f_s1 = np.concatenate([np.cos(ph), -np.sin(ph)], axis=0)
    m = np.arange(n2 // 2)[:, None]
    kk = np.arange(n2 // 2)[None, :]
    ph3 = 2.0 * np.pi * m * (kk + 0.5) / n2
    f_s3 = (2.0 / n) * np.concatenate([np.cos(ph3), -np.sin(ph3)], axis=1)
    a = 2.0 * np.pi * np.outer(np.arange(n1), np.arange(n1)) / n1
    c, s = np.cos(a), -np.sin(a)
    f_fwd = np.block([[c, -s], [s, c]])
    f_inv = np.block([[c, s], [-s, c]])
    th = 2.0 * np.pi * (np.arange(n2 // 2)[:, None] + 0.5) * np.arange(n1)[None, :] / n
    as_bf = lambda x: jnp.asarray(x, F32).astype(BF16)
    tw_c = jnp.broadcast_to(jnp.asarray(np.cos(th), F32)[:, :, None], (n2 // 2, n1, LANES))
    tw_s = jnp.broadcast_to(jnp.asarray(np.sin(th), F32)[:, :, None], (n2 // 2, n1, LANES))
    return dict(f_s1=as_bf(f_s1[:, :n2 // 2]), f_s3=as_bf(f_s3), f_fwd=as_bf(f_fwd), f_inv=as_bf(f_inv),
                f_s1_f32=jnp.asarray(f_s1, F32), f_fwd_f32=jnp.asarray(f_fwd, F32),
                tw_c=tw_c, tw_s=tw_s, n1=n1)


def _slab_tile(n1):
    return min(n1, 8)


HI16 = -65536
HALF_ULP16 = 0x8000


def _pack_c(re, im):
    rb = lax.bitcast_convert_type(re, jnp.int32) + HALF_ULP16
    ib = lax.bitcast_convert_type(im, jnp.int32) + HALF_ULP16
    return (rb & HI16) | lax.shift_right_logical(ib, 16)


def _unpack_c(p):
    re = lax.bitcast_convert_type(p & HI16, F32)
    im = lax.bitcast_convert_type(lax.shift_left(p, 16), F32)
    return re, im


def _shift_rows(x, down):
    rows = x.shape[0]
    row = lax.broadcasted_iota(jnp.int32, x.shape, 0)
    if down:
        return jnp.where(row == 0, 0.0, pltpu.roll(x, 1, 0))
    return jnp.where(row == rows - 1, 0.0, pltpu.roll(x, rows - 1, 0))


def _short_conv_slabs(main_ref, prev_ref, next_ref, w_ref, b_ref, first, last):
    n_slabs = main_ref.shape[0]
    prev = prev_ref[0]
    prev = jnp.where(first, _shift_rows(prev, True), prev)
    nxt = next_ref[0]
    nxt = jnp.where(last, _shift_rows(nxt, False), nxt)
    w = w_ref[...]
    out = []
    for s in range(n_slabs):
        up = prev if s == 0 else main_ref[s - 1]
        dn = nxt if s == n_slabs - 1 else main_ref[s + 1]
        out.append(up * w[0:1, :] + main_ref[s] * w[1:2, :] + dn * w[2:3, :] + b_ref[...])
    return out


def _fft_s1_kernel(*refs, short_conv):
    if short_conv:
        f_ref, m_ref, p_ref, n_ref, w_ref, b_ref, o_ref, u_ref = refs
        t = pl.program_id(1)
        slabs = _short_conv_slabs(m_ref, p_ref, n_ref, w_ref, b_ref, t == 0, t == pl.num_programs(1) - 1)
    else:
        f_ref, m_ref, o_ref = refs
        slabs = [m_ref[s] for s in range(m_ref.shape[0])]
    half = o_ref.shape[0]
    for s, u in enumerate(slabs):
        if short_conv:
            u_ref[s] = u
        r = _dot(f_ref[...], u.astype(BF16))
        o_ref[:, s, :] = _pack_c(r[:half], r[half:])


def _fft_s1(consts, src, col_block, conv=None, ct=256):
    bsz, n1, h, c = src.shape
    d = consts["d"]
    st = _slab_tile(n1)
    cpb = d // ct
    f = consts["f_s1"]
    main = pl.BlockSpec((None, st, h, ct), lambda b, t, j: (b, t, 0, col_block * cpb + j))
    a_spec = pl.BlockSpec((None, h, st, ct), lambda b, t, j: (b, 0, t, j))
    a_shape = jax.ShapeDtypeStruct((bsz, h, n1, d), jnp.int32)
    fspec = pl.BlockSpec(f.shape, lambda b, t, j: (0, 0))
    if conv is None:
        return pl.pallas_call(
            functools.partial(_fft_s1_kernel, short_conv=False),
            grid=(bsz, n1 // st, cpb),
            in_specs=[fspec, main], out_specs=a_spec, out_shape=a_shape,
            compiler_params=_params(("parallel", "parallel", "parallel")), name="fft_s1",
        )(f, src)
    w, b = conv
    prev = pl.BlockSpec((None, 1, h, ct), lambda b, t, j: (b, (t * st + n1 - 1) % n1, 0, col_block * cpb + j))
    nxt = pl.BlockSpec((None, 1, h, ct), lambda b, t, j: (b, ((t + 1) * st) % n1, 0, col_block * cpb + j))
    wspec = pl.BlockSpec((3, ct), lambda b, t, j: (0, col_block * cpb + j))
    bspec = pl.BlockSpec((1, ct), lambda b, t, j: (0, col_block * cpb + j))
    u_spec = pl.BlockSpec((None, st, h, ct), lambda b, t, j: (b, t, 0, j))
    return pl.pallas_call(
        functools.partial(_fft_s1_kernel, short_conv=True),
        grid=(bsz, n1 // st, cpb),
        in_specs=[fspec, main, prev, nxt, wspec, bspec],
        out_specs=[a_spec, u_spec],
        out_shape=[a_shape, jax.ShapeDtypeStruct((bsz, n1, h, d), F32)],
        compiler_params=_params(("parallel", "parallel", "parallel")), name="fft_s1_conv",
    )(f, src, src, src, w, b)


def _filter_s1_kernel(flo_ref, fhi_ref, alo_ref, ahi_ref, zlo_ref, zhi_ref, wf_ref, wb_ref, df_ref, db_ref,
                      o_ref, ss_ref):
    half = o_ref.shape[1]

    @pl.when(pl.program_id(2) == 0)
    def _():
        ss_ref[...] = jnp.zeros_like(ss_ref)

    ss = jnp.zeros(ss_ref.shape, F32)
    for s in range(alo_ref.shape[0]):
        taps = []
        for a_ref, z_ref, w_ref, d_ref in ((alo_ref, zlo_ref, wf_ref, df_ref), (ahi_ref, zhi_ref, wb_ref, db_ref)):
            z = z_ref[s]
            t, sign = z[:, 0:1], z[:, LANES - 1:LANES]
            h = _dot3(a_ref[s], w_ref[...]) * jnp.exp(-t * jnp.abs(d_ref[...])) * sign
            ss = ss + jnp.sum(h * h, axis=0, keepdims=True)
            taps.append(h)
        r = _dot3(flo_ref[...], taps[0]) + _dot3(fhi_ref[...], taps[1])
        o_ref[0, :, s, :] = r[:half]
        o_ref[1, :, s, :] = r[half:]
    ss_ref[...] += ss


def _filter_s1(consts, a2, zf, w3, decay, ct=256):
    n1, h = consts["n1"], SLAB_ROWS
    hid = a2.shape[1]
    d = decay.shape[-1]
    nct = d // ct
    st = _slab_tile(n1)
    f = consts["f_s1_f32"]
    flo, fhi = f[:, :h], f[:, h:]
    dec = decay.reshape(1, HYENA_ORDER * N_DIRS * d)
    fspec = pl.BlockSpec(flo.shape, lambda o, j, t: (0, 0))
    rows = lambda half, width: pl.BlockSpec((None, st, h, width), lambda o, j, t: (half, t, 0, 0))
    wcol = lambda dirn, nrow: pl.BlockSpec((nrow, ct), lambda o, j, t: (0, (o * N_DIRS + dirn) * nct + j))
    return pl.pallas_call(
        _filter_s1_kernel,
        grid=(HYENA_ORDER, nct, n1 // st),
        in_specs=[fspec, fspec, rows(0, hid), rows(1, hid), rows(0, LANES), rows(1, LANES),
                  wcol(0, hid), wcol(1, hid), wcol(0, 1), wcol(1, 1)],
        out_specs=[pl.BlockSpec((None, 2, h, st, ct), lambda o, j, t: (o, 0, 0, t, j)),
                   pl.BlockSpec((None, 1, ct), lambda o, j, t: (o, 0, j))],
        out_shape=[jax.ShapeDtypeStruct((HYENA_ORDER, 2, h, n1, d), F32),
                   jax.ShapeDtypeStruct((HYENA_ORDER, 1, d), F32)],
        compiler_params=_params(("parallel", "parallel", "arbitrary")), name="filter_s1",
    )(flo, fhi, a2.reshape(2, n1, h, hid), a2.reshape(2, n1, h, hid), zf.reshape(2, n1, h, LANES),
      zf.reshape(2, n1, h, LANES), w3, w3, dec, dec)


def _tile_lanes(x, ct):
    return jnp.concatenate([x] * (ct // LANES), axis=-1) if ct > LANES else x


def _fft_s2f_kernel(a_ref, c_ref, s_ref, ff_ref, ss_ref, o_ref, *, kb, ct):
    n1 = a_ref.shape[2]
    scale = lax.rsqrt(ss_ref[...] + EPS)

    def body(kk, carry):
        ar, ai = a_ref[0, kk], a_ref[1, kk]
        c = _tile_lanes(c_ref[kk], ct)
        s = _tile_lanes(s_ref[kk], ct)
        br = ar * c + ai * s
        bi = ai * c - ar * s
        x = _dot3(ff_ref[...], jnp.concatenate([br, bi], axis=0))
        o_ref[kk] = _pack_c(x[:n1] * scale, x[n1:] * scale)
        return carry

    lax.fori_loop(0, kb, body, 0)


def _fft_s2_kernel(a_ref, k_ref, c_ref, s_ref, ff_ref, fi_ref, o_ref, *, kb, ct):
    n1 = a_ref.shape[1]

    def body(kk, carry):
        ar, ai = _unpack_c(a_ref[kk])
        c = _tile_lanes(c_ref[kk], ct)
        s = _tile_lanes(s_ref[kk], ct)
        br = ar * c + ai * s
        bi = ai * c - ar * s
        x = _dot(ff_ref[...], jnp.concatenate([br, bi], axis=0).astype(BF16))
        xr, xi = x[:n1], x[n1:]
        kr, ki = _unpack_c(k_ref[kk])
        zr = xr * kr - xi * ki
        zi = xr * ki + xi * kr
        y = _dot(fi_ref[...], jnp.concatenate([zr, zi], axis=0).astype(BF16))
        yr, yi = y[:n1], y[n1:]
        o_ref[kk] = _pack_c(yr * c - yi * s, yr * s + yi * c)
        return carry

    lax.fori_loop(0, kb, body, 0)


def _s2_tiles(n1, d):
    kb = max(1, 512 // n1)
    ct = min(d, 512)
    return kb, ct


def _fft_s2f(a, sumsq, consts):
    n_o, _, k2n, n1, d = a.shape
    kb, ct = _s2_tiles(n1, d)
    blk = pl.BlockSpec((None, 2, kb, n1, ct), lambda k, j, o: (o, 0, k, 0, j))
    tw = pl.BlockSpec((kb, n1, LANES), lambda k, j, o: (k, 0, 0))
    return pl.pallas_call(
        functools.partial(_fft_s2f_kernel, kb=kb, ct=ct),
        grid=(k2n // kb, d // ct, n_o),
        in_specs=[blk, tw, tw, pl.BlockSpec((2 * n1, 2 * n1), lambda k, j, o: (0, 0)),
                  pl.BlockSpec((None, 1, ct), lambda k, j, o: (o, 0, j))],
        out_specs=pl.BlockSpec((None, kb, n1, ct), lambda k, j, o: (o, k, 0, j)),
        out_shape=jax.ShapeDtypeStruct((n_o, k2n, n1, d), jnp.int32),
        compiler_params=_params(("parallel", "parallel", "parallel")),
        name="fft_s2_filter",
    )(a, consts["tw_c"], consts["tw_s"], consts["f_fwd_f32"], sumsq)


def _fft_s2(a, kf, order, consts):
    bsz, k2n, n1, d = a.shape
    kb, ct = _s2_tiles(n1, d)
    blk = pl.BlockSpec((None, kb, n1, ct), lambda k, j, b: (b, k, 0, j))
    tw = pl.BlockSpec((kb, n1, LANES), lambda k, j, b: (k, 0, 0))
    mat = pl.BlockSpec((2 * n1, 2 * n1), lambda k, j, b: (0, 0))
    return pl.pallas_call(
        functools.partial(_fft_s2_kernel, kb=kb, ct=ct),
        grid=(k2n // kb, d // ct, bsz),
        in_specs=[blk, pl.BlockSpec((None, kb, n1, ct), lambda k, j, b: (order, k, 0, j)),
                  tw, tw, mat, mat],
        out_specs=blk,
        out_shape=jax.ShapeDtypeStruct(a.shape, jnp.int32),
        compiler_params=_params(("parallel", "parallel", "parallel")),
        name="fft_s2",
    )(a, kf, consts["tw_c"], consts["tw_s"], consts["f_fwd"], consts["f_inv"])


def _fft_s3_kernel(f_ref, t_ref, u_ref, gm_ref, gp_ref, gn_ref, w_ref, b_ref, sk_ref, o_ref):
    t_id = pl.program_id(1)
    gates = _short_conv_slabs(gm_ref, gp_ref, gn_ref, w_ref, b_ref, t_id == 0, t_id == pl.num_programs(1) - 1)
    for s, gate in enumerate(gates):
        t = jnp.concatenate(_unpack_c(t_ref[:, s, :]), axis=0).astype(BF16)
        y = _dot(f_ref[...], t)
        o_ref[s] = gate * (y + u_ref[s] * sk_ref[...])


def _fft_s3(consts, t, u, z, gate_block, conv_w, conv_b, skip, order, ct=256):
    bsz, h, n1, d = t.shape
    st = _slab_tile(n1)
    cpb = d // ct
    f = consts["f_s3"]
    gcol = lambda j: gate_block * cpb + j
    slab = lambda idx: pl.BlockSpec((None, 1, h, ct), lambda b, tt, j: (b, idx(tt), 0, gcol(j)))
    return pl.pallas_call(
        _fft_s3_kernel,
        grid=(bsz, n1 // st, cpb),
        in_specs=[
            pl.BlockSpec(f.shape, lambda b, tt, j: (0, 0)),
            pl.BlockSpec((None, h, st, ct), lambda b, tt, j: (b, 0, tt, j)),
            pl.BlockSpec((None, st, h, ct), lambda b, tt, j: (b, tt, 0, j)),
            pl.BlockSpec((None, st, h, ct), lambda b, tt, j: (b, tt, 0, gcol(j))),
            slab(lambda tt: (tt * st + n1 - 1) % n1),
            slab(lambda tt: ((tt + 1) * st) % n1),
            pl.BlockSpec((3, ct), lambda b, tt, j: (0, gcol(j))),
            pl.BlockSpec((1, ct), lambda b, tt, j: (0, gcol(j))),
            pl.BlockSpec((None, 1, ct), lambda b, tt, j: (order, 0, j)),
        ],
        out_specs=pl.BlockSpec((None, st, h, ct), lambda b, tt, j: (b, tt, 0, j)),
        out_shape=jax.ShapeDtypeStruct((bsz, n1, h, d), F32),
        compiler_params=_params(("parallel", "parallel", "parallel")),
        name="fft_s3",
    )(f, t, u, z, z, z, conv_w, conv_b, skip)


def _hyena_filter_spectra(seq_len, consts, fw1, fb1, ffreq, fw2, fb2, fw3, decay):
    d = decay.shape[-1]
    n1 = consts["n1"]
    zf = _filter_positions(seq_len, n1)
    a2 = _filter_mlp(zf, fw1, fb1, ffreq, fw2, fb2)
    a, sumsq = _filter_s1(consts, a2, zf, fw3, decay)
    return _fft_s2f(a, sumsq, consts)


def _hyena_mixer(x, g, shift, scale, gate, p, layer, kf, consts):
    bsz, seq_len, d = x.shape
    n1 = consts["n1"]
    h = SLAB_ROWS
    st = _slab_tile(n1)
    xv = x.reshape(bsz, h, n1, d)
    z = _norm_mm(
        xv, pl.BlockSpec((None, h, st, d), lambda b, i, j: (b, 0, i, 0)), (h, st), st * h, n1 // st, g, shift, scale,
        [p["hy_w_in"]], layer, 0, 3 * d, jax.ShapeDtypeStruct((bsz, n1, h, 3 * d), F32),
        pl.BlockSpec((None, st, h, 512), lambda b, i, j: (b, i, 0, j)),
        mode="bias", bias=p["hy_b_in"][layer].reshape(1, 3 * d))
    cw, cb = p["hy_conv_w"][layer], p["hy_conv_b"][layer].reshape(1, 3 * d)
    skip = p["hy_skip"][layer].reshape(HYENA_ORDER, 1, d)
    a, u = _fft_s1(consts, z, 2, conv=(cw, cb))
    t = _fft_s2(a, kf, 0, consts)
    y1 = _fft_s3(consts, t, u, z, 0, cw, cb, skip, 0)
    a = _fft_s1(consts, y1, 0)
    t = _fft_s2(a, kf, 1, consts)
    y2 = _fft_s3(consts, t, y1, z, 1, cw, cb, skip, 1)
    q = ROW_TILE // n1
    return _mm_res(
        [y2], [pl.BlockSpec((None, n1, q, d), lambda b, i, j: (b, 0, i, 0))], p["hy_w_out"], layer,
        p["hy_b_out"][layer].reshape(1, d), x, gate, ROW_TILE, a_mode="perm", perm=(n1, q))


def _attn_kernel(q_ref, kp_ref, km_ref, kn_ref, vp_ref, vm_ref, vn_ref, o_ref, l_ref, kx_ref, vx_ref, *scr,
                 phases, ta, n_sub, to_classes):
    i = pl.program_id(2)
    halo = ATT_BAND // phases
    qa = ATT_Q // phases
    ka = 2 * qa
    kx_ref[:, 0:halo] = kp_ref[...]
    kx_ref[:, halo:halo + ta] = km_ref[...]
    kx_ref[:, halo + ta:] = kn_ref[...]
    vx_ref[:, 0:halo] = vp_ref[...]
    vx_ref[:, halo:halo + ta] = vm_ref[...]
    vx_ref[:, halo + ta:] = vn_ref[...]
    row = lax.broadcasted_iota(jnp.int32, (ATT_Q, 2 * ATT_Q), 0)
    col = lax.broadcasted_iota(jnp.int32, (ATT_Q, 2 * ATT_Q), 1)
    cq, aq = row >> (qa.bit_length() - 1), row & (qa - 1)
    ck, ak = col >> (ka.bit_length() - 1), col & (ka - 1)
    delta = phases * (ak - aq) - ATT_BAND + ck - cq
    band = (delta >= -ATT_BAND) & (delta <= ATT_BAND)
    for s in range(ta // qa):
        key_idx = phases * (i * ta + s * qa - halo + ak) + ck
        valid = band & (key_idx >= 0) & (key_idx < n_sub)
        for h in range(HEADS_PER_GROUP):
            cs = slice(h * HEAD_DIM, (h + 1) * HEAD_DIM)
            q = jnp.concatenate([q_ref[c, s * qa:(s + 1) * qa, cs] for c in range(phases)], axis=0)
            k = jnp.concatenate([kx_ref[c, s * qa:s * qa + ka, cs] for c in range(phases)], axis=0)
            v = jnp.concatenate([vx_ref[c, s * qa:s * qa + ka, cs] for c in range(phases)], axis=0)
            sc = lax.dot_general(q, k, (((1,), (1,)), ((), ())), preferred_element_type=F32)
            sc = jnp.where(valid, sc, NEG_BIG)
            m = jnp.max(sc, axis=-1, keepdims=True)
            pr = jnp.exp(sc - m)
            den = jnp.sum(pr, axis=-1, keepdims=True)
            o = _dot(pr.astype(BF16), v) / den
            lse = jnp.broadcast_to(m + jnp.log(den), (ATT_Q, HEAD_DIM))
            if to_classes:
                t_ref = scr[0]
                na = ATT_Q // CLASSES
                for val, dst in ((o, o_ref), (lse, l_ref)):
                    t_ref[...] = val.reshape(na, CLASSES, HEAD_DIM)
                    for rho in range(CLASSES):
                        dst[rho, s * na:(s + 1) * na, cs] = t_ref[:, rho, :]
            else:
                for c in range(phases):
                    o_ref[c, s * qa:(s + 1) * qa, cs] = o[c * qa:(c + 1) * qa]
                    l_ref[c, s * qa:(s + 1) * qa, cs] = lse[c * qa:(c + 1) * qa]


def _attn_group(qkv, col0, phases, n_sub, lead_grid, lead_block, lead_index, ta, *, to_classes=False,
                out_arr_shape=None, out_block=None, out_index=None):
    gw = GROUP_WIDTH
    halo = ATT_BAND // phases
    rows = qkv.shape[-2]
    per = ta // halo
    nblk = rows // halo
    cb = col0 // gw

    def spec(nrows, ridx, part):
        return pl.BlockSpec(tuple(lead_block) + (nrows, gw),
                            lambda b, rho, i: tuple(lead_index(b, rho)) + (ridx(i), cb + part))

    main = lambda part: spec(ta, lambda i: i, part)
    prev = lambda part: spec(halo, lambda i: jnp.maximum(i * per - 1, 0), part)
    nxt = lambda part: spec(halo, lambda i: jnp.minimum((i + 1) * per, nblk - 1), part)
    bsz = qkv.shape[0]
    if to_classes:
        out_spec = pl.BlockSpec(out_block, out_index)
        scratch_extra = [pltpu.VMEM((ATT_Q // CLASSES, CLASSES, HEAD_DIM), F32)]
    else:
        out_arr_shape = qkv.shape[:-1] + (gw,)
        out_spec = pl.BlockSpec(tuple(lead_block) + (ta, gw),
                                lambda b, rho, i: tuple(lead_index(b, rho)) + (i, 0))
        scratch_extra = []
    kx_shape = (phases, ta + 2 * halo, gw)
    return pl.pallas_call(
        functools.partial(_attn_kernel, phases=phases, ta=ta, n_sub=n_sub, to_classes=to_classes),
        grid=(bsz, lead_grid, rows // ta),
        in_specs=[main(0), prev(1), main(1), nxt(1), prev(2), main(2), nxt(2)],
        out_specs=[out_spec, out_spec],
        out_shape=[jax.ShapeDtypeStruct(out_arr_shape, F32)] * 2,
        scratch_shapes=[pltpu.VMEM(kx_shape, BF16)] * 2 + scratch_extra,
        compiler_params=_params(("parallel", "parallel", "parallel")),
        name="attn_p%d" % phases + ("_cls" if to_classes else ""),
    )(qkv, qkv, qkv, qkv, qkv, qkv, qkv)


def _rope_tables(pos):
    half = ROT_DIM // 2
    inv = ROPE_THETA ** (-jnp.arange(0, ROT_DIM, 2, dtype=F32) / ROT_DIM)
    ang = pos.astype(F32)[:, None] * inv[None, :]
    cos, sin = jnp.cos(ang), jnp.sin(ang)
    n = pos.shape[0]
    rest = HEAD_DIM - ROT_DIM
    c = jnp.concatenate([cos, cos, jnp.ones((n, rest), F32)], axis=1)
    s = jnp.concatenate([sin, sin, jnp.zeros((n, rest), F32)], axis=1)
    return c, s


def _rope_partner_matrix(width):
    half = ROT_DIM // 2
    m = np.zeros((width, width), np.float32)
    for base in range(0, width, HEAD_DIM):
        for k in range(half):
            m[base + k + half, base + k] = -1.0
            m[base + k, base + k + half] = 1.0
    return jnp.asarray(m, BF16)


def _attn_mixer(x, g, shift, scale, gate, p, layer):
    bsz, seq_len, d = x.shape
    gw = GROUP_WIDTH
    nc = seq_len // CLASSES
    ca = ROW_TILE // CLASSES
    w_in = p["at_w_in"]
    tabs = _rope_tables(jnp.arange(seq_len))
    qkv0 = _norm_mm(
        x, pl.BlockSpec((None, ROW_TILE, d), lambda b, i, j: (b, i, 0)), None, ROW_TILE, seq_len // ROW_TILE,
        g, shift, scale, [w_in], layer, 0, 3 * gw, jax.ShapeDtypeStruct((bsz, seq_len, 3 * gw), BF16),
        pl.BlockSpec((None, ROW_TILE, 512), lambda b, i, j: (b, i, j)), mode="rope",
        rope=(tabs, pl.BlockSpec((ROW_TILE, HEAD_DIM), lambda b, i, j: (i, 0))))
    pos_c = (jnp.arange(nc)[None, :] * CLASSES + jnp.arange(CLASSES)[:, None]).reshape(-1)
    tabs_c = [t.reshape(CLASSES, nc, HEAD_DIM) for t in _rope_tables(pos_c)]
    qkv12 = _norm_mm(
        x.reshape(bsz, nc, CLASSES, d), pl.BlockSpec((None, ca, CLASSES, d), lambda b, i, j: (b, i, 0, 0)),
        (ca, CLASSES), ROW_TILE, nc // ca, g, shift, scale, [w_in], layer, 3 * gw, 6 * gw,
        jax.ShapeDtypeStruct((bsz, CLASSES, nc, 6 * gw), BF16),
        pl.BlockSpec((None, CLASSES, ca, 512), lambda b, i, j: (b, 0, i, j)), mode="rope",
        rope=(tabs_c, pl.BlockSpec((CLASSES, ca, HEAD_DIM), lambda b, i, j: (0, i, 0))))
    cls_shape = (bsz, CLASSES, nc, gw)
    ta0 = min(256, seq_len)
    o0, l0 = _attn_group(
        qkv0.reshape(bsz, 1, seq_len, 3 * gw), 0, 1, seq_len, 1, (None, 1), lambda b, rho: (b, 0), ta0,
        to_classes=True, out_arr_shape=cls_shape, out_block=(None, CLASSES, ta0 // CLASSES, gw),
        out_index=lambda b, rho, i: (b, 0, i, 0))
    dil1 = ATTN_PATTERNS[1][1]
    ph = CLASSES // dil1
    ta1 = min(64, nc)
    o1, l1 = _attn_group(
        qkv12.reshape(bsz, ph, dil1, nc, 6 * gw), 0, ph, seq_len // dil1, dil1, (None, ph, None),
        lambda b, rho: (b, 0, rho), ta1)
    o1, l1 = o1.reshape(cls_shape), l1.reshape(cls_shape)
    ta2 = min(256, nc)
    o2, l2 = _attn_group(
        qkv12.reshape(bsz, CLASSES, 1, nc, 6 * gw), 3 * gw, 1, nc, CLASSES, (None, None, 1),
        lambda b, rho: (b, rho, 0), ta2)
    o2, l2 = o2.reshape(cls_shape), l2.reshape(cls_shape)
    cm = ca // 2
    cls_blk = pl.BlockSpec((None, CLASSES, cm, gw), lambda b, i, j: (b, 0, i, 0))
    return _mm_res([o0, o1, o2, l0, l1, l2], [cls_blk] * 6, p["at_w_out"], layer, jnp.zeros((1, d), F32),
                   x, gate, CLASSES * cm, a_mode="merge", perm=(CLASSES, cm))


def _ffn(x, g, shift, scale, gate, p, layer):
    bsz, seq_len, d = x.shape
    dff = p["ffn_w_gate"].shape[-1]
    tiles = seq_len // ROW_TILE
    xs = pl.BlockSpec((None, ROW_TILE, d), lambda b, i, j: (b, i, 0))
    hs = pl.BlockSpec((None, ROW_TILE, 512), lambda b, i, j: (b, i, j))
    hmid = _norm_mm(x, xs, None, ROW_TILE, tiles, g, shift, scale, [p["ffn_w_gate"], p["ffn_w_up"]], layer, 0,
                    dff, jax.ShapeDtypeStruct((bsz, seq_len, dff), BF16), hs, mode="swiglu")
    return _mm_res([hmid], [pl.BlockSpec((None, ROW_TILE, dff), lambda b, i, j: (b, i, 0))],
                   p["ffn_w_down"], layer, jnp.zeros((1, d), F32), x, gate, ROW_TILE, tn=256)


def _encoder(x, mods, final_mod, p):
    bsz, seq_len, d = x.shape
    n1 = 2 * seq_len // DFT_N2
    consts = _dft_consts(n1, DFT_N2)
    consts["d"] = d
    for i in range(DEPTH):
        sh_m, sc_m, g_m, sh_f, sc_f, g_f = [mods[i][:, None, k * d:(k + 1) * d] for k in range(6)]
        j = i // 2
        if i % 2 == 0:
            kf = _hyena_filter_spectra(seq_len, consts, p["hy_fw1"][j], p["hy_fb1"][j], p["hy_ffreq"][j],
                                       p["hy_fw2"][j], p["hy_fb2"][j], p["hy_fw3"][j], p["hy_decay"][j])
            x = _hyena_mixer(x, p["norm_mix"][i], sh_m, sc_m, g_m, p, j, kf, consts)
        else:
            x = _attn_mixer(x, p["norm_mix"][i], sh_m, sc_m, g_m, p, j)
        x = _ffn(x, p["norm_ffn"][i], sh_f, sc_f, g_f, p, i)
    sh, sc = final_mod[:, None, :d], final_mod[:, None, d:]
    return _final(x, p["final_norm"], sh, sc)


def kernel(x_prompt, x_sample, c_prompt, c_sample, ada_w, ada_b, norm_mix, norm_ffn, hy_w_in, hy_b_in, hy_conv_w, hy_conv_b, hy_fw1, hy_fb1, hy_ffreq, hy_fw2, hy_fb2, hy_fw3, hy_decay, hy_skip, hy_w_out, hy_b_out, at_w_in, at_w_out, ffn_w_gate, ffn_w_up, ffn_w_down, final_norm, final_ada_w, final_ada_b):
    d = x_prompt.shape[-1]
    bp, bs = c_prompt.shape[0], c_sample.shape[0]
    pad = -(bp + bs) % (2 * SUBLANES)
    c_all = jnp.concatenate([c_prompt, c_sample, jnp.zeros((pad, d), F32)], axis=0)
    mods = _ada(c_all, ada_w, ada_b)
    fmod = _ada(c_all, final_ada_w[None], final_ada_b[None])[0]
    p = dict(norm_mix=norm_mix, norm_ffn=norm_ffn,
             hy_w_in=hy_w_in.astype(BF16), hy_b_in=hy_b_in, hy_conv_w=hy_conv_w, hy_conv_b=hy_conv_b,
             hy_fw1=hy_fw1, hy_fb1=hy_fb1, hy_ffreq=hy_ffreq, hy_fw2=hy_fw2, hy_fb2=hy_fb2, hy_fw3=hy_fw3,
             hy_decay=hy_decay, hy_skip=hy_skip, hy_w_out=hy_w_out.astype(BF16), hy_b_out=hy_b_out,
             at_w_in=at_w_in.astype(BF16), at_w_out=at_w_out.astype(BF16),
             ffn_w_gate=ffn_w_gate.astype(BF16), ffn_w_up=ffn_w_up.astype(BF16),
             ffn_w_down=ffn_w_down.astype(BF16), final_norm=final_norm)
    y_prompt = _encoder(x_prompt, mods[:, :bp], fmod[:bp], p)
    y_sample = _encoder(x_sample, mods[:, bp:bp + bs], fmod[bp:bp + bs], p)
    return (y_prompt, y_sample)
```

```python
import functools
import math

import numpy as np
import jax
import jax.numpy as jnp
from jax import lax
from jax.experimental import pallas as pl
from jax.experimental.pallas import tpu as pltpu

F32 = jnp.float32
BF16 = jnp.bfloat16
EPS = 1e-6

DEPTH = 4
HYENA_ORDER = 2
N_DIRS = 2
FILTER_BANDS = 16
FILTER_EMB = 1 + 2 * FILTER_BANDS
ATTN_PATTERNS = ((128, 1), (512, 4), (2048, 16))
HEADS_PER_GROUP = 8
HEAD_DIM = 128
GROUP_WIDTH = HEADS_PER_GROUP * HEAD_DIM
ROT_DIM = HEAD_DIM // 4
ROPE_THETA = 500000.0

LANES = 128
SUBLANES = 8
VMEM_LIMIT_BYTES = 56 * 1024 * 1024

DFT_N2 = 256
SLAB_ROWS = DFT_N2 // 2
ATT_BAND = 64
ATT_Q = 2 * ATT_BAND
CLASSES = 16
ROW_TILE = 1024
NEG_BIG = -1e30


def _params(sem):
    return pltpu.CompilerParams(dimension_semantics=sem, vmem_limit_bytes=VMEM_LIMIT_BYTES)


def _dot(a, b):
    return jnp.dot(a, b, preferred_element_type=F32)


def _split(a):
    hi = a.astype(BF16)
    lo = (a - hi.astype(F32)).astype(BF16)
    return hi, lo


def _dot3(a, b):
    ah, al = _split(a)
    bh, bl = _split(b)
    return _dot(ah, bh) + _dot(al, bh) + _dot(ah, bl)


def _modnorm(x, g, shift, scale):
    ms = jnp.mean(x * x, axis=-1, keepdims=True)
    y = x * lax.rsqrt(ms + EPS) * g
    return y * (1.0 + scale) + shift


PERM_ROWS = 256


def _perm_matrix(p, q):
    m = np.zeros((PERM_ROWS, PERM_ROWS), np.float32)
    pi, qi = np.meshgrid(np.arange(p), np.arange(q), indexing="ij")
    m[(qi * p + pi).ravel(), (pi * q + qi).ravel()] = 1.0
    return jnp.asarray(m, BF16)


def _perm_for(p, q):
    assert (q <= 16 and p % (PERM_ROWS // q) == 0) or (p <= 16 and q % (PERM_ROWS // p) == 0), (p, q)
    return _perm_matrix(PERM_ROWS // q, q) if q <= 16 else _perm_matrix(p, PERM_ROWS // p)


def _transpose_rows(dst_ref, src_ref, pm_ref, p, q):
    if q <= 16:
        pg = PERM_ROWS // q
        for grp in range(p // pg):
            t = _dot(pm_ref[...], src_ref[grp * PERM_ROWS:(grp + 1) * PERM_ROWS, :]).astype(BF16)
            for qi in range(q):
                dst_ref[qi * p + grp * pg:qi * p + (grp + 1) * pg, :] = t[qi * pg:(qi + 1) * pg]
    else:
        qg = PERM_ROWS // p
        for grp in range(q // qg):
            blk = jnp.concatenate([src_ref[pi * q + grp * qg:pi * q + (grp + 1) * qg, :] for pi in range(p)],
                                  axis=0)
            dst_ref[grp * PERM_ROWS:(grp + 1) * PERM_ROWS, :] = _dot(pm_ref[...], blk).astype(BF16)


def _ada_kernel(c_ref, w_ref, b_ref, o_ref):
    c = c_ref[...]
    cs = c * jax.nn.sigmoid(c)
    o_ref[...] = _dot3(cs, w_ref[...]) + b_ref[...]


def _ada(c_all, w, b, tn=1024):
    nl, d, no = w.shape
    r = c_all.shape[0]
    return pl.pallas_call(
        _ada_kernel,
        grid=(nl, no // tn),
        in_specs=[
            pl.BlockSpec((r, d), lambda l, j: (0, 0)),
            pl.BlockSpec((None, d, tn), lambda l, j: (l, 0, j)),
            pl.BlockSpec((None, 1, tn), lambda l, j: (l, 0, j)),
        ],
        out_specs=pl.BlockSpec((None, r, tn), lambda l, j: (l, 0, j)),
        out_shape=jax.ShapeDtypeStruct((nl, r, no), F32),
        compiler_params=_params(("parallel", "parallel")),
        name="ada_mod",
    )(c_all, w, b.reshape(nl, 1, no))


def _norm_mm_kernel(*refs, mode, tn, perm):
    if perm:
        pm_ref, refs = refs[0], refs[1:]
        h0_ref, refs = refs[-1], refs[:-1]
    if mode == "swiglu":
        x_ref, g_ref, sh_ref, sc_ref, wg_ref, wu_ref, o_ref, h_ref = refs
    elif mode == "rope":
        x_ref, g_ref, sh_ref, sc_ref, w_ref, c_ref, s_ref, rot_ref, o_ref, h_ref = refs
    else:
        x_ref, g_ref, sh_ref, sc_ref, w_ref, b_ref, o_ref, h_ref = refs
    j = pl.program_id(2)

    @pl.when(j == 0)
    def _():
        h = _modnorm(x_ref[...].reshape(h_ref.shape), g_ref[...], sh_ref[...], sc_ref[...]).astype(BF16)
        if perm:
            h0_ref[...] = h
            _transpose_rows(h_ref, h0_ref, pm_ref, *perm)
        else:
            h_ref[...] = h

    h = h_ref[...]
    if mode == "swiglu":
        a = _dot(h, wg_ref[...])
        u = _dot(h, wu_ref[...])
        o_ref[...] = (a * jax.nn.sigmoid(a) * u).astype(o_ref.dtype).reshape(o_ref.shape)
    elif mode == "rope":
        acc = _dot(h, w_ref[...])
        part = (j // (GROUP_WIDTH // tn)) % 3

        @pl.when(part == 2)
        def _():
            o_ref[...] = acc.astype(o_ref.dtype).reshape(o_ref.shape)

        @pl.when(part != 2)
        def _():
            reps = tn // HEAD_DIM
            tabs = [t[...].reshape(acc.shape[0], HEAD_DIM) for t in (c_ref, s_ref)]
            c, s = [jnp.concatenate([t] * reps, axis=1) for t in tabs]
            partner = _dot(acc.astype(BF16), rot_ref[...])
            qs = jnp.where(part == 0, HEAD_DIM ** -0.5, 1.0).astype(F32)
            o_ref[...] = ((acc * c + partner * s) * qs).astype(o_ref.dtype).reshape(o_ref.shape)
    else:
        o_ref[...] = (_dot(h, w_ref[...]) + b_ref[...]).astype(o_ref.dtype).reshape(o_ref.shape)


def _norm_mm(x, x_spec, perm, rows, grid_rows, g, shift, scale, ws, w_layer, col0, nout, out_shape,
             out_spec, *, mode, bias=None, rope=None, tn=512):
    d = x.shape[-1]
    bsz = x.shape[0]
    cb = col0 // tn
    vec = pl.BlockSpec((None, 1, d), lambda b, i, j: (b, 0, 0))
    in_specs = [x_spec, pl.BlockSpec((1, d), lambda b, i, j: (0, 0)), vec, vec]
    in_specs += [pl.BlockSpec((None, d, tn), lambda b, i, j: (w_layer, 0, cb + j)) for _ in ws]
    args = [x, g.reshape(1, d), shift, scale, *ws]
    scratch = [pltpu.VMEM((rows, d), BF16)]
    if perm:
        in_specs.insert(0, pl.BlockSpec((PERM_ROWS, PERM_ROWS), lambda b, i, j: (0, 0)))
        args.insert(0, _perm_for(*perm))
        scratch.append(pltpu.VMEM((rows, d), BF16))
    if mode == "rope":
        tabs, tab_spec = rope
        in_specs += [tab_spec] * 2 + [pl.BlockSpec((tn, tn), lambda b, i, j: (0, 0))]
        args += list(tabs) + [_rope_partner_matrix(tn)]
    elif mode == "bias":
        in_specs.append(pl.BlockSpec((1, tn), lambda b, i, j: (0, cb + j)))
        args.append(bias)
    return pl.pallas_call(
        functools.partial(_norm_mm_kernel, mode=mode, tn=tn, perm=perm),
        grid=(bsz, grid_rows, nout // tn),
        in_specs=in_specs,
        out_specs=out_spec,
        out_shape=out_shape,
        scratch_shapes=scratch,
        compiler_params=_params(("parallel", "parallel", "arbitrary")),
        name="norm_mm_" + mode,
    )(*args)


def _mm_res_kernel(*refs, a_mode, perm):
    if a_mode == "merge":
        (pm_ref, o0, o1, o2, l0, l1, l2, w_ref, b_ref, x_ref, gt_ref, out_ref, a_ref, a0_ref) = refs
    elif a_mode == "perm":
        pm_ref, a_in, w_ref, b_ref, x_ref, gt_ref, out_ref, a_ref, a0_ref = refs
    else:
        a_in, w_ref, b_ref, x_ref, gt_ref, out_ref = refs
    j = pl.program_id(2)

    if a_mode != "plain":
        @pl.when(j == 0)
        def _():
            k = a_ref.shape[1]
            if a_mode == "merge":
                ls = [l[...].reshape(-1, k) for l in (l0, l1, l2)]
                os_ = [o[...].reshape(-1, k) for o in (o0, o1, o2)]
                mx = jnp.maximum(jnp.maximum(ls[0], ls[1]), ls[2])
                ws = [jnp.exp(l - mx) for l in ls]
                num = ws[0] * os_[0] + ws[1] * os_[1] + ws[2] * os_[2]
                a0_ref[...] = (num / (ws[0] + ws[1] + ws[2])).astype(BF16)
            else:
                a0_ref[...] = a_in[...].reshape(-1, k).astype(BF16)
            _transpose_rows(a_ref, a0_ref, pm_ref, *perm)

        a = a_ref[...]
    else:
        a = a_in[...]
    out_ref[...] = x_ref[...] + gt_ref[...] * (_dot(a, w_ref[...]) + b_ref[...])


def _mm_res(a_list, a_specs, w, w_layer, bias, x, gate, rows, *, a_mode="plain", perm=None, tn=512):
    bsz, seq_len, d = x.shape
    k = w.shape[1]
    blk = pl.BlockSpec((None, rows, tn), lambda b, i, j: (b, i, j))
    in_specs = list(a_specs) + [
        pl.BlockSpec((None, k, tn), lambda b, i, j: (w_layer, 0, j)),
        pl.BlockSpec((1, tn), lambda b, i, j: (0, j)),
        blk,
        pl.BlockSpec((None, 1, tn), lambda b, i, j: (b, 0, j)),
    ]
    args = [*a_list, w, bias, x, gate]
    scratch = []
    if a_mode != "plain":
        in_specs.insert(0, pl.BlockSpec((PERM_ROWS, PERM_ROWS), lambda b, i, j: (0, 0)))
        args.insert(0, _perm_for(*perm))
        scratch = [pltpu.VMEM((rows, k), BF16)] * 2
    return pl.pallas_call(
        functools.partial(_mm_res_kernel, a_mode=a_mode, perm=perm),
        grid=(bsz, seq_len // rows, d // tn),
        in_specs=in_specs,
        out_specs=blk,
        out_shape=jax.ShapeDtypeStruct(x.shape, F32),
        scratch_shapes=scratch,
        compiler_params=_params(("parallel", "parallel", "arbitrary")),
        name="mm_res_" + a_mode,
    )(*args)


def _final_kernel(x_ref, g_ref, sh_ref, sc_ref, o_ref):
    o_ref[...] = _modnorm(x_ref[...], g_ref[...], sh_ref[...], sc_ref[...])


def _final(x, g, shift, scale, tm=512):
    bsz, seq_len, d = x.shape
    vec = pl.BlockSpec((None, 1, d), lambda b, i: (b, 0, 0))
    blk = pl.BlockSpec((None, tm, d), lambda b, i: (b, i, 0))
    return pl.pallas_call(
        _final_kernel,
        grid=(bsz, seq_len // tm),
        in_specs=[blk, pl.BlockSpec((1, d), lambda b, i: (0, 0)), vec, vec],
        out_specs=blk,
        out_shape=jax.ShapeDtypeStruct(x.shape, F32),
        compiler_params=_params(("parallel", "parallel")),
        name="final_norm",
    )(x, g.reshape(1, d), shift, scale)


def _filter_positions(seq_len, n1):
    n = 2 * seq_len
    h = SLAB_ROWS
    half = jnp.arange(2)[:, None, None]
    s = jnp.arange(n1)[None, :, None]
    r = jnp.arange(h)[None, None, :]
    idx = ((half * h + r) * n1 + s).reshape(n)
    pos = jnp.where(idx < seq_len, idx, n - idx).astype(F32)
    sign = jnp.where(idx < seq_len, 1.0, jnp.where(idx == seq_len, 0.0, -1.0)).astype(F32)
    t = pos / max(seq_len - 1, 1)
    bands = jnp.linspace(1e-4, FILTER_BANDS - 1, FILTER_BANDS, dtype=F32)
    ang = 2.0 * math.pi * pos[:, None] * bands[None, :] / seq_len
    z = jnp.concatenate([t[:, None], jnp.cos(ang), -jnp.sin(ang)], axis=-1)
    z = jnp.pad(z, ((0, 0), (0, LANES - FILTER_EMB - 1)))
    return jnp.concatenate([z, sign[:, None]], axis=-1)


def _filter_mlp_kernel(z_ref, w1_ref, b1_ref, f_ref, w2_ref, b2_ref, o_ref):
    f = f_ref[...]
    a = jnp.sin(f[0:1, :] * (_dot3(z_ref[...], w1_ref[...]) + b1_ref[...]))
    o_ref[...] = jnp.sin(f[1:2, :] * (_dot3(a, w2_ref[...]) + b2_ref[...]))


def _filter_mlp(zf, w1, b1, freq, w2, b2, tr=512):
    n = zf.shape[0]
    hid = w1.shape[1]
    w1p = jnp.pad(w1, ((0, LANES - w1.shape[0]), (0, 0)))
    full = lambda shape: pl.BlockSpec(shape, lambda i: (0,) * len(shape))
    return pl.pallas_call(
        _filter_mlp_kernel,
        grid=(n // tr,),
        in_specs=[pl.BlockSpec((tr, LANES), lambda i: (i, 0)), full((LANES, hid)), full((1, hid)),
                  full((2, hid)), full((hid, hid)), full((1, hid))],
        out_specs=pl.BlockSpec((tr, hid), lambda i: (i, 0)),
        out_shape=jax.ShapeDtypeStruct((n, hid), F32),
        compiler_params=_params(("parallel",)),
        name="filter_mlp",
    )(zf, w1p, b1.reshape(1, hid), freq, w2, b2.reshape(1, hid))


def _dft_consts(n1, n2):
    n = n1 * n2
    k2 = np.arange(n2 // 2)[:, None]
    nn2 = np.arange(n2)[None, :]
    ph = 2.0 * np.pi * nn2 * (k2 + 0.5) / n2
    f_s1 = np.concatenate([np.cos(ph), -np.sin(ph)], axis=0)
    m = np.arange(n2 // 2)[:, None]
    kk = np.arange(n2 // 2)[None, :]
    ph3 = 2.0 * np.pi * m * (kk + 0.5) / n2
    f_s3 = (2.0 / n) * np.concatenate([np.cos(ph3), -np.sin(ph3)], axis=1)
    a = 2.0 * np.pi * np.outer(np.arange(n1), np.arange(n1)) / n1
    c, s = np.cos(a), -np.sin(a)
    f_fwd = np.block([[c, -s], [s, c]])
    f_inv = np.block([[c, s], [-s, c]])
    th = 2.0 * np.pi * (np.arange(n2 // 2)[:, None] + 0.5) * np.arange(n1)[None, :] / n
    as_bf = lambda x: jnp.asarray(x, F32).astype(BF16)
    tw_c = jnp.broadcast_to(jnp.asarray(np.cos(th), F32)[:, :, None], (n2 // 2, n1, LANES))
    tw_s = jnp.broadcast_to(jnp.asarray(np.sin(th), F32)[:, :, None], (n2 // 2, n1, LANES))
    return dict(f_s1=as_bf(f_s1[:, :n2 // 2]), f_s3=as_bf(f_s3), f_fwd=as_bf(f_fwd), f_inv=as_bf(f_inv),
                f_s1_f32=jnp.asarray(f_s1, F32), f_fwd_f32=jnp.asarray(f_fwd, F32),
                tw_c=tw_c, tw_s=tw_s, n1=n1)


def _slab_tile(n1):
    return min(n1, 8)


HI16 = -65536
HALF_ULP16 = 0x8000


def _pack_c(re, im):
    rb = lax.bitcast_convert_type(re, jnp.int32) + HALF_ULP16
    ib = lax.bitcast_convert_type(im, jnp.int32) + HALF_ULP16
    return (rb & HI16) | lax.shift_right_logical(ib, 16)


def _pack_bf16(re, im):
    rb = lax.bitcast_convert_type(re.astype(F32), jnp.int32)
    ib = lax.bitcast_convert_type(im.astype(F32), jnp.int32)
    return rb | lax.shift_right_logical(ib, 16)


def _unpack_c(p):
    re = lax.bitcast_convert_type(p & HI16, F32)
    im = lax.bitcast_convert_type(lax.shift_left(p, 16), F32)
    return re, im


def _shift_rows(x, down):
    rows = x.shape[0]
    row = lax.broadcasted_iota(jnp.int32, x.shape, 0)
    if down:
        return jnp.where(row == 0, 0.0, pltpu.roll(x, 1, 0))
    return jnp.where(row == rows - 1, 0.0, pltpu.roll(x, rows - 1, 0))


def _short_conv_slabs(main_ref, prev_ref, next_ref, w_ref, b_ref, first, last):
    n_slabs = main_ref.shape[0]
    prev = prev_ref[0]
    prev = jnp.where(first, _shift_rows(prev, True), prev)
    nxt = next_ref[0]
    nxt = jnp.where(last, _shift_rows(nxt, False), nxt)
    w = w_ref[...]
    out = []
    for s in range(n_slabs):
        up = prev if s == 0 else main_ref[s - 1]
        dn = nxt if s == n_slabs - 1 else main_ref[s + 1]
        out.append(up * w[0:1, :] + main_ref[s] * w[1:2, :] + dn * w[2:3, :] + b_ref[...])
    return out


def _fft_s1_kernel(*refs, short_conv):
    re_ref, im_ref, ret_ref, imt_ref = refs[-4:]
    if short_conv:
        pm_ref, f_ref, m_ref, p_ref, n_ref, w_ref, b_ref, o_ref, u_ref = refs[:-4]
        t = pl.program_id(1)
        slabs = _short_conv_slabs(m_ref, p_ref, n_ref, w_ref, b_ref, t == 0, t == pl.num_programs(1) - 1)
    else:
        pm_ref, f_ref, m_ref, o_ref = refs[:-4]
        slabs = [m_ref[s] for s in range(m_ref.shape[0])]
    half = o_ref.shape[0]
    for s, u in enumerate(slabs):
        if short_conv:
            u_ref[s] = u
        r = _dot(f_ref[...], u.astype(BF16))
        re_ref[s * half:(s + 1) * half, :] = r[:half].astype(BF16)
        im_ref[s * half:(s + 1) * half, :] = r[half:].astype(BF16)
    _transpose_rows(ret_ref, re_ref, pm_ref, len(slabs), half)
    _transpose_rows(imt_ref, im_ref, pm_ref, len(slabs), half)
    o_ref[...] = _pack_bf16(ret_ref[...], imt_ref[...]).reshape(o_ref.shape)


def _fft_s1(consts, src, col_block, conv=None, ct=256):
    bsz, n1, h, c = src.shape
    d = consts["d"]
    st = _slab_tile(n1)
    cpb = d // ct
    f = consts["f_s1"]
    main = pl.BlockSpec((None, st, h, ct), lambda b, t, j: (b, t, 0, col_block * cpb + j))
    a_spec = pl.BlockSpec((None, h, st, ct), lambda b, t, j: (b, 0, t, j))
    a_shape = jax.ShapeDtypeStruct((bsz, h, n1, d), jnp.int32)
    fspec = pl.BlockSpec(f.shape, lambda b, t, j: (0, 0))
    pspec = pl.BlockSpec((PERM_ROWS, PERM_ROWS), lambda b, t, j: (0, 0))
    pm = _perm_for(st, h)
    scratch = [pltpu.VMEM((st * h, ct), BF16)] * 4
    if conv is None:
        return pl.pallas_call(
            functools.partial(_fft_s1_kernel, short_conv=False),
            grid=(bsz, n1 // st, cpb),
            in_specs=[pspec, fspec, main], out_specs=a_spec, out_shape=a_shape, scratch_shapes=scratch,
            compiler_params=_params(("parallel", "parallel", "parallel")), name="fft_s1",
        )(pm, f, src)
    w, b = conv
    prev = pl.BlockSpec((None, 1, h, ct), lambda b, t, j: (b, (t * st + n1 - 1) % n1, 0, col_block * cpb + j))
    nxt = pl.BlockSpec((None, 1, h, ct), lambda b, t, j: (b, ((t + 1) * st) % n1, 0, col_block * cpb + j))
    wspec = pl.BlockSpec((3, ct), lambda b, t, j: (0, col_block * cpb + j))
    bspec = pl.BlockSpec((1, ct), lambda b, t, j: (0, col_block * cpb + j))
    u_spec = pl.BlockSpec((None, st, h, ct), lambda b, t, j: (b, t, 0, j))
    return pl.pallas_call(
        functools.partial(_fft_s1_kernel, short_conv=True),
        grid=(bsz, n1 // st, cpb),
        in_specs=[pspec, fspec, main, prev, nxt, wspec, bspec],
        out_specs=[a_spec, u_spec],
        out_shape=[a_shape, jax.ShapeDtypeStruct((bsz, n1, h, d), F32)],
        scratch_shapes=scratch,
        compiler_params=_params(("parallel", "parallel", "parallel")), name="fft_s1_conv",
    )(pm, f, src, src, src, w, b)


def _filter_s1_kernel(flo_ref, fhi_ref, alo_ref, ahi_ref, zlo_ref, zhi_ref, wf_ref, wb_ref, df_ref, db_ref,
                      o_ref, ss_ref):
    half = o_ref.shape[1]

    @pl.when(pl.program_id(2) == 0)
    def _():
        ss_ref[...] = jnp.zeros_like(ss_ref)

    ss = jnp.zeros(ss_ref.shape, F32)
    for s in range(alo_ref.shape[0]):
        taps = []
        for a_ref, z_ref, w_ref, d_ref in ((alo_ref, zlo_ref, wf_ref, df_ref), (ahi_ref, zhi_ref, wb_ref, db_ref)):
            z = z_ref[s]
            t, sign = z[:, 0:1], z[:, LANES - 1:LANES]
            h = _dot3(a_ref[s], w_ref[...]) * jnp.exp(-t * jnp.abs(d_ref[...])) * sign
            ss = ss + jnp.sum(h * h, axis=0, keepdims=True)
            taps.append(h)
        r = _dot3(flo_ref[...], taps[0]) + _dot3(fhi_ref[...], taps[1])
        o_ref[0, :, s, :] = r[:half]
        o_ref[1, :, s, :] = r[half:]
    ss_ref[...] += ss


def _filter_s1(consts, a2, zf, w3, decay, ct=256):
    n1, h = consts["n1"], SLAB_ROWS
    hid = a2.shape[1]
    d = decay.shape[-1]
    nct = d // ct
    st = _slab_tile(n1)
    f = consts["f_s1_f32"]
    flo, fhi = f[:, :h], f[:, h:]
    dec = decay.reshape(1, HYENA_ORDER * N_DIRS * d)
    fspec = pl.BlockSpec(flo.shape, lambda o, j, t: (0, 0))
    rows = lambda half, width: pl.BlockSpec((None, st, h, width), lambda o, j, t: (half, t, 0, 0))
    wcol = lambda dirn, nrow: pl.BlockSpec((nrow, ct), lambda o, j, t: (0, (o * N_DIRS + dirn) * nct + j))
    return pl.pallas_call(
        _filter_s1_kernel,
        grid=(HYENA_ORDER, nct, n1 // st),
        in_specs=[fspec, fspec, rows(0, hid), rows(1, hid), rows(0, LANES), rows(1, LANES),
                  wcol(0, hid), wcol(1, hid), wcol(0, 1), wcol(1, 1)],
        out_specs=[pl.BlockSpec((None, 2, h, st, ct), lambda o, j, t: (o, 0, 0, t, j)),
                   pl.BlockSpec((None, 1, ct), lambda o, j, t: (o, 0, j))],
        out_shape=[jax.ShapeDtypeStruct((HYENA_ORDER, 2, h, n1, d), F32),
                   jax.ShapeDtypeStruct((HYENA_ORDER, 1, d), F32)],
        compiler_params=_params(("parallel", "parallel", "arbitrary")), name="filter_s1",
    )(flo, fhi, a2.reshape(2, n1, h, hid), a2.reshape(2, n1, h, hid), zf.reshape(2, n1, h, LANES),
      zf.reshape(2, n1, h, LANES), w3, w3, dec, dec)


def _tile_lanes(x, ct):
    return jnp.concatenate([x] * (ct // LANES), axis=-1) if ct > LANES else x


def _fft_s2f_kernel(a_ref, c_ref, s_ref, ff_ref, ss_ref, o_ref, *, kb, ct):
    n1 = a_ref.shape[2]
    scale = lax.rsqrt(ss_ref[...] + EPS)

    def body(kk, carry):
        ar, ai = a_ref[0, kk], a_ref[1, kk]
        c = _tile_lanes(c_ref[kk], ct)
        s = _tile_lanes(s_ref[kk], ct)
        br = ar * c + ai * s
        bi = ai * c - ar * s
        x = _dot3(ff_ref[...], jnp.concatenate([br, bi], axis=0))
        o_ref[kk] = _pack_c(x[:n1] * scale, x[n1:] * scale)
        return carry

    lax.fori_loop(0, kb, body, 0)


def _fft_s2_kernel(a_ref, k_ref, c_ref, s_ref, ff_ref, fi_ref, o_ref, *, kb, ct):
    n1 = a_ref.shape[1]

    def body(kk, carry):
        ar, ai = _unpack_c(a_ref[kk])
        c = _tile_lanes(c_ref[kk], ct)
        s = _tile_lanes(s_ref[kk], ct)
        br = ar * c + ai * s
        bi = ai * c - ar * s
        x = _dot(ff_ref[...], jnp.concatenate([br, bi], axis=0).astype(BF16))
        xr, xi = x[:n1], x[n1:]
        kr, ki = _unpack_c(k_ref[kk])
        zr = xr * kr - xi * ki
        zi = xr * ki + xi * kr
        y = _dot(fi_ref[...], jnp.concatenate([zr, zi], axis=0).astype(BF16))
        yr, yi = y[:n1], y[n1:]
        o_ref[kk] = _pack_c(yr * c - yi * s, yr * s + yi * c)
        return carry

    lax.fori_loop(0, kb, body, 0, unroll=min(kb, 4))


def _s2_tiles(n1, d):
    kb = max(1, 512 // n1)
    ct = min(d, 512)
    return kb, ct


def _fft_s2f(a, sumsq, consts):
    n_o, _, k2n, n1, d = a.shape
    kb, ct = _s2_tiles(n1, d)
    blk = pl.BlockSpec((None, 2, kb, n1, ct), lambda k, j, o: (o, 0, k, 0, j))
    tw = pl.BlockSpec((kb, n1, LANES), lambda k, j, o: (k, 0, 0))
    return pl.pallas_call(
        functools.partial(_fft_s2f_kernel, kb=kb, ct=ct),
        grid=(k2n // kb, d // ct, n_o),
        in_specs=[blk, tw, tw, pl.BlockSpec((2 * n1, 2 * n1), lambda k, j, o: (0, 0)),
                  pl.BlockSpec((None, 1, ct), lambda k, j, o: (o, 0, j))],
        out_specs=pl.BlockSpec((None, kb, n1, ct), lambda k, j, o: (o, k, 0, j)),
        out_shape=jax.ShapeDtypeStruct((n_o, k2n, n1, d), jnp.int32),
        compiler_params=_params(("parallel", "parallel", "parallel")),
        name="fft_s2_filter",
    )(a, consts["tw_c"], consts["tw_s"], consts["f_fwd_f32"], sumsq)


def _fft_s2(a, kf, order, consts):
    bsz, k2n, n1, d = a.shape
    kb, ct = _s2_tiles(n1, d)
    blk = pl.BlockSpec((None, kb, n1, ct), lambda k, j, b: (b, k, 0, j))
    tw = pl.BlockSpec((kb, n1, LANES), lambda k, j, b: (k, 0, 0))
    mat = pl.BlockSpec((2 * n1, 2 * n1), lambda k, j, b: (0, 0))
    return pl.pallas_call(
        functools.partial(_fft_s2_kernel, kb=kb, ct=ct),
        grid=(k2n // kb, d // ct, bsz),
        in_specs=[blk, pl.BlockSpec((None, kb, n1, ct), lambda k, j, b: (order, k, 0, j)),
                  tw, tw, mat, mat],
        out_specs=blk,
        out_shape=jax.ShapeDtypeStruct(a.shape, jnp.int32),
        compiler_params=_params(("parallel", "parallel", "parallel")),
        name="fft_s2",
    )(a, kf, consts["tw_c"], consts["tw_s"], consts["f_fwd"], consts["f_inv"])


def _fft_s3_kernel(pm_ref, f_ref, t_ref, u_ref, gm_ref, gp_ref, gn_ref, w_ref, b_ref, sk_ref, o_ref,
                   re_ref, im_ref, ret_ref, imt_ref):
    t_id = pl.program_id(1)
    half, st = t_ref.shape[0], t_ref.shape[1]
    re, im = _unpack_c(t_ref[...].reshape(half * st, t_ref.shape[2]))
    re_ref[...] = re.astype(BF16)
    im_ref[...] = im.astype(BF16)
    _transpose_rows(ret_ref, re_ref, pm_ref, half, st)
    _transpose_rows(imt_ref, im_ref, pm_ref, half, st)
    gates = _short_conv_slabs(gm_ref, gp_ref, gn_ref, w_ref, b_ref, t_id == 0, t_id == pl.num_programs(1) - 1)
    for s, gate in enumerate(gates):
        t = jnp.concatenate([ret_ref[s * half:(s + 1) * half, :], imt_ref[s * half:(s + 1) * half, :]], axis=0)
        y = _dot(f_ref[...], t)
        o_ref[s] = gate * (y + u_ref[s] * sk_ref[...])


def _fft_s3(consts, t, u, z, gate_block, conv_w, conv_b, skip, order, ct=256):
    bsz, h, n1, d = t.shape
    st = _slab_tile(n1)
    cpb = d // ct
    f = consts["f_s3"]
    gcol = lambda j: gate_block * cpb + j
    slab = lambda idx: pl.BlockSpec((None, 1, h, ct), lambda b, tt, j: (b, idx(tt), 0, gcol(j)))
    return pl.pallas_call(
        _fft_s3_kernel,
        grid=(bsz, n1 // st, cpb),
        scratch_shapes=[pltpu.VMEM((st * h, ct), BF16)] * 4,
        in_specs=[
            pl.BlockSpec((PERM_ROWS, PERM_ROWS), lambda b, tt, j: (0, 0)),
            pl.BlockSpec(f.shape, lambda b, tt, j: (0, 0)),
            pl.BlockSpec((None, h, st, ct), lambda b, tt, j: (b, 0, tt, j)),
            pl.BlockSpec((None, st, h, ct), lambda b, tt, j: (b, tt, 0, j)),
            pl.BlockSpec((None, st, h, ct), lambda b, tt, j: (b, tt, 0, gcol(j))),
            slab(lambda tt: (tt * st + n1 - 1) % n1),
            slab(lambda tt: ((tt + 1) * st) % n1),
            pl.BlockSpec((3, ct), lambda b, tt, j: (0, gcol(j))),
            pl.BlockSpec((1, ct), lambda b, tt, j: (0, gcol(j))),
            pl.BlockSpec((None, 1, ct), lambda b, tt, j: (order, 0, j)),
        ],
        out_specs=pl.BlockSpec((None, st, h, ct), lambda b, tt, j: (b, tt, 0, j)),
        out_shape=jax.ShapeDtypeStruct((bsz, n1, h, d), F32),
        compiler_params=_params(("parallel", "parallel", "parallel")),
        name="fft_s3",
    )(_perm_for(h, st), f, t, u, z, z, z, conv_w, conv_b, skip)


def _hyena_filter_spectra(seq_len, consts, fw1, fb1, ffreq, fw2, fb2, fw3, decay):
    d = decay.shape[-1]
    n1 = consts["n1"]
    zf = _filter_positions(seq_len, n1)
    a2 = _filter_mlp(zf, fw1, fb1, ffreq, fw2, fb2)
    a, sumsq = _filter_s1(consts, a2, zf, fw3, decay)
    return _fft_s2f(a, sumsq, consts)


def _hyena_mixer(x, g, shift, scale, gate, p, layer, kf, consts):
    bsz, seq_len, d = x.shape
    n1 = consts["n1"]
    h = SLAB_ROWS
    st = _slab_tile(n1)
    xv = x.reshape(bsz, h, n1, d)
    z = _norm_mm(
        xv, pl.BlockSpec((None, h, st, d), lambda b, i, j: (b, 0, i, 0)), (h, st), st * h, n1 // st, g, shift, scale,
        [p["hy_w_in"]], layer, 0, 3 * d, jax.ShapeDtypeStruct((bsz, n1, h, 3 * d), F32),
        pl.BlockSpec((None, st, h, 512), lambda b, i, j: (b, i, 0, j)),
        mode="bias", bias=p["hy_b_in"][layer].reshape(1, 3 * d))
    cw, cb = p["hy_conv_w"][layer], p["hy_conv_b"][layer].reshape(1, 3 * d)
    skip = p["hy_skip"][layer].reshape(HYENA_ORDER, 1, d)
    a, u = _fft_s1(consts, z, 2, conv=(cw, cb))
    t = _fft_s2(a, kf, 0, consts)
    y1 = _fft_s3(consts, t, u, z, 0, cw, cb, skip, 0)
    a = _fft_s1(consts, y1, 0)
    t = _fft_s2(a, kf, 1, consts)
    y2 = _fft_s3(consts, t, y1, z, 1, cw, cb, skip, 1)
    q = ROW_TILE // n1
    return _mm_res(
        [y2], [pl.BlockSpec((None, n1, q, d), lambda b, i, j: (b, 0, i, 0))], p["hy_w_out"], layer,
        p["hy_b_out"][layer].reshape(1, d), x, gate, ROW_TILE, a_mode="perm", perm=(n1, q))


def _attn_kernel(q_ref, kp_ref, km_ref, kn_ref, vp_ref, vm_ref, vn_ref, o_ref, l_ref, kx_ref, vx_ref, *scr,
                 phases, ta, n_sub, to_classes):
    i = pl.program_id(2)
    halo = ATT_BAND // phases
    qa = ATT_Q // phases
    ka = 2 * qa
    kx_ref[:, 0:halo] = kp_ref[...]
    kx_ref[:, halo:halo + ta] = km_ref[...]
    kx_ref[:, halo + ta:] = kn_ref[...]
    vx_ref[:, 0:halo] = vp_ref[...]
    vx_ref[:, halo:halo + ta] = vm_ref[...]
    vx_ref[:, halo + ta:] = vn_ref[...]
    row = lax.broadcasted_iota(jnp.int32, (ATT_Q, 2 * ATT_Q), 0)
    col = lax.broadcasted_iota(jnp.int32, (ATT_Q, 2 * ATT_Q), 1)
    cq, aq = row >> (qa.bit_length() - 1), row & (qa - 1)
    ck, ak = col >> (ka.bit_length() - 1), col & (ka - 1)
    delta = phases * (ak - aq) - ATT_BAND + ck - cq
    band = (delta >= -ATT_BAND) & (delta <= ATT_BAND)
    for s in range(ta // qa):
        key_idx = phases * (i * ta + s * qa - halo + ak) + ck
        valid = band & (key_idx >= 0) & (key_idx < n_sub)
        for h in range(HEADS_PER_GROUP):
            cs = slice(h * HEAD_DIM, (h + 1) * HEAD_DIM)
            q = jnp.concatenate([q_ref[c, s * qa:(s + 1) * qa, cs] for c in range(phases)], axis=0)
            k = jnp.concatenate([kx_ref[c, s * qa:s * qa + ka, cs] for c in range(phases)], axis=0)
            v = jnp.concatenate([vx_ref[c, s * qa:s * qa + ka, cs] for c in range(phases)], axis=0)
            sc = lax.dot_general(q, k, (((1,), (1,)), ((), ())), preferred_element_type=F32)
            sc = jnp.where(valid, sc, NEG_BIG)
            m = jnp.max(sc, axis=-1, keepdims=True)
            pr = jnp.exp(sc - m)
            den = jnp.sum(pr, axis=-1, keepdims=True)
            o = _dot(pr.astype(BF16), v) / den
            lse = jnp.broadcast_to(m + jnp.log(den), (ATT_Q, HEAD_DIM))
            if to_classes:
                t_ref = scr[0]
                na = ATT_Q // CLASSES
                for val, dst in ((o, o_ref), (lse, l_ref)):
                    t_ref[...] = val.reshape(na, CLASSES, HEAD_DIM)
                    for rho in range(CLASSES):
                        dst[rho, s * na:(s + 1) * na, cs] = t_ref[:, rho, :]
            else:
                for c in range(phases):
                    o_ref[c, s * qa:(s + 1) * qa, cs] = o[c * qa:(c + 1) * qa]
                    l_ref[c, s * qa:(s + 1) * qa, cs] = lse[c * qa:(c + 1) * qa]


def _attn_group(qkv, col0, phases, n_sub, lead_grid, lead_block, lead_index, ta, *, to_classes=False,
                out_arr_shape=None, out_block=None, out_index=None):
    gw = GROUP_WIDTH
    halo = ATT_BAND // phases
    rows = qkv.shape[-2]
    per = ta // halo
    nblk = rows // halo
    cb = col0 // gw

    def spec(nrows, ridx, part):
        return pl.BlockSpec(tuple(lead_block) + (nrows, gw),
                            lambda b, rho, i: tuple(lead_index(b, rho)) + (ridx(i), cb + part))

    main = lambda part: spec(ta, lambda i: i, part)
    prev = lambda part: spec(halo, lambda i: jnp.maximum(i * per - 1, 0), part)
    nxt = lambda part: spec(halo, lambda i: jnp.minimum((i + 1) * per, nblk - 1), part)
    bsz = qkv.shape[0]
    if to_classes:
        out_spec = pl.BlockSpec(out_block, out_index)
        scratch_extra = [pltpu.VMEM((ATT_Q // CLASSES, CLASSES, HEAD_DIM), F32)]
    else:
        out_arr_shape = qkv.shape[:-1] + (gw,)
        out_spec = pl.BlockSpec(tuple(lead_block) + (ta, gw),
                                lambda b, rho, i: tuple(lead_index(b, rho)) + (i, 0))
        scratch_extra = []
    kx_shape = (phases, ta + 2 * halo, gw)
    return pl.pallas_call(
        functools.partial(_attn_kernel, phases=phases, ta=ta, n_sub=n_sub, to_classes=to_classes),
        grid=(bsz, lead_grid, rows // ta),
        in_specs=[main(0), prev(1), main(1), nxt(1), prev(2), main(2), nxt(2)],
        out_specs=[out_spec, out_spec],
        out_shape=[jax.ShapeDtypeStruct(out_arr_shape, F32)] * 2,
        scratch_shapes=[pltpu.VMEM(kx_shape, BF16)] * 2 + scratch_extra,
        compiler_params=_params(("parallel", "parallel", "parallel")),
        name="attn_p%d" % phases + ("_cls" if to_classes else ""),
    )(qkv, qkv, qkv, qkv, qkv, qkv, qkv)


def _rope_tables(pos):
    half = ROT_DIM // 2
    inv = ROPE_THETA ** (-jnp.arange(0, ROT_DIM, 2, dtype=F32) / ROT_DIM)
    ang = pos.astype(F32)[:, None] * inv[None, :]
    cos, sin = jnp.cos(ang), jnp.sin(ang)
    n = pos.shape[0]
    rest = HEAD_DIM - ROT_DIM
    c = jnp.concatenate([cos, cos, jnp.ones((n, rest), F32)], axis=1)
    s = jnp.concatenate([sin, sin, jnp.zeros((n, rest), F32)], axis=1)
    return c, s


def _rope_partner_matrix(width):
    half = ROT_DIM // 2
    m = np.zeros((width, width), np.float32)
    for base in range(0, width, HEAD_DIM):
        for k in range(half):
            m[base + k + half, base + k] = -1.0
            m[base + k, base + k + half] = 1.0
    return jnp.asarray(m, BF16)


def _attn_mixer(x, g, shift, scale, gate, p, layer):
    bsz, seq_len, d = x.shape
    gw = GROUP_WIDTH
    nc = seq_len // CLASSES
    ca = ROW_TILE // CLASSES
    w_in = p["at_w_in"]
    tabs = _rope_tables(jnp.arange(seq_len))
    qkv0 = _norm_mm(
        x, pl.BlockSpec((None, ROW_TILE, d), lambda b, i, j: (b, i, 0)), None, ROW_TILE, seq_len // ROW_TILE,
        g, shift, scale, [w_in], layer, 0, 3 * gw, jax.ShapeDtypeStruct((bsz, seq_len, 3 * gw), BF16),
        pl.BlockSpec((None, ROW_TILE, 512), lambda b, i, j: (b, i, j)), mode="rope",
        rope=(tabs, pl.BlockSpec((ROW_TILE, HEAD_DIM), lambda b, i, j: (i, 0))))
    pos_c = (jnp.arange(nc)[None, :] * CLASSES + jnp.arange(CLASSES)[:, None]).reshape(-1)
    tabs_c = [t.reshape(CLASSES, nc, HEAD_DIM) for t in _rope_tables(pos_c)]
    qkv12 = _norm_mm(
        x.reshape(bsz, nc, CLASSES, d), pl.BlockSpec((None, ca, CLASSES, d), lambda b, i, j: (b, i, 0, 0)),
        (ca, CLASSES), ROW_TILE, nc // ca, g, shift, scale, [w_in], layer, 3 * gw, 6 * gw,
        jax.ShapeDtypeStruct((bsz, CLASSES, nc, 6 * gw), BF16),
        pl.BlockSpec((None, CLASSES, ca, 512), lambda b, i, j: (b, 0, i, j)), mode="rope",
        rope=(tabs_c, pl.BlockSpec((CLASSES, ca, HEAD_DIM), lambda b, i, j: (0, i, 0))))
    cls_shape = (bsz, CLASSES, nc, gw)
    ta0 = min(256, seq_len)
    o0, l0 = _attn_group(
        qkv0.reshape(bsz, 1, seq_len, 3 * gw), 0, 1, seq_len, 1, (None, 1), lambda b, rho: (b, 0), ta0,
        to_classes=True, out_arr_shape=cls_shape, out_block=(None, CLASSES, ta0 // CLASSES, gw),
        out_index=lambda b, rho, i: (b, 0, i, 0))
    dil1 = ATTN_PATTERNS[1][1]
    ph = CLASSES // dil1
    ta1 = min(64, nc)
    o1, l1 = _attn_group(
        qkv12.reshape(bsz, ph, dil1, nc, 6 * gw), 0, ph, seq_len // dil1, dil1, (None, ph, None),
        lambda b, rho: (b, 0, rho), ta1)
    o1, l1 = o1.reshape(cls_shape), l1.reshape(cls_shape)
    ta2 = min(256, nc)
    o2, l2 = _attn_group(
        qkv12.reshape(bsz, CLASSES, 1, nc, 6 * gw), 3 * gw, 1, nc, CLASSES, (None, None, 1),
        lambda b, rho: (b, rho, 0), ta2)
    o2, l2 = o2.reshape(cls_shape), l2.reshape(cls_shape)
    cm = ca // 2
    cls_blk = pl.BlockSpec((None, CLASSES, cm, gw), lambda b, i, j: (b, 0, i, 0))
    return _mm_res([o0, o1, o2, l0, l1, l2], [cls_blk] * 6, p["at_w_out"], layer, jnp.zeros((1, d), F32),
                   x, gate, CLASSES * cm, a_mode="merge", perm=(CLASSES, cm))


def _ffn(x, g, shift, scale, gate, p, layer):
    bsz, seq_len, d = x.shape
    dff = p["ffn_w_gate"].shape[-1]
    tiles = seq_len // ROW_TILE
    xs = pl.BlockSpec((None, ROW_TILE, d), lambda b, i, j: (b, i, 0))
    hs = pl.BlockSpec((None, ROW_TILE, 512), lambda b, i, j: (b, i, j))
    hmid = _norm_mm(x, xs, None, ROW_TILE, tiles, g, shift, scale, [p["ffn_w_gate"], p["ffn_w_up"]], layer, 0,
                    dff, jax.ShapeDtypeStruct((bsz, seq_len, dff), BF16), hs, mode="swiglu")
    return _mm_res([hmid], [pl.BlockSpec((None, ROW_TILE, dff), lambda b, i, j: (b, i, 0))],
                   p["ffn_w_down"], layer, jnp.zeros((1, d), F32), x, gate, ROW_TILE, tn=256)


def _encoder(x, mods, final_mod, p):
    bsz, seq_len, d = x.shape
    n1 = 2 * seq_len // DFT_N2
    consts = _dft_consts(n1, DFT_N2)
    consts["d"] = d
    for i in range(DEPTH):
        sh_m, sc_m, g_m, sh_f, sc_f, g_f = [mods[i][:, None, k * d:(k + 1) * d] for k in range(6)]
        j = i // 2
        if i % 2 == 0:
            kf = _hyena_filter_spectra(seq_len, consts, p["hy_fw1"][j], p["hy_fb1"][j], p["hy_ffreq"][j],
                                       p["hy_fw2"][j], p["hy_fb2"][j], p["hy_fw3"][j], p["hy_decay"][j])
            x = _hyena_mixer(x, p["norm_mix"][i], sh_m, sc_m, g_m, p, j, kf, consts)
        else:
            x = _attn_mixer(x, p["norm_mix"][i], sh_m, sc_m, g_m, p, j)
        x = _ffn(x, p["norm_ffn"][i], sh_f, sc_f, g_f, p, i)
    sh, sc = final_mod[:, None, :d], final_mod[:, None, d:]
    return _final(x, p["final_norm"], sh, sc)


def kernel(x_prompt, x_sample, c_prompt, c_sample, ada_w, ada_b, norm_mix, norm_ffn, hy_w_in, hy_b_in, hy_conv_w, hy_conv_b, hy_fw1, hy_fb1, hy_ffreq, hy_fw2, hy_fb2, hy_fw3, hy_decay, hy_skip, hy_w_out, hy_b_out, at_w_in, at_w_out, ffn_w_gate, ffn_w_up, ffn_w_down, final_norm, final_ada_w, final_ada_b):
    d = x_prompt.shape[-1]
    bp, bs = c_prompt.shape[0], c_sample.shape[0]
    pad = -(bp + bs) % (2 * SUBLANES)
    c_all = jnp.concatenate([c_prompt, c_sample, jnp.zeros((pad, d), F32)], axis=0)
    mods = _ada(c_all, ada_w, ada_b)
    fmod = _ada(c_all, final_ada_w[None], final_ada_b[None])[0]
    p = dict(norm_mix=norm_mix, norm_ffn=norm_ffn,
             hy_w_in=hy_w_in.astype(BF16), hy_b_in=hy_b_in, hy_conv_w=hy_conv_w, hy_conv_b=hy_conv_b,
             hy_fw1=hy_fw1, hy_fb1=hy_fb1, hy_ffreq=hy_ffreq, hy_fw2=hy_fw2, hy_fb2=hy_fb2, hy_fw3=hy_fw3,
             hy_decay=hy_decay, hy_skip=hy_skip, hy_w_out=hy_w_out.astype(BF16), hy_b_out=hy_b_out,
             at_w_in=at_w_in.astype(BF16), at_w_out=at_w_out.astype(BF16),
             ffn_w_gate=ffn_w_gate.astype(BF16), ffn_w_up=ffn_w_up.astype(BF16),
             ffn_w_down=ffn_w_down.astype(BF16), final_norm=final_norm)
    y_prompt = _encoder(x_prompt, mods[:, :bp], fmod[:bp], p)
    y_sample = _encoder(x_sample, mods[:, bp:bp + bs], fmod[bp:bp + bs], p)
    return (y_prompt, y_sample)
```

```python
import functools
import math

import numpy as np
import jax
import jax.numpy as jnp
from jax import lax
from jax.experimental import pallas as pl
from jax.experimental.pallas import tpu as pltpu

F32 = jnp.float32
BF16 = jnp.bfloat16
EPS = 1e-6

DEPTH = 4
HYENA_ORDER = 2
N_DIRS = 2
FILTER_BANDS = 16
FILTER_EMB = 1 + 2 * FILTER_BANDS
ATTN_PATTERNS = ((128, 1), (512, 4), (2048, 16))
HEADS_PER_GROUP = 8
HEAD_DIM = 128
GROUP_WIDTH = HEADS_PER_GROUP * HEAD_DIM
ROT_DIM = HEAD_DIM // 4
ROPE_THETA = 500000.0

LANES = 128
SUBLANES = 8
VMEM_LIMIT_BYTES = 56 * 1024 * 1024

DFT_N2 = 256
SLAB_ROWS = DFT_N2 // 2
ATT_BAND = 64
ATT_Q = 2 * ATT_BAND
CLASSES = 16
LSE_LANES = LANES // HEADS_PER_GROUP
ROW_TILE = 1024
NEG_BIG = -1e30


def _params(sem):
    return pltpu.CompilerParams(dimension_semantics=sem, vmem_limit_bytes=VMEM_LIMIT_BYTES)


def _dot(a, b):
    return jnp.dot(a, b, preferred_element_type=F32)


def _split(a):
    hi = a.astype(BF16)
    lo = (a - hi.astype(F32)).astype(BF16)
    return hi, lo


def _dot3(a, b):
    ah, al = _split(a)
    bh, bl = _split(b)
    return _dot(ah, bh) + _dot(al, bh) + _dot(ah, bl)


def _modnorm(x, g, shift, scale):
    ms = jnp.mean(x * x, axis=-1, keepdims=True)
    return (x * lax.rsqrt(ms + EPS)) * (g * (1.0 + scale)) + shift


PERM_ROWS = 256


def _perm_matrix(p, q):
    m = np.zeros((PERM_ROWS, PERM_ROWS), np.float32)
    pi, qi = np.meshgrid(np.arange(p), np.arange(q), indexing="ij")
    m[(qi * p + pi).ravel(), (pi * q + qi).ravel()] = 1.0
    return jnp.asarray(m, BF16)


def _perm_for(p, q):
    assert (q <= 16 and p % (PERM_ROWS // q) == 0) or (p <= 16 and q % (PERM_ROWS // p) == 0), (p, q)
    return _perm_matrix(PERM_ROWS // q, q) if q <= 16 else _perm_matrix(p, PERM_ROWS // p)


def _transpose_rows(dst_ref, src_ref, pm_ref, p, q):
    if q <= 16:
        pg = PERM_ROWS // q
        for grp in range(p // pg):
            t = _dot(pm_ref[...], src_ref[grp * PERM_ROWS:(grp + 1) * PERM_ROWS, :]).astype(BF16)
            for qi in range(q):
                dst_ref[qi * p + grp * pg:qi * p + (grp + 1) * pg, :] = t[qi * pg:(qi + 1) * pg]
    else:
        qg = PERM_ROWS // p
        for grp in range(q // qg):
            blk = jnp.concatenate([src_ref[pi * q + grp * qg:pi * q + (grp + 1) * qg, :] for pi in range(p)],
                                  axis=0)
            dst_ref[grp * PERM_ROWS:(grp + 1) * PERM_ROWS, :] = _dot(pm_ref[...], blk).astype(BF16)


def _ada_kernel(c_ref, w_ref, b_ref, o_ref):
    c = c_ref[...]
    cs = c * jax.nn.sigmoid(c)
    o_ref[...] = _dot3(cs, w_ref[...]) + b_ref[...]


def _ada(c_all, w, b, tn=1024):
    nl, d, no = w.shape
    r = c_all.shape[0]
    return pl.pallas_call(
        _ada_kernel,
        grid=(nl, no // tn),
        in_specs=[
            pl.BlockSpec((r, d), lambda l, j: (0, 0)),
            pl.BlockSpec((None, d, tn), lambda l, j: (l, 0, j)),
            pl.BlockSpec((None, 1, tn), lambda l, j: (l, 0, j)),
        ],
        out_specs=pl.BlockSpec((None, r, tn), lambda l, j: (l, 0, j)),
        out_shape=jax.ShapeDtypeStruct((nl, r, no), F32),
        compiler_params=_params(("parallel", "parallel")),
        name="ada_mod",
    )(c_all, w, b.reshape(nl, 1, no))


def _norm_mm_kernel(*refs, mode, tn, perm):
    if perm:
        pm_ref, refs = refs[0], refs[1:]
        h0_ref, refs = refs[-1], refs[:-1]
    if mode == "swiglu":
        x_ref, g_ref, sh_ref, sc_ref, wg_ref, wu_ref, o_ref, h_ref = refs
    elif mode == "rope":
        x_ref, g_ref, sh_ref, sc_ref, w_ref, c_ref, s_ref, rot_ref, o_ref, h_ref = refs
    else:
        x_ref, g_ref, sh_ref, sc_ref, w_ref, b_ref, o_ref, h_ref = refs
    j = pl.program_id(2)

    @pl.when(j == 0)
    def _():
        h = _modnorm(x_ref[...].reshape(h_ref.shape), g_ref[...], sh_ref[...], sc_ref[...]).astype(BF16)
        if perm:
            h0_ref[...] = h
            _transpose_rows(h_ref, h0_ref, pm_ref, *perm)
        else:
            h_ref[...] = h

    h = h_ref[...]
    if mode == "swiglu":
        a = _dot(h, wg_ref[...])
        u = _dot(h, wu_ref[...])
        o_ref[...] = (a * jax.nn.sigmoid(a) * u).astype(o_ref.dtype).reshape(o_ref.shape)
    elif mode == "rope":
        acc = _dot(h, w_ref[...])
        part = (j // (GROUP_WIDTH // tn)) % 3

        @pl.when(part == 2)
        def _():
            o_ref[...] = acc.astype(o_ref.dtype).reshape(o_ref.shape)

        @pl.when(part != 2)
        def _():
            reps = tn // HEAD_DIM
            tabs = [t[...].reshape(acc.shape[0], HEAD_DIM) for t in (c_ref, s_ref)]
            c, s = [jnp.concatenate([t] * reps, axis=1) for t in tabs]
            partner = _dot(acc.astype(BF16), rot_ref[...])
            qs = jnp.where(part == 0, HEAD_DIM ** -0.5, 1.0).astype(F32)
            o_ref[...] = ((acc * c + partner * s) * qs).astype(o_ref.dtype).reshape(o_ref.shape)
    else:
        o_ref[...] = (_dot(h, w_ref[...]) + b_ref[...]).astype(o_ref.dtype).reshape(o_ref.shape)


def _norm_mm(x, x_spec, perm, rows, grid_rows, g, shift, scale, ws, w_layer, col0, nout, out_shape,
             out_spec, *, mode, bias=None, rope=None, tn=512):
    d = x.shape[-1]
    bsz = x.shape[0]
    cb = col0 // tn
    vec = pl.BlockSpec((None, 1, d), lambda b, i, j: (b, 0, 0))
    in_specs = [x_spec, pl.BlockSpec((1, d), lambda b, i, j: (0, 0)), vec, vec]
    in_specs += [pl.BlockSpec((None, d, tn), lambda b, i, j: (w_layer, 0, cb + j)) for _ in ws]
    args = [x, g.reshape(1, d), shift, scale, *ws]
    scratch = [pltpu.VMEM((rows, d), BF16)]
    if perm:
        in_specs.insert(0, pl.BlockSpec((PERM_ROWS, PERM_ROWS), lambda b, i, j: (0, 0)))
        args.insert(0, _perm_for(*perm))
        scratch.append(pltpu.VMEM((rows, d), BF16))
    if mode == "rope":
        tabs, tab_spec = rope
        in_specs += [tab_spec] * 2 + [pl.BlockSpec((tn, tn), lambda b, i, j: (0, 0))]
        args += list(tabs) + [_rope_partner_matrix(tn)]
    elif mode == "bias":
        in_specs.append(pl.BlockSpec((1, tn), lambda b, i, j: (0, cb + j)))
        args.append(bias)
    return pl.pallas_call(
        functools.partial(_norm_mm_kernel, mode=mode, tn=tn, perm=perm),
        grid=(bsz, grid_rows, nout // tn),
        in_specs=in_specs,
        out_specs=out_spec,
        out_shape=out_shape,
        scratch_shapes=scratch,
        compiler_params=_params(("parallel", "parallel", "arbitrary")),
        name="norm_mm_" + mode,
    )(*args)


def _mm_res_kernel(*refs, a_mode, perm):
    if a_mode == "merge":
        (pm_ref, o0, o1, o2, l0, l1, l2, e_ref, w_ref, b_ref, x_ref, gt_ref, out_ref, a_ref, a0_ref) = refs
    elif a_mode == "perm":
        pm_ref, a_in, w_ref, b_ref, x_ref, gt_ref, out_ref, a_ref, a0_ref = refs
    else:
        a_in, w_ref, b_ref, x_ref, gt_ref, out_ref = refs
    j = pl.program_id(2)

    if a_mode != "plain":
        @pl.when(j == 0)
        def _():
            k = a_ref.shape[1]
            if a_mode == "merge":
                ls = [l[...].reshape(-1, LANES) for l in (l0, l1, l2)]
                mx = jnp.maximum(jnp.maximum(ls[0], ls[1]), ls[2])
                ws = [jnp.exp(l - mx) for l in ls]
                inv = 1.0 / (ws[0] + ws[1] + ws[2])
                num = jnp.zeros((ls[0].shape[0], k), F32)
                for w, o in zip(ws, (o0, o1, o2)):
                    hi, lo = _split(w * inv)
                    wide = _dot(hi, e_ref[...]) + _dot(lo, e_ref[...])
                    num = num + wide * o[...].reshape(-1, k).astype(F32)
                a0_ref[...] = num.astype(BF16)
            else:
                a0_ref[...] = a_in[...].reshape(-1, k).astype(BF16)
            _transpose_rows(a_ref, a0_ref, pm_ref, *perm)

        a = a_ref[...]
    else:
        a = a_in[...]
    out_ref[...] = x_ref[...] + gt_ref[...] * (_dot(a, w_ref[...]) + b_ref[...])


def _head_spread_matrix(width):
    m = np.zeros((LANES, width), np.float32)
    for hd in range(width // HEAD_DIM):
        m[hd * LSE_LANES, hd * HEAD_DIM:(hd + 1) * HEAD_DIM] = 1.0
    return jnp.asarray(m, BF16)


def _mm_res(a_list, a_specs, w, w_layer, bias, x, gate, rows, *, a_mode="plain", perm=None, tn=512):
    bsz, seq_len, d = x.shape
    k = w.shape[1]
    blk = pl.BlockSpec((None, rows, tn), lambda b, i, j: (b, i, j))
    if a_mode == "merge":
        a_list = list(a_list) + [_head_spread_matrix(k)]
        a_specs = list(a_specs) + [pl.BlockSpec((LANES, k), lambda b, i, j: (0, 0))]
    in_specs = list(a_specs) + [
        pl.BlockSpec((None, k, tn), lambda b, i, j: (w_layer, 0, j)),
        pl.BlockSpec((1, tn), lambda b, i, j: (0, j)),
        blk,
        pl.BlockSpec((None, 1, tn), lambda b, i, j: (b, 0, j)),
    ]
    args = [*a_list, w, bias, x, gate]
    scratch = []
    if a_mode != "plain":
        in_specs.insert(0, pl.BlockSpec((PERM_ROWS, PERM_ROWS), lambda b, i, j: (0, 0)))
        args.insert(0, _perm_for(*perm))
        scratch = [pltpu.VMEM((rows, k), BF16)] * 2
    return pl.pallas_call(
        functools.partial(_mm_res_kernel, a_mode=a_mode, perm=perm),
        grid=(bsz, seq_len // rows, d // tn),
        in_specs=in_specs,
        out_specs=blk,
        out_shape=jax.ShapeDtypeStruct(x.shape, F32),
        scratch_shapes=scratch,
        compiler_params=_params(("parallel", "parallel", "arbitrary")),
        name="mm_res_" + a_mode,
    )(*args)


def _final_kernel(x_ref, g_ref, sh_ref, sc_ref, o_ref):
    o_ref[...] = _modnorm(x_ref[...], g_ref[...], sh_ref[...], sc_ref[...])


def _final(x, g, shift, scale, tm=512):
    bsz, seq_len, d = x.shape
    vec = pl.BlockSpec((None, 1, d), lambda b, i: (b, 0, 0))
    blk = pl.BlockSpec((None, tm, d), lambda b, i: (b, i, 0))
    return pl.pallas_call(
        _final_kernel,
        grid=(bsz, seq_len // tm),
        in_specs=[blk, pl.BlockSpec((1, d), lambda b, i: (0, 0)), vec, vec],
        out_specs=blk,
        out_shape=jax.ShapeDtypeStruct(x.shape, F32),
        compiler_params=_params(("parallel", "parallel")),
        name="final_norm",
    )(x, g.reshape(1, d), shift, scale)


def _filter_positions(seq_len, n1):
    n = 2 * seq_len
    h = SLAB_ROWS
    half = jnp.arange(2)[:, None, None]
    s = jnp.arange(n1)[None, :, None]
    r = jnp.arange(h)[None, None, :]
    idx = ((half * h + r) * n1 + s).reshape(n)
    pos = jnp.where(idx < seq_len, idx, n - idx).astype(F32)
    sign = jnp.where(idx < seq_len, 1.0, jnp.where(idx == seq_len, 0.0, -1.0)).astype(F32)
    t = pos / max(seq_len - 1, 1)
    bands = jnp.linspace(1e-4, FILTER_BANDS - 1, FILTER_BANDS, dtype=F32)
    ang = 2.0 * math.pi * pos[:, None] * bands[None, :] / seq_len
    z = jnp.concatenate([t[:, None], jnp.cos(ang), -jnp.sin(ang)], axis=-1)
    z = jnp.pad(z, ((0, 0), (0, LANES - FILTER_EMB - 1)))
    return jnp.concatenate([z, sign[:, None]], axis=-1)


def _filter_mlp_kernel(z_ref, w1_ref, b1_ref, f_ref, w2_ref, b2_ref, o_ref):
    f = f_ref[...]
    a = jnp.sin(f[0:1, :] * (_dot3(z_ref[...], w1_ref[...]) + b1_ref[...]))
    o_ref[...] = jnp.sin(f[1:2, :] * (_dot3(a, w2_ref[...]) + b2_ref[...]))


def _filter_mlp(zf, w1, b1, freq, w2, b2, tr=512):
    n = zf.shape[0]
    hid = w1.shape[1]
    w1p = jnp.pad(w1, ((0, LANES - w1.shape[0]), (0, 0)))
    full = lambda shape: pl.BlockSpec(shape, lambda i: (0,) * len(shape))
    return pl.pallas_call(
        _filter_mlp_kernel,
        grid=(n // tr,),
        in_specs=[pl.BlockSpec((tr, LANES), lambda i: (i, 0)), full((LANES, hid)), full((1, hid)),
                  full((2, hid)), full((hid, hid)), full((1, hid))],
        out_specs=pl.BlockSpec((tr, hid), lambda i: (i, 0)),
        out_shape=jax.ShapeDtypeStruct((n, hid), F32),
        compiler_params=_params(("parallel",)),
        name="filter_mlp",
    )(zf, w1p, b1.reshape(1, hid), freq, w2, b2.reshape(1, hid))


def _dft_consts(n1, n2):
    n = n1 * n2
    k2 = np.arange(n2 // 2)[:, None]
    nn2 = np.arange(n2)[None, :]
    ph = 2.0 * np.pi * nn2 * (k2 + 0.5) / n2
    f_s1 = np.concatenate([np.cos(ph), -np.sin(ph)], axis=0)
    m = np.arange(n2 // 2)[:, None]
    kk = np.arange(n2 // 2)[None, :]
    ph3 = 2.0 * np.pi * m * (kk + 0.5) / n2
    f_s3 = (2.0 / n) * np.concatenate([np.cos(ph3), -np.sin(ph3)], axis=1)
    a = 2.0 * np.pi * np.outer(np.arange(n1), np.arange(n1)) / n1
    c, s = np.cos(a), -np.sin(a)
    f_fwd = np.block([[c, -s], [s, c]])
    f_inv = np.block([[c, s], [-s, c]])
    th = 2.0 * np.pi * (np.arange(n2 // 2)[:, None] + 0.5) * np.arange(n1)[None, :] / n
    as_bf = lambda x: jnp.asarray(x, F32).astype(BF16)
    tw_c = jnp.broadcast_to(jnp.asarray(np.cos(th), F32)[:, :, None], (n2 // 2, n1, LANES))
    tw_s = jnp.broadcast_to(jnp.asarray(np.sin(th), F32)[:, :, None], (n2 // 2, n1, LANES))
    return dict(f_s1=as_bf(f_s1[:, :n2 // 2]), f_s3=as_bf(f_s3), f_fwd=as_bf(f_fwd), f_inv=as_bf(f_inv),
                f_s1_f32=jnp.asarray(f_s1, F32), f_fwd_f32=jnp.asarray(f_fwd, F32),
                tw_c=tw_c, tw_s=tw_s, n1=n1)


def _slab_tile(n1):
    return min(n1, 8)


HI16 = -65536
HALF_ULP16 = 0x8000


def _pack_c(re, im):
    rb = lax.bitcast_convert_type(re, jnp.int32) + HALF_ULP16
    ib = lax.bitcast_convert_type(im, jnp.int32) + HALF_ULP16
    return (rb & HI16) | lax.shift_right_logical(ib, 16)


def _pack_bf16(re, im):
    rb = lax.bitcast_convert_type(re.astype(F32), jnp.int32)
    ib = lax.bitcast_convert_type(im.astype(F32), jnp.int32)
    return rb | lax.shift_right_logical(ib, 16)


def _unpack_c(p):
    re = lax.bitcast_convert_type(p & HI16, F32)
    im = lax.bitcast_convert_type(lax.shift_left(p, 16), F32)
    return re, im


def _shift_rows(x, down):
    rows = x.shape[0]
    row = lax.broadcasted_iota(jnp.int32, x.shape, 0)
    if down:
        return jnp.where(row == 0, 0.0, pltpu.roll(x, 1, 0))
    return jnp.where(row == rows - 1, 0.0, pltpu.roll(x, rows - 1, 0))


def _short_conv_slabs(main_ref, prev_ref, next_ref, w_ref, b_ref, first, last):
    n_slabs = main_ref.shape[0]
    prev = prev_ref[0].astype(F32)
    prev = jnp.where(first, _shift_rows(prev, True), prev)
    nxt = next_ref[0].astype(F32)
    nxt = jnp.where(last, _shift_rows(nxt, False), nxt)
    w = w_ref[...]
    out = []
    for s in range(n_slabs):
        up = prev if s == 0 else main_ref[s - 1].astype(F32)
        dn = nxt if s == n_slabs - 1 else main_ref[s + 1].astype(F32)
        out.append(up * w[0:1, :] + main_ref[s].astype(F32) * w[1:2, :] + dn * w[2:3, :] + b_ref[...])
    return out


def _fft_s1_kernel(*refs, short_conv):
    re_ref, im_ref, ret_ref, imt_ref = refs[-4:]
    if short_conv:
        pm_ref, f_ref, m_ref, p_ref, n_ref, w_ref, b_ref, o_ref, u_ref = refs[:-4]
        t = pl.program_id(1)
        slabs = _short_conv_slabs(m_ref, p_ref, n_ref, w_ref, b_ref, t == 0, t == pl.num_programs(1) - 1)
    else:
        pm_ref, f_ref, m_ref, o_ref = refs[:-4]
        slabs = [m_ref[s] for s in range(m_ref.shape[0])]
    half = o_ref.shape[0]
    for s, u in enumerate(slabs):
        if short_conv:
            u_ref[s] = u.astype(u_ref.dtype)
        r = _dot(f_ref[...], u.astype(BF16))
        re_ref[s * half:(s + 1) * half, :] = r[:half].astype(BF16)
        im_ref[s * half:(s + 1) * half, :] = r[half:].astype(BF16)
    _transpose_rows(ret_ref, re_ref, pm_ref, len(slabs), half)
    _transpose_rows(imt_ref, im_ref, pm_ref, len(slabs), half)
    o_ref[...] = _pack_bf16(ret_ref[...], imt_ref[...]).reshape(o_ref.shape)


def _fft_s1(consts, src, col_block, conv=None, ct=256):
    bsz, n1, h, c = src.shape
    d = consts["d"]
    st = _slab_tile(n1)
    cpb = d // ct
    f = consts["f_s1"]
    main = pl.BlockSpec((None, st, h, ct), lambda b, t, j: (b, t, 0, col_block * cpb + j))
    a_spec = pl.BlockSpec((None, h, st, ct), lambda b, t, j: (b, 0, t, j))
    a_shape = jax.ShapeDtypeStruct((bsz, h, n1, d), jnp.int32)
    fspec = pl.BlockSpec(f.shape, lambda b, t, j: (0, 0))
    pspec = pl.BlockSpec((PERM_ROWS, PERM_ROWS), lambda b, t, j: (0, 0))
    pm = _perm_for(st, h)
    scratch = [pltpu.VMEM((st * h, ct), BF16)] * 4
    if conv is None:
        return pl.pallas_call(
            functools.partial(_fft_s1_kernel, short_conv=False),
            grid=(bsz, n1 // st, cpb),
            in_specs=[pspec, fspec, main], out_specs=a_spec, out_shape=a_shape, scratch_shapes=scratch,
            compiler_params=_params(("parallel", "parallel", "parallel")), name="fft_s1",
        )(pm, f, src)
    w, b = conv
    prev = pl.BlockSpec((None, 1, h, ct), lambda b, t, j: (b, (t * st + n1 - 1) % n1, 0, col_block * cpb + j))
    nxt = pl.BlockSpec((None, 1, h, ct), lambda b, t, j: (b, ((t + 1) * st) % n1, 0, col_block * cpb + j))
    wspec = pl.BlockSpec((3, ct), lambda b, t, j: (0, col_block * cpb + j))
    bspec = pl.BlockSpec((1, ct), lambda b, t, j: (0, col_block * cpb + j))
    u_spec = pl.BlockSpec((None, st, h, ct), lambda b, t, j: (b, t, 0, j))
    return pl.pallas_call(
        functools.partial(_fft_s1_kernel, short_conv=True),
        grid=(bsz, n1 // st, cpb),
        in_specs=[pspec, fspec, main, prev, nxt, wspec, bspec],
        out_specs=[a_spec, u_spec],
        out_shape=[a_shape, jax.ShapeDtypeStruct((bsz, n1, h, d), BF16)],
        scratch_shapes=scratch,
        compiler_params=_params(("parallel", "parallel", "parallel")), name="fft_s1_conv",
    )(pm, f, src, src, src, w, b)


def _filter_s1_kernel(flo_ref, fhi_ref, alo_ref, ahi_ref, zlo_ref, zhi_ref, wf_ref, wb_ref, df_ref, db_ref,
                      o_ref, ss_ref):
    half = o_ref.shape[1]

    @pl.when(pl.program_id(2) == 0)
    def _():
        ss_ref[...] = jnp.zeros_like(ss_ref)

    ss = jnp.zeros(ss_ref.shape, F32)
    for s in range(alo_ref.shape[0]):
        taps = []
        for a_ref, z_ref, w_ref, d_ref in ((alo_ref, zlo_ref, wf_ref, df_ref), (ahi_ref, zhi_ref, wb_ref, db_ref)):
            z = z_ref[s]
            t, sign = z[:, 0:1], z[:, LANES - 1:LANES]
            h = _dot3(a_ref[s], w_ref[...]) * jnp.exp(-t * jnp.abs(d_ref[...])) * sign
            ss = ss + jnp.sum(h * h, axis=0, keepdims=True)
            taps.append(h)
        r = _dot3(flo_ref[...], taps[0]) + _dot3(fhi_ref[...], taps[1])
        o_ref[0, :, s, :] = r[:half]
        o_ref[1, :, s, :] = r[half:]
    ss_ref[...] += ss


def _filter_s1(consts, a2, zf, w3, decay, ct=256):
    n1, h = consts["n1"], SLAB_ROWS
    hid = a2.shape[1]
    d = decay.shape[-1]
    nct = d // ct
    st = _slab_tile(n1)
    f = consts["f_s1_f32"]
    flo, fhi = f[:, :h], f[:, h:]
    dec = decay.reshape(1, HYENA_ORDER * N_DIRS * d)
    fspec = pl.BlockSpec(flo.shape, lambda o, j, t: (0, 0))
    rows = lambda half, width: pl.BlockSpec((None, st, h, width), lambda o, j, t: (half, t, 0, 0))
    wcol = lambda dirn, nrow: pl.BlockSpec((nrow, ct), lambda o, j, t: (0, (o * N_DIRS + dirn) * nct + j))
    return pl.pallas_call(
        _filter_s1_kernel,
        grid=(HYENA_ORDER, nct, n1 // st),
        in_specs=[fspec, fspec, rows(0, hid), rows(1, hid), rows(0, LANES), rows(1, LANES),
                  wcol(0, hid), wcol(1, hid), wcol(0, 1), wcol(1, 1)],
        out_specs=[pl.BlockSpec((None, 2, h, st, ct), lambda o, j, t: (o, 0, 0, t, j)),
                   pl.BlockSpec((None, 1, ct), lambda o, j, t: (o, 0, j))],
        out_shape=[jax.ShapeDtypeStruct((HYENA_ORDER, 2, h, n1, d), F32),
                   jax.ShapeDtypeStruct((HYENA_ORDER, 1, d), F32)],
        compiler_params=_params(("parallel", "parallel", "arbitrary")), name="filter_s1",
    )(flo, fhi, a2.reshape(2, n1, h, hid), a2.reshape(2, n1, h, hid), zf.reshape(2, n1, h, LANES),
      zf.reshape(2, n1, h, LANES), w3, w3, dec, dec)


def _tile_lanes(x, ct):
    return jnp.concatenate([x] * (ct // LANES), axis=-1) if ct > LANES else x


def _fft_s2f_kernel(a_ref, c_ref, s_ref, ff_ref, ss_ref, o_ref, *, kb, ct):
    n1 = a_ref.shape[2]
    scale = lax.rsqrt(ss_ref[...] + EPS)

    def body(kk, carry):
        ar, ai = a_ref[0, kk], a_ref[1, kk]
        c = _tile_lanes(c_ref[kk], ct)
        s = _tile_lanes(s_ref[kk], ct)
        br = ar * c + ai * s
        bi = ai * c - ar * s
        x = _dot3(ff_ref[...], jnp.concatenate([br, bi], axis=0))
        o_ref[kk] = _pack_c(x[:n1] * scale, x[n1:] * scale)
        return carry

    lax.fori_loop(0, kb, body, 0, unroll=min(kb, 4))


def _fft_s2_kernel(a_ref, k_ref, c_ref, s_ref, ff_ref, fi_ref, o_ref, *, kb, ct):
    n1 = a_ref.shape[1]

    def body(kk, carry):
        ar, ai = _unpack_c(a_ref[kk])
        c = _tile_lanes(c_ref[kk], ct)
        s = _tile_lanes(s_ref[kk], ct)
        br = ar * c + ai * s
        bi = ai * c - ar * s
        x = _dot(ff_ref[...], jnp.concatenate([br, bi], axis=0).astype(BF16))
        xr, xi = x[:n1], x[n1:]
        kr, ki = _unpack_c(k_ref[kk])
        zr = xr * kr - xi * ki
        zi = xr * ki + xi * kr
        y = _dot(fi_ref[...], jnp.concatenate([zr, zi], axis=0).astype(BF16))
        yr, yi = y[:n1], y[n1:]
        o_ref[kk] = _pack_c(yr * c - yi * s, yr * s + yi * c)
        return carry

    lax.fori_loop(0, kb, body, 0, unroll=min(kb, 4))


def _s2_tiles(n1, d):
    kb = max(1, 512 // n1)
    ct = min(d, 512)
    return kb, ct


def _fft_s2f(a, sumsq, consts):
    n_o, _, k2n, n1, d = a.shape
    kb, ct = _s2_tiles(n1, d)
    blk = pl.BlockSpec((None, 2, kb, n1, ct), lambda k, j, o: (o, 0, k, 0, j))
    tw = pl.BlockSpec((kb, n1, LANES), lambda k, j, o: (k, 0, 0))
    return pl.pallas_call(
        functools.partial(_fft_s2f_kernel, kb=kb, ct=ct),
        grid=(k2n // kb, d // ct, n_o),
        in_specs=[blk, tw, tw, pl.BlockSpec((2 * n1, 2 * n1), lambda k, j, o: (0, 0)),
                  pl.BlockSpec((None, 1, ct), lambda k, j, o: (o, 0, j))],
        out_specs=pl.BlockSpec((None, kb, n1, ct), lambda k, j, o: (o, k, 0, j)),
        out_shape=jax.ShapeDtypeStruct((n_o, k2n, n1, d), jnp.int32),
        compiler_params=_params(("parallel", "parallel", "parallel")),
        name="fft_s2_filter",
    )(a, consts["tw_c"], consts["tw_s"], consts["f_fwd_f32"], sumsq)


def _fft_s2(a, kf, order, consts):
    bsz, k2n, n1, d = a.shape
    kb, ct = _s2_tiles(n1, d)
    blk = pl.BlockSpec((None, kb, n1, ct), lambda k, j, b: (b, k, 0, j))
    tw = pl.BlockSpec((kb, n1, LANES), lambda k, j, b: (k, 0, 0))
    mat = pl.BlockSpec((2 * n1, 2 * n1), lambda k, j, b: (0, 0))
    return pl.pallas_call(
        functools.partial(_fft_s2_kernel, kb=kb, ct=ct),
        grid=(k2n // kb, d // ct, bsz),
        in_specs=[blk, pl.BlockSpec((None, kb, n1, ct), lambda k, j, b: (order, k, 0, j)),
                  tw, tw, mat, mat],
        out_specs=blk,
        out_shape=jax.ShapeDtypeStruct(a.shape, jnp.int32),
        compiler_params=_params(("parallel", "parallel", "parallel")),
        name="fft_s2",
    )(a, kf, consts["tw_c"], consts["tw_s"], consts["f_fwd"], consts["f_inv"])


def _fft_s3_kernel(pm_ref, f_ref, t_ref, u_ref, gm_ref, gp_ref, gn_ref, w_ref, b_ref, sk_ref, o_ref,
                   re_ref, im_ref, ret_ref, imt_ref):
    t_id = pl.program_id(1)
    half, st = t_ref.shape[0], t_ref.shape[1]
    re, im = _unpack_c(t_ref[...].reshape(half * st, t_ref.shape[2]))
    re_ref[...] = re.astype(BF16)
    im_ref[...] = im.astype(BF16)
    _transpose_rows(ret_ref, re_ref, pm_ref, half, st)
    _transpose_rows(imt_ref, im_ref, pm_ref, half, st)
    gates = _short_conv_slabs(gm_ref, gp_ref, gn_ref, w_ref, b_ref, t_id == 0, t_id == pl.num_programs(1) - 1)
    for s, gate in enumerate(gates):
        t = jnp.concatenate([ret_ref[s * half:(s + 1) * half, :], imt_ref[s * half:(s + 1) * half, :]], axis=0)
        y = _dot(f_ref[...], t)
        o_ref[s] = (gate * (y + u_ref[s].astype(F32) * sk_ref[...])).astype(o_ref.dtype)


def _fft_s3(consts, t, u, z, gate_block, conv_w, conv_b, skip, order, out_dtype, ct=256):
    bsz, h, n1, d = t.shape
    st = _slab_tile(n1)
    cpb = d // ct
    f = consts["f_s3"]
    gcol = lambda j: gate_block * cpb + j
    slab = lambda idx: pl.BlockSpec((None, 1, h, ct), lambda b, tt, j: (b, idx(tt), 0, gcol(j)))
    return pl.pallas_call(
        _fft_s3_kernel,
        grid=(bsz, n1 // st, cpb),
        scratch_shapes=[pltpu.VMEM((st * h, ct), BF16)] * 4,
        in_specs=[
            pl.BlockSpec((PERM_ROWS, PERM_ROWS), lambda b, tt, j: (0, 0)),
            pl.BlockSpec(f.shape, lambda b, tt, j: (0, 0)),
            pl.BlockSpec((None, h, st, ct), lambda b, tt, j: (b, 0, tt, j)),
            pl.BlockSpec((None, st, h, ct), lambda b, tt, j: (b, tt, 0, j)),
            pl.BlockSpec((None, st, h, ct), lambda b, tt, j: (b, tt, 0, gcol(j))),
            slab(lambda tt: (tt * st + n1 - 1) % n1),
            slab(lambda tt: ((tt + 1) * st) % n1),
            pl.BlockSpec((3, ct), lambda b, tt, j: (0, gcol(j))),
            pl.BlockSpec((1, ct), lambda b, tt, j: (0, gcol(j))),
            pl.BlockSpec((None, 1, ct), lambda b, tt, j: (order, 0, j)),
        ],
        out_specs=pl.BlockSpec((None, st, h, ct), lambda b, tt, j: (b, tt, 0, j)),
        out_shape=jax.ShapeDtypeStruct((bsz, n1, h, d), out_dtype),
        compiler_params=_params(("parallel", "parallel", "parallel")),
        name="fft_s3",
    )(_perm_for(h, st), f, t, u, z, z, z, conv_w, conv_b, skip)


def _hyena_filter_spectra(seq_len, consts, fw1, fb1, ffreq, fw2, fb2, fw3, decay):
    d = decay.shape[-1]
    n1 = consts["n1"]
    zf = _filter_positions(seq_len, n1)
    a2 = _filter_mlp(zf, fw1, fb1, ffreq, fw2, fb2)
    a, sumsq = _filter_s1(consts, a2, zf, fw3, decay)
    return _fft_s2f(a, sumsq, consts)


def _hyena_mixer(x, g, shift, scale, gate, p, layer, kf, consts):
    bsz, seq_len, d = x.shape
    n1 = consts["n1"]
    h = SLAB_ROWS
    st = _slab_tile(n1)
    xv = x.reshape(bsz, h, n1, d)
    z = _norm_mm(
        xv, pl.BlockSpec((None, h, st, d), lambda b, i, j: (b, 0, i, 0)), (h, st), st * h, n1 // st, g, shift, scale,
        [p["hy_w_in"]], layer, 0, 3 * d, jax.ShapeDtypeStruct((bsz, n1, h, 3 * d), BF16),
        pl.BlockSpec((None, st, h, 512), lambda b, i, j: (b, i, 0, j)),
        mode="bias", bias=p["hy_b_in"][layer].reshape(1, 3 * d))
    cw, cb = p["hy_conv_w"][layer], p["hy_conv_b"][layer].reshape(1, 3 * d)
    skip = p["hy_skip"][layer].reshape(HYENA_ORDER, 1, d)
    a, u = _fft_s1(consts, z, 2, conv=(cw, cb))
    t = _fft_s2(a, kf, 0, consts)
    y1 = _fft_s3(consts, t, u, z, 0, cw, cb, skip, 0, BF16)
    a = _fft_s1(consts, y1, 0)
    t = _fft_s2(a, kf, 1, consts)
    y2 = _fft_s3(consts, t, y1, z, 1, cw, cb, skip, 1, F32)
    q = ROW_TILE // n1
    return _mm_res(
        [y2], [pl.BlockSpec((None, n1, q, d), lambda b, i, j: (b, 0, i, 0))], p["hy_w_out"], layer,
        p["hy_b_out"][layer].reshape(1, d), x, gate, ROW_TILE, a_mode="perm", perm=(n1, q))


def _attn_kernel(*refs, phases, ta, n_sub, to_classes):
    if to_classes:
        (pm_ref, q_ref, kp_ref, km_ref, kn_ref, vp_ref, vm_ref, vn_ref, o_ref, l_ref, kx_ref, vx_ref,
         on_ref, ot_ref, ln_ref, ls_ref, lt_ref) = refs
    else:
        q_ref, kp_ref, km_ref, kn_ref, vp_ref, vm_ref, vn_ref, o_ref, l_ref, kx_ref, vx_ref = refs
    i = pl.program_id(2)
    halo = ATT_BAND // phases
    qa = ATT_Q // phases
    ka = 2 * qa
    kx_ref[:, 0:halo] = kp_ref[...]
    kx_ref[:, halo:halo + ta] = km_ref[...]
    kx_ref[:, halo + ta:] = kn_ref[...]
    vx_ref[:, 0:halo] = vp_ref[...]
    vx_ref[:, halo:halo + ta] = vm_ref[...]
    vx_ref[:, halo + ta:] = vn_ref[...]
    row = lax.broadcasted_iota(jnp.int32, (ATT_Q, 2 * ATT_Q), 0)
    col = lax.broadcasted_iota(jnp.int32, (ATT_Q, 2 * ATT_Q), 1)
    cq, aq = row >> (qa.bit_length() - 1), row & (qa - 1)
    ck, ak = col >> (ka.bit_length() - 1), col & (ka - 1)
    delta = phases * (ak - aq) - ATT_BAND + ck - cq
    band = (delta >= -ATT_BAND) & (delta <= ATT_BAND)
    lane_head = lax.broadcasted_iota(jnp.int32, (ATT_Q, LANES), 1) >> (LSE_LANES.bit_length() - 1)
    for s in range(ta // qa):
        key_idx = phases * (i * ta + s * qa - halo + ak) + ck
        valid = band & (key_idx >= 0) & (key_idx < n_sub)
        lse_tile = jnp.zeros((ATT_Q, LANES), F32)
        for h in range(HEADS_PER_GROUP):
            cs = slice(h * HEAD_DIM, (h + 1) * HEAD_DIM)
            q = jnp.concatenate([q_ref[c, s * qa:(s + 1) * qa, cs] for c in range(phases)], axis=0)
            k = jnp.concatenate([kx_ref[c, s * qa:s * qa + ka, cs] for c in range(phases)], axis=0)
            v = jnp.concatenate([vx_ref[c, s * qa:s * qa + ka, cs] for c in range(phases)], axis=0)
            sc = lax.dot_general(q, k, (((1,), (1,)), ((), ())), preferred_element_type=F32)
            sc = jnp.where(valid, sc, NEG_BIG)
            m = jnp.max(sc, axis=-1, keepdims=True)
            pr = jnp.exp(sc - m)
            den = jnp.sum(pr, axis=-1, keepdims=True)
            o = (_dot(pr.astype(BF16), v) / den).astype(BF16)
            lse_tile = jnp.where(lane_head == h, m + jnp.log(den), lse_tile)
            if to_classes:
                on_ref[s * ATT_Q:(s + 1) * ATT_Q, cs] = o
            else:
                for c in range(phases):
                    o_ref[c, s * qa:(s + 1) * qa, cs] = o[c * qa:(c + 1) * qa]
        if to_classes:
            ln_ref[s * ATT_Q:(s + 1) * ATT_Q, :] = lse_tile
        else:
            for c in range(phases):
                l_ref[c, s * qa:(s + 1) * qa, :] = lse_tile[c * qa:(c + 1) * qa]
    if to_classes:
        na = ta // CLASSES
        _transpose_rows(ot_ref, on_ref, pm_ref, na, CLASSES)
        o_ref[...] = ot_ref[...].reshape(o_ref.shape)
        rest = ln_ref[...]
        total = jnp.zeros(rest.shape, F32)
        for _ in range(3):
            piece = rest.astype(BF16)
            rest = rest - piece.astype(F32)
            ls_ref[...] = piece
            _transpose_rows(lt_ref, ls_ref, pm_ref, na, CLASSES)
            total = total + lt_ref[...].astype(F32)
        l_ref[...] = total.reshape(l_ref.shape)


def _attn_group(qkv, col0, phases, n_sub, lead_grid, lead_block, lead_index, ta, *, to_classes=False,
                out_arr_shape=None, out_block=None, out_index=None):
    gw = GROUP_WIDTH
    halo = ATT_BAND // phases
    rows = qkv.shape[-2]
    per = ta // halo
    nblk = rows // halo
    cb = col0 // gw

    def spec(nrows, ridx, part):
        return pl.BlockSpec(tuple(lead_block) + (nrows, gw),
                            lambda b, rho, i: tuple(lead_index(b, rho)) + (ridx(i), cb + part))

    main = lambda part: spec(ta, lambda i: i, part)
    prev = lambda part: spec(halo, lambda i: jnp.maximum(i * per - 1, 0), part)
    nxt = lambda part: spec(halo, lambda i: jnp.minimum((i + 1) * per, nblk - 1), part)
    bsz = qkv.shape[0]
    in_specs = [main(0), prev(1), main(1), nxt(1), prev(2), main(2), nxt(2)]
    args = [qkv] * 7
    if to_classes:
        assert ta == PERM_ROWS
        out_specs = [pl.BlockSpec(out_block + (gw,), out_index), pl.BlockSpec(out_block + (LANES,), out_index)]
        scratch_extra = [pltpu.VMEM((ta, gw), BF16)] * 2 + [pltpu.VMEM((ta, LANES), F32)] + \
                        [pltpu.VMEM((ta, LANES), BF16)] * 2
        in_specs.insert(0, pl.BlockSpec((PERM_ROWS, PERM_ROWS), lambda b, rho, i: (0, 0)))
        args.insert(0, _perm_for(ta // CLASSES, CLASSES))
    else:
        out_arr_shape = qkv.shape[:-1]
        oidx = lambda b, rho, i: tuple(lead_index(b, rho)) + (i, 0)
        out_specs = [pl.BlockSpec(tuple(lead_block) + (ta, gw), oidx),
                     pl.BlockSpec(tuple(lead_block) + (ta, LANES), oidx)]
        scratch_extra = []
    kx_shape = (phases, ta + 2 * halo, gw)
    return pl.pallas_call(
        functools.partial(_attn_kernel, phases=phases, ta=ta, n_sub=n_sub, to_classes=to_classes),
        grid=(bsz, lead_grid, rows // ta),
        in_specs=in_specs,
        out_specs=out_specs,
        out_shape=[jax.ShapeDtypeStruct(tuple(out_arr_shape) + (gw,), BF16),
                   jax.ShapeDtypeStruct(tuple(out_arr_shape) + (LANES,), F32)],
        scratch_shapes=[pltpu.VMEM(kx_shape, BF16)] * 2 + scratch_extra,
        compiler_params=_params(("parallel", "parallel", "parallel")),
        name="attn_p%d" % phases + ("_cls" if to_classes else ""),
    )(*args)


def _rope_tables(pos):
    half = ROT_DIM // 2
    inv = ROPE_THETA ** (-jnp.arange(0, ROT_DIM, 2, dtype=F32) / ROT_DIM)
    ang = pos.astype(F32)[:, None] * inv[None, :]
    cos, sin = jnp.cos(ang), jnp.sin(ang)
    n = pos.shape[0]
    rest = HEAD_DIM - ROT_DIM
    c = jnp.concatenate([cos, cos, jnp.ones((n, rest), F32)], axis=1)
    s = jnp.concatenate([sin, sin, jnp.zeros((n, rest), F32)], axis=1)
    return c, s


def _rope_partner_matrix(width):
    half = ROT_DIM // 2
    m = np.zeros((width, width), np.float32)
    for base in range(0, width, HEAD_DIM):
        for k in range(half):
            m[base + k + half, base + k] = -1.0
            m[base + k, base + k + half] = 1.0
    return jnp.asarray(m, BF16)


def _attn_mixer(x, g, shift, scale, gate, p, layer):
    bsz, seq_len, d = x.shape
    gw = GROUP_WIDTH
    nc = seq_len // CLASSES
    ca = ROW_TILE // CLASSES
    w_in = p["at_w_in"]
    tabs = _rope_tables(jnp.arange(seq_len))
    qkv0 = _norm_mm(
        x, pl.BlockSpec((None, ROW_TILE, d), lambda b, i, j: (b, i, 0)), None, ROW_TILE, seq_len // ROW_TILE,
        g, shift, scale, [w_in], layer, 0, 3 * gw, jax.ShapeDtypeStruct((bsz, seq_len, 3 * gw), BF16),
        pl.BlockSpec((None, ROW_TILE, 512), lambda b, i, j: (b, i, j)), mode="rope",
        rope=(tabs, pl.BlockSpec((ROW_TILE, HEAD_DIM), lambda b, i, j: (i, 0))))
    pos_c = (jnp.arange(nc)[None, :] * CLASSES + jnp.arange(CLASSES)[:, None]).reshape(-1)
    tabs_c = [t.reshape(CLASSES, nc, HEAD_DIM) for t in _rope_tables(pos_c)]
    qkv12 = _norm_mm(
        x.reshape(bsz, nc, CLASSES, d), pl.BlockSpec((None, ca, CLASSES, d), lambda b, i, j: (b, i, 0, 0)),
        (ca, CLASSES), ROW_TILE, nc // ca, g, shift, scale, [w_in], layer, 3 * gw, 6 * gw,
        jax.ShapeDtypeStruct((bsz, CLASSES, nc, 6 * gw), BF16),
        pl.BlockSpec((None, CLASSES, ca, 512), lambda b, i, j: (b, 0, i, j)), mode="rope",
        rope=(tabs_c, pl.BlockSpec((CLASSES, ca, HEAD_DIM), lambda b, i, j: (0, i, 0))))
    cls_shape = (bsz, CLASSES, nc)
    ta0 = PERM_ROWS
    o0, l0 = _attn_group(
        qkv0.reshape(bsz, 1, seq_len, 3 * gw), 0, 1, seq_len, 1, (None, 1), lambda b, rho: (b, 0), ta0,
        to_classes=True, out_arr_shape=cls_shape, out_block=(None, CLASSES, ta0 // CLASSES),
        out_index=lambda b, rho, i: (b, 0, i, 0))
    dil1 = ATTN_PATTERNS[1][1]
    ph = CLASSES // dil1
    ta1 = min(64, nc)
    o1, l1 = _attn_group(
        qkv12.reshape(bsz, ph, dil1, nc, 6 * gw), 0, ph, seq_len // dil1, dil1, (None, ph, None),
        lambda b, rho: (b, 0, rho), ta1)
    ta2 = min(256, nc)
    o2, l2 = _attn_group(
        qkv12.reshape(bsz, CLASSES, 1, nc, 6 * gw), 3 * gw, 1, nc, CLASSES, (None, None, 1),
        lambda b, rho: (b, rho, 0), ta2)
    os_ = [o.reshape(cls_shape + (gw,)) for o in (o0, o1, o2)]
    ls = [l.reshape(cls_shape + (LANES,)) for l in (l0, l1, l2)]
    blk = lambda width: pl.BlockSpec((None, CLASSES, ca, width), lambda b, i, j: (b, 0, i, 0))
    return _mm_res(os_ + ls, [blk(gw)] * 3 + [blk(LANES)] * 3, p["at_w_out"], layer, jnp.zeros((1, d), F32),
                   x, gate, ROW_TILE, a_mode="merge", perm=(CLASSES, ca))


def _ffn(x, g, shift, scale, gate, p, layer):
    bsz, seq_len, d = x.shape
    dff = p["ffn_w_gate"].shape[-1]
    tiles = seq_len // ROW_TILE
    xs = pl.BlockSpec((None, ROW_TILE, d), lambda b, i, j: (b, i, 0))
    hs = pl.BlockSpec((None, ROW_TILE, 512), lambda b, i, j: (b, i, j))
    hmid = _norm_mm(x, xs, None, ROW_TILE, tiles, g, shift, scale, [p["ffn_w_gate"], p["ffn_w_up"]], layer, 0,
                    dff, jax.ShapeDtypeStruct((bsz, seq_len, dff), BF16), hs, mode="swiglu")
    return _mm_res([hmid], [pl.BlockSpec((None, ROW_TILE, dff), lambda b, i, j: (b, i, 0))],
                   p["ffn_w_down"], layer, jnp.zeros((1, d), F32), x, gate, ROW_TILE, tn=256)


def _encoder(x, mods, final_mod, p):
    bsz, seq_len, d = x.shape
    n1 = 2 * seq_len // DFT_N2
    consts = _dft_consts(n1, DFT_N2)
    consts["d"] = d
    for i in range(DEPTH):
        sh_m, sc_m, g_m, sh_f, sc_f, g_f = [mods[i][:, None, k * d:(k + 1) * d] for k in range(6)]
        j = i // 2
        if i % 2 == 0:
            kf = _hyena_filter_spectra(seq_len, consts, p["hy_fw1"][j], p["hy_fb1"][j], p["hy_ffreq"][j],
                                       p["hy_fw2"][j], p["hy_fb2"][j], p["hy_fw3"][j], p["hy_decay"][j])
            x = _hyena_mixer(x, p["norm_mix"][i], sh_m, sc_m, g_m, p, j, kf, consts)
        else:
            x = _attn_mixer(x, p["norm_mix"][i], sh_m, sc_m, g_m, p, j)
        x = _ffn(x, p["norm_ffn"][i], sh_f, sc_f, g_f, p, i)
    sh, sc = final_mod[:, None, :d], final_mod[:, None, d:]
    return _final(x, p["final_norm"], sh, sc)


def kernel(x_prompt, x_sample, c_prompt, c_sample, ada_w, ada_b, norm_mix, norm_ffn, hy_w_in, hy_b_in, hy_conv_w, hy_conv_b, hy_fw1, hy_fb1, hy_ffreq, hy_fw2, hy_fb2, hy_fw3, hy_decay, hy_skip, hy_w_out, hy_b_out, at_w_in, at_w_out, ffn_w_gate, ffn_w_up, ffn_w_down, final_norm, final_ada_w, final_ada_b):
    d = x_prompt.shape[-1]
    bp, bs = c_prompt.shape[0], c_sample.shape[0]
    pad = -(bp + bs) % (2 * SUBLANES)
    c_all = jnp.concatenate([c_prompt, c_sample, jnp.zeros((pad, d), F32)], axis=0)
    mods = _ada(c_all, ada_w, ada_b)
    fmod = _ada(c_all, final_ada_w[None], final_ada_b[None])[0]
    p = dict(norm_mix=norm_mix, norm_ffn=norm_ffn,
             hy_w_in=hy_w_in.astype(BF16), hy_b_in=hy_b_in, hy_conv_w=hy_conv_w, hy_conv_b=hy_conv_b,
             hy_fw1=hy_fw1, hy_fb1=hy_fb1, hy_ffreq=hy_ffreq, hy_fw2=hy_fw2, hy_fb2=hy_fb2, hy_fw3=hy_fw3,
             hy_decay=hy_decay, hy_skip=hy_skip, hy_w_out=hy_w_out.astype(BF16), hy_b_out=hy_b_out,
             at_w_in=at_w_in.astype(BF16), at_w_out=at_w_out.astype(BF16),
             ffn_w_gate=ffn_w_gate.astype(BF16), ffn_w_up=ffn_w_up.astype(BF16),
             ffn_w_down=ffn_w_down.astype(BF16), final_norm=final_norm)
    y_prompt = _encoder(x_prompt, mods[:, :bp], fmod[:bp], p)
    y_sample = _encoder(x_sample, mods[:, bp:bp + bs], fmod[bp:bp + bs], p)
    return (y_prompt, y_sample)
```

```python
import functools
import math

import numpy as np
import jax
import jax.numpy as jnp
from jax import lax
from jax.experimental import pallas as pl
from jax.experimental.pallas import tpu as pltpu

F32 = jnp.float32
BF16 = jnp.bfloat16
EPS = 1e-6

DEPTH = 4
HYENA_ORDER = 2
N_DIRS = 2
FILTER_BANDS = 16
FILTER_EMB = 1 + 2 * FILTER_BANDS
ATTN_PATTERNS = ((128, 1), (512, 4), (2048, 16))
HEADS_PER_GROUP = 8
HEAD_DIM = 128
GROUP_WIDTH = HEADS_PER_GROUP * HEAD_DIM
ROT_DIM = HEAD_DIM // 4
ROPE_THETA = 500000.0

LANES = 128
SUBLANES = 8
VMEM_LIMIT_BYTES = 56 * 1024 * 1024

DFT_N2 = 256
SLAB_ROWS = DFT_N2 // 2
ATT_BAND = 64
ATT_Q = 2 * ATT_BAND
CLASSES = 16
LSE_LANES = LANES // HEADS_PER_GROUP
ROW_TILE = 1024
NEG_BIG = -1e30


def _params(sem):
    return pltpu.CompilerParams(dimension_semantics=sem, vmem_limit_bytes=VMEM_LIMIT_BYTES)


def _dot(a, b):
    return jnp.dot(a, b, preferred_element_type=F32)


def _split(a):
    hi = a.astype(BF16)
    lo = (a - hi.astype(F32)).astype(BF16)
    return hi, lo


def _dot3(a, b):
    ah, al = _split(a)
    bh, bl = _split(b)
    return _dot(ah, bh) + _dot(al, bh) + _dot(ah, bl)


def _modnorm(x, g, shift, scale):
    ms = jnp.mean(x * x, axis=-1, keepdims=True)
    return (x * lax.rsqrt(ms + EPS)) * (g * (1.0 + scale)) + shift


PERM_ROWS = 256


def _perm_matrix(p, q):
    m = np.zeros((PERM_ROWS, PERM_ROWS), np.float32)
    pi, qi = np.meshgrid(np.arange(p), np.arange(q), indexing="ij")
    m[(qi * p + pi).ravel(), (pi * q + qi).ravel()] = 1.0
    return jnp.asarray(m, BF16)


def _perm_for(p, q):
    assert (q <= 16 and p % (PERM_ROWS // q) == 0) or (p <= 16 and q % (PERM_ROWS // p) == 0), (p, q)
    return _perm_matrix(PERM_ROWS // q, q) if q <= 16 else _perm_matrix(p, PERM_ROWS // p)


def _transpose_rows(dst_ref, src_ref, pm_ref, p, q):
    if q <= 16:
        pg = PERM_ROWS // q
        for grp in range(p // pg):
            t = _dot(pm_ref[...], src_ref[grp * PERM_ROWS:(grp + 1) * PERM_ROWS, :]).astype(BF16)
            for qi in range(q):
                dst_ref[qi * p + grp * pg:qi * p + (grp + 1) * pg, :] = t[qi * pg:(qi + 1) * pg]
    else:
        qg = PERM_ROWS // p
        for grp in range(q // qg):
            blk = jnp.concatenate([src_ref[pi * q + grp * qg:pi * q + (grp + 1) * qg, :] for pi in range(p)],
                                  axis=0)
            dst_ref[grp * PERM_ROWS:(grp + 1) * PERM_ROWS, :] = _dot(pm_ref[...], blk).astype(BF16)


def _ada_kernel(c_ref, w_ref, b_ref, o_ref):
    c = c_ref[...]
    cs = c * jax.nn.sigmoid(c)
    o_ref[...] = _dot3(cs, w_ref[...]) + b_ref[...]


def _ada(c_all, w, b, tn=1024):
    nl, d, no = w.shape
    r = c_all.shape[0]
    return pl.pallas_call(
        _ada_kernel,
        grid=(nl, no // tn),
        in_specs=[
            pl.BlockSpec((r, d), lambda l, j: (0, 0)),
            pl.BlockSpec((None, d, tn), lambda l, j: (l, 0, j)),
            pl.BlockSpec((None, 1, tn), lambda l, j: (l, 0, j)),
        ],
        out_specs=pl.BlockSpec((None, r, tn), lambda l, j: (l, 0, j)),
        out_shape=jax.ShapeDtypeStruct((nl, r, no), F32),
        compiler_params=_params(("parallel", "parallel")),
        name="ada_mod",
    )(c_all, w, b.reshape(nl, 1, no))


def _norm_mm_kernel(*refs, mode, tn, perm):
    if perm:
        pm_ref, refs = refs[0], refs[1:]
        h0_ref, refs = refs[-1], refs[:-1]
    if mode == "swiglu":
        x_ref, g_ref, sh_ref, sc_ref, wg_ref, wu_ref, o_ref, h_ref = refs
    elif mode == "rope":
        x_ref, g_ref, sh_ref, sc_ref, w_ref, c_ref, s_ref, rot_ref, o_ref, h_ref = refs
    else:
        x_ref, g_ref, sh_ref, sc_ref, w_ref, b_ref, o_ref, h_ref = refs
    j = pl.program_id(2)

    @pl.when(j == 0)
    def _():
        h = _modnorm(x_ref[...].reshape(h_ref.shape), g_ref[...], sh_ref[...], sc_ref[...]).astype(BF16)
        if perm:
            h0_ref[...] = h
            _transpose_rows(h_ref, h0_ref, pm_ref, *perm)
        else:
            h_ref[...] = h

    h = h_ref[...]
    if mode == "swiglu":
        a = _dot(h, wg_ref[...])
        u = _dot(h, wu_ref[...])
        o_ref[...] = (a * jax.nn.sigmoid(a) * u).astype(o_ref.dtype).reshape(o_ref.shape)
    elif mode == "rope":
        acc = _dot(h, w_ref[...])
        part = (j // (GROUP_WIDTH // tn)) % 3

        @pl.when(part == 2)
        def _():
            o_ref[...] = acc.astype(o_ref.dtype).reshape(o_ref.shape)

        @pl.when(part != 2)
        def _():
            reps = tn // HEAD_DIM
            tabs = [t[...].reshape(acc.shape[0], HEAD_DIM) for t in (c_ref, s_ref)]
            c, s = [jnp.concatenate([t] * reps, axis=1) for t in tabs]
            partner = _dot(acc.astype(BF16), rot_ref[...])
            qs = jnp.where(part == 0, HEAD_DIM ** -0.5, 1.0).astype(F32)
            o_ref[...] = ((acc * c + partner * s) * qs).astype(o_ref.dtype).reshape(o_ref.shape)
    else:
        o_ref[...] = (_dot(h, w_ref[...]) + b_ref[...]).astype(o_ref.dtype).reshape(o_ref.shape)


def _norm_mm(x, x_spec, perm, rows, grid_rows, g, shift, scale, ws, w_layer, col0, nout, out_shape,
             out_spec, *, mode, bias=None, rope=None, tn=512):
    d = x.shape[-1]
    bsz = x.shape[0]
    cb = col0 // tn
    vec = pl.BlockSpec((None, 1, d), lambda b, i, j: (b, 0, 0))
    in_specs = [x_spec, pl.BlockSpec((1, d), lambda b, i, j: (0, 0)), vec, vec]
    in_specs += [pl.BlockSpec((None, d, tn), lambda b, i, j: (w_layer, 0, cb + j)) for _ in ws]
    args = [x, g.reshape(1, d), shift, scale, *ws]
    scratch = [pltpu.VMEM((rows, d), BF16)]
    if perm:
        in_specs.insert(0, pl.BlockSpec((PERM_ROWS, PERM_ROWS), lambda b, i, j: (0, 0)))
        args.insert(0, _perm_for(*perm))
        scratch.append(pltpu.VMEM((rows, d), BF16))
    if mode == "rope":
        tabs, tab_spec = rope
        in_specs += [tab_spec] * 2 + [pl.BlockSpec((tn, tn), lambda b, i, j: (0, 0))]
        args += list(tabs) + [_rope_partner_matrix(tn)]
    elif mode == "bias":
        in_specs.append(pl.BlockSpec((1, tn), lambda b, i, j: (0, cb + j)))
        args.append(bias)
    return pl.pallas_call(
        functools.partial(_norm_mm_kernel, mode=mode, tn=tn, perm=perm),
        grid=(bsz, grid_rows, nout // tn),
        in_specs=in_specs,
        out_specs=out_spec,
        out_shape=out_shape,
        scratch_shapes=scratch,
        compiler_params=_params(("parallel", "parallel", "arbitrary")),
        name="norm_mm_" + mode,
    )(*args)


def _mm_res_kernel(*refs, a_mode, perm):
    if a_mode == "merge":
        (pm_ref, o0, o1, o2, l0, l1, l2, e_ref, w_ref, b_ref, x_ref, gt_ref, out_ref, a_ref, a0_ref) = refs
    elif a_mode == "perm":
        pm_ref, a_in, w_ref, b_ref, x_ref, gt_ref, out_ref, a_ref, a0_ref = refs
    else:
        a_in, w_ref, b_ref, x_ref, gt_ref, out_ref = refs
    j = pl.program_id(2)

    if a_mode != "plain":
        @pl.when(j == 0)
        def _():
            k = a_ref.shape[1]
            if a_mode == "merge":
                ls = [l[...].reshape(-1, LANES) for l in (l0, l1, l2)]
                mx = jnp.maximum(jnp.maximum(ls[0], ls[1]), ls[2])
                ws = [jnp.exp(l - mx) for l in ls]
                inv = 1.0 / (ws[0] + ws[1] + ws[2])
                num = jnp.zeros((ls[0].shape[0], k), F32)
                for w, o in zip(ws, (o0, o1, o2)):
                    hi, lo = _split(w * inv)
                    wide = _dot(hi, e_ref[...]) + _dot(lo, e_ref[...])
                    num = num + wide * o[...].reshape(-1, k).astype(F32)
                a0_ref[...] = num.astype(BF16)
            else:
                a0_ref[...] = a_in[...].reshape(-1, k).astype(BF16)
            _transpose_rows(a_ref, a0_ref, pm_ref, *perm)

        a = a_ref[...]
    else:
        a = a_in[...]
    out_ref[...] = x_ref[...] + gt_ref[...] * (_dot(a, w_ref[...]) + b_ref[...])


def _head_spread_matrix(width):
    m = np.zeros((LANES, width), np.float32)
    for hd in range(width // HEAD_DIM):
        m[hd * LSE_LANES, hd * HEAD_DIM:(hd + 1) * HEAD_DIM] = 1.0
    return jnp.asarray(m, BF16)


def _mm_res(a_list, a_specs, w, w_layer, bias, x, gate, rows, *, a_mode="plain", perm=None, tn=512):
    bsz, seq_len, d = x.shape
    k = w.shape[1]
    blk = pl.BlockSpec((None, rows, tn), lambda b, i, j: (b, i, j))
    if a_mode == "merge":
        a_list = list(a_list) + [_head_spread_matrix(k)]
        a_specs = list(a_specs) + [pl.BlockSpec((LANES, k), lambda b, i, j: (0, 0))]
    in_specs = list(a_specs) + [
        pl.BlockSpec((None, k, tn), lambda b, i, j: (w_layer, 0, j)),
        pl.BlockSpec((1, tn), lambda b, i, j: (0, j)),
        blk,
        pl.BlockSpec((None, 1, tn), lambda b, i, j: (b, 0, j)),
    ]
    args = [*a_list, w, bias, x, gate]
    scratch = []
    if a_mode != "plain":
        in_specs.insert(0, pl.BlockSpec((PERM_ROWS, PERM_ROWS), lambda b, i, j: (0, 0)))
        args.insert(0, _perm_for(*perm))
        scratch = [pltpu.VMEM((rows, k), BF16)] * 2
    return pl.pallas_call(
        functools.partial(_mm_res_kernel, a_mode=a_mode, perm=perm),
        grid=(bsz, seq_len // rows, d // tn),
        in_specs=in_specs,
        out_specs=blk,
        out_shape=jax.ShapeDtypeStruct(x.shape, F32),
        scratch_shapes=scratch,
        compiler_params=_params(("parallel", "parallel", "arbitrary")),
        name="mm_res_" + a_mode,
    )(*args)


def _final_kernel(x_ref, g_ref, sh_ref, sc_ref, o_ref):
    o_ref[...] = _modnorm(x_ref[...], g_ref[...], sh_ref[...], sc_ref[...])


def _final(x, g, shift, scale, tm=512):
    bsz, seq_len, d = x.shape
    vec = pl.BlockSpec((None, 1, d), lambda b, i: (b, 0, 0))
    blk = pl.BlockSpec((None, tm, d), lambda b, i: (b, i, 0))
    return pl.pallas_call(
        _final_kernel,
        grid=(bsz, seq_len // tm),
        in_specs=[blk, pl.BlockSpec((1, d), lambda b, i: (0, 0)), vec, vec],
        out_specs=blk,
        out_shape=jax.ShapeDtypeStruct(x.shape, F32),
        compiler_params=_params(("parallel", "parallel")),
        name="final_norm",
    )(x, g.reshape(1, d), shift, scale)


def _filter_positions(seq_len, n1):
    n = 2 * seq_len
    h = SLAB_ROWS
    half = jnp.arange(2)[:, None, None]
    s = jnp.arange(n1)[None, :, None]
    r = jnp.arange(h)[None, None, :]
    idx = ((half * h + r) * n1 + s).reshape(n)
    pos = jnp.where(idx < seq_len, idx, n - idx).astype(F32)
    sign = jnp.where(idx < seq_len, 1.0, jnp.where(idx == seq_len, 0.0, -1.0)).astype(F32)
    t = pos / max(seq_len - 1, 1)
    bands = jnp.linspace(1e-4, FILTER_BANDS - 1, FILTER_BANDS, dtype=F32)
    ang = 2.0 * math.pi * pos[:, None] * bands[None, :] / seq_len
    z = jnp.concatenate([t[:, None], jnp.cos(ang), -jnp.sin(ang)], axis=-1)
    z = jnp.pad(z, ((0, 0), (0, LANES - FILTER_EMB - 1)))
    return jnp.concatenate([z, sign[:, None]], axis=-1)


def _filter_mlp_kernel(z_ref, w1_ref, b1_ref, f_ref, w2_ref, b2_ref, o_ref):
    f = f_ref[...]
    a = jnp.sin(f[0:1, :] * (_dot3(z_ref[...], w1_ref[...]) + b1_ref[...]))
    o_ref[...] = jnp.sin(f[1:2, :] * (_dot3(a, w2_ref[...]) + b2_ref[...]))


def _filter_mlp(zf, w1, b1, freq, w2, b2, tr=512):
    n = zf.shape[0]
    hid = w1.shape[1]
    w1p = jnp.pad(w1, ((0, LANES - w1.shape[0]), (0, 0)))
    full = lambda shape: pl.BlockSpec(shape, lambda i: (0,) * len(shape))
    return pl.pallas_call(
        _filter_mlp_kernel,
        grid=(n // tr,),
        in_specs=[pl.BlockSpec((tr, LANES), lambda i: (i, 0)), full((LANES, hid)), full((1, hid)),
                  full((2, hid)), full((hid, hid)), full((1, hid))],
        out_specs=pl.BlockSpec((tr, hid), lambda i: (i, 0)),
        out_shape=jax.ShapeDtypeStruct((n, hid), F32),
        compiler_params=_params(("parallel",)),
        name="filter_mlp",
    )(zf, w1p, b1.reshape(1, hid), freq, w2, b2.reshape(1, hid))


def _dft_consts(n1, n2):
    n = n1 * n2
    k2 = np.arange(n2 // 2)[:, None]
    nn2 = np.arange(n2)[None, :]
    ph = 2.0 * np.pi * nn2 * (k2 + 0.5) / n2
    f_s1 = np.concatenate([np.cos(ph), -np.sin(ph)], axis=0)
    m = np.arange(n2 // 2)[:, None]
    kk = np.arange(n2 // 2)[None, :]
    ph3 = 2.0 * np.pi * m * (kk + 0.5) / n2
    f_s3 = (2.0 / n) * np.concatenate([np.cos(ph3), -np.sin(ph3)], axis=1)
    a = 2.0 * np.pi * np.outer(np.arange(n1), np.arange(n1)) / n1
    c, s = np.cos(a), -np.sin(a)
    f_fwd = np.block([[c, -s], [s, c]])
    f_inv = np.block([[c, s], [-s, c]])
    th = 2.0 * np.pi * (np.arange(n2 // 2)[:, None] + 0.5) * np.arange(n1)[None, :] / n
    as_bf = lambda x: jnp.asarray(x, F32).astype(BF16)
    tw_c = jnp.broadcast_to(jnp.asarray(np.cos(th), F32)[:, :, None], (n2 // 2, n1, LANES))
    tw_s = jnp.broadcast_to(jnp.asarray(np.sin(th), F32)[:, :, None], (n2 // 2, n1, LANES))
    tc, ts = jnp.asarray(np.cos(th), F32), jnp.asarray(np.sin(th), F32)
    ff, fi = jnp.asarray(f_fwd, F32), jnp.asarray(f_inv, F32)
    fl, fr = ff[None, :, :n1], ff[None, :, n1:]
    g_fwd = jnp.concatenate([fl * tc[:, None, :] - fr * ts[:, None, :],
                             fl * ts[:, None, :] + fr * tc[:, None, :]], axis=2).astype(BF16)
    it, ib = fi[None, :n1, :], fi[None, n1:, :]
    g_inv = jnp.concatenate([tc[:, :, None] * it - ts[:, :, None] * ib,
                             ts[:, :, None] * it + tc[:, :, None] * ib], axis=1).astype(BF16)
    return dict(f_s1=as_bf(f_s1[:, :n2 // 2]), f_s3=as_bf(f_s3), f_fwd=as_bf(f_fwd), f_inv=as_bf(f_inv),
                f_s1_f32=jnp.asarray(f_s1, F32), f_fwd_f32=jnp.asarray(f_fwd, F32),
                tw_c=tw_c, tw_s=tw_s, g_fwd=g_fwd, g_inv=g_inv, n1=n1)


def _slab_tile(n1):
    return min(n1, 8)


HI16 = -65536
HALF_ULP16 = 0x8000


def _pack_c(re, im):
    rb = lax.bitcast_convert_type(re, jnp.int32) + HALF_ULP16
    ib = lax.bitcast_convert_type(im, jnp.int32) + HALF_ULP16
    return (rb & HI16) | lax.shift_right_logical(ib, 16)


def _pack_bf16(re, im):
    rb = lax.bitcast_convert_type(re.astype(F32), jnp.int32)
    ib = lax.bitcast_convert_type(im.astype(F32), jnp.int32)
    return rb | lax.shift_right_logical(ib, 16)


def _unpack_c(p):
    re = lax.bitcast_convert_type(p & HI16, F32)
    im = lax.bitcast_convert_type(lax.shift_left(p, 16), F32)
    return re, im


def _shift_rows(x, down):
    rows = x.shape[0]
    row = lax.broadcasted_iota(jnp.int32, x.shape, 0)
    if down:
        return jnp.where(row == 0, 0.0, pltpu.roll(x, 1, 0))
    return jnp.where(row == rows - 1, 0.0, pltpu.roll(x, rows - 1, 0))


def _short_conv_slabs(main_ref, prev_ref, next_ref, w_ref, b_ref, first, last):
    n_slabs = main_ref.shape[0]
    prev = prev_ref[0].astype(F32)
    prev = jnp.where(first, _shift_rows(prev, True), prev)
    nxt = next_ref[0].astype(F32)
    nxt = jnp.where(last, _shift_rows(nxt, False), nxt)
    w = w_ref[...]
    out = []
    for s in range(n_slabs):
        up = prev if s == 0 else main_ref[s - 1].astype(F32)
        dn = nxt if s == n_slabs - 1 else main_ref[s + 1].astype(F32)
        out.append(up * w[0:1, :] + main_ref[s].astype(F32) * w[1:2, :] + dn * w[2:3, :] + b_ref[...])
    return out


def _fft_s1_kernel(*refs, short_conv):
    re_ref, im_ref, ret_ref, imt_ref = refs[-4:]
    if short_conv:
        pm_ref, f_ref, m_ref, p_ref, n_ref, w_ref, b_ref, o_ref, u_ref = refs[:-4]
        t = pl.program_id(1)
        slabs = _short_conv_slabs(m_ref, p_ref, n_ref, w_ref, b_ref, t == 0, t == pl.num_programs(1) - 1)
    else:
        pm_ref, f_ref, m_ref, o_ref = refs[:-4]
        slabs = [m_ref[s] for s in range(m_ref.shape[0])]
    half = o_ref.shape[0]
    for s, u in enumerate(slabs):
        if short_conv:
            u_ref[s] = u.astype(u_ref.dtype)
        r = _dot(f_ref[...], u.astype(BF16))
        re_ref[s * half:(s + 1) * half, :] = r[:half].astype(BF16)
        im_ref[s * half:(s + 1) * half, :] = r[half:].astype(BF16)
    _transpose_rows(ret_ref, re_ref, pm_ref, len(slabs), half)
    _transpose_rows(imt_ref, im_ref, pm_ref, len(slabs), half)
    o_ref[...] = _pack_bf16(ret_ref[...], imt_ref[...]).reshape(o_ref.shape)


def _fft_s1(consts, src, col_block, conv=None, ct=512):
    bsz, n1, h, c = src.shape
    d = consts["d"]
    st = _slab_tile(n1)
    cpb = d // ct
    f = consts["f_s1"]
    main = pl.BlockSpec((None, st, h, ct), lambda b, t, j: (b, t, 0, col_block * cpb + j))
    a_spec = pl.BlockSpec((None, h, st, ct), lambda b, t, j: (b, 0, t, j))
    a_shape = jax.ShapeDtypeStruct((bsz, h, n1, d), jnp.int32)
    fspec = pl.BlockSpec(f.shape, lambda b, t, j: (0, 0))
    pspec = pl.BlockSpec((PERM_ROWS, PERM_ROWS), lambda b, t, j: (0, 0))
    pm = _perm_for(st, h)
    scratch = [pltpu.VMEM((st * h, ct), BF16)] * 4
    if conv is None:
        return pl.pallas_call(
            functools.partial(_fft_s1_kernel, short_conv=False),
            grid=(bsz, n1 // st, cpb),
            in_specs=[pspec, fspec, main], out_specs=a_spec, out_shape=a_shape, scratch_shapes=scratch,
            compiler_params=_params(("parallel", "parallel", "parallel")), name="fft_s1",
        )(pm, f, src)
    w, b = conv
    prev = pl.BlockSpec((None, 1, h, ct), lambda b, t, j: (b, (t * st + n1 - 1) % n1, 0, col_block * cpb + j))
    nxt = pl.BlockSpec((None, 1, h, ct), lambda b, t, j: (b, ((t + 1) * st) % n1, 0, col_block * cpb + j))
    wspec = pl.BlockSpec((3, ct), lambda b, t, j: (0, col_block * cpb + j))
    bspec = pl.BlockSpec((1, ct), lambda b, t, j: (0, col_block * cpb + j))
    u_spec = pl.BlockSpec((None, st, h, ct), lambda b, t, j: (b, t, 0, j))
    return pl.pallas_call(
        functools.partial(_fft_s1_kernel, short_conv=True),
        grid=(bsz, n1 // st, cpb),
        in_specs=[pspec, fspec, main, prev, nxt, wspec, bspec],
        out_specs=[a_spec, u_spec],
        out_shape=[a_shape, jax.ShapeDtypeStruct((bsz, n1, h, d), BF16)],
        scratch_shapes=scratch,
        compiler_params=_params(("parallel", "parallel", "parallel")), name="fft_s1_conv",
    )(pm, f, src, src, src, w, b)


def _filter_s1_kernel(flo_ref, fhi_ref, alo_ref, ahi_ref, zlo_ref, zhi_ref, wf_ref, wb_ref, df_ref, db_ref,
                      o_ref, ss_ref):
    half = o_ref.shape[1]

    @pl.when(pl.program_id(2) == 0)
    def _():
        ss_ref[...] = jnp.zeros_like(ss_ref)

    ss = jnp.zeros(ss_ref.shape, F32)
    for s in range(alo_ref.shape[0]):
        taps = []
        for a_ref, z_ref, w_ref, d_ref in ((alo_ref, zlo_ref, wf_ref, df_ref), (ahi_ref, zhi_ref, wb_ref, db_ref)):
            z = z_ref[s]
            t, sign = z[:, 0:1], z[:, LANES - 1:LANES]
            h = _dot3(a_ref[s], w_ref[...]) * jnp.exp(-t * jnp.abs(d_ref[...])) * sign
            ss = ss + jnp.sum(h * h, axis=0, keepdims=True)
            taps.append(h)
        r = _dot3(flo_ref[...], taps[0]) + _dot3(fhi_ref[...], taps[1])
        o_ref[0, :, s, :] = r[:half]
        o_ref[1, :, s, :] = r[half:]
    ss_ref[...] += ss


def _filter_s1(consts, a2, zf, w3, decay, ct=256):
    n1, h = consts["n1"], SLAB_ROWS
    hid = a2.shape[1]
    d = decay.shape[-1]
    nct = d // ct
    st = _slab_tile(n1)
    f = consts["f_s1_f32"]
    flo, fhi = f[:, :h], f[:, h:]
    dec = decay.reshape(1, HYENA_ORDER * N_DIRS * d)
    fspec = pl.BlockSpec(flo.shape, lambda o, j, t: (0, 0))
    rows = lambda half, width: pl.BlockSpec((None, st, h, width), lambda o, j, t: (half, t, 0, 0))
    wcol = lambda dirn, nrow: pl.BlockSpec((nrow, ct), lambda o, j, t: (0, (o * N_DIRS + dirn) * nct + j))
    return pl.pallas_call(
        _filter_s1_kernel,
        grid=(HYENA_ORDER, nct, n1 // st),
        in_specs=[fspec, fspec, rows(0, hid), rows(1, hid), rows(0, LANES), rows(1, LANES),
                  wcol(0, hid), wcol(1, hid), wcol(0, 1), wcol(1, 1)],
        out_specs=[pl.BlockSpec((None, 2, h, st, ct), lambda o, j, t: (o, 0, 0, t, j)),
                   pl.BlockSpec((None, 1, ct), lambda o, j, t: (o, 0, j))],
        out_shape=[jax.ShapeDtypeStruct((HYENA_ORDER, 2, h, n1, d), F32),
                   jax.ShapeDtypeStruct((HYENA_ORDER, 1, d), F32)],
        compiler_params=_params(("parallel", "parallel", "arbitrary")), name="filter_s1",
    )(flo, fhi, a2.reshape(2, n1, h, hid), a2.reshape(2, n1, h, hid), zf.reshape(2, n1, h, LANES),
      zf.reshape(2, n1, h, LANES), w3, w3, dec, dec)


def _tile_lanes(x, ct):
    return jnp.concatenate([x] * (ct // LANES), axis=-1) if ct > LANES else x


def _fft_s2f_kernel(a_ref, c_ref, s_ref, ff_ref, ss_ref, o_ref, *, kb, ct):
    n1 = a_ref.shape[2]
    scale = lax.rsqrt(ss_ref[...] + EPS)

    def body(kk, carry):
        ar, ai = a_ref[0, kk], a_ref[1, kk]
        c = _tile_lanes(c_ref[kk], ct)
        s = _tile_lanes(s_ref[kk], ct)
        br = ar * c + ai * s
        bi = ai * c - ar * s
        x = _dot3(ff_ref[...], jnp.concatenate([br, bi], axis=0))
        o_ref[kk] = _pack_c(x[:n1] * scale, x[n1:] * scale)
        return carry

    lax.fori_loop(0, kb, body, 0, unroll=min(kb, 4))


def _fft_s2_kernel(a_ref, k_ref, gf_ref, gi_ref, o_ref, *, kb):
    n1 = a_ref.shape[1]

    def body(kk, carry):
        ar, ai = _unpack_c(a_ref[kk])
        x = _dot(gf_ref[kk], jnp.concatenate([ar, ai], axis=0).astype(BF16))
        xr, xi = x[:n1], x[n1:]
        kr, ki = _unpack_c(k_ref[kk])
        zr = xr * kr - xi * ki
        zi = xr * ki + xi * kr
        y = _dot(gi_ref[kk], jnp.concatenate([zr, zi], axis=0).astype(BF16))
        o_ref[kk] = _pack_c(y[:n1], y[n1:])
        return carry

    lax.fori_loop(0, kb, body, 0, unroll=min(kb, 4))


def _s2_tiles(n1, d):
    kb = max(1, 512 // n1)
    ct = min(d, 512)
    return kb, ct


def _fft_s2f(a, sumsq, consts):
    n_o, _, k2n, n1, d = a.shape
    kb, ct = _s2_tiles(n1, d)
    blk = pl.BlockSpec((None, 2, kb, n1, ct), lambda k, j, o: (o, 0, k, 0, j))
    tw = pl.BlockSpec((kb, n1, LANES), lambda k, j, o: (k, 0, 0))
    return pl.pallas_call(
        functools.partial(_fft_s2f_kernel, kb=kb, ct=ct),
        grid=(k2n // kb, d // ct, n_o),
        in_specs=[blk, tw, tw, pl.BlockSpec((2 * n1, 2 * n1), lambda k, j, o: (0, 0)),
                  pl.BlockSpec((None, 1, ct), lambda k, j, o: (o, 0, j))],
        out_specs=pl.BlockSpec((None, kb, n1, ct), lambda k, j, o: (o, k, 0, j)),
        out_shape=jax.ShapeDtypeStruct((n_o, k2n, n1, d), jnp.int32),
        compiler_params=_params(("parallel", "parallel", "parallel")),
        name="fft_s2_filter",
    )(a, consts["tw_c"], consts["tw_s"], consts["f_fwd_f32"], sumsq)


def _fft_s2(a, kf, order, consts):
    bsz, k2n, n1, d = a.shape
    kb, ct = _s2_tiles(n1, d)
    blk = pl.BlockSpec((None, kb, n1, ct), lambda k, j, b: (b, k, 0, j))
    mat = pl.BlockSpec((kb, 2 * n1, 2 * n1), lambda k, j, b: (k, 0, 0))
    return pl.pallas_call(
        functools.partial(_fft_s2_kernel, kb=kb),
        grid=(k2n // kb, d // ct, bsz),
        in_specs=[blk, pl.BlockSpec((None, kb, n1, ct), lambda k, j, b: (order, k, 0, j)), mat, mat],
        out_specs=blk,
        out_shape=jax.ShapeDtypeStruct(a.shape, jnp.int32),
        compiler_params=_params(("parallel", "parallel", "parallel")),
        name="fft_s2",
    )(a, kf, consts["g_fwd"], consts["g_inv"])


def _fft_s3_kernel(pm_ref, f_ref, t_ref, u_ref, gm_ref, gp_ref, gn_ref, w_ref, b_ref, sk_ref, o_ref,
                   re_ref, im_ref, ret_ref, imt_ref):
    t_id = pl.program_id(1)
    half, st = t_ref.shape[0], t_ref.shape[1]
    re, im = _unpack_c(t_ref[...].reshape(half * st, t_ref.shape[2]))
    re_ref[...] = re.astype(BF16)
    im_ref[...] = im.astype(BF16)
    _transpose_rows(ret_ref, re_ref, pm_ref, half, st)
    _transpose_rows(imt_ref, im_ref, pm_ref, half, st)
    gates = _short_conv_slabs(gm_ref, gp_ref, gn_ref, w_ref, b_ref, t_id == 0, t_id == pl.num_programs(1) - 1)
    for s, gate in enumerate(gates):
        t = jnp.concatenate([ret_ref[s * half:(s + 1) * half, :], imt_ref[s * half:(s + 1) * half, :]], axis=0)
        y = _dot(f_ref[...], t)
        o_ref[s] = (gate * (y + u_ref[s].astype(F32) * sk_ref[...])).astype(o_ref.dtype)


def _fft_s3(consts, t, u, z, gate_block, conv_w, conv_b, skip, order, out_dtype, ct=512):
    bsz, h, n1, d = t.shape
    st = _slab_tile(n1)
    cpb = d // ct
    f = consts["f_s3"]
    gcol = lambda j: gate_block * cpb + j
    slab = lambda idx: pl.BlockSpec((None, 1, h, ct), lambda b, tt, j: (b, idx(tt), 0, gcol(j)))
    return pl.pallas_call(
        _fft_s3_kernel,
        grid=(bsz, n1 // st, cpb),
        scratch_shapes=[pltpu.VMEM((st * h, ct), BF16)] * 4,
        in_specs=[
            pl.BlockSpec((PERM_ROWS, PERM_ROWS), lambda b, tt, j: (0, 0)),
            pl.BlockSpec(f.shape, lambda b, tt, j: (0, 0)),
            pl.BlockSpec((None, h, st, ct), lambda b, tt, j: (b, 0, tt, j)),
            pl.BlockSpec((None, st, h, ct), lambda b, tt, j: (b, tt, 0, j)),
            pl.BlockSpec((None, st, h, ct), lambda b, tt, j: (b, tt, 0, gcol(j))),
            slab(lambda tt: (tt * st + n1 - 1) % n1),
            slab(lambda tt: ((tt + 1) * st) % n1),
            pl.BlockSpec((3, ct), lambda b, tt, j: (0, gcol(j))),
            pl.BlockSpec((1, ct), lambda b, tt, j: (0, gcol(j))),
            pl.BlockSpec((None, 1, ct), lambda b, tt, j: (order, 0, j)),
        ],
        out_specs=pl.BlockSpec((None, st, h, ct), lambda b, tt, j: (b, tt, 0, j)),
        out_shape=jax.ShapeDtypeStruct((bsz, n1, h, d), out_dtype),
        compiler_params=_params(("parallel", "parallel", "parallel")),
        name="fft_s3",
    )(_perm_for(h, st), f, t, u, z, z, z, conv_w, conv_b, skip)


def _hyena_filter_spectra(seq_len, consts, fw1, fb1, ffreq, fw2, fb2, fw3, decay):
    d = decay.shape[-1]
    n1 = consts["n1"]
    zf = _filter_positions(seq_len, n1)
    a2 = _filter_mlp(zf, fw1, fb1, ffreq, fw2, fb2)
    a, sumsq = _filter_s1(consts, a2, zf, fw3, decay)
    return _fft_s2f(a, sumsq, consts)


def _hyena_mixer(x, g, shift, scale, gate, p, layer, kf, consts):
    bsz, seq_len, d = x.shape
    n1 = consts["n1"]
    h = SLAB_ROWS
    st = _slab_tile(n1)
    xv = x.reshape(bsz, h, n1, d)
    z = _norm_mm(
        xv, pl.BlockSpec((None, h, st, d), lambda b, i, j: (b, 0, i, 0)), (h, st), st * h, n1 // st, g, shift, scale,
        [p["hy_w_in"]], layer, 0, 3 * d, jax.ShapeDtypeStruct((bsz, n1, h, 3 * d), BF16),
        pl.BlockSpec((None, st, h, 512), lambda b, i, j: (b, i, 0, j)),
        mode="bias", bias=p["hy_b_in"][layer].reshape(1, 3 * d))
    cw, cb = p["hy_conv_w"][layer], p["hy_conv_b"][layer].reshape(1, 3 * d)
    skip = p["hy_skip"][layer].reshape(HYENA_ORDER, 1, d)
    a, u = _fft_s1(consts, z, 2, conv=(cw, cb))
    t = _fft_s2(a, kf, 0, consts)
    y1 = _fft_s3(consts, t, u, z, 0, cw, cb, skip, 0, BF16)
    a = _fft_s1(consts, y1, 0)
    t = _fft_s2(a, kf, 1, consts)
    y2 = _fft_s3(consts, t, y1, z, 1, cw, cb, skip, 1, F32)
    q = ROW_TILE // n1
    return _mm_res(
        [y2], [pl.BlockSpec((None, n1, q, d), lambda b, i, j: (b, 0, i, 0))], p["hy_w_out"], layer,
        p["hy_b_out"][layer].reshape(1, d), x, gate, ROW_TILE, a_mode="perm", perm=(n1, q))


def _attn_kernel(*refs, phases, ta, n_sub, to_classes):
    if to_classes:
        (pm_ref, q_ref, kp_ref, km_ref, kn_ref, vp_ref, vm_ref, vn_ref, o_ref, l_ref, kx_ref, vx_ref,
         on_ref, ot_ref, ln_ref, ls_ref, lt_ref) = refs
    else:
        q_ref, kp_ref, km_ref, kn_ref, vp_ref, vm_ref, vn_ref, o_ref, l_ref, kx_ref, vx_ref = refs
    i = pl.program_id(2)
    halo = ATT_BAND // phases
    qa = ATT_Q // phases
    ka = 2 * qa
    kx_ref[:, 0:halo] = kp_ref[...]
    kx_ref[:, halo:halo + ta] = km_ref[...]
    kx_ref[:, halo + ta:] = kn_ref[...]
    vx_ref[:, 0:halo] = vp_ref[...]
    vx_ref[:, halo:halo + ta] = vm_ref[...]
    vx_ref[:, halo + ta:] = vn_ref[...]
    row = lax.broadcasted_iota(jnp.int32, (ATT_Q, 2 * ATT_Q), 0)
    col = lax.broadcasted_iota(jnp.int32, (ATT_Q, 2 * ATT_Q), 1)
    cq, aq = row >> (qa.bit_length() - 1), row & (qa - 1)
    ck, ak = col >> (ka.bit_length() - 1), col & (ka - 1)
    delta = phases * (ak - aq) - ATT_BAND + ck - cq
    band = (delta >= -ATT_BAND) & (delta <= ATT_BAND)
    lane_head = lax.broadcasted_iota(jnp.int32, (ATT_Q, LANES), 1) >> (LSE_LANES.bit_length() - 1)
    for s in range(ta // qa):
        key_idx = phases * (i * ta + s * qa - halo + ak) + ck
        valid = band & (key_idx >= 0) & (key_idx < n_sub)
        lse_tile = jnp.zeros((ATT_Q, LANES), F32)
        for h in range(HEADS_PER_GROUP):
            cs = slice(h * HEAD_DIM, (h + 1) * HEAD_DIM)
            q = jnp.concatenate([q_ref[c, s * qa:(s + 1) * qa, cs] for c in range(phases)], axis=0)
            k = jnp.concatenate([kx_ref[c, s * qa:s * qa + ka, cs] for c in range(phases)], axis=0)
            v = jnp.concatenate([vx_ref[c, s * qa:s * qa + ka, cs] for c in range(phases)], axis=0)
            sc = lax.dot_general(q, k, (((1,), (1,)), ((), ())), preferred_element_type=F32)
            sc = jnp.where(valid, sc, NEG_BIG)
            m = jnp.max(sc, axis=-1, keepdims=True)
            pr = jnp.exp(sc - m)
            den = jnp.sum(pr, axis=-1, keepdims=True)
            o = (_dot(pr.astype(BF16), v) / den).astype(BF16)
            lse_tile = jnp.where(lane_head == h, m + jnp.log(den), lse_tile)
            if to_classes:
                on_ref[s * ATT_Q:(s + 1) * ATT_Q, cs] = o
            else:
                for c in range(phases):
                    o_ref[c, s * qa:(s + 1) * qa, cs] = o[c * qa:(c + 1) * qa]
        if to_classes:
            ln_ref[s * ATT_Q:(s + 1) * ATT_Q, :] = lse_tile
        else:
            for c in range(phases):
                l_ref[c, s * qa:(s + 1) * qa, :] = lse_tile[c * qa:(c + 1) * qa]
    if to_classes:
        na = ta // CLASSES
        _transpose_rows(ot_ref, on_ref, pm_ref, na, CLASSES)
        o_ref[...] = ot_ref[...].reshape(o_ref.shape)
        rest = ln_ref[...]
        total = jnp.zeros(rest.shape, F32)
        for _ in range(3):
            piece = rest.astype(BF16)
            rest = rest - piece.astype(F32)
            ls_ref[...] = piece
            _transpose_rows(lt_ref, ls_ref, pm_ref, na, CLASSES)
            total = total + lt_ref[...].astype(F32)
        l_ref[...] = total.reshape(l_ref.shape)


def _attn_group(qkv, col0, phases, n_sub, lead_grid, lead_block, lead_index, ta, *, to_classes=False,
                out_arr_shape=None, out_block=None, out_index=None):
    gw = GROUP_WIDTH
    halo = ATT_BAND // phases
    rows = qkv.shape[-2]
    per = ta // halo
    nblk = rows // halo
    cb = col0 // gw

    def spec(nrows, ridx, part):
        return pl.BlockSpec(tuple(lead_block) + (nrows, gw),
                            lambda b, rho, i: tuple(lead_index(b, rho)) + (ridx(i), cb + part))

    main = lambda part: spec(ta, lambda i: i, part)
    prev = lambda part: spec(halo, lambda i: jnp.maximum(i * per - 1, 0), part)
    nxt = lambda part: spec(halo, lambda i: jnp.minimum((i + 1) * per, nblk - 1), part)
    bsz = qkv.shape[0]
    in_specs = [main(0), prev(1), main(1), nxt(1), prev(2), main(2), nxt(2)]
    args = [qkv] * 7
    if to_classes:
        assert ta == PERM_ROWS
        out_specs = [pl.BlockSpec(out_block + (gw,), out_index), pl.BlockSpec(out_block + (LANES,), out_index)]
        scratch_extra = [pltpu.VMEM((ta, gw), BF16)] * 2 + [pltpu.VMEM((ta, LANES), F32)] + \
                        [pltpu.VMEM((ta, LANES), BF16)] * 2
        in_specs.insert(0, pl.BlockSpec((PERM_ROWS, PERM_ROWS), lambda b, rho, i: (0, 0)))
        args.insert(0, _perm_for(ta // CLASSES, CLASSES))
    else:
        out_arr_shape = qkv.shape[:-1]
        oidx = lambda b, rho, i: tuple(lead_index(b, rho)) + (i, 0)
        out_specs = [pl.BlockSpec(tuple(lead_block) + (ta, gw), oidx),
                     pl.BlockSpec(tuple(lead_block) + (ta, LANES), oidx)]
        scratch_extra = []
    kx_shape = (phases, ta + 2 * halo, gw)
    return pl.pallas_call(
        functools.partial(_attn_kernel, phases=phases, ta=ta, n_sub=n_sub, to_classes=to_classes),
        grid=(bsz, lead_grid, rows // ta),
        in_specs=in_specs,
        out_specs=out_specs,
        out_shape=[jax.ShapeDtypeStruct(tuple(out_arr_shape) + (gw,), BF16),
                   jax.ShapeDtypeStruct(tuple(out_arr_shape) + (LANES,), F32)],
        scratch_shapes=[pltpu.VMEM(kx_shape, BF16)] * 2 + scratch_extra,
        compiler_params=_params(("parallel", "parallel", "parallel")),
        name="attn_p%d" % phases + ("_cls" if to_classes else ""),
    )(*args)


def _rope_tables(pos):
    half = ROT_DIM // 2
    inv = ROPE_THETA ** (-jnp.arange(0, ROT_DIM, 2, dtype=F32) / ROT_DIM)
    ang = pos.astype(F32)[:, None] * inv[None, :]
    cos, sin = jnp.cos(ang), jnp.sin(ang)
    n = pos.shape[0]
    rest = HEAD_DIM - ROT_DIM
    c = jnp.concatenate([cos, cos, jnp.ones((n, rest), F32)], axis=1)
    s = jnp.concatenate([sin, sin, jnp.zeros((n, rest), F32)], axis=1)
    return c, s


def _rope_partner_matrix(width):
    half = ROT_DIM // 2
    m = np.zeros((width, width), np.float32)
    for base in range(0, width, HEAD_DIM):
        for k in range(half):
            m[base + k + half, base + k] = -1.0
            m[base + k, base + k + half] = 1.0
    return jnp.asarray(m, BF16)


def _attn_mixer(x, g, shift, scale, gate, p, layer):
    bsz, seq_len, d = x.shape
    gw = GROUP_WIDTH
    nc = seq_len // CLASSES
    ca = ROW_TILE // CLASSES
    w_in = p["at_w_in"]
    tabs = _rope_tables(jnp.arange(seq_len))
    qkv0 = _norm_mm(
        x, pl.BlockSpec((None, ROW_TILE, d), lambda b, i, j: (b, i, 0)), None, ROW_TILE, seq_len // ROW_TILE,
        g, shift, scale, [w_in], layer, 0, 3 * gw, jax.ShapeDtypeStruct((bsz, seq_len, 3 * gw), BF16),
        pl.BlockSpec((None, ROW_TILE, 512), lambda b, i, j: (b, i, j)), mode="rope",
        rope=(tabs, pl.BlockSpec((ROW_TILE, HEAD_DIM), lambda b, i, j: (i, 0))))
    pos_c = (jnp.arange(nc)[None, :] * CLASSES + jnp.arange(CLASSES)[:, None]).reshape(-1)
    tabs_c = [t.reshape(CLASSES, nc, HEAD_DIM) for t in _rope_tables(pos_c)]
    qkv12 = _norm_mm(
        x.reshape(bsz, nc, CLASSES, d), pl.BlockSpec((None, ca, CLASSES, d), lambda b, i, j: (b, i, 0, 0)),
        (ca, CLASSES), ROW_TILE, nc // ca, g, shift, scale, [w_in], layer, 3 * gw, 6 * gw,
        jax.ShapeDtypeStruct((bsz, CLASSES, nc, 6 * gw), BF16),
        pl.BlockSpec((None, CLASSES, ca, 512), lambda b, i, j: (b, 0, i, j)), mode="rope",
        rope=(tabs_c, pl.BlockSpec((CLASSES, ca, HEAD_DIM), lambda b, i, j: (0, i, 0))))
    cls_shape = (bsz, CLASSES, nc)
    ta0 = PERM_ROWS
    o0, l0 = _attn_group(
        qkv0.reshape(bsz, 1, seq_len, 3 * gw), 0, 1, seq_len, 1, (None, 1), lambda b, rho: (b, 0), ta0,
        to_classes=True, out_arr_shape=cls_shape, out_block=(None, CLASSES, ta0 // CLASSES),
        out_index=lambda b, rho, i: (b, 0, i, 0))
    dil1 = ATTN_PATTERNS[1][1]
    ph = CLASSES // dil1
    ta1 = min(64, nc)
    o1, l1 = _attn_group(
        qkv12.reshape(bsz, ph, dil1, nc, 6 * gw), 0, ph, seq_len // dil1, dil1, (None, ph, None),
        lambda b, rho: (b, 0, rho), ta1)
    ta2 = min(256, nc)
    o2, l2 = _attn_group(
        qkv12.reshape(bsz, CLASSES, 1, nc, 6 * gw), 3 * gw, 1, nc, CLASSES, (None, None, 1),
        lambda b, rho: (b, rho, 0), ta2)
    os_ = [o.reshape(cls_shape + (gw,)) for o in (o0, o1, o2)]
    ls = [l.reshape(cls_shape + (LANES,)) for l in (l0, l1, l2)]
    blk = lambda width: pl.BlockSpec((None, CLASSES, ca, width), lambda b, i, j: (b, 0, i, 0))
    return _mm_res(os_ + ls, [blk(gw)] * 3 + [blk(LANES)] * 3, p["at_w_out"], layer, jnp.zeros((1, d), F32),
                   x, gate, ROW_TILE, a_mode="merge", perm=(CLASSES, ca))


def _ffn(x, g, shift, scale, gate, p, layer):
    bsz, seq_len, d = x.shape
    dff = p["ffn_w_gate"].shape[-1]
    tiles = seq_len // ROW_TILE
    xs = pl.BlockSpec((None, ROW_TILE, d), lambda b, i, j: (b, i, 0))
    hs = pl.BlockSpec((None, ROW_TILE, 512), lambda b, i, j: (b, i, j))
    hmid = _norm_mm(x, xs, None, ROW_TILE, tiles, g, shift, scale, [p["ffn_w_gate"], p["ffn_w_up"]], layer, 0,
                    dff, jax.ShapeDtypeStruct((bsz, seq_len, dff), BF16), hs, mode="swiglu")
    return _mm_res([hmid], [pl.BlockSpec((None, ROW_TILE, dff), lambda b, i, j: (b, i, 0))],
                   p["ffn_w_down"], layer, jnp.zeros((1, d), F32), x, gate, ROW_TILE)


def _encoder(x, mods, final_mod, p):
    bsz, seq_len, d = x.shape
    n1 = 2 * seq_len // DFT_N2
    consts = _dft_consts(n1, DFT_N2)
    consts["d"] = d
    for i in range(DEPTH):
        sh_m, sc_m, g_m, sh_f, sc_f, g_f = [mods[i][:, None, k * d:(k + 1) * d] for k in range(6)]
        j = i // 2
        if i % 2 == 0:
            kf = _hyena_filter_spectra(seq_len, consts, p["hy_fw1"][j], p["hy_fb1"][j], p["hy_ffreq"][j],
                                       p["hy_fw2"][j], p["hy_fb2"][j], p["hy_fw3"][j], p["hy_decay"][j])
            x = _hyena_mixer(x, p["norm_mix"][i], sh_m, sc_m, g_m, p, j, kf, consts)
        else:
            x = _attn_mixer(x, p["norm_mix"][i], sh_m, sc_m, g_m, p, j)
        x = _ffn(x, p["norm_ffn"][i], sh_f, sc_f, g_f, p, i)
    sh, sc = final_mod[:, None, :d], final_mod[:, None, d:]
    return _final(x, p["final_norm"], sh, sc)


def kernel(x_prompt, x_sample, c_prompt, c_sample, ada_w, ada_b, norm_mix, norm_ffn, hy_w_in, hy_b_in, hy_conv_w, hy_conv_b, hy_fw1, hy_fb1, hy_ffreq, hy_fw2, hy_fb2, hy_fw3, hy_decay, hy_skip, hy_w_out, hy_b_out, at_w_in, at_w_out, ffn_w_gate, ffn_w_up, ffn_w_down, final_norm, final_ada_w, final_ada_b):
    d = x_prompt.shape[-1]
    bp, bs = c_prompt.shape[0], c_sample.shape[0]
    pad = -(bp + bs) % (2 * SUBLANES)
    c_all = jnp.concatenate([c_prompt, c_sample, jnp.zeros((pad, d), F32)], axis=0)
    mods = _ada(c_all, ada_w, ada_b)
    fmod = _ada(c_all, final_ada_w[None], final_ada_b[None])[0]
    p = dict(norm_mix=norm_mix, norm_ffn=norm_ffn,
             hy_w_in=hy_w_in.astype(BF16), hy_b_in=hy_b_in, hy_conv_w=hy_conv_w, hy_conv_b=hy_conv_b,
             hy_fw1=hy_fw1, hy_fb1=hy_fb1, hy_ffreq=hy_ffreq, hy_fw2=hy_fw2, hy_fb2=hy_fb2, hy_fw3=hy_fw3,
             hy_decay=hy_decay, hy_skip=hy_skip, hy_w_out=hy_w_out.astype(BF16), hy_b_out=hy_b_out,
             at_w_in=at_w_in.astype(BF16), at_w_out=at_w_out.astype(BF16),
             ffn_w_gate=ffn_w_gate.astype(BF16), ffn_w_up=ffn_w_up.astype(BF16),
             ffn_w_down=ffn_w_down.astype(BF16), final_norm=final_norm)
    y_prompt = _encoder(x_prompt, mods[:, :bp], fmod[:bp], p)
    y_sample = _encoder(x_sample, mods[:, bp:bp + bs], fmod[bp:bp + bs], p)
    return (y_prompt, y_sample)
```

```python
import functools
import math

import numpy as np
import jax
import jax.numpy as jnp
from jax import lax
from jax.experimental import pallas as pl
from jax.experimental.pallas import tpu as pltpu

F32 = jnp.float32
BF16 = jnp.bfloat16
EPS = 1e-6

DEPTH = 4
HYENA_ORDER = 2
N_DIRS = 2
FILTER_BANDS = 16
FILTER_EMB = 1 + 2 * FILTER_BANDS
ATTN_PATTERNS = ((128, 1), (512, 4), (2048, 16))
HEADS_PER_GROUP = 8
HEAD_DIM = 128
GROUP_WIDTH = HEADS_PER_GROUP * HEAD_DIM
ROT_DIM = HEAD_DIM // 4
ROPE_THETA = 500000.0

LANES = 128
SUBLANES = 8
VMEM_LIMIT_BYTES = 56 * 1024 * 1024

DFT_N2 = 256
SLAB_ROWS = DFT_N2 // 2
ATT_BAND = 64
ATT_Q = 2 * ATT_BAND
CLASSES = 16
LSE_LANES = LANES // HEADS_PER_GROUP
ROW_TILE = 1024
COL_TILE = 1024
NEG_BIG = -1e30


def _params(sem):
    return pltpu.CompilerParams(dimension_semantics=sem, vmem_limit_bytes=VMEM_LIMIT_BYTES)


def _dot(a, b):
    return jnp.dot(a, b, preferred_element_type=F32)


def _split(a):
    hi = a.astype(BF16)
    lo = (a - hi.astype(F32)).astype(BF16)
    return hi, lo


def _dot3(a, b):
    ah, al = _split(a)
    bh, bl = _split(b)
    return _dot(ah, bh) + _dot(al, bh) + _dot(ah, bl)


def _modnorm(x, g, shift, scale):
    ms = jnp.mean(x * x, axis=-1, keepdims=True)
    return (x * lax.rsqrt(ms + EPS)) * (g * (1.0 + scale)) + shift


PERM_ROWS = 256


def _perm_matrix(p, q):
    m = np.zeros((PERM_ROWS, PERM_ROWS), np.float32)
    pi, qi = np.meshgrid(np.arange(p), np.arange(q), indexing="ij")
    m[(qi * p + pi).ravel(), (pi * q + qi).ravel()] = 1.0
    return jnp.asarray(m, BF16)


def _perm_for(p, q):
    assert (q <= 16 and p % (PERM_ROWS // q) == 0) or (p <= 16 and q % (PERM_ROWS // p) == 0), (p, q)
    return _perm_matrix(PERM_ROWS // q, q) if q <= 16 else _perm_matrix(p, PERM_ROWS // p)


def _transpose_rows(dst_ref, src_ref, pm_ref, p, q):
    if q <= 16:
        pg = PERM_ROWS // q
        for grp in range(p // pg):
            t = _dot(pm_ref[...], src_ref[grp * PERM_ROWS:(grp + 1) * PERM_ROWS, :]).astype(BF16)
            for qi in range(q):
                dst_ref[qi * p + grp * pg:qi * p + (grp + 1) * pg, :] = t[qi * pg:(qi + 1) * pg]
    else:
        qg = PERM_ROWS // p
        for grp in range(q // qg):
            blk = jnp.concatenate([src_ref[pi * q + grp * qg:pi * q + (grp + 1) * qg, :] for pi in range(p)],
                                  axis=0)
            dst_ref[grp * PERM_ROWS:(grp + 1) * PERM_ROWS, :] = _dot(pm_ref[...], blk).astype(BF16)


def _ada_kernel(c_ref, w_ref, b_ref, o_ref):
    c = c_ref[...]
    cs = c * jax.nn.sigmoid(c)
    o_ref[...] = _dot3(cs, w_ref[...]) + b_ref[...]


def _ada(c_all, w, b, tn=1024):
    nl, d, no = w.shape
    r = c_all.shape[0]
    return pl.pallas_call(
        _ada_kernel,
        grid=(nl, no // tn),
        in_specs=[
            pl.BlockSpec((r, d), lambda l, j: (0, 0)),
            pl.BlockSpec((None, d, tn), lambda l, j: (l, 0, j)),
            pl.BlockSpec((None, 1, tn), lambda l, j: (l, 0, j)),
        ],
        out_specs=pl.BlockSpec((None, r, tn), lambda l, j: (l, 0, j)),
        out_shape=jax.ShapeDtypeStruct((nl, r, no), F32),
        compiler_params=_params(("parallel", "parallel")),
        name="ada_mod",
    )(c_all, w, b.reshape(nl, 1, no))


def _norm_mm_kernel(*refs, mode, tn, perm):
    if perm:
        pm_ref, refs = refs[0], refs[1:]
        h0_ref, refs = refs[-1], refs[:-1]
    if mode == "swiglu":
        x_ref, g_ref, sh_ref, sc_ref, wg_ref, wu_ref, o_ref, h_ref = refs
    elif mode == "rope":
        x_ref, g_ref, sh_ref, sc_ref, w_ref, c_ref, s_ref, rot_ref, o_ref, h_ref = refs
    else:
        x_ref, g_ref, sh_ref, sc_ref, w_ref, b_ref, o_ref, h_ref = refs
    j = pl.program_id(2)

    @pl.when(j == 0)
    def _():
        h = _modnorm(x_ref[...].reshape(h_ref.shape), g_ref[...], sh_ref[...], sc_ref[...]).astype(BF16)
        if perm:
            h0_ref[...] = h
            _transpose_rows(h_ref, h0_ref, pm_ref, *perm)
        else:
            h_ref[...] = h

    h = h_ref[...]
    if mode == "swiglu":
        a = _dot(h, wg_ref[...])
        u = _dot(h, wu_ref[...])
        o_ref[...] = (a * jax.nn.sigmoid(a) * u).astype(o_ref.dtype).reshape(o_ref.shape)
    elif mode == "rope":
        acc = _dot(h, w_ref[...])
        part = (j // (GROUP_WIDTH // tn)) % 3

        @pl.when(part == 2)
        def _():
            o_ref[...] = acc.astype(o_ref.dtype).reshape(o_ref.shape)

        @pl.when(part != 2)
        def _():
            reps = tn // HEAD_DIM
            tabs = [t[...].reshape(acc.shape[0], HEAD_DIM) for t in (c_ref, s_ref)]
            c, s = [jnp.concatenate([t] * reps, axis=1) for t in tabs]
            accb = acc.astype(BF16)
            rw = rot_ref.shape[0]
            partner = jnp.concatenate([_dot(accb[:, k * rw:(k + 1) * rw], rot_ref[...]) for k in range(tn // rw)],
                                      axis=1)
            qs = jnp.where(part == 0, HEAD_DIM ** -0.5, 1.0).astype(F32)
            o_ref[...] = ((acc * c + partner * s) * qs).astype(o_ref.dtype).reshape(o_ref.shape)
    else:
        o_ref[...] = (_dot(h, w_ref[...]) + b_ref[...]).astype(o_ref.dtype).reshape(o_ref.shape)


def _norm_mm(x, x_spec, perm, rows, grid_rows, g, shift, scale, ws, w_layer, col0, nout, out_shape,
             out_spec, *, mode, bias=None, rope=None, tn=512):
    d = x.shape[-1]
    bsz = x.shape[0]
    cb = col0 // tn
    vec = pl.BlockSpec((None, 1, d), lambda b, i, j: (b, 0, 0))
    in_specs = [x_spec, pl.BlockSpec((1, d), lambda b, i, j: (0, 0)), vec, vec]
    in_specs += [pl.BlockSpec((None, d, tn), lambda b, i, j: (w_layer, 0, cb + j)) for _ in ws]
    args = [x, g.reshape(1, d), shift, scale, *ws]
    scratch = [pltpu.VMEM((rows, d), BF16)]
    if perm:
        in_specs.insert(0, pl.BlockSpec((PERM_ROWS, PERM_ROWS), lambda b, i, j: (0, 0)))
        args.insert(0, _perm_for(*perm))
        scratch.append(pltpu.VMEM((rows, d), BF16))
    if mode == "rope":
        tabs, tab_spec = rope
        rw = 4 * HEAD_DIM
        in_specs += [tab_spec] * 2 + [pl.BlockSpec((rw, rw), lambda b, i, j: (0, 0))]
        args += list(tabs) + [_rope_partner_matrix(rw)]
    elif mode == "bias":
        in_specs.append(pl.BlockSpec((1, tn), lambda b, i, j: (0, cb + j)))
        args.append(bias)
    return pl.pallas_call(
        functools.partial(_norm_mm_kernel, mode=mode, tn=tn, perm=perm),
        grid=(bsz, grid_rows, nout // tn),
        in_specs=in_specs,
        out_specs=out_spec,
        out_shape=out_shape,
        scratch_shapes=scratch,
        compiler_params=_params(("parallel", "parallel", "arbitrary")),
        name="norm_mm_" + mode,
    )(*args)


def _mm_res_kernel(*refs, a_mode, perm):
    if a_mode == "merge":
        (pm_ref, o0, o1, o2, l0, l1, l2, e_ref, w_ref, b_ref, x_ref, gt_ref, out_ref, a_ref, a0_ref) = refs
    elif a_mode == "perm":
        pm_ref, a_in, w_ref, b_ref, x_ref, gt_ref, out_ref, a_ref, a0_ref = refs
    else:
        a_in, w_ref, b_ref, x_ref, gt_ref, out_ref = refs
    j = pl.program_id(2)

    if a_mode != "plain":
        @pl.when(j == 0)
        def _():
            k = a_ref.shape[1]
            if a_mode == "merge":
                ls = [l[...].reshape(-1, LANES) for l in (l0, l1, l2)]
                mx = jnp.maximum(jnp.maximum(ls[0], ls[1]), ls[2])
                ws = [jnp.exp(l - mx) for l in ls]
                inv = 1.0 / (ws[0] + ws[1] + ws[2])
                num = jnp.zeros((ls[0].shape[0], k), F32)
                for w, o in zip(ws, (o0, o1, o2)):
                    hi, lo = _split(w * inv)
                    wide = _dot(hi, e_ref[...]) + _dot(lo, e_ref[...])
                    num = num + wide * o[...].reshape(-1, k).astype(F32)
                a0_ref[...] = num.astype(BF16)
            else:
                a0_ref[...] = a_in[...].reshape(-1, k).astype(BF16)
            _transpose_rows(a_ref, a0_ref, pm_ref, *perm)

        a = a_ref[...]
    else:
        a = a_in[...]
    out_ref[...] = x_ref[...] + gt_ref[...] * (_dot(a, w_ref[...]) + b_ref[...])


def _head_spread_matrix(width):
    m = np.zeros((LANES, width), np.float32)
    for hd in range(width // HEAD_DIM):
        m[hd * LSE_LANES, hd * HEAD_DIM:(hd + 1) * HEAD_DIM] = 1.0
    return jnp.asarray(m, BF16)


def _mm_res(a_list, a_specs, w, w_layer, bias, x, gate, rows, *, a_mode="plain", perm=None, tn=512):
    bsz, seq_len, d = x.shape
    k = w.shape[1]
    blk = pl.BlockSpec((None, rows, tn), lambda b, i, j: (b, i, j))
    if a_mode == "merge":
        a_list = list(a_list) + [_head_spread_matrix(k)]
        a_specs = list(a_specs) + [pl.BlockSpec((LANES, k), lambda b, i, j: (0, 0))]
    in_specs = list(a_specs) + [
        pl.BlockSpec((None, k, tn), lambda b, i, j: (w_layer, 0, j)),
        pl.BlockSpec((1, tn), lambda b, i, j: (0, j)),
        blk,
        pl.BlockSpec((None, 1, tn), lambda b, i, j: (b, 0, j)),
    ]
    args = [*a_list, w, bias, x, gate]
    scratch = []
    if a_mode != "plain":
        in_specs.insert(0, pl.BlockSpec((PERM_ROWS, PERM_ROWS), lambda b, i, j: (0, 0)))
        args.insert(0, _perm_for(*perm))
        scratch = [pltpu.VMEM((rows, k), BF16)] * 2
    return pl.pallas_call(
        functools.partial(_mm_res_kernel, a_mode=a_mode, perm=perm),
        grid=(bsz, seq_len // rows, d // tn),
        in_specs=in_specs,
        out_specs=blk,
        out_shape=jax.ShapeDtypeStruct(x.shape, F32),
        scratch_shapes=scratch,
        compiler_params=_params(("parallel", "parallel", "arbitrary")),
        name="mm_res_" + a_mode,
    )(*args)


def _final_kernel(x_ref, g_ref, sh_ref, sc_ref, o_ref):
    o_ref[...] = _modnorm(x_ref[...], g_ref[...], sh_ref[...], sc_ref[...])


def _final(x, g, shift, scale, tm=512):
    bsz, seq_len, d = x.shape
    vec = pl.BlockSpec((None, 1, d), lambda b, i: (b, 0, 0))
    blk = pl.BlockSpec((None, tm, d), lambda b, i: (b, i, 0))
    return pl.pallas_call(
        _final_kernel,
        grid=(bsz, seq_len // tm),
        in_specs=[blk, pl.BlockSpec((1, d), lambda b, i: (0, 0)), vec, vec],
        out_specs=blk,
        out_shape=jax.ShapeDtypeStruct(x.shape, F32),
        compiler_params=_params(("parallel", "parallel")),
        name="final_norm",
    )(x, g.reshape(1, d), shift, scale)


def _filter_positions(seq_len, n1):
    n = 2 * seq_len
    h = SLAB_ROWS
    half = jnp.arange(2)[:, None, None]
    s = jnp.arange(n1)[None, :, None]
    r = jnp.arange(h)[None, None, :]
    idx = ((half * h + r) * n1 + s).reshape(n)
    pos = jnp.where(idx < seq_len, idx, n - idx).astype(F32)
    sign = jnp.where(idx < seq_len, 1.0, jnp.where(idx == seq_len, 0.0, -1.0)).astype(F32)
    t = pos / max(seq_len - 1, 1)
    bands = jnp.linspace(1e-4, FILTER_BANDS - 1, FILTER_BANDS, dtype=F32)
    ang = 2.0 * math.pi * pos[:, None] * bands[None, :] / seq_len
    z = jnp.concatenate([t[:, None], jnp.cos(ang), -jnp.sin(ang)], axis=-1)
    z = jnp.pad(z, ((0, 0), (0, LANES - FILTER_EMB - 1)))
    return jnp.concatenate([z, sign[:, None]], axis=-1)


def _filter_mlp_kernel(z_ref, w1_ref, b1_ref, f_ref, w2_ref, b2_ref, o_ref):
    f = f_ref[...]
    a = jnp.sin(f[0:1, :] * (_dot3(z_ref[...], w1_ref[...]) + b1_ref[...]))
    o_ref[...] = jnp.sin(f[1:2, :] * (_dot3(a, w2_ref[...]) + b2_ref[...]))


def _filter_mlp(zf, w1, b1, freq, w2, b2, tr=512):
    n = zf.shape[0]
    hid = w1.shape[1]
    w1p = jnp.pad(w1, ((0, LANES - w1.shape[0]), (0, 0)))
    full = lambda shape: pl.BlockSpec(shape, lambda i: (0,) * len(shape))
    return pl.pallas_call(
        _filter_mlp_kernel,
        grid=(n // tr,),
        in_specs=[pl.BlockSpec((tr, LANES), lambda i: (i, 0)), full((LANES, hid)), full((1, hid)),
                  full((2, hid)), full((hid, hid)), full((1, hid))],
        out_specs=pl.BlockSpec((tr, hid), lambda i: (i, 0)),
        out_shape=jax.ShapeDtypeStruct((n, hid), F32),
        compiler_params=_params(("parallel",)),
        name="filter_mlp",
    )(zf, w1p, b1.reshape(1, hid), freq, w2, b2.reshape(1, hid))


def _dft_consts(n1, n2):
    n = n1 * n2
    k2 = np.arange(n2 // 2)[:, None]
    nn2 = np.arange(n2)[None, :]
    ph = 2.0 * np.pi * nn2 * (k2 + 0.5) / n2
    f_s1 = np.concatenate([np.cos(ph), -np.sin(ph)], axis=0)
    m = np.arange(n2 // 2)[:, None]
    kk = np.arange(n2 // 2)[None, :]
    ph3 = 2.0 * np.pi * m * (kk + 0.5) / n2
    f_s3 = (2.0 / n) * np.concatenate([np.cos(ph3), -np.sin(ph3)], axis=1)
    a = 2.0 * np.pi * np.outer(np.arange(n1), np.arange(n1)) / n1
    c, s = np.cos(a), -np.sin(a)
    f_fwd = np.block([[c, -s], [s, c]])
    f_inv = np.block([[c, s], [-s, c]])
    th = 2.0 * np.pi * (np.arange(n2 // 2)[:, None] + 0.5) * np.arange(n1)[None, :] / n
    as_bf = lambda x: jnp.asarray(x, F32).astype(BF16)
    tw_c = jnp.broadcast_to(jnp.asarray(np.cos(th), F32)[:, :, None], (n2 // 2, n1, LANES))
    tw_s = jnp.broadcast_to(jnp.asarray(np.sin(th), F32)[:, :, None], (n2 // 2, n1, LANES))
    tc, ts = jnp.asarray(np.cos(th), F32), jnp.asarray(np.sin(th), F32)
    ff, fi = jnp.asarray(f_fwd, F32), jnp.asarray(f_inv, F32)
    fl, fr = ff[None, :, :n1], ff[None, :, n1:]
    g_fwd = jnp.concatenate([fl * tc[:, None, :] - fr * ts[:, None, :],
                             fl * ts[:, None, :] + fr * tc[:, None, :]], axis=2).astype(BF16)
    it, ib = fi[None, :n1, :], fi[None, n1:, :]
    g_inv = jnp.concatenate([tc[:, :, None] * it - ts[:, :, None] * ib,
                             ts[:, :, None] * it + tc[:, :, None] * ib], axis=1).astype(BF16)
    return dict(f_s1=as_bf(f_s1[:, :n2 // 2]), f_s3=as_bf(f_s3), f_fwd=as_bf(f_fwd), f_inv=as_bf(f_inv),
                f_s1_f32=jnp.asarray(f_s1, F32), f_fwd_f32=jnp.asarray(f_fwd, F32),
                tw_c=tw_c, tw_s=tw_s, g_fwd=g_fwd, g_inv=g_inv, n1=n1)


def _slab_tile(n1):
    return min(n1, 8)


HI16 = -65536
HALF_ULP16 = 0x8000


def _pack_c(re, im):
    rb = lax.bitcast_convert_type(re, jnp.int32) + HALF_ULP16
    ib = lax.bitcast_convert_type(im, jnp.int32) + HALF_ULP16
    return (rb & HI16) | lax.shift_right_logical(ib, 16)


def _pack_bf16(re, im):
    rb = lax.bitcast_convert_type(re.astype(F32), jnp.int32)
    ib = lax.bitcast_convert_type(im.astype(F32), jnp.int32)
    return rb | lax.shift_right_logical(ib, 16)


def _unpack_c(p):
    re = lax.bitcast_convert_type(p & HI16, F32)
    im = lax.bitcast_convert_type(lax.shift_left(p, 16), F32)
    return re, im


def _shift_rows(x, down):
    rows = x.shape[0]
    row = lax.broadcasted_iota(jnp.int32, x.shape, 0)
    if down:
        return jnp.where(row == 0, 0.0, pltpu.roll(x, 1, 0))
    return jnp.where(row == rows - 1, 0.0, pltpu.roll(x, rows - 1, 0))


def _short_conv_slabs(main_ref, prev_ref, next_ref, w_ref, b_ref, first, last):
    n_slabs = main_ref.shape[0]
    prev = prev_ref[0].astype(F32)
    prev = jnp.where(first, _shift_rows(prev, True), prev)
    nxt = next_ref[0].astype(F32)
    nxt = jnp.where(last, _shift_rows(nxt, False), nxt)
    w = w_ref[...]
    out = []
    for s in range(n_slabs):
        up = prev if s == 0 else main_ref[s - 1].astype(F32)
        dn = nxt if s == n_slabs - 1 else main_ref[s + 1].astype(F32)
        out.append(up * w[0:1, :] + main_ref[s].astype(F32) * w[1:2, :] + dn * w[2:3, :] + b_ref[...])
    return out


def _fft_s1_kernel(*refs, short_conv):
    re_ref, im_ref, ret_ref, imt_ref = refs[-4:]
    if short_conv:
        pm_ref, f_ref, m_ref, p_ref, n_ref, w_ref, b_ref, o_ref, u_ref = refs[:-4]
        t = pl.program_id(1)
        slabs = _short_conv_slabs(m_ref, p_ref, n_ref, w_ref, b_ref, t == 0, t == pl.num_programs(1) - 1)
    else:
        pm_ref, f_ref, m_ref, o_ref = refs[:-4]
        slabs = [m_ref[s] for s in range(m_ref.shape[0])]
    half = o_ref.shape[0]
    for s, u in enumerate(slabs):
        if short_conv:
            u_ref[s] = u.astype(u_ref.dtype)
        r = _dot(f_ref[...], u.astype(BF16))
        re_ref[s * half:(s + 1) * half, :] = r[:half].astype(BF16)
        im_ref[s * half:(s + 1) * half, :] = r[half:].astype(BF16)
    _transpose_rows(ret_ref, re_ref, pm_ref, len(slabs), half)
    _transpose_rows(imt_ref, im_ref, pm_ref, len(slabs), half)
    o_ref[...] = _pack_bf16(ret_ref[...], imt_ref[...]).reshape(o_ref.shape)


def _fft_s1(consts, src, col_block, conv=None, ct=512):
    bsz, n1, h, c = src.shape
    d = consts["d"]
    st = _slab_tile(n1)
    cpb = d // ct
    f = consts["f_s1"]
    main = pl.BlockSpec((None, st, h, ct), lambda b, t, j: (b, t, 0, col_block * cpb + j))
    a_spec = pl.BlockSpec((None, h, st, ct), lambda b, t, j: (b, 0, t, j))
    a_shape = jax.ShapeDtypeStruct((bsz, h, n1, d), jnp.int32)
    fspec = pl.BlockSpec(f.shape, lambda b, t, j: (0, 0))
    pspec = pl.BlockSpec((PERM_ROWS, PERM_ROWS), lambda b, t, j: (0, 0))
    pm = _perm_for(st, h)
    scratch = [pltpu.VMEM((st * h, ct), BF16)] * 4
    if conv is None:
        return pl.pallas_call(
            functools.partial(_fft_s1_kernel, short_conv=False),
            grid=(bsz, n1 // st, cpb),
            in_specs=[pspec, fspec, main], out_specs=a_spec, out_shape=a_shape, scratch_shapes=scratch,
            compiler_params=_params(("parallel", "parallel", "parallel")), name="fft_s1",
        )(pm, f, src)
    w, b = conv
    prev = pl.BlockSpec((None, 1, h, ct), lambda b, t, j: (b, (t * st + n1 - 1) % n1, 0, col_block * cpb + j))
    nxt = pl.BlockSpec((None, 1, h, ct), lambda b, t, j: (b, ((t + 1) * st) % n1, 0, col_block * cpb + j))
    wspec = pl.BlockSpec((3, ct), lambda b, t, j: (0, col_block * cpb + j))
    bspec = pl.BlockSpec((1, ct), lambda b, t, j: (0, col_block * cpb + j))
    u_spec = pl.BlockSpec((None, st, h, ct), lambda b, t, j: (b, t, 0, j))
    return pl.pallas_call(
        functools.partial(_fft_s1_kernel, short_conv=True),
        grid=(bsz, n1 // st, cpb),
        in_specs=[pspec, fspec, main, prev, nxt, wspec, bspec],
        out_specs=[a_spec, u_spec],
        out_shape=[a_shape, jax.ShapeDtypeStruct((bsz, n1, h, d), BF16)],
        scratch_shapes=scratch,
        compiler_params=_params(("parallel", "parallel", "parallel")), name="fft_s1_conv",
    )(pm, f, src, src, src, w, b)


def _filter_s1_kernel(flo_ref, fhi_ref, alo_ref, ahi_ref, zlo_ref, zhi_ref, wf_ref, wb_ref, df_ref, db_ref,
                      o_ref, ss_ref):
    half = o_ref.shape[1]

    @pl.when(pl.program_id(2) == 0)
    def _():
        ss_ref[...] = jnp.zeros_like(ss_ref)

    ss = jnp.zeros(ss_ref.shape, F32)
    for s in range(alo_ref.shape[0]):
        taps = []
        for a_ref, z_ref, w_ref, d_ref in ((alo_ref, zlo_ref, wf_ref, df_ref), (ahi_ref, zhi_ref, wb_ref, db_ref)):
            z = z_ref[s]
            t, sign = z[:, 0:1], z[:, LANES - 1:LANES]
            h = _dot3(a_ref[s], w_ref[...]) * jnp.exp(-t * jnp.abs(d_ref[...])) * sign
            ss = ss + jnp.sum(h * h, axis=0, keepdims=True)
            taps.append(h)
        r = _dot3(flo_ref[...], taps[0]) + _dot3(fhi_ref[...], taps[1])
        o_ref[0, :, s, :] = r[:half]
        o_ref[1, :, s, :] = r[half:]
    ss_ref[...] += ss


def _filter_s1(consts, a2, zf, w3, decay, ct=256):
    n1, h = consts["n1"], SLAB_ROWS
    hid = a2.shape[1]
    d = decay.shape[-1]
    nct = d // ct
    st = _slab_tile(n1)
    f = consts["f_s1_f32"]
    flo, fhi = f[:, :h], f[:, h:]
    dec = decay.reshape(1, HYENA_ORDER * N_DIRS * d)
    fspec = pl.BlockSpec(flo.shape, lambda o, j, t: (0, 0))
    rows = lambda half, width: pl.BlockSpec((None, st, h, width), lambda o, j, t: (half, t, 0, 0))
    wcol = lambda dirn, nrow: pl.BlockSpec((nrow, ct), lambda o, j, t: (0, (o * N_DIRS + dirn) * nct + j))
    return pl.pallas_call(
        _filter_s1_kernel,
        grid=(HYENA_ORDER, nct, n1 // st),
        in_specs=[fspec, fspec, rows(0, hid), rows(1, hid), rows(0, LANES), rows(1, LANES),
                  wcol(0, hid), wcol(1, hid), wcol(0, 1), wcol(1, 1)],
        out_specs=[pl.BlockSpec((None, 2, h, st, ct), lambda o, j, t: (o, 0, 0, t, j)),
                   pl.BlockSpec((None, 1, ct), lambda o, j, t: (o, 0, j))],
        out_shape=[jax.ShapeDtypeStruct((HYENA_ORDER, 2, h, n1, d), F32),
                   jax.ShapeDtypeStruct((HYENA_ORDER, 1, d), F32)],
        compiler_params=_params(("parallel", "parallel", "arbitrary")), name="filter_s1",
    )(flo, fhi, a2.reshape(2, n1, h, hid), a2.reshape(2, n1, h, hid), zf.reshape(2, n1, h, LANES),
      zf.reshape(2, n1, h, LANES), w3, w3, dec, dec)


def _tile_lanes(x, ct):
    return jnp.concatenate([x] * (ct // LANES), axis=-1) if ct > LANES else x


def _fft_s2f_kernel(a_ref, c_ref, s_ref, ff_ref, ss_ref, o_ref, *, kb, ct):
    n1 = a_ref.shape[2]
    scale = lax.rsqrt(ss_ref[...] + EPS)

    def body(kk, carry):
        ar, ai = a_ref[0, kk], a_ref[1, kk]
        c = _tile_lanes(c_ref[kk], ct)
        s = _tile_lanes(s_ref[kk], ct)
        br = ar * c + ai * s
        bi = ai * c - ar * s
        x = _dot3(ff_ref[...], jnp.concatenate([br, bi], axis=0))
        o_ref[kk] = _pack_c(x[:n1] * scale, x[n1:] * scale)
        return carry

    lax.fori_loop(0, kb, body, 0, unroll=min(kb, 4))


def _fft_s2_kernel(a_ref, k_ref, gf_ref, gi_ref, o_ref, *, kb):
    n1 = a_ref.shape[1]

    def body(kk, carry):
        ar, ai = _unpack_c(a_ref[kk])
        x = _dot(gf_ref[kk], jnp.concatenate([ar, ai], axis=0).astype(BF16))
        xr, xi = x[:n1], x[n1:]
        kr, ki = _unpack_c(k_ref[kk])
        zr = xr * kr - xi * ki
        zi = xr * ki + xi * kr
        y = _dot(gi_ref[kk], jnp.concatenate([zr, zi], axis=0).astype(BF16))
        o_ref[kk] = _pack_c(y[:n1], y[n1:])
        return carry

    lax.fori_loop(0, kb, body, 0, unroll=min(kb, 4))


def _s2_tiles(n1, d):
    kb = max(1, 512 // n1)
    ct = min(d, 512)
    return kb, ct


def _fft_s2f(a, sumsq, consts):
    n_o, _, k2n, n1, d = a.shape
    kb, ct = _s2_tiles(n1, d)
    blk = pl.BlockSpec((None, 2, kb, n1, ct), lambda k, j, o: (o, 0, k, 0, j))
    tw = pl.BlockSpec((kb, n1, LANES), lambda k, j, o: (k, 0, 0))
    return pl.pallas_call(
        functools.partial(_fft_s2f_kernel, kb=kb, ct=ct),
        grid=(k2n // kb, d // ct, n_o),
        in_specs=[blk, tw, tw, pl.BlockSpec((2 * n1, 2 * n1), lambda k, j, o: (0, 0)),
                  pl.BlockSpec((None, 1, ct), lambda k, j, o: (o, 0, j))],
        out_specs=pl.BlockSpec((None, kb, n1, ct), lambda k, j, o: (o, k, 0, j)),
        out_shape=jax.ShapeDtypeStruct((n_o, k2n, n1, d), jnp.int32),
        compiler_params=_params(("parallel", "parallel", "parallel")),
        name="fft_s2_filter",
    )(a, consts["tw_c"], consts["tw_s"], consts["f_fwd_f32"], sumsq)


def _fft_s2(a, kf, order, consts):
    bsz, k2n, n1, d = a.shape
    kb, ct = _s2_tiles(n1, d)
    blk = pl.BlockSpec((None, kb, n1, ct), lambda k, j, b: (b, k, 0, j))
    mat = pl.BlockSpec((kb, 2 * n1, 2 * n1), lambda k, j, b: (k, 0, 0))
    return pl.pallas_call(
        functools.partial(_fft_s2_kernel, kb=kb),
        grid=(k2n // kb, d // ct, bsz),
        in_specs=[blk, pl.BlockSpec((None, kb, n1, ct), lambda k, j, b: (order, k, 0, j)), mat, mat],
        out_specs=blk,
        out_shape=jax.ShapeDtypeStruct(a.shape, jnp.int32),
        compiler_params=_params(("parallel", "parallel", "parallel")),
        name="fft_s2",
    )(a, kf, consts["g_fwd"], consts["g_inv"])


def _fft_s3_kernel(pm_ref, f_ref, t_ref, u_ref, gm_ref, gp_ref, gn_ref, w_ref, b_ref, sk_ref, o_ref,
                   re_ref, im_ref, ret_ref, imt_ref):
    t_id = pl.program_id(1)
    half, st = t_ref.shape[0], t_ref.shape[1]
    re, im = _unpack_c(t_ref[...].reshape(half * st, t_ref.shape[2]))
    re_ref[...] = re.astype(BF16)
    im_ref[...] = im.astype(BF16)
    _transpose_rows(ret_ref, re_ref, pm_ref, half, st)
    _transpose_rows(imt_ref, im_ref, pm_ref, half, st)
    gates = _short_conv_slabs(gm_ref, gp_ref, gn_ref, w_ref, b_ref, t_id == 0, t_id == pl.num_programs(1) - 1)
    for s, gate in enumerate(gates):
        t = jnp.concatenate([ret_ref[s * half:(s + 1) * half, :], imt_ref[s * half:(s + 1) * half, :]], axis=0)
        y = _dot(f_ref[...], t)
        o_ref[s] = (gate * (y + u_ref[s].astype(F32) * sk_ref[...])).astype(o_ref.dtype)


def _fft_s3(consts, t, u, z, gate_block, conv_w, conv_b, skip, order, out_dtype, ct=512):
    bsz, h, n1, d = t.shape
    st = _slab_tile(n1)
    cpb = d // ct
    f = consts["f_s3"]
    gcol = lambda j: gate_block * cpb + j
    slab = lambda idx: pl.BlockSpec((None, 1, h, ct), lambda b, tt, j: (b, idx(tt), 0, gcol(j)))
    return pl.pallas_call(
        _fft_s3_kernel,
        grid=(bsz, n1 // st, cpb),
        scratch_shapes=[pltpu.VMEM((st * h, ct), BF16)] * 4,
        in_specs=[
            pl.BlockSpec((PERM_ROWS, PERM_ROWS), lambda b, tt, j: (0, 0)),
            pl.BlockSpec(f.shape, lambda b, tt, j: (0, 0)),
            pl.BlockSpec((None, h, st, ct), lambda b, tt, j: (b, 0, tt, j)),
            pl.BlockSpec((None, st, h, ct), lambda b, tt, j: (b, tt, 0, j)),
            pl.BlockSpec((None, st, h, ct), lambda b, tt, j: (b, tt, 0, gcol(j))),
            slab(lambda tt: (tt * st + n1 - 1) % n1),
            slab(lambda tt: ((tt + 1) * st) % n1),
            pl.BlockSpec((3, ct), lambda b, tt, j: (0, gcol(j))),
            pl.BlockSpec((1, ct), lambda b, tt, j: (0, gcol(j))),
            pl.BlockSpec((None, 1, ct), lambda b, tt, j: (order, 0, j)),
        ],
        out_specs=pl.BlockSpec((None, st, h, ct), lambda b, tt, j: (b, tt, 0, j)),
        out_shape=jax.ShapeDtypeStruct((bsz, n1, h, d), out_dtype),
        compiler_params=_params(("parallel", "parallel", "parallel")),
        name="fft_s3",
    )(_perm_for(h, st), f, t, u, z, z, z, conv_w, conv_b, skip)


def _hyena_filter_spectra(seq_len, consts, fw1, fb1, ffreq, fw2, fb2, fw3, decay):
    d = decay.shape[-1]
    n1 = consts["n1"]
    zf = _filter_positions(seq_len, n1)
    a2 = _filter_mlp(zf, fw1, fb1, ffreq, fw2, fb2)
    a, sumsq = _filter_s1(consts, a2, zf, fw3, decay)
    return _fft_s2f(a, sumsq, consts)


def _hyena_mixer(x, g, shift, scale, gate, p, layer, kf, consts):
    bsz, seq_len, d = x.shape
    n1 = consts["n1"]
    h = SLAB_ROWS
    st = _slab_tile(n1)
    xv = x.reshape(bsz, h, n1, d)
    z = _norm_mm(
        xv, pl.BlockSpec((None, h, st, d), lambda b, i, j: (b, 0, i, 0)), (h, st), st * h, n1 // st, g, shift, scale,
        [p["hy_w_in"]], layer, 0, 3 * d, jax.ShapeDtypeStruct((bsz, n1, h, 3 * d), BF16),
        pl.BlockSpec((None, st, h, COL_TILE), lambda b, i, j: (b, i, 0, j)),
        mode="bias", bias=p["hy_b_in"][layer].reshape(1, 3 * d), tn=COL_TILE)
    cw, cb = p["hy_conv_w"][layer], p["hy_conv_b"][layer].reshape(1, 3 * d)
    skip = p["hy_skip"][layer].reshape(HYENA_ORDER, 1, d)
    a, u = _fft_s1(consts, z, 2, conv=(cw, cb))
    t = _fft_s2(a, kf, 0, consts)
    y1 = _fft_s3(consts, t, u, z, 0, cw, cb, skip, 0, BF16)
    a = _fft_s1(consts, y1, 0)
    t = _fft_s2(a, kf, 1, consts)
    y2 = _fft_s3(consts, t, y1, z, 1, cw, cb, skip, 1, F32)
    q = ROW_TILE // n1
    return _mm_res(
        [y2], [pl.BlockSpec((None, n1, q, d), lambda b, i, j: (b, 0, i, 0))], p["hy_w_out"], layer,
        p["hy_b_out"][layer].reshape(1, d), x, gate, ROW_TILE, a_mode="perm", perm=(n1, q))


def _attn_kernel(*refs, phases, ta, n_sub, to_classes):
    if to_classes:
        (pm_ref, q_ref, kp_ref, km_ref, kn_ref, vp_ref, vm_ref, vn_ref, o_ref, l_ref, kx_ref, vx_ref,
         on_ref, ot_ref, ln_ref, ls_ref, lt_ref) = refs
    else:
        q_ref, kp_ref, km_ref, kn_ref, vp_ref, vm_ref, vn_ref, o_ref, l_ref, kx_ref, vx_ref = refs
    i = pl.program_id(2)
    halo = ATT_BAND // phases
    qa = ATT_Q // phases
    ka = 2 * qa
    kx_ref[:, 0:halo] = kp_ref[...]
    kx_ref[:, halo:halo + ta] = km_ref[...]
    kx_ref[:, halo + ta:] = kn_ref[...]
    vx_ref[:, 0:halo] = vp_ref[...]
    vx_ref[:, halo:halo + ta] = vm_ref[...]
    vx_ref[:, halo + ta:] = vn_ref[...]
    row = lax.broadcasted_iota(jnp.int32, (ATT_Q, 2 * ATT_Q), 0)
    col = lax.broadcasted_iota(jnp.int32, (ATT_Q, 2 * ATT_Q), 1)
    cq, aq = row >> (qa.bit_length() - 1), row & (qa - 1)
    ck, ak = col >> (ka.bit_length() - 1), col & (ka - 1)
    delta = phases * (ak - aq) - ATT_BAND + ck - cq
    band = (delta >= -ATT_BAND) & (delta <= ATT_BAND)
    lane_head = lax.broadcasted_iota(jnp.int32, (ATT_Q, LANES), 1) >> (LSE_LANES.bit_length() - 1)
    for s in range(ta // qa):
        key_idx = phases * (i * ta + s * qa - halo + ak) + ck
        valid = band & (key_idx >= 0) & (key_idx < n_sub)
        lse_tile = jnp.zeros((ATT_Q, LANES), F32)
        for h in range(HEADS_PER_GROUP):
            cs = slice(h * HEAD_DIM, (h + 1) * HEAD_DIM)
            q = jnp.concatenate([q_ref[c, s * qa:(s + 1) * qa, cs] for c in range(phases)], axis=0)
            k = jnp.concatenate([kx_ref[c, s * qa:s * qa + ka, cs] for c in range(phases)], axis=0)
            v = jnp.concatenate([vx_ref[c, s * qa:s * qa + ka, cs] for c in range(phases)], axis=0)
            sc = lax.dot_general(q, k, (((1,), (1,)), ((), ())), preferred_element_type=F32)
            sc = jnp.where(valid, sc, NEG_BIG)
            m = jnp.max(sc, axis=-1, keepdims=True)
            pr = jnp.exp(sc - m)
            den = jnp.sum(pr, axis=-1, keepdims=True)
            o = (_dot(pr.astype(BF16), v) / den).astype(BF16)
            lse_tile = jnp.where(lane_head == h, m + jnp.log(den), lse_tile)
            if to_classes:
                on_ref[s * ATT_Q:(s + 1) * ATT_Q, cs] = o
            else:
                for c in range(phases):
                    o_ref[c, s * qa:(s + 1) * qa, cs] = o[c * qa:(c + 1) * qa]
        if to_classes:
            ln_ref[s * ATT_Q:(s + 1) * ATT_Q, :] = lse_tile
        else:
            for c in range(phases):
                l_ref[c, s * qa:(s + 1) * qa, :] = lse_tile[c * qa:(c + 1) * qa]
    if to_classes:
        na = ta // CLASSES
        _transpose_rows(ot_ref, on_ref, pm_ref, na, CLASSES)
        o_ref[...] = ot_ref[...].reshape(o_ref.shape)
        rest = ln_ref[...]
        total = jnp.zeros(rest.shape, F32)
        for _ in range(3):
            piece = rest.astype(BF16)
            rest = rest - piece.astype(F32)
            ls_ref[...] = piece
            _transpose_rows(lt_ref, ls_ref, pm_ref, na, CLASSES)
            total = total + lt_ref[...].astype(F32)
        l_ref[...] = total.reshape(l_ref.shape)


def _attn_group(qkv, col0, phases, n_sub, lead_grid, lead_block, lead_index, ta, *, to_classes=False,
                out_arr_shape=None, out_block=None, out_index=None):
    gw = GROUP_WIDTH
    halo = ATT_BAND // phases
    rows = qkv.shape[-2]
    per = ta // halo
    nblk = rows // halo
    cb = col0 // gw

    def spec(nrows, ridx, part):
        return pl.BlockSpec(tuple(lead_block) + (nrows, gw),
                            lambda b, rho, i: tuple(lead_index(b, rho)) + (ridx(i), cb + part))

    main = lambda part: spec(ta, lambda i: i, part)
    prev = lambda part: spec(halo, lambda i: jnp.maximum(i * per - 1, 0), part)
    nxt = lambda part: spec(halo, lambda i: jnp.minimum((i + 1) * per, nblk - 1), part)
    bsz = qkv.shape[0]
    in_specs = [main(0), prev(1), main(1), nxt(1), prev(2), main(2), nxt(2)]
    args = [qkv] * 7
    if to_classes:
        assert ta == PERM_ROWS
        out_specs = [pl.BlockSpec(out_block + (gw,), out_index), pl.BlockSpec(out_block + (LANES,), out_index)]
        scratch_extra = [pltpu.VMEM((ta, gw), BF16)] * 2 + [pltpu.VMEM((ta, LANES), F32)] + \
                        [pltpu.VMEM((ta, LANES), BF16)] * 2
        in_specs.insert(0, pl.BlockSpec((PERM_ROWS, PERM_ROWS), lambda b, rho, i: (0, 0)))
        args.insert(0, _perm_for(ta // CLASSES, CLASSES))
    else:
        out_arr_shape = qkv.shape[:-1]
        oidx = lambda b, rho, i: tuple(lead_index(b, rho)) + (i, 0)
        out_specs = [pl.BlockSpec(tuple(lead_block) + (ta, gw), oidx),
                     pl.BlockSpec(tuple(lead_block) + (ta, LANES), oidx)]
        scratch_extra = []
    kx_shape = (phases, ta + 2 * halo, gw)
    return pl.pallas_call(
        functools.partial(_attn_kernel, phases=phases, ta=ta, n_sub=n_sub, to_classes=to_classes),
        grid=(bsz, lead_grid, rows // ta),
        in_specs=in_specs,
        out_specs=out_specs,
        out_shape=[jax.ShapeDtypeStruct(tuple(out_arr_shape) + (gw,), BF16),
                   jax.ShapeDtypeStruct(tuple(out_arr_shape) + (LANES,), F32)],
        scratch_shapes=[pltpu.VMEM(kx_shape, BF16)] * 2 + scratch_extra,
        compiler_params=_params(("parallel", "parallel", "parallel")),
        name="attn_p%d" % phases + ("_cls" if to_classes else ""),
    )(*args)


def _rope_tables(pos):
    half = ROT_DIM // 2
    inv = ROPE_THETA ** (-jnp.arange(0, ROT_DIM, 2, dtype=F32) / ROT_DIM)
    ang = pos.astype(F32)[:, None] * inv[None, :]
    cos, sin = jnp.cos(ang), jnp.sin(ang)
    n = pos.shape[0]
    rest = HEAD_DIM - ROT_DIM
    c = jnp.concatenate([cos, cos, jnp.ones((n, rest), F32)], axis=1)
    s = jnp.concatenate([sin, sin, jnp.zeros((n, rest), F32)], axis=1)
    return c, s


def _rope_partner_matrix(width):
    half = ROT_DIM // 2
    m = np.zeros((width, width), np.float32)
    for base in range(0, width, HEAD_DIM):
        for k in range(half):
            m[base + k + half, base + k] = -1.0
            m[base + k, base + k + half] = 1.0
    return jnp.asarray(m, BF16)


def _attn_mixer(x, g, shift, scale, gate, p, layer):
    bsz, seq_len, d = x.shape
    gw = GROUP_WIDTH
    nc = seq_len // CLASSES
    ca = ROW_TILE // CLASSES
    w_in = p["at_w_in"]
    tabs = _rope_tables(jnp.arange(seq_len))
    qkv0 = _norm_mm(
        x, pl.BlockSpec((None, ROW_TILE, d), lambda b, i, j: (b, i, 0)), None, ROW_TILE, seq_len // ROW_TILE,
        g, shift, scale, [w_in], layer, 0, 3 * gw, jax.ShapeDtypeStruct((bsz, seq_len, 3 * gw), BF16),
        pl.BlockSpec((None, ROW_TILE, COL_TILE), lambda b, i, j: (b, i, j)), mode="rope",
        rope=(tabs, pl.BlockSpec((ROW_TILE, HEAD_DIM), lambda b, i, j: (i, 0))), tn=COL_TILE)
    pos_c = (jnp.arange(nc)[None, :] * CLASSES + jnp.arange(CLASSES)[:, None]).reshape(-1)
    tabs_c = [t.reshape(CLASSES, nc, HEAD_DIM) for t in _rope_tables(pos_c)]
    qkv12 = _norm_mm(
        x.reshape(bsz, nc, CLASSES, d), pl.BlockSpec((None, ca, CLASSES, d), lambda b, i, j: (b, i, 0, 0)),
        (ca, CLASSES), ROW_TILE, nc // ca, g, shift, scale, [w_in], layer, 3 * gw, 6 * gw,
        jax.ShapeDtypeStruct((bsz, CLASSES, nc, 6 * gw), BF16),
        pl.BlockSpec((None, CLASSES, ca, COL_TILE), lambda b, i, j: (b, 0, i, j)), mode="rope",
        rope=(tabs_c, pl.BlockSpec((CLASSES, ca, HEAD_DIM), lambda b, i, j: (0, i, 0))), tn=COL_TILE)
    cls_shape = (bsz, CLASSES, nc)
    ta0 = PERM_ROWS
    o0, l0 = _attn_group(
        qkv0.reshape(bsz, 1, seq_len, 3 * gw), 0, 1, seq_len, 1, (None, 1), lambda b, rho: (b, 0), ta0,
        to_classes=True, out_arr_shape=cls_shape, out_block=(None, CLASSES, ta0 // CLASSES),
        out_index=lambda b, rho, i: (b, 0, i, 0))
    dil1 = ATTN_PATTERNS[1][1]
    ph = CLASSES // dil1
    ta1 = min(64, nc)
    o1, l1 = _attn_group(
        qkv12.reshape(bsz, ph, dil1, nc, 6 * gw), 0, ph, seq_len // dil1, dil1, (None, ph, None),
        lambda b, rho: (b, 0, rho), ta1)
    ta2 = min(256, nc)
    o2, l2 = _attn_group(
        qkv12.reshape(bsz, CLASSES, 1, nc, 6 * gw), 3 * gw, 1, nc, CLASSES, (None, None, 1),
        lambda b, rho: (b, rho, 0), ta2)
    os_ = [o.reshape(cls_shape + (gw,)) for o in (o0, o1, o2)]
    ls = [l.reshape(cls_shape + (LANES,)) for l in (l0, l1, l2)]
    blk = lambda width: pl.BlockSpec((None, CLASSES, ca, width), lambda b, i, j: (b, 0, i, 0))
    return _mm_res(os_ + ls, [blk(gw)] * 3 + [blk(LANES)] * 3, p["at_w_out"], layer, jnp.zeros((1, d), F32),
                   x, gate, ROW_TILE, a_mode="merge", perm=(CLASSES, ca))


def _ffn(x, g, shift, scale, gate, p, layer):
    bsz, seq_len, d = x.shape
    dff = p["ffn_w_gate"].shape[-1]
    tiles = seq_len // ROW_TILE
    xs = pl.BlockSpec((None, ROW_TILE, d), lambda b, i, j: (b, i, 0))
    hs = pl.BlockSpec((None, ROW_TILE, 512), lambda b, i, j: (b, i, j))
    hmid = _norm_mm(x, xs, None, ROW_TILE, tiles, g, shift, scale, [p["ffn_w_gate"], p["ffn_w_up"]], layer, 0,
                    dff, jax.ShapeDtypeStruct((bsz, seq_len, dff), BF16), hs, mode="swiglu")
    return _mm_res([hmid], [pl.BlockSpec((None, ROW_TILE, dff), lambda b, i, j: (b, i, 0))],
                   p["ffn_w_down"], layer, jnp.zeros((1, d), F32), x, gate, ROW_TILE)


def _encoder(x, mods, final_mod, p):
    bsz, seq_len, d = x.shape
    n1 = 2 * seq_len // DFT_N2
    consts = _dft_consts(n1, DFT_N2)
    consts["d"] = d
    for i in range(DEPTH):
        sh_m, sc_m, g_m, sh_f, sc_f, g_f = [mods[i][:, None, k * d:(k + 1) * d] for k in range(6)]
        j = i // 2
        if i % 2 == 0:
            kf = _hyena_filter_spectra(seq_len, consts, p["hy_fw1"][j], p["hy_fb1"][j], p["hy_ffreq"][j],
                                       p["hy_fw2"][j], p["hy_fb2"][j], p["hy_fw3"][j], p["hy_decay"][j])
            x = _hyena_mixer(x, p["norm_mix"][i], sh_m, sc_m, g_m, p, j, kf, consts)
        else:
            x = _attn_mixer(x, p["norm_mix"][i], sh_m, sc_m, g_m, p, j)
        x = _ffn(x, p["norm_ffn"][i], sh_f, sc_f, g_f, p, i)
    sh, sc = final_mod[:, None, :d], final_mod[:, None, d:]
    return _final(x, p["final_norm"], sh, sc)


def kernel(x_prompt, x_sample, c_prompt, c_sample, ada_w, ada_b, norm_mix, norm_ffn, hy_w_in, hy_b_in, hy_conv_w, hy_conv_b, hy_fw1, hy_fb1, hy_ffreq, hy_fw2, hy_fb2, hy_fw3, hy_decay, hy_skip, hy_w_out, hy_b_out, at_w_in, at_w_out, ffn_w_gate, ffn_w_up, ffn_w_down, final_norm, final_ada_w, final_ada_b):
    d = x_prompt.shape[-1]
    bp, bs = c_prompt.shape[0], c_sample.shape[0]
    pad = -(bp + bs) % (2 * SUBLANES)
    c_all = jnp.concatenate([c_prompt, c_sample, jnp.zeros((pad, d), F32)], axis=0)
    mods = _ada(c_all, ada_w, ada_b)
    fmod = _ada(c_all, final_ada_w[None], final_ada_b[None])[0]
    p = dict(norm_mix=norm_mix, norm_ffn=norm_ffn,
             hy_w_in=hy_w_in.astype(BF16), hy_b_in=hy_b_in, hy_conv_w=hy_conv_w, hy_conv_b=hy_conv_b,
             hy_fw1=hy_fw1, hy_fb1=hy_fb1, hy_ffreq=hy_ffreq, hy_fw2=hy_fw2, hy_fb2=hy_fb2, hy_fw3=hy_fw3,
             hy_decay=hy_decay, hy_skip=hy_skip, hy_w_out=hy_w_out.astype(BF16), hy_b_out=hy_b_out,
             at_w_in=at_w_in.astype(BF16), at_w_out=at_w_out.astype(BF16),
             ffn_w_gate=ffn_w_gate.astype(BF16), ffn_w_up=ffn_w_up.astype(BF16),
             ffn_w_down=ffn_w_down.astype(BF16), final_norm=final_norm)
    y_prompt = _encoder(x_prompt, mods[:, :bp], fmod[:bp], p)
    y_sample = _encoder(x_sample, mods[:, bp:bp + bs], fmod[bp:bp + bs], p)
    return (y_prompt, y_sample)
```

```python
import functools
import math

import numpy as np
import jax
import jax.numpy as jnp
from jax import lax
from jax.experimental import pallas as pl
from jax.experimental.pallas import tpu as pltpu

F32 = jnp.float32
BF16 = jnp.bfloat16
EPS = 1e-6

DEPTH = 4
HYENA_ORDER = 2
N_DIRS = 2
FILTER_BANDS = 16
FILTER_EMB = 1 + 2 * FILTER_BANDS
ATTN_PATTERNS = ((128, 1), (512, 4), (2048, 16))
HEADS_PER_GROUP = 8
HEAD_DIM = 128
GROUP_WIDTH = HEADS_PER_GROUP * HEAD_DIM
ROT_DIM = HEAD_DIM // 4
ROPE_THETA = 500000.0

LANES = 128
SUBLANES = 8
VMEM_LIMIT_BYTES = 56 * 1024 * 1024

DFT_N2 = 256
SLAB_ROWS = DFT_N2 // 2
ATT_BAND = 64
ATT_Q = 2 * ATT_BAND
CLASSES = 16
LSE_LANES = LANES // HEADS_PER_GROUP
ROW_TILE = 1024
COL_TILE = 1024
NEG_BIG = -1e30


def _params(sem):
    return pltpu.CompilerParams(dimension_semantics=sem, vmem_limit_bytes=VMEM_LIMIT_BYTES)


def _dot(a, b):
    return jnp.dot(a, b, preferred_element_type=F32)


def _split(a):
    hi = a.astype(BF16)
    lo = (a - hi.astype(F32)).astype(BF16)
    return hi, lo


def _dot3(a, b):
    ah, al = _split(a)
    bh, bl = _split(b)
    return _dot(ah, bh) + _dot(al, bh) + _dot(ah, bl)


def _modnorm(x, g, shift, scale):
    ms = jnp.mean(x * x, axis=-1, keepdims=True)
    return (x * lax.rsqrt(ms + EPS)) * (g * (1.0 + scale)) + shift


PERM_ROWS = 256


def _perm_matrix(p, q):
    m = np.zeros((PERM_ROWS, PERM_ROWS), np.float32)
    pi, qi = np.meshgrid(np.arange(p), np.arange(q), indexing="ij")
    m[(qi * p + pi).ravel(), (pi * q + qi).ravel()] = 1.0
    return jnp.asarray(m, BF16)


def _perm_for(p, q):
    assert (q <= 16 and p % (PERM_ROWS // q) == 0) or (p <= 16 and q % (PERM_ROWS // p) == 0), (p, q)
    return _perm_matrix(PERM_ROWS // q, q) if q <= 16 else _perm_matrix(p, PERM_ROWS // p)


def _transpose_rows(dst_ref, src_ref, pm_ref, p, q):
    if q <= 16:
        pg = PERM_ROWS // q
        for grp in range(p // pg):
            t = _dot(pm_ref[...], src_ref[grp * PERM_ROWS:(grp + 1) * PERM_ROWS, :]).astype(BF16)
            for qi in range(q):
                dst_ref[qi * p + grp * pg:qi * p + (grp + 1) * pg, :] = t[qi * pg:(qi + 1) * pg]
    else:
        qg = PERM_ROWS // p
        for grp in range(q // qg):
            blk = jnp.concatenate([src_ref[pi * q + grp * qg:pi * q + (grp + 1) * qg, :] for pi in range(p)],
                                  axis=0)
            dst_ref[grp * PERM_ROWS:(grp + 1) * PERM_ROWS, :] = _dot(pm_ref[...], blk).astype(BF16)


def _ada_kernel(c_ref, w_ref, b_ref, o_ref):
    c = c_ref[...]
    cs = c * jax.nn.sigmoid(c)
    o_ref[...] = _dot3(cs, w_ref[...]) + b_ref[...]


def _ada(c_all, w, b, tn=1024):
    nl, d, no = w.shape
    r = c_all.shape[0]
    return pl.pallas_call(
        _ada_kernel,
        grid=(nl, no // tn),
        in_specs=[
            pl.BlockSpec((r, d), lambda l, j: (0, 0)),
            pl.BlockSpec((None, d, tn), lambda l, j: (l, 0, j)),
            pl.BlockSpec((None, 1, tn), lambda l, j: (l, 0, j)),
        ],
        out_specs=pl.BlockSpec((None, r, tn), lambda l, j: (l, 0, j)),
        out_shape=jax.ShapeDtypeStruct((nl, r, no), F32),
        compiler_params=_params(("parallel", "parallel")),
        name="ada_mod",
    )(c_all, w, b.reshape(nl, 1, no))


def _norm_mm_kernel(*refs, mode, tn, perm):
    if perm:
        pm_ref, refs = refs[0], refs[1:]
        h0_ref, refs = refs[-1], refs[:-1]
    if mode == "swiglu":
        x_ref, g_ref, sh_ref, sc_ref, wg_ref, wu_ref, o_ref, h_ref = refs
    elif mode == "rope":
        x_ref, g_ref, sh_ref, sc_ref, w_ref, c_ref, s_ref, rot_ref, o_ref, h_ref = refs
    else:
        x_ref, g_ref, sh_ref, sc_ref, w_ref, b_ref, o_ref, h_ref = refs
    j = pl.program_id(2)

    @pl.when(j == 0)
    def _():
        h = _modnorm(x_ref[...].reshape(h_ref.shape), g_ref[...], sh_ref[...], sc_ref[...]).astype(BF16)
        if perm:
            h0_ref[...] = h
            _transpose_rows(h_ref, h0_ref, pm_ref, *perm)
        else:
            h_ref[...] = h

    h = h_ref[...]
    if mode == "swiglu":
        a = _dot(h, wg_ref[...])
        u = _dot(h, wu_ref[...])
        o_ref[...] = (a * jax.nn.sigmoid(a) * u).astype(o_ref.dtype).reshape(o_ref.shape)
    elif mode == "rope":
        acc = _dot(h, w_ref[...])
        part = (j // (GROUP_WIDTH // tn)) % 3

        @pl.when(part == 2)
        def _():
            o_ref[...] = acc.astype(o_ref.dtype).reshape(o_ref.shape)

        @pl.when(part != 2)
        def _():
            reps = tn // HEAD_DIM
            tabs = [t[...].reshape(acc.shape[0], HEAD_DIM) for t in (c_ref, s_ref)]
            c, s = [jnp.concatenate([t] * reps, axis=1) for t in tabs]
            accb = acc.astype(BF16)
            rw = rot_ref.shape[0]
            partner = jnp.concatenate([_dot(accb[:, k * rw:(k + 1) * rw], rot_ref[...]) for k in range(tn // rw)],
                                      axis=1)
            qs = jnp.where(part == 0, HEAD_DIM ** -0.5, 1.0).astype(F32)
            o_ref[...] = ((acc * c + partner * s) * qs).astype(o_ref.dtype).reshape(o_ref.shape)
    else:
        o_ref[...] = (_dot(h, w_ref[...]) + b_ref[...]).astype(o_ref.dtype).reshape(o_ref.shape)


def _norm_mm(x, x_spec, perm, rows, grid_rows, g, shift, scale, ws, w_layer, col0, nout, out_shape,
             out_spec, *, mode, bias=None, rope=None, tn=512):
    d = x.shape[-1]
    bsz = x.shape[0]
    cb = col0 // tn
    vec = pl.BlockSpec((None, 1, d), lambda b, i, j: (b, 0, 0))
    in_specs = [x_spec, pl.BlockSpec((1, d), lambda b, i, j: (0, 0)), vec, vec]
    in_specs += [pl.BlockSpec((None, d, tn), lambda b, i, j: (w_layer, 0, cb + j)) for _ in ws]
    args = [x, g.reshape(1, d), shift, scale, *ws]
    scratch = [pltpu.VMEM((rows, d), BF16)]
    if perm:
        in_specs.insert(0, pl.BlockSpec((PERM_ROWS, PERM_ROWS), lambda b, i, j: (0, 0)))
        args.insert(0, _perm_for(*perm))
        scratch.append(pltpu.VMEM((rows, d), BF16))
    if mode == "rope":
        tabs, tab_spec = rope
        rw = 4 * HEAD_DIM
        in_specs += [tab_spec] * 2 + [pl.BlockSpec((rw, rw), lambda b, i, j: (0, 0))]
        args += list(tabs) + [_rope_partner_matrix(rw)]
    elif mode == "bias":
        in_specs.append(pl.BlockSpec((1, tn), lambda b, i, j: (0, cb + j)))
        args.append(bias)
    return pl.pallas_call(
        functools.partial(_norm_mm_kernel, mode=mode, tn=tn, perm=perm),
        grid=(bsz, grid_rows, nout // tn),
        in_specs=in_specs,
        out_specs=out_spec,
        out_shape=out_shape,
        scratch_shapes=scratch,
        compiler_params=_params(("parallel", "parallel", "arbitrary")),
        name="norm_mm_" + mode,
    )(*args)


def _mm_res_kernel(*refs, a_mode, perm):
    if a_mode == "merge":
        (pm_ref, o0, o1, o2, l0, l1, l2, e_ref, w_ref, b_ref, x_ref, gt_ref, out_ref, a_ref, a0_ref) = refs
    elif a_mode == "perm":
        pm_ref, a_in, w_ref, b_ref, x_ref, gt_ref, out_ref, a_ref, a0_ref = refs
    else:
        a_in, w_ref, b_ref, x_ref, gt_ref, out_ref = refs
    j = pl.program_id(2)

    if a_mode != "plain":
        @pl.when(j == 0)
        def _():
            k = a_ref.shape[1]
            if a_mode == "merge":
                ls = [l[...].reshape(-1, LANES) for l in (l0, l1, l2)]
                mx = jnp.maximum(jnp.maximum(ls[0], ls[1]), ls[2])
                ws = [jnp.exp(l - mx) for l in ls]
                inv = 1.0 / (ws[0] + ws[1] + ws[2])
                num = jnp.zeros((ls[0].shape[0], k), F32)
                for w, o in zip(ws, (o0, o1, o2)):
                    hi, lo = _split(w * inv)
                    wide = _dot(hi, e_ref[...]) + _dot(lo, e_ref[...])
                    num = num + wide * o[...].reshape(-1, k).astype(F32)
                a0_ref[...] = num.astype(BF16)
            else:
                a0_ref[...] = a_in[...].reshape(-1, k).astype(BF16)
            _transpose_rows(a_ref, a0_ref, pm_ref, *perm)

        a = a_ref[...]
    else:
        a = a_in[...]
    out_ref[...] = x_ref[...] + gt_ref[...] * (_dot(a, w_ref[...]) + b_ref[...])


def _head_spread_matrix(width):
    m = np.zeros((LANES, width), np.float32)
    for hd in range(width // HEAD_DIM):
        m[hd * LSE_LANES, hd * HEAD_DIM:(hd + 1) * HEAD_DIM] = 1.0
    return jnp.asarray(m, BF16)


def _mm_res(a_list, a_specs, w, w_layer, bias, x, gate, rows, *, a_mode="plain", perm=None, tn=512):
    bsz, seq_len, d = x.shape
    k = w.shape[1]
    blk = pl.BlockSpec((None, rows, tn), lambda b, i, j: (b, i, j))
    if a_mode == "merge":
        a_list = list(a_list) + [_head_spread_matrix(k)]
        a_specs = list(a_specs) + [pl.BlockSpec((LANES, k), lambda b, i, j: (0, 0))]
    in_specs = list(a_specs) + [
        pl.BlockSpec((None, k, tn), lambda b, i, j: (w_layer, 0, j)),
        pl.BlockSpec((1, tn), lambda b, i, j: (0, j)),
        blk,
        pl.BlockSpec((None, 1, tn), lambda b, i, j: (b, 0, j)),
    ]
    args = [*a_list, w, bias, x, gate]
    scratch = []
    if a_mode != "plain":
        in_specs.insert(0, pl.BlockSpec((PERM_ROWS, PERM_ROWS), lambda b, i, j: (0, 0)))
        args.insert(0, _perm_for(*perm))
        scratch = [pltpu.VMEM((rows, k), BF16)] * 2
    return pl.pallas_call(
        functools.partial(_mm_res_kernel, a_mode=a_mode, perm=perm),
        grid=(bsz, seq_len // rows, d // tn),
        in_specs=in_specs,
        out_specs=blk,
        out_shape=jax.ShapeDtypeStruct(x.shape, F32),
        scratch_shapes=scratch,
        compiler_params=_params(("parallel", "parallel", "arbitrary")),
        name="mm_res_" + a_mode,
    )(*args)


def _final_kernel(x_ref, g_ref, sh_ref, sc_ref, o_ref):
    o_ref[...] = _modnorm(x_ref[...], g_ref[...], sh_ref[...], sc_ref[...])


def _final(x, g, shift, scale, tm=512):
    bsz, seq_len, d = x.shape
    vec = pl.BlockSpec((None, 1, d), lambda b, i: (b, 0, 0))
    blk = pl.BlockSpec((None, tm, d), lambda b, i: (b, i, 0))
    return pl.pallas_call(
        _final_kernel,
        grid=(bsz, seq_len // tm),
        in_specs=[blk, pl.BlockSpec((1, d), lambda b, i: (0, 0)), vec, vec],
        out_specs=blk,
        out_shape=jax.ShapeDtypeStruct(x.shape, F32),
        compiler_params=_params(("parallel", "parallel")),
        name="final_norm",
    )(x, g.reshape(1, d), shift, scale)


def _filter_positions(seq_len, n1):
    n = 2 * seq_len
    h = SLAB_ROWS
    half = jnp.arange(2)[:, None, None]
    s = jnp.arange(n1)[None, :, None]
    r = jnp.arange(h)[None, None, :]
    idx = ((half * h + r) * n1 + s).reshape(n)
    pos = jnp.where(idx < seq_len, idx, n - idx).astype(F32)
    sign = jnp.where(idx < seq_len, 1.0, jnp.where(idx == seq_len, 0.0, -1.0)).astype(F32)
    t = pos / max(seq_len - 1, 1)
    bands = jnp.linspace(1e-4, FILTER_BANDS - 1, FILTER_BANDS, dtype=F32)
    ang = 2.0 * math.pi * pos[:, None] * bands[None, :] / seq_len
    z = jnp.concatenate([t[:, None], jnp.cos(ang), -jnp.sin(ang)], axis=-1)
    z = jnp.pad(z, ((0, 0), (0, LANES - FILTER_EMB - 1)))
    return jnp.concatenate([z, sign[:, None]], axis=-1)


def _filter_mlp_kernel(z_ref, w1_ref, b1_ref, f_ref, w2_ref, b2_ref, o_ref):
    f = f_ref[...]
    a = jnp.sin(f[0:1, :] * (_dot3(z_ref[...], w1_ref[...]) + b1_ref[...]))
    o_ref[...] = jnp.sin(f[1:2, :] * (_dot3(a, w2_ref[...]) + b2_ref[...]))


def _filter_mlp(zf, w1, b1, freq, w2, b2, tr=512):
    n = zf.shape[0]
    hid = w1.shape[1]
    w1p = jnp.pad(w1, ((0, LANES - w1.shape[0]), (0, 0)))
    full = lambda shape: pl.BlockSpec(shape, lambda i: (0,) * len(shape))
    return pl.pallas_call(
        _filter_mlp_kernel,
        grid=(n // tr,),
        in_specs=[pl.BlockSpec((tr, LANES), lambda i: (i, 0)), full((LANES, hid)), full((1, hid)),
                  full((2, hid)), full((hid, hid)), full((1, hid))],
        out_specs=pl.BlockSpec((tr, hid), lambda i: (i, 0)),
        out_shape=jax.ShapeDtypeStruct((n, hid), F32),
        compiler_params=_params(("parallel",)),
        name="filter_mlp",
    )(zf, w1p, b1.reshape(1, hid), freq, w2, b2.reshape(1, hid))


def _dft_consts(n1, n2):
    n = n1 * n2
    k2 = np.arange(n2 // 2)[:, None]
    nn2 = np.arange(n2)[None, :]
    ph = 2.0 * np.pi * nn2 * (k2 + 0.5) / n2
    f_s1 = np.concatenate([np.cos(ph), -np.sin(ph)], axis=0)
    m = np.arange(n2 // 2)[:, None]
    kk = np.arange(n2 // 2)[None, :]
    ph3 = 2.0 * np.pi * m * (kk + 0.5) / n2
    f_s3 = (2.0 / n) * np.concatenate([np.cos(ph3), -np.sin(ph3)], axis=1)
    a = 2.0 * np.pi * np.outer(np.arange(n1), np.arange(n1)) / n1
    c, s = np.cos(a), -np.sin(a)
    f_fwd = np.block([[c, -s], [s, c]])
    f_inv = np.block([[c, s], [-s, c]])
    th = 2.0 * np.pi * (np.arange(n2 // 2)[:, None] + 0.5) * np.arange(n1)[None, :] / n
    as_bf = lambda x: jnp.asarray(x, F32).astype(BF16)
    tc, ts = jnp.asarray(np.cos(th), F32), jnp.asarray(np.sin(th), F32)
    ff, fi = jnp.asarray(f_fwd, F32), jnp.asarray(f_inv, F32)
    fl, fr = ff[None, :, :n1], ff[None, :, n1:]
    g_fwd = jnp.concatenate([fl * tc[:, None, :] - fr * ts[:, None, :],
                             fl * ts[:, None, :] + fr * tc[:, None, :]], axis=2).astype(BF16)
    it, ib = fi[None, :n1, :], fi[None, n1:, :]
    g_inv = jnp.concatenate([tc[:, :, None] * it - ts[:, :, None] * ib,
                             ts[:, :, None] * it + tc[:, :, None] * ib], axis=1).astype(BF16)
    return dict(f_s1=as_bf(f_s1[:, :n2 // 2]), f_s1_hi=as_bf(f_s1[:, n2 // 2:]), f_s3=as_bf(f_s3),
                g_fwd=g_fwd, g_inv=g_inv, n1=n1)


def _slab_tile(n1):
    return min(n1, 8)


HI16 = -65536
HALF_ULP16 = 0x8000


def _pack_c(re, im):
    rb = lax.bitcast_convert_type(re, jnp.int32) + HALF_ULP16
    ib = lax.bitcast_convert_type(im, jnp.int32) + HALF_ULP16
    return (rb & HI16) | lax.shift_right_logical(ib, 16)


def _pack_bf16(re, im):
    rb = lax.bitcast_convert_type(re.astype(F32), jnp.int32)
    ib = lax.bitcast_convert_type(im.astype(F32), jnp.int32)
    return rb | lax.shift_right_logical(ib, 16)


def _unpack_c(p):
    re = lax.bitcast_convert_type(p & HI16, F32)
    im = lax.bitcast_convert_type(lax.shift_left(p, 16), F32)
    return re, im


def _shift_rows(x, down):
    rows = x.shape[0]
    row = lax.broadcasted_iota(jnp.int32, x.shape, 0)
    if down:
        return jnp.where(row == 0, 0.0, pltpu.roll(x, 1, 0))
    return jnp.where(row == rows - 1, 0.0, pltpu.roll(x, rows - 1, 0))


def _short_conv_slabs(main_ref, prev_ref, next_ref, w_ref, b_ref, first, last):
    n_slabs = main_ref.shape[0]
    prev = prev_ref[0].astype(F32)
    prev = jnp.where(first, _shift_rows(prev, True), prev)
    nxt = next_ref[0].astype(F32)
    nxt = jnp.where(last, _shift_rows(nxt, False), nxt)
    w = w_ref[...]
    out = []
    for s in range(n_slabs):
        up = prev if s == 0 else main_ref[s - 1].astype(F32)
        dn = nxt if s == n_slabs - 1 else main_ref[s + 1].astype(F32)
        out.append(up * w[0:1, :] + main_ref[s].astype(F32) * w[1:2, :] + dn * w[2:3, :] + b_ref[...])
    return out


def _fft_s1_kernel(*refs, short_conv):
    re_ref, im_ref, ret_ref, imt_ref = refs[-4:]
    if short_conv:
        pm_ref, f_ref, m_ref, p_ref, n_ref, w_ref, b_ref, o_ref, u_ref = refs[:-4]
        t = pl.program_id(1)
        slabs = _short_conv_slabs(m_ref, p_ref, n_ref, w_ref, b_ref, t == 0, t == pl.num_programs(1) - 1)
    else:
        pm_ref, f_ref, m_ref, o_ref = refs[:-4]
        slabs = [m_ref[s] for s in range(m_ref.shape[0])]
    half = o_ref.shape[0]
    for s, u in enumerate(slabs):
        if short_conv:
            u_ref[s] = u.astype(u_ref.dtype)
        r = _dot(f_ref[...], u.astype(BF16))
        re_ref[s * half:(s + 1) * half, :] = r[:half].astype(BF16)
        im_ref[s * half:(s + 1) * half, :] = r[half:].astype(BF16)
    _transpose_rows(ret_ref, re_ref, pm_ref, len(slabs), half)
    _transpose_rows(imt_ref, im_ref, pm_ref, len(slabs), half)
    o_ref[...] = _pack_bf16(ret_ref[...], imt_ref[...]).reshape(o_ref.shape)


def _fft_s1(consts, src, col_block, conv=None, ct=512):
    bsz, n1, h, c = src.shape
    d = consts["d"]
    st = _slab_tile(n1)
    cpb = d // ct
    f = consts["f_s1"]
    main = pl.BlockSpec((None, st, h, ct), lambda b, t, j: (b, t, 0, col_block * cpb + j))
    a_spec = pl.BlockSpec((None, h, st, ct), lambda b, t, j: (b, 0, t, j))
    a_shape = jax.ShapeDtypeStruct((bsz, h, n1, d), jnp.int32)
    fspec = pl.BlockSpec(f.shape, lambda b, t, j: (0, 0))
    pspec = pl.BlockSpec((PERM_ROWS, PERM_ROWS), lambda b, t, j: (0, 0))
    pm = _perm_for(st, h)
    scratch = [pltpu.VMEM((st * h, ct), BF16)] * 4
    if conv is None:
        return pl.pallas_call(
            functools.partial(_fft_s1_kernel, short_conv=False),
            grid=(bsz, n1 // st, cpb),
            in_specs=[pspec, fspec, main], out_specs=a_spec, out_shape=a_shape, scratch_shapes=scratch,
            compiler_params=_params(("parallel", "parallel", "parallel")), name="fft_s1",
        )(pm, f, src)
    w, b = conv
    prev = pl.BlockSpec((None, 1, h, ct), lambda b, t, j: (b, (t * st + n1 - 1) % n1, 0, col_block * cpb + j))
    nxt = pl.BlockSpec((None, 1, h, ct), lambda b, t, j: (b, ((t + 1) * st) % n1, 0, col_block * cpb + j))
    wspec = pl.BlockSpec((3, ct), lambda b, t, j: (0, col_block * cpb + j))
    bspec = pl.BlockSpec((1, ct), lambda b, t, j: (0, col_block * cpb + j))
    u_spec = pl.BlockSpec((None, st, h, ct), lambda b, t, j: (b, t, 0, j))
    return pl.pallas_call(
        functools.partial(_fft_s1_kernel, short_conv=True),
        grid=(bsz, n1 // st, cpb),
        in_specs=[pspec, fspec, main, prev, nxt, wspec, bspec],
        out_specs=[a_spec, u_spec],
        out_shape=[a_shape, jax.ShapeDtypeStruct((bsz, n1, h, d), BF16)],
        scratch_shapes=scratch,
        compiler_params=_params(("parallel", "parallel", "parallel")), name="fft_s1_conv",
    )(pm, f, src, src, src, w, b)


def _filter_s1_kernel(pm_ref, flo_ref, fhi_ref, alo_ref, ahi_ref, zlo_ref, zhi_ref, wf_ref, wb_ref, df_ref,
                      db_ref, o_ref, ss_ref, re_ref, im_ref, ret_ref, imt_ref):
    half = o_ref.shape[0]

    @pl.when(pl.program_id(2) == 0)
    def _():
        ss_ref[...] = jnp.zeros_like(ss_ref)

    ss = jnp.zeros(ss_ref.shape, F32)
    for s in range(alo_ref.shape[0]):
        taps = []
        for a_ref, z_ref, w_ref, d_ref in ((alo_ref, zlo_ref, wf_ref, df_ref), (ahi_ref, zhi_ref, wb_ref, db_ref)):
            z = z_ref[s]
            t, sign = z[:, 0:1], z[:, LANES - 1:LANES]
            h = _dot3(a_ref[s], w_ref[...]) * jnp.exp(-t * jnp.abs(d_ref[...])) * sign
            ss = ss + jnp.sum(h * h, axis=0, keepdims=True)
            taps.append(h)
        r = _dot(flo_ref[...], taps[0].astype(BF16)) + _dot(fhi_ref[...], taps[1].astype(BF16))
        re_ref[s * half:(s + 1) * half, :] = r[:half].astype(BF16)
        im_ref[s * half:(s + 1) * half, :] = r[half:].astype(BF16)
    ss_ref[...] += ss
    n_slabs = alo_ref.shape[0]
    _transpose_rows(ret_ref, re_ref, pm_ref, n_slabs, half)
    _transpose_rows(imt_ref, im_ref, pm_ref, n_slabs, half)
    o_ref[...] = _pack_bf16(ret_ref[...], imt_ref[...]).reshape(o_ref.shape)


def _filter_s1(consts, a2, zf, w3, decay, ct=256):
    n1, h = consts["n1"], SLAB_ROWS
    hid = a2.shape[1]
    d = decay.shape[-1]
    nct = d // ct
    st = _slab_tile(n1)
    flo, fhi = consts["f_s1"], consts["f_s1_hi"]
    dec = decay.reshape(1, HYENA_ORDER * N_DIRS * d)
    fspec = pl.BlockSpec(flo.shape, lambda o, j, t: (0, 0))
    rows = lambda half, width: pl.BlockSpec((None, st, h, width), lambda o, j, t: (half, t, 0, 0))
    wcol = lambda dirn, nrow: pl.BlockSpec((nrow, ct), lambda o, j, t: (0, (o * N_DIRS + dirn) * nct + j))
    return pl.pallas_call(
        _filter_s1_kernel,
        grid=(HYENA_ORDER, nct, n1 // st),
        in_specs=[pl.BlockSpec((PERM_ROWS, PERM_ROWS), lambda o, j, t: (0, 0)), fspec, fspec,
                  rows(0, hid), rows(1, hid), rows(0, LANES), rows(1, LANES),
                  wcol(0, hid), wcol(1, hid), wcol(0, 1), wcol(1, 1)],
        out_specs=[pl.BlockSpec((None, h, st, ct), lambda o, j, t: (o, 0, t, j)),
                   pl.BlockSpec((None, 1, ct), lambda o, j, t: (o, 0, j))],
        out_shape=[jax.ShapeDtypeStruct((HYENA_ORDER, h, n1, d), jnp.int32),
                   jax.ShapeDtypeStruct((HYENA_ORDER, 1, d), F32)],
        scratch_shapes=[pltpu.VMEM((st * h, ct), BF16)] * 4,
        compiler_params=_params(("parallel", "parallel", "arbitrary")), name="filter_s1",
    )(_perm_for(st, h), flo, fhi, a2.reshape(2, n1, h, hid), a2.reshape(2, n1, h, hid),
      zf.reshape(2, n1, h, LANES), zf.reshape(2, n1, h, LANES), w3, w3, dec, dec)


def _fft_s2f_kernel(a_ref, gf_ref, ss_ref, o_ref, *, kb):
    n1 = a_ref.shape[1]
    scale = lax.rsqrt(ss_ref[...] + EPS)

    def body(kk, carry):
        ar, ai = _unpack_c(a_ref[kk])
        x = _dot(gf_ref[kk], jnp.concatenate([ar, ai], axis=0).astype(BF16))
        o_ref[kk] = _pack_c(x[:n1] * scale, x[n1:] * scale)
        return carry

    lax.fori_loop(0, kb, body, 0, unroll=min(kb, 4))


def _fft_s2_kernel(a_ref, k_ref, gf_ref, gi_ref, o_ref, *, kb):
    n1 = a_ref.shape[1]

    def body(kk, carry):
        ar, ai = _unpack_c(a_ref[kk])
        x = _dot(gf_ref[kk], jnp.concatenate([ar, ai], axis=0).astype(BF16))
        xr, xi = x[:n1], x[n1:]
        kr, ki = _unpack_c(k_ref[kk])
        zr = xr * kr - xi * ki
        zi = xr * ki + xi * kr
        y = _dot(gi_ref[kk], jnp.concatenate([zr, zi], axis=0).astype(BF16))
        o_ref[kk] = _pack_c(y[:n1], y[n1:])
        return carry

    lax.fori_loop(0, kb, body, 0, unroll=min(kb, 4))


def _s2_tiles(n1, d):
    kb = max(1, 512 // n1)
    ct = min(d, 512)
    return kb, ct


def _fft_s2f(a, sumsq, consts):
    n_o, k2n, n1, d = a.shape
    kb, ct = _s2_tiles(n1, d)
    blk = pl.BlockSpec((None, kb, n1, ct), lambda k, j, o: (o, k, 0, j))
    return pl.pallas_call(
        functools.partial(_fft_s2f_kernel, kb=kb),
        grid=(k2n // kb, d // ct, n_o),
        in_specs=[blk, pl.BlockSpec((kb, 2 * n1, 2 * n1), lambda k, j, o: (k, 0, 0)),
                  pl.BlockSpec((None, 1, ct), lambda k, j, o: (o, 0, j))],
        out_specs=blk,
        out_shape=jax.ShapeDtypeStruct(a.shape, jnp.int32),
        compiler_params=_params(("parallel", "parallel", "parallel")),
        name="fft_s2_filter",
    )(a, consts["g_fwd"], sumsq)


def _fft_s2(a, kf, order, consts):
    bsz, k2n, n1, d = a.shape
    kb, ct = _s2_tiles(n1, d)
    blk = pl.BlockSpec((None, kb, n1, ct), lambda k, j, b: (b, k, 0, j))
    mat = pl.BlockSpec((kb, 2 * n1, 2 * n1), lambda k, j, b: (k, 0, 0))
    return pl.pallas_call(
        functools.partial(_fft_s2_kernel, kb=kb),
        grid=(k2n // kb, d // ct, bsz),
        in_specs=[blk, pl.BlockSpec((None, kb, n1, ct), lambda k, j, b: (order, k, 0, j)), mat, mat],
        out_specs=blk,
        out_shape=jax.ShapeDtypeStruct(a.shape, jnp.int32),
        compiler_params=_params(("parallel", "parallel", "parallel")),
        name="fft_s2",
    )(a, kf, consts["g_fwd"], consts["g_inv"])


def _fft_s3_kernel(pm_ref, f_ref, t_ref, u_ref, gm_ref, gp_ref, gn_ref, w_ref, b_ref, sk_ref, o_ref,
                   re_ref, im_ref, ret_ref, imt_ref):
    t_id = pl.program_id(1)
    half, st = t_ref.shape[0], t_ref.shape[1]
    re, im = _unpack_c(t_ref[...].reshape(half * st, t_ref.shape[2]))
    re_ref[...] = re.astype(BF16)
    im_ref[...] = im.astype(BF16)
    _transpose_rows(ret_ref, re_ref, pm_ref, half, st)
    _transpose_rows(imt_ref, im_ref, pm_ref, half, st)
    gates = _short_conv_slabs(gm_ref, gp_ref, gn_ref, w_ref, b_ref, t_id == 0, t_id == pl.num_programs(1) - 1)
    for s, gate in enumerate(gates):
        t = jnp.concatenate([ret_ref[s * half:(s + 1) * half, :], imt_ref[s * half:(s + 1) * half, :]], axis=0)
        y = _dot(f_ref[...], t)
        o_ref[s] = (gate * (y + u_ref[s].astype(F32) * sk_ref[...])).astype(o_ref.dtype)


def _fft_s3(consts, t, u, z, gate_block, conv_w, conv_b, skip, order, out_dtype, ct=512):
    bsz, h, n1, d = t.shape
    st = _slab_tile(n1)
    cpb = d // ct
    f = consts["f_s3"]
    gcol = lambda j: gate_block * cpb + j
    slab = lambda idx: pl.BlockSpec((None, 1, h, ct), lambda b, tt, j: (b, idx(tt), 0, gcol(j)))
    return pl.pallas_call(
        _fft_s3_kernel,
        grid=(bsz, n1 // st, cpb),
        scratch_shapes=[pltpu.VMEM((st * h, ct), BF16)] * 4,
        in_specs=[
            pl.BlockSpec((PERM_ROWS, PERM_ROWS), lambda b, tt, j: (0, 0)),
            pl.BlockSpec(f.shape, lambda b, tt, j: (0, 0)),
            pl.BlockSpec((None, h, st, ct), lambda b, tt, j: (b, 0, tt, j)),
            pl.BlockSpec((None, st, h, ct), lambda b, tt, j: (b, tt, 0, j)),
            pl.BlockSpec((None, st, h, ct), lambda b, tt, j: (b, tt, 0, gcol(j))),
            slab(lambda tt: (tt * st + n1 - 1) % n1),
            slab(lambda tt: ((tt + 1) * st) % n1),
            pl.BlockSpec((3, ct), lambda b, tt, j: (0, gcol(j))),
            pl.BlockSpec((1, ct), lambda b, tt, j: (0, gcol(j))),
            pl.BlockSpec((None, 1, ct), lambda b, tt, j: (order, 0, j)),
        ],
        out_specs=pl.BlockSpec((None, st, h, ct), lambda b, tt, j: (b, tt, 0, j)),
        out_shape=jax.ShapeDtypeStruct((bsz, n1, h, d), out_dtype),
        compiler_params=_params(("parallel", "parallel", "parallel")),
        name="fft_s3",
    )(_perm_for(h, st), f, t, u, z, z, z, conv_w, conv_b, skip)


def _hyena_filter_spectra(seq_len, consts, fw1, fb1, ffreq, fw2, fb2, fw3, decay):
    d = decay.shape[-1]
    n1 = consts["n1"]
    zf = _filter_positions(seq_len, n1)
    a2 = _filter_mlp(zf, fw1, fb1, ffreq, fw2, fb2)
    a, sumsq = _filter_s1(consts, a2, zf, fw3, decay)
    return _fft_s2f(a, sumsq, consts)


def _hyena_mixer(x, g, shift, scale, gate, p, layer, kf, consts):
    bsz, seq_len, d = x.shape
    n1 = consts["n1"]
    h = SLAB_ROWS
    st = _slab_tile(n1)
    xv = x.reshape(bsz, h, n1, d)
    z = _norm_mm(
        xv, pl.BlockSpec((None, h, st, d), lambda b, i, j: (b, 0, i, 0)), (h, st), st * h, n1 // st, g, shift, scale,
        [p["hy_w_in"]], layer, 0, 3 * d, jax.ShapeDtypeStruct((bsz, n1, h, 3 * d), BF16),
        pl.BlockSpec((None, st, h, COL_TILE), lambda b, i, j: (b, i, 0, j)),
        mode="bias", bias=p["hy_b_in"][layer].reshape(1, 3 * d), tn=COL_TILE)
    cw, cb = p["hy_conv_w"][layer], p["hy_conv_b"][layer].reshape(1, 3 * d)
    skip = p["hy_skip"][layer].reshape(HYENA_ORDER, 1, d)
    a, u = _fft_s1(consts, z, 2, conv=(cw, cb))
    t = _fft_s2(a, kf, 0, consts)
    y1 = _fft_s3(consts, t, u, z, 0, cw, cb, skip, 0, BF16)
    a = _fft_s1(consts, y1, 0)
    t = _fft_s2(a, kf, 1, consts)
    y2 = _fft_s3(consts, t, y1, z, 1, cw, cb, skip, 1, F32)
    q = ROW_TILE // n1
    return _mm_res(
        [y2], [pl.BlockSpec((None, n1, q, d), lambda b, i, j: (b, 0, i, 0))], p["hy_w_out"], layer,
        p["hy_b_out"][layer].reshape(1, d), x, gate, ROW_TILE, a_mode="perm", perm=(n1, q))


def _attn_kernel(*refs, phases, ta, n_sub, to_classes):
    if to_classes:
        (pm_ref, q_ref, kp_ref, km_ref, kn_ref, vp_ref, vm_ref, vn_ref, o_ref, l_ref, kx_ref, vx_ref,
         on_ref, ot_ref, ln_ref, ls_ref, lt_ref) = refs
    else:
        q_ref, kp_ref, km_ref, kn_ref, vp_ref, vm_ref, vn_ref, o_ref, l_ref, kx_ref, vx_ref = refs
    i = pl.program_id(2)
    halo = ATT_BAND // phases
    qa = ATT_Q // phases
    ka = 2 * qa
    kx_ref[:, 0:halo] = kp_ref[...]
    kx_ref[:, halo:halo + ta] = km_ref[...]
    kx_ref[:, halo + ta:] = kn_ref[...]
    vx_ref[:, 0:halo] = vp_ref[...]
    vx_ref[:, halo:halo + ta] = vm_ref[...]
    vx_ref[:, halo + ta:] = vn_ref[...]
    row = lax.broadcasted_iota(jnp.int32, (ATT_Q, 2 * ATT_Q), 0)
    col = lax.broadcasted_iota(jnp.int32, (ATT_Q, 2 * ATT_Q), 1)
    cq, aq = row >> (qa.bit_length() - 1), row & (qa - 1)
    ck, ak = col >> (ka.bit_length() - 1), col & (ka - 1)
    delta = phases * (ak - aq) - ATT_BAND + ck - cq
    band = (delta >= -ATT_BAND) & (delta <= ATT_BAND)
    lane_head = lax.broadcasted_iota(jnp.int32, (ATT_Q, LANES), 1) >> (LSE_LANES.bit_length() - 1)
    for s in range(ta // qa):
        key_idx = phases * (i * ta + s * qa - halo + ak) + ck
        valid = band & (key_idx >= 0) & (key_idx < n_sub)
        lse_tile = jnp.zeros((ATT_Q, LANES), F32)
        for h in range(HEADS_PER_GROUP):
            cs = slice(h * HEAD_DIM, (h + 1) * HEAD_DIM)
            q = jnp.concatenate([q_ref[c, s * qa:(s + 1) * qa, cs] for c in range(phases)], axis=0)
            k = jnp.concatenate([kx_ref[c, s * qa:s * qa + ka, cs] for c in range(phases)], axis=0)
            v = jnp.concatenate([vx_ref[c, s * qa:s * qa + ka, cs] for c in range(phases)], axis=0)
            sc = lax.dot_general(q, k, (((1,), (1,)), ((), ())), preferred_element_type=F32)
            sc = jnp.where(valid, sc, NEG_BIG)
            m = jnp.max(sc, axis=-1, keepdims=True)
            pr = jnp.exp(sc - m)
            den = jnp.sum(pr, axis=-1, keepdims=True)
            o = (_dot(pr.astype(BF16), v) / den).astype(BF16)
            lse_tile = jnp.where(lane_head == h, m + jnp.log(den), lse_tile)
            if to_classes:
                on_ref[s * ATT_Q:(s + 1) * ATT_Q, cs] = o
            else:
                for c in range(phases):
                    o_ref[c, s * qa:(s + 1) * qa, cs] = o[c * qa:(c + 1) * qa]
        if to_classes:
            ln_ref[s * ATT_Q:(s + 1) * ATT_Q, :] = lse_tile
        else:
            for c in range(phases):
                l_ref[c, s * qa:(s + 1) * qa, :] = lse_tile[c * qa:(c + 1) * qa]
    if to_classes:
        na = ta // CLASSES
        _transpose_rows(ot_ref, on_ref, pm_ref, na, CLASSES)
        o_ref[...] = ot_ref[...].reshape(o_ref.shape)
        rest = ln_ref[...]
        total = jnp.zeros(rest.shape, F32)
        for _ in range(3):
            piece = rest.astype(BF16)
            rest = rest - piece.astype(F32)
            ls_ref[...] = piece
            _transpose_rows(lt_ref, ls_ref, pm_ref, na, CLASSES)
            total = total + lt_ref[...].astype(F32)
        l_ref[...] = total.reshape(l_ref.shape)


def _attn_group(qkv, col0, phases, n_sub, lead_grid, lead_block, lead_index, ta, *, to_classes=False,
                out_arr_shape=None, out_block=None, out_index=None):
    gw = GROUP_WIDTH
    halo = ATT_BAND // phases
    rows = qkv.shape[-2]
    per = ta // halo
    nblk = rows // halo
    cb = col0 // gw

    def spec(nrows, ridx, part):
        return pl.BlockSpec(tuple(lead_block) + (nrows, gw),
                            lambda b, rho, i: tuple(lead_index(b, rho)) + (ridx(i), cb + part))

    main = lambda part: spec(ta, lambda i: i, part)
    prev = lambda part: spec(halo, lambda i: jnp.maximum(i * per - 1, 0), part)
    nxt = lambda part: spec(halo, lambda i: jnp.minimum((i + 1) * per, nblk - 1), part)
    bsz = qkv.shape[0]
    in_specs = [main(0), prev(1), main(1), nxt(1), prev(2), main(2), nxt(2)]
    args = [qkv] * 7
    if to_classes:
        assert ta == PERM_ROWS
        out_specs = [pl.BlockSpec(out_block + (gw,), out_index), pl.BlockSpec(out_block + (LANES,), out_index)]
        scratch_extra = [pltpu.VMEM((ta, gw), BF16)] * 2 + [pltpu.VMEM((ta, LANES), F32)] + \
                        [pltpu.VMEM((ta, LANES), BF16)] * 2
        in_specs.insert(0, pl.BlockSpec((PERM_ROWS, PERM_ROWS), lambda b, rho, i: (0, 0)))
        args.insert(0, _perm_for(ta // CLASSES, CLASSES))
    else:
        out_arr_shape = qkv.shape[:-1]
        oidx = lambda b, rho, i: tuple(lead_index(b, rho)) + (i, 0)
        out_specs = [pl.BlockSpec(tuple(lead_block) + (ta, gw), oidx),
                     pl.BlockSpec(tuple(lead_block) + (ta, LANES), oidx)]
        scratch_extra = []
    kx_shape = (phases, ta + 2 * halo, gw)
    return pl.pallas_call(
        functools.partial(_attn_kernel, phases=phases, ta=ta, n_sub=n_sub, to_classes=to_classes),
        grid=(bsz, lead_grid, rows // ta),
        in_specs=in_specs,
        out_specs=out_specs,
        out_shape=[jax.ShapeDtypeStruct(tuple(out_arr_shape) + (gw,), BF16),
                   jax.ShapeDtypeStruct(tuple(out_arr_shape) + (LANES,), F32)],
        scratch_shapes=[pltpu.VMEM(kx_shape, BF16)] * 2 + scratch_extra,
        compiler_params=_params(("parallel", "parallel", "parallel")),
        name="attn_p%d" % phases + ("_cls" if to_classes else ""),
    )(*args)


def _rope_tables(pos):
    half = ROT_DIM // 2
    inv = ROPE_THETA ** (-jnp.arange(0, ROT_DIM, 2, dtype=F32) / ROT_DIM)
    ang = pos.astype(F32)[:, None] * inv[None, :]
    cos, sin = jnp.cos(ang), jnp.sin(ang)
    n = pos.shape[0]
    rest = HEAD_DIM - ROT_DIM
    c = jnp.concatenate([cos, cos, jnp.ones((n, rest), F32)], axis=1)
    s = jnp.concatenate([sin, sin, jnp.zeros((n, rest), F32)], axis=1)
    return c, s


def _rope_partner_matrix(width):
    half = ROT_DIM // 2
    m = np.zeros((width, width), np.float32)
    for base in range(0, width, HEAD_DIM):
        for k in range(half):
            m[base + k + half, base + k] = -1.0
            m[base + k, base + k + half] = 1.0
    return jnp.asarray(m, BF16)


def _attn_mixer(x, g, shift, scale, gate, p, layer):
    bsz, seq_len, d = x.shape
    gw = GROUP_WIDTH
    nc = seq_len // CLASSES
    ca = ROW_TILE // CLASSES
    w_in = p["at_w_in"]
    tabs = _rope_tables(jnp.arange(seq_len))
    qkv0 = _norm_mm(
        x, pl.BlockSpec((None, ROW_TILE, d), lambda b, i, j: (b, i, 0)), None, ROW_TILE, seq_len // ROW_TILE,
        g, shift, scale, [w_in], layer, 0, 3 * gw, jax.ShapeDtypeStruct((bsz, seq_len, 3 * gw), BF16),
        pl.BlockSpec((None, ROW_TILE, COL_TILE), lambda b, i, j: (b, i, j)), mode="rope",
        rope=(tabs, pl.BlockSpec((ROW_TILE, HEAD_DIM), lambda b, i, j: (i, 0))), tn=COL_TILE)
    pos_c = (jnp.arange(nc)[None, :] * CLASSES + jnp.arange(CLASSES)[:, None]).reshape(-1)
    tabs_c = [t.reshape(CLASSES, nc, HEAD_DIM) for t in _rope_tables(pos_c)]
    qkv12 = _norm_mm(
        x.reshape(bsz, nc, CLASSES, d), pl.BlockSpec((None, ca, CLASSES, d), lambda b, i, j: (b, i, 0, 0)),
        (ca, CLASSES), ROW_TILE, nc // ca, g, shift, scale, [w_in], layer, 3 * gw, 6 * gw,
        jax.ShapeDtypeStruct((bsz, CLASSES, nc, 6 * gw), BF16),
        pl.BlockSpec((None, CLASSES, ca, COL_TILE), lambda b, i, j: (b, 0, i, j)), mode="rope",
        rope=(tabs_c, pl.BlockSpec((CLASSES, ca, HEAD_DIM), lambda b, i, j: (0, i, 0))), tn=COL_TILE)
    cls_shape = (bsz, CLASSES, nc)
    ta0 = PERM_ROWS
    o0, l0 = _attn_group(
        qkv0.reshape(bsz, 1, seq_len, 3 * gw), 0, 1, seq_len, 1, (None, 1), lambda b, rho: (b, 0), ta0,
        to_classes=True, out_arr_shape=cls_shape, out_block=(None, CLASSES, ta0 // CLASSES),
        out_index=lambda b, rho, i: (b, 0, i, 0))
    dil1 = ATTN_PATTERNS[1][1]
    ph = CLASSES // dil1
    ta1 = min(64, nc)
    o1, l1 = _attn_group(
        qkv12.reshape(bsz, ph, dil1, nc, 6 * gw), 0, ph, seq_len // dil1, dil1, (None, ph, None),
        lambda b, rho: (b, 0, rho), ta1)
    ta2 = min(256, nc)
    o2, l2 = _attn_group(
        qkv12.reshape(bsz, CLASSES, 1, nc, 6 * gw), 3 * gw, 1, nc, CLASSES, (None, None, 1),
        lambda b, rho: (b, rho, 0), ta2)
    os_ = [o.reshape(cls_shape + (gw,)) for o in (o0, o1, o2)]
    ls = [l.reshape(cls_shape + (LANES,)) for l in (l0, l1, l2)]
    blk = lambda width: pl.BlockSpec((None, CLASSES, ca, width), lambda b, i, j: (b, 0, i, 0))
    return _mm_res(os_ + ls, [blk(gw)] * 3 + [blk(LANES)] * 3, p["at_w_out"], layer, jnp.zeros((1, d), F32),
                   x, gate, ROW_TILE, a_mode="merge", perm=(CLASSES, ca), tn=COL_TILE)


def _ffn(x, g, shift, scale, gate, p, layer):
    bsz, seq_len, d = x.shape
    dff = p["ffn_w_gate"].shape[-1]
    tiles = seq_len // ROW_TILE
    xs = pl.BlockSpec((None, ROW_TILE, d), lambda b, i, j: (b, i, 0))
    hs = pl.BlockSpec((None, ROW_TILE, 512), lambda b, i, j: (b, i, j))
    hmid = _norm_mm(x, xs, None, ROW_TILE, tiles, g, shift, scale, [p["ffn_w_gate"], p["ffn_w_up"]], layer, 0,
                    dff, jax.ShapeDtypeStruct((bsz, seq_len, dff), BF16), hs, mode="swiglu")
    return _mm_res([hmid], [pl.BlockSpec((None, ROW_TILE, dff), lambda b, i, j: (b, i, 0))],
                   p["ffn_w_down"], layer, jnp.zeros((1, d), F32), x, gate, ROW_TILE)


def _encoder(x, mods, final_mod, p):
    bsz, seq_len, d = x.shape
    n1 = 2 * seq_len // DFT_N2
    consts = _dft_consts(n1, DFT_N2)
    consts["d"] = d
    for i in range(DEPTH):
        sh_m, sc_m, g_m, sh_f, sc_f, g_f = [mods[i][:, None, k * d:(k + 1) * d] for k in range(6)]
        j = i // 2
        if i % 2 == 0:
            kf = _hyena_filter_spectra(seq_len, consts, p["hy_fw1"][j], p["hy_fb1"][j], p["hy_ffreq"][j],
                                       p["hy_fw2"][j], p["hy_fb2"][j], p["hy_fw3"][j], p["hy_decay"][j])
            x = _hyena_mixer(x, p["norm_mix"][i], sh_m, sc_m, g_m, p, j, kf, consts)
        else:
            x = _attn_mixer(x, p["norm_mix"][i], sh_m, sc_m, g_m, p, j)
        x = _ffn(x, p["norm_ffn"][i], sh_f, sc_f, g_f, p, i)
    sh, sc = final_mod[:, None, :d], final_mod[:, None, d:]
    return _final(x, p["final_norm"], sh, sc)


def kernel(x_prompt, x_sample, c_prompt, c_sample, ada_w, ada_b, norm_mix, norm_ffn, hy_w_in, hy_b_in, hy_conv_w, hy_conv_b, hy_fw1, hy_fb1, hy_ffreq, hy_fw2, hy_fb2, hy_fw3, hy_decay, hy_skip, hy_w_out, hy_b_out, at_w_in, at_w_out, ffn_w_gate, ffn_w_up, ffn_w_down, final_norm, final_ada_w, final_ada_b):
    d = x_prompt.shape[-1]
    bp, bs = c_prompt.shape[0], c_sample.shape[0]
    pad = -(bp + bs) % (2 * SUBLANES)
    c_all = jnp.concatenate([c_prompt, c_sample, jnp.zeros((pad, d), F32)], axis=0)
    mods = _ada(c_all, ada_w, ada_b)
    fmod = _ada(c_all, final_ada_w[None], final_ada_b[None])[0]
    p = dict(norm_mix=norm_mix, norm_ffn=norm_ffn,
             hy_w_in=hy_w_in.astype(BF16), hy_b_in=hy_b_in, hy_conv_w=hy_conv_w, hy_conv_b=hy_conv_b,
             hy_fw1=hy_fw1, hy_fb1=hy_fb1, hy_ffreq=hy_ffreq, hy_fw2=hy_fw2, hy_fb2=hy_fb2, hy_fw3=hy_fw3,
             hy_decay=hy_decay, hy_skip=hy_skip, hy_w_out=hy_w_out.astype(BF16), hy_b_out=hy_b_out,
             at_w_in=at_w_in.astype(BF16), at_w_out=at_w_out.astype(BF16),
             ffn_w_gate=ffn_w_gate.astype(BF16), ffn_w_up=ffn_w_up.astype(BF16),
             ffn_w_down=ffn_w_down.astype(BF16), final_norm=final_norm)
    y_prompt = _encoder(x_prompt, mods[:, :bp], fmod[:bp], p)
    y_sample = _encoder(x_sample, mods[:, bp:bp + bs], fmod[bp:bp + bs], p)
    return (y_prompt, y_sample)
```

```python
import functools
import math

import numpy as np
import jax
import jax.numpy as jnp
from jax import lax
from jax.experimental import pallas as pl
from jax.experimental.pallas import tpu as pltpu

F32 = jnp.float32
BF16 = jnp.bfloat16
EPS = 1e-6

DEPTH = 4
HYENA_ORDER = 2
N_DIRS = 2
FILTER_BANDS = 16
FILTER_EMB = 1 + 2 * FILTER_BANDS
ATTN_PATTERNS = ((128, 1), (512, 4), (2048, 16))
HEADS_PER_GROUP = 8
HEAD_DIM = 128
GROUP_WIDTH = HEADS_PER_GROUP * HEAD_DIM
ROT_DIM = HEAD_DIM // 4
ROPE_THETA = 500000.0

LANES = 128
SUBLANES = 8
VMEM_LIMIT_BYTES = 56 * 1024 * 1024

DFT_N2 = 256
SLAB_ROWS = DFT_N2 // 2
ATT_BAND = 64
ATT_Q = 2 * ATT_BAND
CLASSES = 16
LSE_LANES = LANES // HEADS_PER_GROUP
ROW_TILE = 1024
COL_TILE = 1024
NEG_BIG = -1e30


def _params(sem):
    return pltpu.CompilerParams(dimension_semantics=sem, vmem_limit_bytes=VMEM_LIMIT_BYTES)


def _dot(a, b):
    return jnp.dot(a, b, preferred_element_type=F32)


def _split(a):
    hi = a.astype(BF16)
    lo = (a - hi.astype(F32)).astype(BF16)
    return hi, lo


def _dot3(a, b):
    ah, al = _split(a)
    bh, bl = _split(b)
    return _dot(ah, bh) + _dot(al, bh) + _dot(ah, bl)


def _modnorm(x, g, shift, scale):
    ms = jnp.mean(x * x, axis=-1, keepdims=True)
    return (x * lax.rsqrt(ms + EPS)) * (g * (1.0 + scale)) + shift


PERM_ROWS = 256


def _perm_matrix(p, q):
    m = np.zeros((PERM_ROWS, PERM_ROWS), np.float32)
    pi, qi = np.meshgrid(np.arange(p), np.arange(q), indexing="ij")
    m[(qi * p + pi).ravel(), (pi * q + qi).ravel()] = 1.0
    return jnp.asarray(m, BF16)


def _perm_for(p, q):
    assert (q <= 16 and p % (PERM_ROWS // q) == 0) or (p <= 16 and q % (PERM_ROWS // p) == 0), (p, q)
    return _perm_matrix(PERM_ROWS // q, q) if q <= 16 else _perm_matrix(p, PERM_ROWS // p)


def _transpose_rows(dst_ref, src_ref, pm_ref, p, q):
    if q <= 16:
        pg = PERM_ROWS // q
        for grp in range(p // pg):
            t = _dot(pm_ref[...], src_ref[grp * PERM_ROWS:(grp + 1) * PERM_ROWS, :]).astype(BF16)
            for qi in range(q):
                dst_ref[qi * p + grp * pg:qi * p + (grp + 1) * pg, :] = t[qi * pg:(qi + 1) * pg]
    else:
        qg = PERM_ROWS // p
        for grp in range(q // qg):
            blk = jnp.concatenate([src_ref[pi * q + grp * qg:pi * q + (grp + 1) * qg, :] for pi in range(p)],
                                  axis=0)
            dst_ref[grp * PERM_ROWS:(grp + 1) * PERM_ROWS, :] = _dot(pm_ref[...], blk).astype(BF16)


def _ada_kernel(c_ref, w_ref, b_ref, o_ref):
    c = c_ref[...]
    cs = c * jax.nn.sigmoid(c)
    o_ref[...] = _dot3(cs, w_ref[...]) + b_ref[...]


def _ada(c_all, w, b, tn=1024):
    nl, d, no = w.shape
    r = c_all.shape[0]
    return pl.pallas_call(
        _ada_kernel,
        grid=(nl, no // tn),
        in_specs=[
            pl.BlockSpec((r, d), lambda l, j: (0, 0)),
            pl.BlockSpec((None, d, tn), lambda l, j: (l, 0, j)),
            pl.BlockSpec((None, 1, tn), lambda l, j: (l, 0, j)),
        ],
        out_specs=pl.BlockSpec((None, r, tn), lambda l, j: (l, 0, j)),
        out_shape=jax.ShapeDtypeStruct((nl, r, no), F32),
        compiler_params=_params(("parallel", "parallel")),
        name="ada_mod",
    )(c_all, w, b.reshape(nl, 1, no))


def _norm_mm_kernel(*refs, mode, tn, perm):
    if perm:
        pm_ref, refs = refs[0], refs[1:]
        h0_ref, refs = refs[-1], refs[:-1]
    if mode == "swiglu":
        x_ref, g_ref, sh_ref, sc_ref, wg_ref, wu_ref, o_ref, h_ref = refs
    elif mode == "rope":
        x_ref, g_ref, sh_ref, sc_ref, w_ref, c_ref, s_ref, rot_ref, o_ref, h_ref = refs
    else:
        x_ref, g_ref, sh_ref, sc_ref, w_ref, b_ref, o_ref, h_ref = refs
    j = pl.program_id(2)

    @pl.when(j == 0)
    def _():
        h = _modnorm(x_ref[...].reshape(h_ref.shape), g_ref[...], sh_ref[...], sc_ref[...]).astype(BF16)
        if perm:
            h0_ref[...] = h
            _transpose_rows(h_ref, h0_ref, pm_ref, *perm)
        else:
            h_ref[...] = h

    h = h_ref[...]
    if mode == "swiglu":
        a = _dot(h, wg_ref[...])
        u = _dot(h, wu_ref[...])
        o_ref[...] = (a * jax.nn.sigmoid(a) * u).astype(o_ref.dtype).reshape(o_ref.shape)
    elif mode == "rope":
        acc = _dot(h, w_ref[...])
        part = (j // (GROUP_WIDTH // tn)) % 3

        @pl.when(part == 2)
        def _():
            o_ref[...] = acc.astype(o_ref.dtype).reshape(o_ref.shape)

        @pl.when(part != 2)
        def _():
            reps = tn // HEAD_DIM
            tabs = [t[...].reshape(acc.shape[0], HEAD_DIM) for t in (c_ref, s_ref)]
            c, s = [jnp.concatenate([t] * reps, axis=1) for t in tabs]
            accb = acc.astype(BF16)
            rw = rot_ref.shape[0]
            partner = jnp.concatenate([_dot(accb[:, k * rw:(k + 1) * rw], rot_ref[...]) for k in range(tn // rw)],
                                      axis=1)
            qs = jnp.where(part == 0, HEAD_DIM ** -0.5, 1.0).astype(F32)
            o_ref[...] = ((acc * c + partner * s) * qs).astype(o_ref.dtype).reshape(o_ref.shape)
    else:
        o_ref[...] = (_dot(h, w_ref[...]) + b_ref[...]).astype(o_ref.dtype).reshape(o_ref.shape)


def _norm_mm(x, x_spec, perm, rows, grid_rows, g, shift, scale, ws, w_layer, col0, nout, out_shape,
             out_spec, *, mode, bias=None, rope=None, tn=512):
    d = x.shape[-1]
    bsz = x.shape[0]
    cb = col0 // tn
    vec = pl.BlockSpec((None, 1, d), lambda b, i, j: (b, 0, 0))
    in_specs = [x_spec, pl.BlockSpec((1, d), lambda b, i, j: (0, 0)), vec, vec]
    in_specs += [pl.BlockSpec((None, d, tn), lambda b, i, j: (w_layer, 0, cb + j)) for _ in ws]
    args = [x, g.reshape(1, d), shift, scale, *ws]
    scratch = [pltpu.VMEM((rows, d), BF16)]
    if perm:
        in_specs.insert(0, pl.BlockSpec((PERM_ROWS, PERM_ROWS), lambda b, i, j: (0, 0)))
        args.insert(0, _perm_for(*perm))
        scratch.append(pltpu.VMEM((rows, d), BF16))
    if mode == "rope":
        tabs, tab_spec = rope
        rw = 4 * HEAD_DIM
        in_specs += [tab_spec] * 2 + [pl.BlockSpec((rw, rw), lambda b, i, j: (0, 0))]
        args += list(tabs) + [_rope_partner_matrix(rw)]
    elif mode == "bias":
        in_specs.append(pl.BlockSpec((1, tn), lambda b, i, j: (0, cb + j)))
        args.append(bias)
    return pl.pallas_call(
        functools.partial(_norm_mm_kernel, mode=mode, tn=tn, perm=perm),
        grid=(bsz, grid_rows, nout // tn),
        in_specs=in_specs,
        out_specs=out_spec,
        out_shape=out_shape,
        scratch_shapes=scratch,
        compiler_params=_params(("parallel", "parallel", "arbitrary")),
        name="norm_mm_" + mode,
    )(*args)


def _mm_res_kernel(*refs, a_mode, perm):
    if a_mode == "merge":
        (pm_ref, o0, o1, o2, l0, l1, l2, e_ref, w_ref, b_ref, x_ref, gt_ref, out_ref, a_ref, a0_ref) = refs
    elif a_mode == "perm":
        pm_ref, a_in, w_ref, b_ref, x_ref, gt_ref, out_ref, a_ref, a0_ref = refs
    else:
        a_in, w_ref, b_ref, x_ref, gt_ref, out_ref = refs
    j = pl.program_id(2)

    if a_mode != "plain":
        @pl.when(j == 0)
        def _():
            k = a_ref.shape[1]
            if a_mode == "merge":
                ls = [l[...].reshape(-1, LANES) for l in (l0, l1, l2)]
                mx = jnp.maximum(jnp.maximum(ls[0], ls[1]), ls[2])
                ws = [jnp.exp(l - mx) for l in ls]
                inv = 1.0 / (ws[0] + ws[1] + ws[2])
                num = jnp.zeros((ls[0].shape[0], k), F32)
                for w, o in zip(ws, (o0, o1, o2)):
                    hi, lo = _split(w * inv)
                    wide = _dot(hi, e_ref[...]) + _dot(lo, e_ref[...])
                    num = num + wide * o[...].reshape(-1, k).astype(F32)
                a0_ref[...] = num.astype(BF16)
            else:
                a0_ref[...] = a_in[...].reshape(-1, k).astype(BF16)
            _transpose_rows(a_ref, a0_ref, pm_ref, *perm)

        a = a_ref[...]
    else:
        a = a_in[...]
    out_ref[...] = x_ref[...] + gt_ref[...] * (_dot(a, w_ref[...]) + b_ref[...])


def _head_spread_matrix(width):
    m = np.zeros((LANES, width), np.float32)
    for hd in range(width // HEAD_DIM):
        m[hd * LSE_LANES, hd * HEAD_DIM:(hd + 1) * HEAD_DIM] = 1.0
    return jnp.asarray(m, BF16)


def _mm_res(a_list, a_specs, w, w_layer, bias, x, gate, rows, *, a_mode="plain", perm=None, tn=512):
    bsz, seq_len, d = x.shape
    k = w.shape[1]
    blk = pl.BlockSpec((None, rows, tn), lambda b, i, j: (b, i, j))
    if a_mode == "merge":
        a_list = list(a_list) + [_head_spread_matrix(k)]
        a_specs = list(a_specs) + [pl.BlockSpec((LANES, k), lambda b, i, j: (0, 0))]
    in_specs = list(a_specs) + [
        pl.BlockSpec((None, k, tn), lambda b, i, j: (w_layer, 0, j)),
        pl.BlockSpec((1, tn), lambda b, i, j: (0, j)),
        blk,
        pl.BlockSpec((None, 1, tn), lambda b, i, j: (b, 0, j)),
    ]
    args = [*a_list, w, bias, x, gate]
    scratch = []
    if a_mode != "plain":
        in_specs.insert(0, pl.BlockSpec((PERM_ROWS, PERM_ROWS), lambda b, i, j: (0, 0)))
        args.insert(0, _perm_for(*perm))
        scratch = [pltpu.VMEM((rows, k), BF16)] * 2
    return pl.pallas_call(
        functools.partial(_mm_res_kernel, a_mode=a_mode, perm=perm),
        grid=(bsz, seq_len // rows, d // tn),
        in_specs=in_specs,
        out_specs=blk,
        out_shape=jax.ShapeDtypeStruct(x.shape, F32),
        scratch_shapes=scratch,
        compiler_params=_params(("parallel", "parallel", "arbitrary")),
        name="mm_res_" + a_mode,
    )(*args)


def _final_kernel(x_ref, g_ref, sh_ref, sc_ref, o_ref):
    o_ref[...] = _modnorm(x_ref[...], g_ref[...], sh_ref[...], sc_ref[...])


def _final(x, g, shift, scale, tm=512):
    bsz, seq_len, d = x.shape
    vec = pl.BlockSpec((None, 1, d), lambda b, i: (b, 0, 0))
    blk = pl.BlockSpec((None, tm, d), lambda b, i: (b, i, 0))
    return pl.pallas_call(
        _final_kernel,
        grid=(bsz, seq_len // tm),
        in_specs=[blk, pl.BlockSpec((1, d), lambda b, i: (0, 0)), vec, vec],
        out_specs=blk,
        out_shape=jax.ShapeDtypeStruct(x.shape, F32),
        compiler_params=_params(("parallel", "parallel")),
        name="final_norm",
    )(x, g.reshape(1, d), shift, scale)


def _filter_positions(seq_len, n1):
    n = 2 * seq_len
    h = SLAB_ROWS
    half = jnp.arange(2)[:, None, None]
    s = jnp.arange(n1)[None, :, None]
    r = jnp.arange(h)[None, None, :]
    idx = ((half * h + r) * n1 + s).reshape(n)
    pos = jnp.where(idx < seq_len, idx, n - idx).astype(F32)
    sign = jnp.where(idx < seq_len, 1.0, jnp.where(idx == seq_len, 0.0, -1.0)).astype(F32)
    t = pos / max(seq_len - 1, 1)
    bands = jnp.linspace(1e-4, FILTER_BANDS - 1, FILTER_BANDS, dtype=F32)
    ang = 2.0 * math.pi * pos[:, None] * bands[None, :] / seq_len
    z = jnp.concatenate([t[:, None], jnp.cos(ang), -jnp.sin(ang)], axis=-1)
    z = jnp.pad(z, ((0, 0), (0, LANES - FILTER_EMB - 1)))
    return jnp.concatenate([z, sign[:, None]], axis=-1)


def _filter_mlp_kernel(z_ref, w1_ref, b1_ref, f_ref, w2_ref, b2_ref, o_ref):
    f = f_ref[...]
    a = jnp.sin(f[0:1, :] * (_dot3(z_ref[...], w1_ref[...]) + b1_ref[...]))
    o_ref[...] = jnp.sin(f[1:2, :] * (_dot3(a, w2_ref[...]) + b2_ref[...]))


def _filter_mlp(zf, w1, b1, freq, w2, b2, tr=512):
    n = zf.shape[0]
    hid = w1.shape[1]
    w1p = jnp.pad(w1, ((0, LANES - w1.shape[0]), (0, 0)))
    full = lambda shape: pl.BlockSpec(shape, lambda i: (0,) * len(shape))
    return pl.pallas_call(
        _filter_mlp_kernel,
        grid=(n // tr,),
        in_specs=[pl.BlockSpec((tr, LANES), lambda i: (i, 0)), full((LANES, hid)), full((1, hid)),
                  full((2, hid)), full((hid, hid)), full((1, hid))],
        out_specs=pl.BlockSpec((tr, hid), lambda i: (i, 0)),
        out_shape=jax.ShapeDtypeStruct((n, hid), F32),
        compiler_params=_params(("parallel",)),
        name="filter_mlp",
    )(zf, w1p, b1.reshape(1, hid), freq, w2, b2.reshape(1, hid))


def _dft_consts(n1, n2):
    n = n1 * n2
    k2 = np.arange(n2 // 2)[:, None]
    nn2 = np.arange(n2)[None, :]
    ph = 2.0 * np.pi * nn2 * (k2 + 0.5) / n2
    f_s1 = np.concatenate([np.cos(ph), -np.sin(ph)], axis=0)
    m = np.arange(n2 // 2)[:, None]
    kk = np.arange(n2 // 2)[None, :]
    ph3 = 2.0 * np.pi * m * (kk + 0.5) / n2
    f_s3 = (2.0 / n) * np.concatenate([np.cos(ph3), -np.sin(ph3)], axis=1)
    a = 2.0 * np.pi * np.outer(np.arange(n1), np.arange(n1)) / n1
    c, s = np.cos(a), -np.sin(a)
    f_fwd = np.block([[c, -s], [s, c]])
    f_inv = np.block([[c, s], [-s, c]])
    th = 2.0 * np.pi * (np.arange(n2 // 2)[:, None] + 0.5) * np.arange(n1)[None, :] / n
    as_bf = lambda x: jnp.asarray(x, F32).astype(BF16)
    tc, ts = jnp.asarray(np.cos(th), F32), jnp.asarray(np.sin(th), F32)
    ff, fi = jnp.asarray(f_fwd, F32), jnp.asarray(f_inv, F32)
    fl, fr = ff[None, :, :n1], ff[None, :, n1:]
    g_fwd = jnp.concatenate([fl * tc[:, None, :] - fr * ts[:, None, :],
                             fl * ts[:, None, :] + fr * tc[:, None, :]], axis=2).astype(BF16)
    it, ib = fi[None, :n1, :], fi[None, n1:, :]
    g_inv = jnp.concatenate([tc[:, :, None] * it - ts[:, :, None] * ib,
                             ts[:, :, None] * it + tc[:, :, None] * ib], axis=1).astype(BF16)
    return dict(f_s1=as_bf(f_s1[:, :n2 // 2]), f_s1_hi=as_bf(f_s1[:, n2 // 2:]), f_s3=as_bf(f_s3),
                g_fwd=g_fwd, g_inv=g_inv, n1=n1)


def _slab_tile(n1):
    return min(n1, 8)


HI16 = -65536
HALF_ULP16 = 0x8000


def _pack_c(re, im):
    rb = lax.bitcast_convert_type(re, jnp.int32) + HALF_ULP16
    ib = lax.bitcast_convert_type(im, jnp.int32) + HALF_ULP16
    return (rb & HI16) | lax.shift_right_logical(ib, 16)


def _pack_bf16(re, im):
    rb = lax.bitcast_convert_type(re.astype(F32), jnp.int32)
    ib = lax.bitcast_convert_type(im.astype(F32), jnp.int32)
    return rb | lax.shift_right_logical(ib, 16)


def _unpack_c(p):
    re = lax.bitcast_convert_type(p & HI16, F32)
    im = lax.bitcast_convert_type(lax.shift_left(p, 16), F32)
    return re, im


def _shift_rows(x, down):
    rows = x.shape[0]
    row = lax.broadcasted_iota(jnp.int32, x.shape, 0)
    if down:
        return jnp.where(row == 0, 0.0, pltpu.roll(x, 1, 0))
    return jnp.where(row == rows - 1, 0.0, pltpu.roll(x, rows - 1, 0))


def _short_conv_slabs(main_ref, prev_ref, next_ref, w_ref, b_ref, first, last):
    n_slabs = main_ref.shape[0]
    prev = prev_ref[0].astype(F32)
    prev = jnp.where(first, _shift_rows(prev, True), prev)
    nxt = next_ref[0].astype(F32)
    nxt = jnp.where(last, _shift_rows(nxt, False), nxt)
    w = w_ref[...]
    out = []
    for s in range(n_slabs):
        up = prev if s == 0 else main_ref[s - 1].astype(F32)
        dn = nxt if s == n_slabs - 1 else main_ref[s + 1].astype(F32)
        out.append(up * w[0:1, :] + main_ref[s].astype(F32) * w[1:2, :] + dn * w[2:3, :] + b_ref[...])
    return out


def _fft_s1_kernel(*refs, short_conv):
    re_ref, im_ref, ret_ref, imt_ref = refs[-4:]
    if short_conv:
        pm_ref, f_ref, m_ref, p_ref, n_ref, w_ref, b_ref, o_ref, u_ref = refs[:-4]
        t = pl.program_id(1)
        slabs = _short_conv_slabs(m_ref, p_ref, n_ref, w_ref, b_ref, t == 0, t == pl.num_programs(1) - 1)
    else:
        pm_ref, f_ref, m_ref, o_ref = refs[:-4]
        slabs = [m_ref[s] for s in range(m_ref.shape[0])]
    half = o_ref.shape[0]
    for s, u in enumerate(slabs):
        if short_conv:
            u_ref[s] = u.astype(u_ref.dtype)
        r = _dot(f_ref[...], u.astype(BF16))
        re_ref[s * half:(s + 1) * half, :] = r[:half].astype(BF16)
        im_ref[s * half:(s + 1) * half, :] = r[half:].astype(BF16)
    _transpose_rows(ret_ref, re_ref, pm_ref, len(slabs), half)
    _transpose_rows(imt_ref, im_ref, pm_ref, len(slabs), half)
    o_ref[...] = _pack_bf16(ret_ref[...], imt_ref[...]).reshape(o_ref.shape)


def _fft_s1(consts, src, col_block, conv=None, ct=512):
    bsz, n1, h, c = src.shape
    d = consts["d"]
    st = _slab_tile(n1)
    cpb = d // ct
    f = consts["f_s1"]
    main = pl.BlockSpec((None, st, h, ct), lambda b, t, j: (b, t, 0, col_block * cpb + j))
    a_spec = pl.BlockSpec((None, h, st, ct), lambda b, t, j: (b, 0, t, j))
    a_shape = jax.ShapeDtypeStruct((bsz, h, n1, d), jnp.int32)
    fspec = pl.BlockSpec(f.shape, lambda b, t, j: (0, 0))
    pspec = pl.BlockSpec((PERM_ROWS, PERM_ROWS), lambda b, t, j: (0, 0))
    pm = _perm_for(st, h)
    scratch = [pltpu.VMEM((st * h, ct), BF16)] * 4
    if conv is None:
        return pl.pallas_call(
            functools.partial(_fft_s1_kernel, short_conv=False),
            grid=(bsz, n1 // st, cpb),
            in_specs=[pspec, fspec, main], out_specs=a_spec, out_shape=a_shape, scratch_shapes=scratch,
            compiler_params=_params(("parallel", "parallel", "parallel")), name="fft_s1",
        )(pm, f, src)
    w, b = conv
    prev = pl.BlockSpec((None, 1, h, ct), lambda b, t, j: (b, (t * st + n1 - 1) % n1, 0, col_block * cpb + j))
    nxt = pl.BlockSpec((None, 1, h, ct), lambda b, t, j: (b, ((t + 1) * st) % n1, 0, col_block * cpb + j))
    wspec = pl.BlockSpec((3, ct), lambda b, t, j: (0, col_block * cpb + j))
    bspec = pl.BlockSpec((1, ct), lambda b, t, j: (0, col_block * cpb + j))
    u_spec = pl.BlockSpec((None, st, h, ct), lambda b, t, j: (b, t, 0, j))
    return pl.pallas_call(
        functools.partial(_fft_s1_kernel, short_conv=True),
        grid=(bsz, n1 // st, cpb),
        in_specs=[pspec, fspec, main, prev, nxt, wspec, bspec],
        out_specs=[a_spec, u_spec],
        out_shape=[a_shape, jax.ShapeDtypeStruct((bsz, n1, h, d), BF16)],
        scratch_shapes=scratch,
        compiler_params=_params(("parallel", "parallel", "parallel")), name="fft_s1_conv",
    )(pm, f, src, src, src, w, b)


def _filter_s1_kernel(pm_ref, flo_ref, fhi_ref, alo_ref, ahi_ref, zlo_ref, zhi_ref, wf_ref, wb_ref, df_ref,
                      db_ref, o_ref, ss_ref, re_ref, im_ref, ret_ref, imt_ref):
    half = o_ref.shape[0]

    @pl.when(pl.program_id(2) == 0)
    def _():
        ss_ref[...] = jnp.zeros_like(ss_ref)

    n_slabs, rows = alo_ref.shape[0], alo_ref.shape[1]
    ct = o_ref.shape[2]
    ss = jnp.zeros(ss_ref.shape, F32)
    r = jnp.zeros((2 * half, n_slabs * ct), F32)
    for f_ref, a_ref, z_ref, w_ref, d_ref in ((flo_ref, alo_ref, zlo_ref, wf_ref, df_ref),
                                              (fhi_ref, ahi_ref, zhi_ref, wb_ref, db_ref)):
        z = z_ref[...].reshape(n_slabs * rows, LANES)
        t, sign = z[:, 0:1], z[:, LANES - 1:LANES]
        taps = _dot3(a_ref[...].reshape(n_slabs * rows, a_ref.shape[2]), w_ref[...])
        taps = taps * jnp.exp(-t * jnp.abs(d_ref[...])) * sign
        ss = ss + jnp.sum(taps * taps, axis=0, keepdims=True)
        tb = taps.astype(BF16)
        wide = jnp.concatenate([tb[s * rows:(s + 1) * rows, :] for s in range(n_slabs)], axis=1)
        r = r + _dot(f_ref[...], wide)
    for s in range(n_slabs):
        re_ref[s * half:(s + 1) * half, :] = r[:half, s * ct:(s + 1) * ct].astype(BF16)
        im_ref[s * half:(s + 1) * half, :] = r[half:, s * ct:(s + 1) * ct].astype(BF16)
    ss_ref[...] += ss
    _transpose_rows(ret_ref, re_ref, pm_ref, n_slabs, half)
    _transpose_rows(imt_ref, im_ref, pm_ref, n_slabs, half)
    o_ref[...] = _pack_bf16(ret_ref[...], imt_ref[...]).reshape(o_ref.shape)


def _filter_s1(consts, a2, zf, w3, decay, ct=256):
    n1, h = consts["n1"], SLAB_ROWS
    hid = a2.shape[1]
    d = decay.shape[-1]
    nct = d // ct
    st = _slab_tile(n1)
    flo, fhi = consts["f_s1"], consts["f_s1_hi"]
    dec = decay.reshape(1, HYENA_ORDER * N_DIRS * d)
    fspec = pl.BlockSpec(flo.shape, lambda o, j, t: (0, 0))
    rows = lambda half, width: pl.BlockSpec((None, st, h, width), lambda o, j, t: (half, t, 0, 0))
    wcol = lambda dirn, nrow: pl.BlockSpec((nrow, ct), lambda o, j, t: (0, (o * N_DIRS + dirn) * nct + j))
    return pl.pallas_call(
        _filter_s1_kernel,
        grid=(HYENA_ORDER, nct, n1 // st),
        in_specs=[pl.BlockSpec((PERM_ROWS, PERM_ROWS), lambda o, j, t: (0, 0)), fspec, fspec,
                  rows(0, hid), rows(1, hid), rows(0, LANES), rows(1, LANES),
                  wcol(0, hid), wcol(1, hid), wcol(0, 1), wcol(1, 1)],
        out_specs=[pl.BlockSpec((None, h, st, ct), lambda o, j, t: (o, 0, t, j)),
                   pl.BlockSpec((None, 1, ct), lambda o, j, t: (o, 0, j))],
        out_shape=[jax.ShapeDtypeStruct((HYENA_ORDER, h, n1, d), jnp.int32),
                   jax.ShapeDtypeStruct((HYENA_ORDER, 1, d), F32)],
        scratch_shapes=[pltpu.VMEM((st * h, ct), BF16)] * 4,
        compiler_params=_params(("parallel", "parallel", "arbitrary")), name="filter_s1",
    )(_perm_for(st, h), flo, fhi, a2.reshape(2, n1, h, hid), a2.reshape(2, n1, h, hid),
      zf.reshape(2, n1, h, LANES), zf.reshape(2, n1, h, LANES), w3, w3, dec, dec)


def _fft_s2f_kernel(a_ref, gf_ref, ss_ref, o_ref, *, kb):
    n1 = a_ref.shape[1]
    scale = lax.rsqrt(ss_ref[...] + EPS)

    def body(kk, carry):
        ar, ai = _unpack_c(a_ref[kk])
        x = _dot(gf_ref[kk], jnp.concatenate([ar, ai], axis=0).astype(BF16))
        o_ref[kk] = _pack_c(x[:n1] * scale, x[n1:] * scale)
        return carry

    lax.fori_loop(0, kb, body, 0, unroll=min(kb, 4))


def _fft_s2_kernel(a_ref, k_ref, gf_ref, gi_ref, o_ref, *, kb):
    n1 = a_ref.shape[1]

    def body(kk, carry):
        ar, ai = _unpack_c(a_ref[kk])
        x = _dot(gf_ref[kk], jnp.concatenate([ar, ai], axis=0).astype(BF16))
        xr, xi = x[:n1], x[n1:]
        kr, ki = _unpack_c(k_ref[kk])
        zr = xr * kr - xi * ki
        zi = xr * ki + xi * kr
        y = _dot(gi_ref[kk], jnp.concatenate([zr, zi], axis=0).astype(BF16))
        o_ref[kk] = _pack_c(y[:n1], y[n1:])
        return carry

    lax.fori_loop(0, kb, body, 0, unroll=min(kb, 4))


def _s2_tiles(n1, d):
    kb = max(1, 512 // n1)
    ct = min(d, 512)
    return kb, ct


def _fft_s2f(a, sumsq, consts):
    n_o, k2n, n1, d = a.shape
    kb, ct = _s2_tiles(n1, d)
    blk = pl.BlockSpec((None, kb, n1, ct), lambda k, j, o: (o, k, 0, j))
    return pl.pallas_call(
        functools.partial(_fft_s2f_kernel, kb=kb),
        grid=(k2n // kb, d // ct, n_o),
        in_specs=[blk, pl.BlockSpec((kb, 2 * n1, 2 * n1), lambda k, j, o: (k, 0, 0)),
                  pl.BlockSpec((None, 1, ct), lambda k, j, o: (o, 0, j))],
        out_specs=blk,
        out_shape=jax.ShapeDtypeStruct(a.shape, jnp.int32),
        compiler_params=_params(("parallel", "parallel", "parallel")),
        name="fft_s2_filter",
    )(a, consts["g_fwd"], sumsq)


def _fft_s2(a, kf, order, consts):
    bsz, k2n, n1, d = a.shape
    kb, ct = _s2_tiles(n1, d)
    blk = pl.BlockSpec((None, kb, n1, ct), lambda k, j, b: (b, k, 0, j))
    mat = pl.BlockSpec((kb, 2 * n1, 2 * n1), lambda k, j, b: (k, 0, 0))
    return pl.pallas_call(
        functools.partial(_fft_s2_kernel, kb=kb),
        grid=(k2n // kb, d // ct, bsz),
        in_specs=[blk, pl.BlockSpec((None, kb, n1, ct), lambda k, j, b: (order, k, 0, j)), mat, mat],
        out_specs=blk,
        out_shape=jax.ShapeDtypeStruct(a.shape, jnp.int32),
        compiler_params=_params(("parallel", "parallel", "parallel")),
        name="fft_s2",
    )(a, kf, consts["g_fwd"], consts["g_inv"])


def _fft_s3_kernel(pm_ref, f_ref, t_ref, u_ref, gm_ref, gp_ref, gn_ref, w_ref, b_ref, sk_ref, o_ref,
                   re_ref, im_ref, ret_ref, imt_ref):
    t_id = pl.program_id(1)
    half, st = t_ref.shape[0], t_ref.shape[1]
    re, im = _unpack_c(t_ref[...].reshape(half * st, t_ref.shape[2]))
    re_ref[...] = re.astype(BF16)
    im_ref[...] = im.astype(BF16)
    _transpose_rows(ret_ref, re_ref, pm_ref, half, st)
    _transpose_rows(imt_ref, im_ref, pm_ref, half, st)
    gates = _short_conv_slabs(gm_ref, gp_ref, gn_ref, w_ref, b_ref, t_id == 0, t_id == pl.num_programs(1) - 1)
    for s, gate in enumerate(gates):
        t = jnp.concatenate([ret_ref[s * half:(s + 1) * half, :], imt_ref[s * half:(s + 1) * half, :]], axis=0)
        y = _dot(f_ref[...], t)
        o_ref[s] = (gate * (y + u_ref[s].astype(F32) * sk_ref[...])).astype(o_ref.dtype)


def _fft_s3(consts, t, u, z, gate_block, conv_w, conv_b, skip, order, out_dtype, ct=512):
    bsz, h, n1, d = t.shape
    st = _slab_tile(n1)
    cpb = d // ct
    f = consts["f_s3"]
    gcol = lambda j: gate_block * cpb + j
    slab = lambda idx: pl.BlockSpec((None, 1, h, ct), lambda b, tt, j: (b, idx(tt), 0, gcol(j)))
    return pl.pallas_call(
        _fft_s3_kernel,
        grid=(bsz, n1 // st, cpb),
        scratch_shapes=[pltpu.VMEM((st * h, ct), BF16)] * 4,
        in_specs=[
            pl.BlockSpec((PERM_ROWS, PERM_ROWS), lambda b, tt, j: (0, 0)),
            pl.BlockSpec(f.shape, lambda b, tt, j: (0, 0)),
            pl.BlockSpec((None, h, st, ct), lambda b, tt, j: (b, 0, tt, j)),
            pl.BlockSpec((None, st, h, ct), lambda b, tt, j: (b, tt, 0, j)),
            pl.BlockSpec((None, st, h, ct), lambda b, tt, j: (b, tt, 0, gcol(j))),
            slab(lambda tt: (tt * st + n1 - 1) % n1),
            slab(lambda tt: ((tt + 1) * st) % n1),
            pl.BlockSpec((3, ct), lambda b, tt, j: (0, gcol(j))),
            pl.BlockSpec((1, ct), lambda b, tt, j: (0, gcol(j))),
            pl.BlockSpec((None, 1, ct), lambda b, tt, j: (order, 0, j)),
        ],
        out_specs=pl.BlockSpec((None, st, h, ct), lambda b, tt, j: (b, tt, 0, j)),
        out_shape=jax.ShapeDtypeStruct((bsz, n1, h, d), out_dtype),
        compiler_params=_params(("parallel", "parallel", "parallel")),
        name="fft_s3",
    )(_perm_for(h, st), f, t, u, z, z, z, conv_w, conv_b, skip)


def _hyena_filter_spectra(seq_len, consts, fw1, fb1, ffreq, fw2, fb2, fw3, decay):
    d = decay.shape[-1]
    n1 = consts["n1"]
    zf = _filter_positions(seq_len, n1)
    a2 = _filter_mlp(zf, fw1, fb1, ffreq, fw2, fb2)
    a, sumsq = _filter_s1(consts, a2, zf, fw3, decay)
    return _fft_s2f(a, sumsq, consts)


def _hyena_mixer(x, g, shift, scale, gate, p, layer, kf, consts):
    bsz, seq_len, d = x.shape
    n1 = consts["n1"]
    h = SLAB_ROWS
    st = _slab_tile(n1)
    xv = x.reshape(bsz, h, n1, d)
    z = _norm_mm(
        xv, pl.BlockSpec((None, h, st, d), lambda b, i, j: (b, 0, i, 0)), (h, st), st * h, n1 // st, g, shift, scale,
        [p["hy_w_in"]], layer, 0, 3 * d, jax.ShapeDtypeStruct((bsz, n1, h, 3 * d), BF16),
        pl.BlockSpec((None, st, h, COL_TILE), lambda b, i, j: (b, i, 0, j)),
        mode="bias", bias=p["hy_b_in"][layer].reshape(1, 3 * d), tn=COL_TILE)
    cw, cb = p["hy_conv_w"][layer], p["hy_conv_b"][layer].reshape(1, 3 * d)
    skip = p["hy_skip"][layer].reshape(HYENA_ORDER, 1, d)
    a, u = _fft_s1(consts, z, 2, conv=(cw, cb))
    t = _fft_s2(a, kf, 0, consts)
    y1 = _fft_s3(consts, t, u, z, 0, cw, cb, skip, 0, BF16)
    a = _fft_s1(consts, y1, 0)
    t = _fft_s2(a, kf, 1, consts)
    y2 = _fft_s3(consts, t, y1, z, 1, cw, cb, skip, 1, F32)
    q = ROW_TILE // n1
    return _mm_res(
        [y2], [pl.BlockSpec((None, n1, q, d), lambda b, i, j: (b, 0, i, 0))], p["hy_w_out"], layer,
        p["hy_b_out"][layer].reshape(1, d), x, gate, ROW_TILE, a_mode="perm", perm=(n1, q), tn=COL_TILE)


def _attn_kernel(*refs, phases, ta, n_sub, to_classes):
    if to_classes:
        (pm_ref, q_ref, kp_ref, km_ref, kn_ref, vp_ref, vm_ref, vn_ref, o_ref, l_ref, kx_ref, vx_ref,
         on_ref, ot_ref, ln_ref, ls_ref, lt_ref) = refs
    else:
        q_ref, kp_ref, km_ref, kn_ref, vp_ref, vm_ref, vn_ref, o_ref, l_ref, kx_ref, vx_ref = refs
    i = pl.program_id(2)
    halo = ATT_BAND // phases
    qa = ATT_Q // phases
    ka = 2 * qa
    kx_ref[:, 0:halo] = kp_ref[...]
    kx_ref[:, halo:halo + ta] = km_ref[...]
    kx_ref[:, halo + ta:] = kn_ref[...]
    vx_ref[:, 0:halo] = vp_ref[...]
    vx_ref[:, halo:halo + ta] = vm_ref[...]
    vx_ref[:, halo + ta:] = vn_ref[...]
    row = lax.broadcasted_iota(jnp.int32, (ATT_Q, 2 * ATT_Q), 0)
    col = lax.broadcasted_iota(jnp.int32, (ATT_Q, 2 * ATT_Q), 1)
    cq, aq = row >> (qa.bit_length() - 1), row & (qa - 1)
    ck, ak = col >> (ka.bit_length() - 1), col & (ka - 1)
    delta = phases * (ak - aq) - ATT_BAND + ck - cq
    band = (delta >= -ATT_BAND) & (delta <= ATT_BAND)
    lane_head = lax.broadcasted_iota(jnp.int32, (ATT_Q, LANES), 1) >> (LSE_LANES.bit_length() - 1)
    for s in range(ta // qa):
        key_idx = phases * (i * ta + s * qa - halo + ak) + ck
        valid = band & (key_idx >= 0) & (key_idx < n_sub)
        lse_tile = jnp.zeros((ATT_Q, LANES), F32)
        for h in range(HEADS_PER_GROUP):
            cs = slice(h * HEAD_DIM, (h + 1) * HEAD_DIM)
            q = jnp.concatenate([q_ref[c, s * qa:(s + 1) * qa, cs] for c in range(phases)], axis=0)
            k = jnp.concatenate([kx_ref[c, s * qa:s * qa + ka, cs] for c in range(phases)], axis=0)
            v = jnp.concatenate([vx_ref[c, s * qa:s * qa + ka, cs] for c in range(phases)], axis=0)
            sc = lax.dot_general(q, k, (((1,), (1,)), ((), ())), preferred_element_type=F32)
            sc = jnp.where(valid, sc, NEG_BIG)
            m = jnp.max(sc, axis=-1, keepdims=True)
            pr = jnp.exp(sc - m)
            den = jnp.sum(pr, axis=-1, keepdims=True)
            o = (_dot(pr.astype(BF16), v) / den).astype(BF16)
            lse_tile = jnp.where(lane_head == h, m + jnp.log(den), lse_tile)
            if to_classes:
                on_ref[s * ATT_Q:(s + 1) * ATT_Q, cs] = o
            else:
                for c in range(phases):
                    o_ref[c, s * qa:(s + 1) * qa, cs] = o[c * qa:(c + 1) * qa]
        if to_classes:
            ln_ref[s * ATT_Q:(s + 1) * ATT_Q, :] = lse_tile
        else:
            for c in range(phases):
                l_ref[c, s * qa:(s + 1) * qa, :] = lse_tile[c * qa:(c + 1) * qa]
    if to_classes:
        na = ta // CLASSES
        _transpose_rows(ot_ref, on_ref, pm_ref, na, CLASSES)
        o_ref[...] = ot_ref[...].reshape(o_ref.shape)
        rest = ln_ref[...]
        total = jnp.zeros(rest.shape, F32)
        for _ in range(3):
            piece = rest.astype(BF16)
            rest = rest - piece.astype(F32)
            ls_ref[...] = piece
            _transpose_rows(lt_ref, ls_ref, pm_ref, na, CLASSES)
            total = total + lt_ref[...].astype(F32)
        l_ref[...] = total.reshape(l_ref.shape)


def _attn_group(qkv, col0, phases, n_sub, lead_grid, lead_block, lead_index, ta, *, to_classes=False,
                out_arr_shape=None, out_block=None, out_index=None):
    gw = GROUP_WIDTH
    halo = ATT_BAND // phases
    rows = qkv.shape[-2]
    per = ta // halo
    nblk = rows // halo
    cb = col0 // gw

    def spec(nrows, ridx, part):
        return pl.BlockSpec(tuple(lead_block) + (nrows, gw),
                            lambda b, rho, i: tuple(lead_index(b, rho)) + (ridx(i), cb + part))

    main = lambda part: spec(ta, lambda i: i, part)
    prev = lambda part: spec(halo, lambda i: jnp.maximum(i * per - 1, 0), part)
    nxt = lambda part: spec(halo, lambda i: jnp.minimum((i + 1) * per, nblk - 1), part)
    bsz = qkv.shape[0]
    in_specs = [main(0), prev(1), main(1), nxt(1), prev(2), main(2), nxt(2)]
    args = [qkv] * 7
    if to_classes:
        assert ta == PERM_ROWS
        out_specs = [pl.BlockSpec(out_block + (gw,), out_index), pl.BlockSpec(out_block + (LANES,), out_index)]
        scratch_extra = [pltpu.VMEM((ta, gw), BF16)] * 2 + [pltpu.VMEM((ta, LANES), F32)] + \
                        [pltpu.VMEM((ta, LANES), BF16)] * 2
        in_specs.insert(0, pl.BlockSpec((PERM_ROWS, PERM_ROWS), lambda b, rho, i: (0, 0)))
        args.insert(0, _perm_for(ta // CLASSES, CLASSES))
    else:
        out_arr_shape = qkv.shape[:-1]
        oidx = lambda b, rho, i: tuple(lead_index(b, rho)) + (i, 0)
        out_specs = [pl.BlockSpec(tuple(lead_block) + (ta, gw), oidx),
                     pl.BlockSpec(tuple(lead_block) + (ta, LANES), oidx)]
        scratch_extra = []
    kx_shape = (phases, ta + 2 * halo, gw)
    return pl.pallas_call(
        functools.partial(_attn_kernel, phases=phases, ta=ta, n_sub=n_sub, to_classes=to_classes),
        grid=(bsz, lead_grid, rows // ta),
        in_specs=in_specs,
        out_specs=out_specs,
        out_shape=[jax.ShapeDtypeStruct(tuple(out_arr_shape) + (gw,), BF16),
                   jax.ShapeDtypeStruct(tuple(out_arr_shape) + (LANES,), F32)],
        scratch_shapes=[pltpu.VMEM(kx_shape, BF16)] * 2 + scratch_extra,
        compiler_params=_params(("parallel", "parallel", "parallel")),
        name="attn_p%d" % phases + ("_cls" if to_classes else ""),
    )(*args)


def _rope_tables(pos):
    half = ROT_DIM // 2
    inv = ROPE_THETA ** (-jnp.arange(0, ROT_DIM, 2, dtype=F32) / ROT_DIM)
    ang = pos.astype(F32)[:, None] * inv[None, :]
    cos, sin = jnp.cos(ang), jnp.sin(ang)
    n = pos.shape[0]
    rest = HEAD_DIM - ROT_DIM
    c = jnp.concatenate([cos, cos, jnp.ones((n, rest), F32)], axis=1)
    s = jnp.concatenate([sin, sin, jnp.zeros((n, rest), F32)], axis=1)
    return c, s


def _rope_partner_matrix(width):
    half = ROT_DIM // 2
    m = np.zeros((width, width), np.float32)
    for base in range(0, width, HEAD_DIM):
        for k in range(half):
            m[base + k + half, base + k] = -1.0
            m[base + k, base + k + half] = 1.0
    return jnp.asarray(m, BF16)


def _attn_mixer(x, g, shift, scale, gate, p, layer):
    bsz, seq_len, d = x.shape
    gw = GROUP_WIDTH
    nc = seq_len // CLASSES
    ca = ROW_TILE // CLASSES
    w_in = p["at_w_in"]
    tabs = _rope_tables(jnp.arange(seq_len))
    qkv0 = _norm_mm(
        x, pl.BlockSpec((None, ROW_TILE, d), lambda b, i, j: (b, i, 0)), None, ROW_TILE, seq_len // ROW_TILE,
        g, shift, scale, [w_in], layer, 0, 3 * gw, jax.ShapeDtypeStruct((bsz, seq_len, 3 * gw), BF16),
        pl.BlockSpec((None, ROW_TILE, COL_TILE), lambda b, i, j: (b, i, j)), mode="rope",
        rope=(tabs, pl.BlockSpec((ROW_TILE, HEAD_DIM), lambda b, i, j: (i, 0))), tn=COL_TILE)
    pos_c = (jnp.arange(nc)[None, :] * CLASSES + jnp.arange(CLASSES)[:, None]).reshape(-1)
    tabs_c = [t.reshape(CLASSES, nc, HEAD_DIM) for t in _rope_tables(pos_c)]
    qkv12 = _norm_mm(
        x.reshape(bsz, nc, CLASSES, d), pl.BlockSpec((None, ca, CLASSES, d), lambda b, i, j: (b, i, 0, 0)),
        (ca, CLASSES), ROW_TILE, nc // ca, g, shift, scale, [w_in], layer, 3 * gw, 6 * gw,
        jax.ShapeDtypeStruct((bsz, CLASSES, nc, 6 * gw), BF16),
        pl.BlockSpec((None, CLASSES, ca, COL_TILE), lambda b, i, j: (b, 0, i, j)), mode="rope",
        rope=(tabs_c, pl.BlockSpec((CLASSES, ca, HEAD_DIM), lambda b, i, j: (0, i, 0))), tn=COL_TILE)
    cls_shape = (bsz, CLASSES, nc)
    ta0 = PERM_ROWS
    o0, l0 = _attn_group(
        qkv0.reshape(bsz, 1, seq_len, 3 * gw), 0, 1, seq_len, 1, (None, 1), lambda b, rho: (b, 0), ta0,
        to_classes=True, out_arr_shape=cls_shape, out_block=(None, CLASSES, ta0 // CLASSES),
        out_index=lambda b, rho, i: (b, 0, i, 0))
    dil1 = ATTN_PATTERNS[1][1]
    ph = CLASSES // dil1
    ta1 = min(64, nc)
    o1, l1 = _attn_group(
        qkv12.reshape(bsz, ph, dil1, nc, 6 * gw), 0, ph, seq_len // dil1, dil1, (None, ph, None),
        lambda b, rho: (b, 0, rho), ta1)
    ta2 = min(256, nc)
    o2, l2 = _attn_group(
        qkv12.reshape(bsz, CLASSES, 1, nc, 6 * gw), 3 * gw, 1, nc, CLASSES, (None, None, 1),
        lambda b, rho: (b, rho, 0), ta2)
    os_ = [o.reshape(cls_shape + (gw,)) for o in (o0, o1, o2)]
    ls = [l.reshape(cls_shape + (LANES,)) for l in (l0, l1, l2)]
    blk = lambda width: pl.BlockSpec((None, CLASSES, ca, width), lambda b, i, j: (b, 0, i, 0))
    return _mm_res(os_ + ls, [blk(gw)] * 3 + [blk(LANES)] * 3, p["at_w_out"], layer, jnp.zeros((1, d), F32),
                   x, gate, ROW_TILE, a_mode="merge", perm=(CLASSES, ca), tn=COL_TILE)


def _ffn(x, g, shift, scale, gate, p, layer):
    bsz, seq_len, d = x.shape
    dff = p["ffn_w_gate"].shape[-1]
    tiles = seq_len // ROW_TILE
    xs = pl.BlockSpec((None, ROW_TILE, d), lambda b, i, j: (b, i, 0))
    hs = pl.BlockSpec((None, ROW_TILE, 512), lambda b, i, j: (b, i, j))
    hmid = _norm_mm(x, xs, None, ROW_TILE, tiles, g, shift, scale, [p["ffn_w_gate"], p["ffn_w_up"]], layer, 0,
                    dff, jax.ShapeDtypeStruct((bsz, seq_len, dff), BF16), hs, mode="swiglu")
    return _mm_res([hmid], [pl.BlockSpec((None, ROW_TILE, dff), lambda b, i, j: (b, i, 0))],
                   p["ffn_w_down"], layer, jnp.zeros((1, d), F32), x, gate, ROW_TILE)


def _encoder(x, mods, final_mod, p):
    bsz, seq_len, d = x.shape
    n1 = 2 * seq_len // DFT_N2
    consts = _dft_consts(n1, DFT_N2)
    consts["d"] = d
    for i in range(DEPTH):
        sh_m, sc_m, g_m, sh_f, sc_f, g_f = [mods[i][:, None, k * d:(k + 1) * d] for k in range(6)]
        j = i // 2
        if i % 2 == 0:
            kf = _hyena_filter_spectra(seq_len, consts, p["hy_fw1"][j], p["hy_fb1"][j], p["hy_ffreq"][j],
                                       p["hy_fw2"][j], p["hy_fb2"][j], p["hy_fw3"][j], p["hy_decay"][j])
            x = _hyena_mixer(x, p["norm_mix"][i], sh_m, sc_m, g_m, p, j, kf, consts)
        else:
            x = _attn_mixer(x, p["norm_mix"][i], sh_m, sc_m, g_m, p, j)
        x = _ffn(x, p["norm_ffn"][i], sh_f, sc_f, g_f, p, i)
    sh, sc = final_mod[:, None, :d], final_mod[:, None, d:]
    return _final(x, p["final_norm"], sh, sc)


def kernel(x_prompt, x_sample, c_prompt, c_sample, ada_w, ada_b, norm_mix, norm_ffn, hy_w_in, hy_b_in, hy_conv_w, hy_conv_b, hy_fw1, hy_fb1, hy_ffreq, hy_fw2, hy_fb2, hy_fw3, hy_decay, hy_skip, hy_w_out, hy_b_out, at_w_in, at_w_out, ffn_w_gate, ffn_w_up, ffn_w_down, final_norm, final_ada_w, final_ada_b):
    d = x_prompt.shape[-1]
    bp, bs = c_prompt.shape[0], c_sample.shape[0]
    pad = -(bp + bs) % (2 * SUBLANES)
    c_all = jnp.concatenate([c_prompt, c_sample, jnp.zeros((pad, d), F32)], axis=0)
    mods = _ada(c_all, ada_w, ada_b)
    fmod = _ada(c_all, final_ada_w[None], final_ada_b[None])[0]
    p = dict(norm_mix=norm_mix, norm_ffn=norm_ffn,
             hy_w_in=hy_w_in.astype(BF16), hy_b_in=hy_b_in, hy_conv_w=hy_conv_w, hy_conv_b=hy_conv_b,
             hy_fw1=hy_fw1, hy_fb1=hy_fb1, hy_ffreq=hy_ffreq, hy_fw2=hy_fw2, hy_fb2=hy_fb2, hy_fw3=hy_fw3,
             hy_decay=hy_decay, hy_skip=hy_skip, hy_w_out=hy_w_out.astype(BF16), hy_b_out=hy_b_out,
             at_w_in=at_w_in.astype(BF16), at_w_out=at_w_out.astype(BF16),
             ffn_w_gate=ffn_w_gate.astype(BF16), ffn_w_up=ffn_w_up.astype(BF16),
             ffn_w_down=ffn_w_down.astype(BF16), final_norm=final_norm)
    y_prompt = _encoder(x_prompt, mods[:, :bp], fmod[:bp], p)
    y_sample = _encoder(x_sample, mods[:, bp:bp + bs], fmod[bp:bp + bs], p)
    return (y_prompt, y_sample)
```

```python
import functools
import math

import numpy as np
import jax
import jax.numpy as jnp
from jax import lax
from jax.experimental import pallas as pl
from jax.experimental.pallas import tpu as pltpu

F32 = jnp.float32
BF16 = jnp.bfloat16
EPS = 1e-6

DEPTH = 4
HYENA_ORDER = 2
N_DIRS = 2
FILTER_BANDS = 16
FILTER_EMB = 1 + 2 * FILTER_BANDS
ATTN_PATTERNS = ((128, 1), (512, 4), (2048, 16))
HEADS_PER_GROUP = 8
HEAD_DIM = 128
GROUP_WIDTH = HEADS_PER_GROUP * HEAD_DIM
ROT_DIM = HEAD_DIM // 4
ROPE_THETA = 500000.0

LANES = 128
SUBLANES = 8
VMEM_LIMIT_BYTES = 56 * 1024 * 1024

DFT_N2 = 256
SLAB_ROWS = DFT_N2 // 2
ATT_BAND = 64
ATT_Q = 2 * ATT_BAND
CLASSES = 16
LSE_LANES = LANES // HEADS_PER_GROUP
ROW_TILE = 1024
COL_TILE = 1024
NEG_BIG = -1e30


def _params(sem):
    return pltpu.CompilerParams(dimension_semantics=sem, vmem_limit_bytes=VMEM_LIMIT_BYTES)


def _dot(a, b):
    return jnp.dot(a, b, preferred_element_type=F32)


def _split(a):
    hi = a.astype(BF16)
    lo = (a - hi.astype(F32)).astype(BF16)
    return hi, lo


def _dot3(a, b):
    ah, al = _split(a)
    bh, bl = _split(b)
    return _dot(ah, bh) + _dot(al, bh) + _dot(ah, bl)


def _modnorm(x, g, shift, scale):
    ms = jnp.mean(x * x, axis=-1, keepdims=True)
    return (x * lax.rsqrt(ms + EPS)) * (g * (1.0 + scale)) + shift


PERM_ROWS = 256


def _perm_matrix(p, q):
    m = np.zeros((PERM_ROWS, PERM_ROWS), np.float32)
    pi, qi = np.meshgrid(np.arange(p), np.arange(q), indexing="ij")
    m[(qi * p + pi).ravel(), (pi * q + qi).ravel()] = 1.0
    return jnp.asarray(m, BF16)


def _perm_for(p, q):
    assert (q <= 16 and p % (PERM_ROWS // q) == 0) or (p <= 16 and q % (PERM_ROWS // p) == 0), (p, q)
    return _perm_matrix(PERM_ROWS // q, q) if q <= 16 else _perm_matrix(p, PERM_ROWS // p)


def _transpose_rows(dst_ref, src_ref, pm_ref, p, q):
    if q <= 16:
        pg = PERM_ROWS // q
        for grp in range(p // pg):
            t = _dot(pm_ref[...], src_ref[grp * PERM_ROWS:(grp + 1) * PERM_ROWS, :]).astype(BF16)
            for qi in range(q):
                dst_ref[qi * p + grp * pg:qi * p + (grp + 1) * pg, :] = t[qi * pg:(qi + 1) * pg]
    else:
        qg = PERM_ROWS // p
        for grp in range(q // qg):
            blk = jnp.concatenate([src_ref[pi * q + grp * qg:pi * q + (grp + 1) * qg, :] for pi in range(p)],
                                  axis=0)
            dst_ref[grp * PERM_ROWS:(grp + 1) * PERM_ROWS, :] = _dot(pm_ref[...], blk).astype(BF16)


def _ada_kernel(c_ref, w_ref, b_ref, o_ref):
    c = c_ref[...]
    cs = c * jax.nn.sigmoid(c)
    o_ref[...] = _dot3(cs, w_ref[...]) + b_ref[...]


def _ada(c_all, w, b, tn=1024):
    nl, d, no = w.shape
    r = c_all.shape[0]
    return pl.pallas_call(
        _ada_kernel,
        grid=(nl, no // tn),
        in_specs=[
            pl.BlockSpec((r, d), lambda l, j: (0, 0)),
            pl.BlockSpec((None, d, tn), lambda l, j: (l, 0, j)),
            pl.BlockSpec((None, 1, tn), lambda l, j: (l, 0, j)),
        ],
        out_specs=pl.BlockSpec((None, r, tn), lambda l, j: (l, 0, j)),
        out_shape=jax.ShapeDtypeStruct((nl, r, no), F32),
        compiler_params=_params(("parallel", "parallel")),
        name="ada_mod",
    )(c_all, w, b.reshape(nl, 1, no))


def _norm_mm_kernel(*refs, mode, tn, perm):
    if perm:
        pm_ref, refs = refs[0], refs[1:]
        h0_ref, refs = refs[-1], refs[:-1]
    if mode == "swiglu":
        x_ref, g_ref, sh_ref, sc_ref, wg_ref, wu_ref, o_ref, h_ref = refs
    elif mode == "rope":
        x_ref, g_ref, sh_ref, sc_ref, w_ref, c_ref, s_ref, rot_ref, o_ref, h_ref = refs
    else:
        x_ref, g_ref, sh_ref, sc_ref, w_ref, b_ref, o_ref, h_ref = refs
    j = pl.program_id(2)

    @pl.when(j == 0)
    def _():
        h = _modnorm(x_ref[...].reshape(h_ref.shape), g_ref[...], sh_ref[...], sc_ref[...]).astype(BF16)
        if perm:
            h0_ref[...] = h
            _transpose_rows(h_ref, h0_ref, pm_ref, *perm)
        else:
            h_ref[...] = h

    h = h_ref[...]
    if mode == "swiglu":
        a = _dot(h, wg_ref[...])
        u = _dot(h, wu_ref[...])
        o_ref[...] = (a * jax.nn.sigmoid(a) * u).astype(o_ref.dtype).reshape(o_ref.shape)
    elif mode == "rope":
        acc = _dot(h, w_ref[...])
        part = (j // (GROUP_WIDTH // tn)) % 3

        @pl.when(part == 2)
        def _():
            o_ref[...] = acc.astype(o_ref.dtype).reshape(o_ref.shape)

        @pl.when(part != 2)
        def _():
            reps = tn // HEAD_DIM
            tabs = [t[...].reshape(acc.shape[0], HEAD_DIM) for t in (c_ref, s_ref)]
            c, s = [jnp.concatenate([t] * reps, axis=1) for t in tabs]
            accb = acc.astype(BF16)
            rw = rot_ref.shape[0]
            partner = jnp.concatenate([_dot(accb[:, k * rw:(k + 1) * rw], rot_ref[...]) for k in range(tn // rw)],
                                      axis=1)
            qs = jnp.where(part == 0, HEAD_DIM ** -0.5, 1.0).astype(F32)
            o_ref[...] = ((acc * c + partner * s) * qs).astype(o_ref.dtype).reshape(o_ref.shape)
    else:
        o_ref[...] = (_dot(h, w_ref[...]) + b_ref[...]).astype(o_ref.dtype).reshape(o_ref.shape)


def _norm_mm(x, x_spec, perm, rows, grid_rows, g, shift, scale, ws, w_layer, col0, nout, out_shape,
             out_spec, *, mode, bias=None, rope=None, tn=512):
    d = x.shape[-1]
    bsz = x.shape[0]
    cb = col0 // tn
    vec = pl.BlockSpec((None, 1, d), lambda b, i, j: (b, 0, 0))
    in_specs = [x_spec, pl.BlockSpec((1, d), lambda b, i, j: (0, 0)), vec, vec]
    in_specs += [pl.BlockSpec((None, d, tn), lambda b, i, j: (w_layer, 0, cb + j)) for _ in ws]
    args = [x, g.reshape(1, d), shift, scale, *ws]
    scratch = [pltpu.VMEM((rows, d), BF16)]
    if perm:
        in_specs.insert(0, pl.BlockSpec((PERM_ROWS, PERM_ROWS), lambda b, i, j: (0, 0)))
        args.insert(0, _perm_for(*perm))
        scratch.append(pltpu.VMEM((rows, d), BF16))
    if mode == "rope":
        tabs, tab_spec = rope
        rw = 4 * HEAD_DIM
        in_specs += [tab_spec] * 2 + [pl.BlockSpec((rw, rw), lambda b, i, j: (0, 0))]
        args += list(tabs) + [_rope_partner_matrix(rw)]
    elif mode == "bias":
        in_specs.append(pl.BlockSpec((1, tn), lambda b, i, j: (0, cb + j)))
        args.append(bias)
    return pl.pallas_call(
        functools.partial(_norm_mm_kernel, mode=mode, tn=tn, perm=perm),
        grid=(bsz, grid_rows, nout // tn),
        in_specs=in_specs,
        out_specs=out_spec,
        out_shape=out_shape,
        scratch_shapes=scratch,
        compiler_params=_params(("parallel", "parallel", "arbitrary")),
        name="norm_mm_" + mode,
    )(*args)


def _mm_res_kernel(*refs, a_mode, perm):
    if a_mode == "merge":
        (pm_ref, o0, o1, o2, l0, l1, l2, e_ref, w_ref, b_ref, x_ref, gt_ref, out_ref, a_ref, a0_ref) = refs
    elif a_mode == "perm":
        pm_ref, a_in, w_ref, b_ref, x_ref, gt_ref, out_ref, a_ref, a0_ref = refs
    else:
        a_in, w_ref, b_ref, x_ref, gt_ref, out_ref = refs
    j = pl.program_id(2)

    if a_mode != "plain":
        @pl.when(j == 0)
        def _():
            k = a_ref.shape[1]
            if a_mode == "merge":
                ls = [l[...].reshape(-1, LANES) for l in (l0, l1, l2)]
                mx = jnp.maximum(jnp.maximum(ls[0], ls[1]), ls[2])
                ws = [jnp.exp(l - mx) for l in ls]
                inv = 1.0 / (ws[0] + ws[1] + ws[2])
                num = jnp.zeros((ls[0].shape[0], k), F32)
                for w, o in zip(ws, (o0, o1, o2)):
                    hi, lo = _split(w * inv)
                    wide = _dot(hi, e_ref[...]) + _dot(lo, e_ref[...])
                    num = num + wide * o[...].reshape(-1, k).astype(F32)
                a0_ref[...] = num.astype(BF16)
            else:
                a0_ref[...] = a_in[...].reshape(-1, k).astype(BF16)
            _transpose_rows(a_ref, a0_ref, pm_ref, *perm)

        a = a_ref[...]
    else:
        a = a_in[...]
    out_ref[...] = x_ref[...] + gt_ref[...] * (_dot(a, w_ref[...]) + b_ref[...])


def _head_spread_matrix(width):
    m = np.zeros((LANES, width), np.float32)
    for hd in range(width // HEAD_DIM):
        m[hd * LSE_LANES, hd * HEAD_DIM:(hd + 1) * HEAD_DIM] = 1.0
    return jnp.asarray(m, BF16)


def _mm_res(a_list, a_specs, w, w_layer, bias, x, gate, rows, *, a_mode="plain", perm=None, tn=512):
    bsz, seq_len, d = x.shape
    k = w.shape[1]
    blk = pl.BlockSpec((None, rows, tn), lambda b, i, j: (b, i, j))
    if a_mode == "merge":
        a_list = list(a_list) + [_head_spread_matrix(k)]
        a_specs = list(a_specs) + [pl.BlockSpec((LANES, k), lambda b, i, j: (0, 0))]
    in_specs = list(a_specs) + [
        pl.BlockSpec((None, k, tn), lambda b, i, j: (w_layer, 0, j)),
        pl.BlockSpec((1, tn), lambda b, i, j: (0, j)),
        blk,
        pl.BlockSpec((None, 1, tn), lambda b, i, j: (b, 0, j)),
    ]
    args = [*a_list, w, bias, x, gate]
    scratch = []
    if a_mode != "plain":
        in_specs.insert(0, pl.BlockSpec((PERM_ROWS, PERM_ROWS), lambda b, i, j: (0, 0)))
        args.insert(0, _perm_for(*perm))
        scratch = [pltpu.VMEM((rows, k), BF16)] * 2
    return pl.pallas_call(
        functools.partial(_mm_res_kernel, a_mode=a_mode, perm=perm),
        grid=(bsz, seq_len // rows, d // tn),
        in_specs=in_specs,
        out_specs=blk,
        out_shape=jax.ShapeDtypeStruct(x.shape, F32),
        scratch_shapes=scratch,
        compiler_params=_params(("parallel", "parallel", "arbitrary")),
        name="mm_res_" + a_mode,
    )(*args)


def _final_kernel(x_ref, g_ref, sh_ref, sc_ref, o_ref):
    o_ref[...] = _modnorm(x_ref[...], g_ref[...], sh_ref[...], sc_ref[...])


def _final(x, g, shift, scale, tm=512):
    bsz, seq_len, d = x.shape
    vec = pl.BlockSpec((None, 1, d), lambda b, i: (b, 0, 0))
    blk = pl.BlockSpec((None, tm, d), lambda b, i: (b, i, 0))
    return pl.pallas_call(
        _final_kernel,
        grid=(bsz, seq_len // tm),
        in_specs=[blk, pl.BlockSpec((1, d), lambda b, i: (0, 0)), vec, vec],
        out_specs=blk,
        out_shape=jax.ShapeDtypeStruct(x.shape, F32),
        compiler_params=_params(("parallel", "parallel")),
        name="final_norm",
    )(x, g.reshape(1, d), shift, scale)


def _filter_positions(seq_len, n1):
    n = 2 * seq_len
    h = SLAB_ROWS
    half = jnp.arange(2)[:, None, None]
    s = jnp.arange(n1)[None, :, None]
    r = jnp.arange(h)[None, None, :]
    idx = ((half * h + r) * n1 + s).reshape(n)
    pos = jnp.where(idx < seq_len, idx, n - idx).astype(F32)
    sign = jnp.where(idx < seq_len, 1.0, jnp.where(idx == seq_len, 0.0, -1.0)).astype(F32)
    t = pos / max(seq_len - 1, 1)
    bands = jnp.linspace(1e-4, FILTER_BANDS - 1, FILTER_BANDS, dtype=F32)
    ang = 2.0 * math.pi * pos[:, None] * bands[None, :] / seq_len
    z = jnp.concatenate([t[:, None], jnp.cos(ang), -jnp.sin(ang)], axis=-1)
    z = jnp.pad(z, ((0, 0), (0, LANES - FILTER_EMB - 1)))
    return jnp.concatenate([z, sign[:, None]], axis=-1)


def _filter_mlp_kernel(z_ref, w1_ref, b1_ref, f_ref, w2_ref, b2_ref, o_ref):
    f = f_ref[...]
    a = jnp.sin(f[0:1, :] * (_dot3(z_ref[...], w1_ref[...]) + b1_ref[...]))
    o_ref[...] = jnp.sin(f[1:2, :] * (_dot3(a, w2_ref[...]) + b2_ref[...]))


def _filter_mlp(zf, w1, b1, freq, w2, b2, tr=512):
    n = zf.shape[0]
    hid = w1.shape[1]
    w1p = jnp.pad(w1, ((0, LANES - w1.shape[0]), (0, 0)))
    full = lambda shape: pl.BlockSpec(shape, lambda i: (0,) * len(shape))
    return pl.pallas_call(
        _filter_mlp_kernel,
        grid=(n // tr,),
        in_specs=[pl.BlockSpec((tr, LANES), lambda i: (i, 0)), full((LANES, hid)), full((1, hid)),
                  full((2, hid)), full((hid, hid)), full((1, hid))],
        out_specs=pl.BlockSpec((tr, hid), lambda i: (i, 0)),
        out_shape=jax.ShapeDtypeStruct((n, hid), F32),
        compiler_params=_params(("parallel",)),
        name="filter_mlp",
    )(zf, w1p, b1.reshape(1, hid), freq, w2, b2.reshape(1, hid))


def _dft_consts(n1, n2):
    n = n1 * n2
    k2 = np.arange(n2 // 2)[:, None]
    nn2 = np.arange(n2)[None, :]
    ph = 2.0 * np.pi * nn2 * (k2 + 0.5) / n2
    f_s1 = np.concatenate([np.cos(ph), -np.sin(ph)], axis=0)
    m = np.arange(n2 // 2)[:, None]
    kk = np.arange(n2 // 2)[None, :]
    ph3 = 2.0 * np.pi * m * (kk + 0.5) / n2
    f_s3 = (2.0 / n) * np.concatenate([np.cos(ph3), -np.sin(ph3)], axis=1)
    a = 2.0 * np.pi * np.outer(np.arange(n1), np.arange(n1)) / n1
    c, s = np.cos(a), -np.sin(a)
    f_fwd = np.block([[c, -s], [s, c]])
    f_inv = np.block([[c, s], [-s, c]])
    th = 2.0 * np.pi * (np.arange(n2 // 2)[:, None] + 0.5) * np.arange(n1)[None, :] / n
    as_bf = lambda x: jnp.asarray(x, F32).astype(BF16)
    tc, ts = jnp.asarray(np.cos(th), F32), jnp.asarray(np.sin(th), F32)
    ff, fi = jnp.asarray(f_fwd, F32), jnp.asarray(f_inv, F32)
    fl, fr = ff[None, :, :n1], ff[None, :, n1:]
    g_fwd = jnp.concatenate([fl * tc[:, None, :] - fr * ts[:, None, :],
                             fl * ts[:, None, :] + fr * tc[:, None, :]], axis=2).astype(BF16)
    it, ib = fi[None, :n1, :], fi[None, n1:, :]
    g_inv = jnp.concatenate([tc[:, :, None] * it - ts[:, :, None] * ib,
                             ts[:, :, None] * it + tc[:, :, None] * ib], axis=1).astype(BF16)
    return dict(f_s1=as_bf(f_s1[:, :n2 // 2]), f_s1_hi=as_bf(f_s1[:, n2 // 2:]), f_s3=as_bf(f_s3),
                g_fwd=g_fwd, g_inv=g_inv, n1=n1)


def _slab_tile(n1):
    return min(n1, 8)


HI16 = -65536
HALF_ULP16 = 0x8000


def _pack_c(re, im):
    rb = lax.bitcast_convert_type(re, jnp.int32) + HALF_ULP16
    ib = lax.bitcast_convert_type(im, jnp.int32) + HALF_ULP16
    return (rb & HI16) | lax.shift_right_logical(ib, 16)


def _pack_bf16(re, im):
    rb = lax.bitcast_convert_type(re.astype(F32), jnp.int32)
    ib = lax.bitcast_convert_type(im.astype(F32), jnp.int32)
    return rb | lax.shift_right_logical(ib, 16)


def _unpack_c(p):
    re = lax.bitcast_convert_type(p & HI16, F32)
    im = lax.bitcast_convert_type(lax.shift_left(p, 16), F32)
    return re, im


def _shift_rows(x, down):
    rows = x.shape[0]
    row = lax.broadcasted_iota(jnp.int32, x.shape, 0)
    if down:
        return jnp.where(row == 0, 0.0, pltpu.roll(x, 1, 0))
    return jnp.where(row == rows - 1, 0.0, pltpu.roll(x, rows - 1, 0))


def _short_conv_slabs(main_ref, prev_ref, next_ref, w_ref, b_ref, first, last):
    n_slabs = main_ref.shape[0]
    prev = prev_ref[0].astype(F32)
    prev = jnp.where(first, _shift_rows(prev, True), prev)
    nxt = next_ref[0].astype(F32)
    nxt = jnp.where(last, _shift_rows(nxt, False), nxt)
    w = w_ref[...]
    out = []
    for s in range(n_slabs):
        up = prev if s == 0 else main_ref[s - 1].astype(F32)
        dn = nxt if s == n_slabs - 1 else main_ref[s + 1].astype(F32)
        out.append(up * w[0:1, :] + main_ref[s].astype(F32) * w[1:2, :] + dn * w[2:3, :] + b_ref[...])
    return out


def _fft_s1_kernel(*refs, short_conv):
    re_ref, im_ref, ret_ref, imt_ref = refs[-4:]
    if short_conv:
        pm_ref, f_ref, m_ref, p_ref, n_ref, w_ref, b_ref, o_ref, u_ref = refs[:-4]
        t = pl.program_id(1)
        slabs = _short_conv_slabs(m_ref, p_ref, n_ref, w_ref, b_ref, t == 0, t == pl.num_programs(1) - 1)
    else:
        pm_ref, f_ref, m_ref, o_ref = refs[:-4]
        slabs = [m_ref[s] for s in range(m_ref.shape[0])]
    half = o_ref.shape[0]
    for s, u in enumerate(slabs):
        if short_conv:
            u_ref[s] = u.astype(u_ref.dtype)
        r = _dot(f_ref[...], u.astype(BF16))
        re_ref[s * half:(s + 1) * half, :] = r[:half].astype(BF16)
        im_ref[s * half:(s + 1) * half, :] = r[half:].astype(BF16)
    _transpose_rows(ret_ref, re_ref, pm_ref, len(slabs), half)
    _transpose_rows(imt_ref, im_ref, pm_ref, len(slabs), half)
    o_ref[...] = _pack_bf16(ret_ref[...], imt_ref[...]).reshape(o_ref.shape)


def _fft_s1(consts, src, col_block, conv=None, ct=512):
    bsz, n1, h, c = src.shape
    d = consts["d"]
    st = _slab_tile(n1)
    cpb = d // ct
    f = consts["f_s1"]
    main = pl.BlockSpec((None, st, h, ct), lambda b, t, j: (b, t, 0, col_block * cpb + j))
    a_spec = pl.BlockSpec((None, h, st, ct), lambda b, t, j: (b, 0, t, j))
    a_shape = jax.ShapeDtypeStruct((bsz, h, n1, d), jnp.int32)
    fspec = pl.BlockSpec(f.shape, lambda b, t, j: (0, 0))
    pspec = pl.BlockSpec((PERM_ROWS, PERM_ROWS), lambda b, t, j: (0, 0))
    pm = _perm_for(st, h)
    scratch = [pltpu.VMEM((st * h, ct), BF16)] * 4
    if conv is None:
        return pl.pallas_call(
            functools.partial(_fft_s1_kernel, short_conv=False),
            grid=(bsz, n1 // st, cpb),
            in_specs=[pspec, fspec, main], out_specs=a_spec, out_shape=a_shape, scratch_shapes=scratch,
            compiler_params=_params(("parallel", "parallel", "parallel")), name="fft_s1",
        )(pm, f, src)
    w, b = conv
    prev = pl.BlockSpec((None, 1, h, ct), lambda b, t, j: (b, (t * st + n1 - 1) % n1, 0, col_block * cpb + j))
    nxt = pl.BlockSpec((None, 1, h, ct), lambda b, t, j: (b, ((t + 1) * st) % n1, 0, col_block * cpb + j))
    wspec = pl.BlockSpec((3, ct), lambda b, t, j: (0, col_block * cpb + j))
    bspec = pl.BlockSpec((1, ct), lambda b, t, j: (0, col_block * cpb + j))
    u_spec = pl.BlockSpec((None, st, h, ct), lambda b, t, j: (b, t, 0, j))
    return pl.pallas_call(
        functools.partial(_fft_s1_kernel, short_conv=True),
        grid=(bsz, n1 // st, cpb),
        in_specs=[pspec, fspec, main, prev, nxt, wspec, bspec],
        out_specs=[a_spec, u_spec],
        out_shape=[a_shape, jax.ShapeDtypeStruct((bsz, n1, h, d), BF16)],
        scratch_shapes=scratch,
        compiler_params=_params(("parallel", "parallel", "parallel")), name="fft_s1_conv",
    )(pm, f, src, src, src, w, b)


def _filter_s1_kernel(pm_ref, flo_ref, fhi_ref, alo_ref, ahi_ref, zlo_ref, zhi_ref, wf_ref, wb_ref, df_ref,
                      db_ref, o_ref, ss_ref, re_ref, im_ref, ret_ref, imt_ref):
    half = o_ref.shape[0]

    @pl.when(pl.program_id(2) == 0)
    def _():
        ss_ref[...] = jnp.zeros_like(ss_ref)

    n_slabs, rows = alo_ref.shape[0], alo_ref.shape[1]
    ct = o_ref.shape[2]
    ss = jnp.zeros(ss_ref.shape, F32)
    r = jnp.zeros((2 * half, n_slabs * ct), F32)
    for f_ref, a_ref, z_ref, w_ref, d_ref in ((flo_ref, alo_ref, zlo_ref, wf_ref, df_ref),
                                              (fhi_ref, ahi_ref, zhi_ref, wb_ref, db_ref)):
        z = z_ref[...].reshape(n_slabs * rows, LANES)
        t, sign = z[:, 0:1], z[:, LANES - 1:LANES]
        taps = _dot3(a_ref[...].reshape(n_slabs * rows, a_ref.shape[2]), w_ref[...])
        taps = taps * jnp.exp(-t * jnp.abs(d_ref[...])) * sign
        ss = ss + jnp.sum(taps * taps, axis=0, keepdims=True)
        tb = taps.astype(BF16)
        wide = jnp.concatenate([tb[s * rows:(s + 1) * rows, :] for s in range(n_slabs)], axis=1)
        r = r + _dot(f_ref[...], wide)
    for s in range(n_slabs):
        re_ref[s * half:(s + 1) * half, :] = r[:half, s * ct:(s + 1) * ct].astype(BF16)
        im_ref[s * half:(s + 1) * half, :] = r[half:, s * ct:(s + 1) * ct].astype(BF16)
    ss_ref[...] += ss
    _transpose_rows(ret_ref, re_ref, pm_ref, n_slabs, half)
    _transpose_rows(imt_ref, im_ref, pm_ref, n_slabs, half)
    o_ref[...] = _pack_bf16(ret_ref[...], imt_ref[...]).reshape(o_ref.shape)


def _filter_s1(consts, a2, zf, w3, decay, ct=256):
    n1, h = consts["n1"], SLAB_ROWS
    hid = a2.shape[1]
    d = decay.shape[-1]
    nct = d // ct
    st = _slab_tile(n1)
    flo, fhi = consts["f_s1"], consts["f_s1_hi"]
    dec = decay.reshape(1, HYENA_ORDER * N_DIRS * d)
    fspec = pl.BlockSpec(flo.shape, lambda o, j, t: (0, 0))
    rows = lambda half, width: pl.BlockSpec((None, st, h, width), lambda o, j, t: (half, t, 0, 0))
    wcol = lambda dirn, nrow: pl.BlockSpec((nrow, ct), lambda o, j, t: (0, (o * N_DIRS + dirn) * nct + j))
    return pl.pallas_call(
        _filter_s1_kernel,
        grid=(HYENA_ORDER, nct, n1 // st),
        in_specs=[pl.BlockSpec((PERM_ROWS, PERM_ROWS), lambda o, j, t: (0, 0)), fspec, fspec,
                  rows(0, hid), rows(1, hid), rows(0, LANES), rows(1, LANES),
                  wcol(0, hid), wcol(1, hid), wcol(0, 1), wcol(1, 1)],
        out_specs=[pl.BlockSpec((None, h, st, ct), lambda o, j, t: (o, 0, t, j)),
                   pl.BlockSpec((None, 1, ct), lambda o, j, t: (o, 0, j))],
        out_shape=[jax.ShapeDtypeStruct((HYENA_ORDER, h, n1, d), jnp.int32),
                   jax.ShapeDtypeStruct((HYENA_ORDER, 1, d), F32)],
        scratch_shapes=[pltpu.VMEM((st * h, ct), BF16)] * 4,
        compiler_params=_params(("parallel", "parallel", "arbitrary")), name="filter_s1",
    )(_perm_for(st, h), flo, fhi, a2.reshape(2, n1, h, hid), a2.reshape(2, n1, h, hid),
      zf.reshape(2, n1, h, LANES), zf.reshape(2, n1, h, LANES), w3, w3, dec, dec)


def _fft_s2f_kernel(a_ref, gf_ref, ss_ref, o_ref, *, kb):
    n1 = a_ref.shape[1]
    scale = lax.rsqrt(ss_ref[...] + EPS)

    def body(kk, carry):
        ar, ai = _unpack_c(a_ref[kk])
        x = _dot(gf_ref[kk], jnp.concatenate([ar, ai], axis=0).astype(BF16))
        o_ref[kk] = _pack_c(x[:n1] * scale, x[n1:] * scale)
        return carry

    lax.fori_loop(0, kb, body, 0, unroll=min(kb, 4))


def _fft_s2_kernel(a_ref, k_ref, gf_ref, gi_ref, o_ref, *, kb):
    nb, n1, ct = a_ref.shape[0], a_ref.shape[2], a_ref.shape[3]

    def body(kk, carry):
        parts = [_unpack_c(a_ref[b, kk]) for b in range(nb)]
        ar = jnp.concatenate([p[0] for p in parts], axis=1)
        ai = jnp.concatenate([p[1] for p in parts], axis=1)
        x = _dot(gf_ref[kk], jnp.concatenate([ar, ai], axis=0).astype(BF16))
        xr, xi = x[:n1], x[n1:]
        kr, ki = [jnp.concatenate([v] * nb, axis=1) for v in _unpack_c(k_ref[kk])]
        zr = xr * kr - xi * ki
        zi = xr * ki + xi * kr
        y = _dot(gi_ref[kk], jnp.concatenate([zr, zi], axis=0).astype(BF16))
        packed = _pack_c(y[:n1], y[n1:])
        for b in range(nb):
            o_ref[b, kk] = packed[:, b * ct:(b + 1) * ct]
        return carry

    lax.fori_loop(0, kb, body, 0, unroll=min(kb, 4))


def _s2_tiles(n1, d):
    kb = max(1, 512 // n1)
    ct = min(d, 512)
    return kb, ct


def _fft_s2f(a, sumsq, consts):
    n_o, k2n, n1, d = a.shape
    kb, ct = _s2_tiles(n1, d)
    blk = pl.BlockSpec((None, kb, n1, ct), lambda k, j, o: (o, k, 0, j))
    return pl.pallas_call(
        functools.partial(_fft_s2f_kernel, kb=kb),
        grid=(k2n // kb, d // ct, n_o),
        in_specs=[blk, pl.BlockSpec((kb, 2 * n1, 2 * n1), lambda k, j, o: (k, 0, 0)),
                  pl.BlockSpec((None, 1, ct), lambda k, j, o: (o, 0, j))],
        out_specs=blk,
        out_shape=jax.ShapeDtypeStruct(a.shape, jnp.int32),
        compiler_params=_params(("parallel", "parallel", "parallel")),
        name="fft_s2_filter",
    )(a, consts["g_fwd"], sumsq)


def _fft_s2(a, kf, order, consts):
    bsz, k2n, n1, d = a.shape
    kb, ct = _s2_tiles(n1, d)
    nb = bsz if n1 * bsz <= SLAB_ROWS else 1
    blk = pl.BlockSpec((nb, kb, n1, ct), lambda k, j, b: (b, k, 0, j))
    mat = pl.BlockSpec((kb, 2 * n1, 2 * n1), lambda k, j, b: (k, 0, 0))
    return pl.pallas_call(
        functools.partial(_fft_s2_kernel, kb=kb),
        grid=(k2n // kb, d // ct, bsz // nb),
        in_specs=[blk, pl.BlockSpec((None, kb, n1, ct), lambda k, j, b: (order, k, 0, j)), mat, mat],
        out_specs=blk,
        out_shape=jax.ShapeDtypeStruct(a.shape, jnp.int32),
        compiler_params=_params(("parallel", "parallel", "parallel")),
        name="fft_s2",
    )(a, kf, consts["g_fwd"], consts["g_inv"])


def _fft_s3_kernel(pm_ref, f_ref, t_ref, u_ref, gm_ref, gp_ref, gn_ref, w_ref, b_ref, sk_ref, o_ref,
                   re_ref, im_ref, ret_ref, imt_ref):
    t_id = pl.program_id(1)
    half, st = t_ref.shape[0], t_ref.shape[1]
    re, im = _unpack_c(t_ref[...].reshape(half * st, t_ref.shape[2]))
    re_ref[...] = re.astype(BF16)
    im_ref[...] = im.astype(BF16)
    _transpose_rows(ret_ref, re_ref, pm_ref, half, st)
    _transpose_rows(imt_ref, im_ref, pm_ref, half, st)
    gates = _short_conv_slabs(gm_ref, gp_ref, gn_ref, w_ref, b_ref, t_id == 0, t_id == pl.num_programs(1) - 1)
    for s, gate in enumerate(gates):
        t = jnp.concatenate([ret_ref[s * half:(s + 1) * half, :], imt_ref[s * half:(s + 1) * half, :]], axis=0)
        y = _dot(f_ref[...], t)
        o_ref[s] = (gate * (y + u_ref[s].astype(F32) * sk_ref[...])).astype(o_ref.dtype)


def _fft_s3(consts, t, u, z, gate_block, conv_w, conv_b, skip, order, out_dtype, ct=512):
    bsz, h, n1, d = t.shape
    st = _slab_tile(n1)
    cpb = d // ct
    f = consts["f_s3"]
    gcol = lambda j: gate_block * cpb + j
    slab = lambda idx: pl.BlockSpec((None, 1, h, ct), lambda b, tt, j: (b, idx(tt), 0, gcol(j)))
    return pl.pallas_call(
        _fft_s3_kernel,
        grid=(bsz, n1 // st, cpb),
        scratch_shapes=[pltpu.VMEM((st * h, ct), BF16)] * 4,
        in_specs=[
            pl.BlockSpec((PERM_ROWS, PERM_ROWS), lambda b, tt, j: (0, 0)),
            pl.BlockSpec(f.shape, lambda b, tt, j: (0, 0)),
            pl.BlockSpec((None, h, st, ct), lambda b, tt, j: (b, 0, tt, j)),
            pl.BlockSpec((None, st, h, ct), lambda b, tt, j: (b, tt, 0, j)),
            pl.BlockSpec((None, st, h, ct), lambda b, tt, j: (b, tt, 0, gcol(j))),
            slab(lambda tt: (tt * st + n1 - 1) % n1),
            slab(lambda tt: ((tt + 1) * st) % n1),
            pl.BlockSpec((3, ct), lambda b, tt, j: (0, gcol(j))),
            pl.BlockSpec((1, ct), lambda b, tt, j: (0, gcol(j))),
            pl.BlockSpec((None, 1, ct), lambda b, tt, j: (order, 0, j)),
        ],
        out_specs=pl.BlockSpec((None, st, h, ct), lambda b, tt, j: (b, tt, 0, j)),
        out_shape=jax.ShapeDtypeStruct((bsz, n1, h, d), out_dtype),
        compiler_params=_params(("parallel", "parallel", "parallel")),
        name="fft_s3",
    )(_perm_for(h, st), f, t, u, z, z, z, conv_w, conv_b, skip)


def _hyena_filter_spectra(seq_len, consts, fw1, fb1, ffreq, fw2, fb2, fw3, decay):
    d = decay.shape[-1]
    n1 = consts["n1"]
    zf = _filter_positions(seq_len, n1)
    a2 = _filter_mlp(zf, fw1, fb1, ffreq, fw2, fb2)
    a, sumsq = _filter_s1(consts, a2, zf, fw3, decay)
    return _fft_s2f(a, sumsq, consts)


def _hyena_mixer(x, g, shift, scale, gate, p, layer, kf, consts):
    bsz, seq_len, d = x.shape
    n1 = consts["n1"]
    h = SLAB_ROWS
    st = _slab_tile(n1)
    xv = x.reshape(bsz, h, n1, d)
    z = _norm_mm(
        xv, pl.BlockSpec((None, h, st, d), lambda b, i, j: (b, 0, i, 0)), (h, st), st * h, n1 // st, g, shift, scale,
        [p["hy_w_in"]], layer, 0, 3 * d, jax.ShapeDtypeStruct((bsz, n1, h, 3 * d), BF16),
        pl.BlockSpec((None, st, h, COL_TILE), lambda b, i, j: (b, i, 0, j)),
        mode="bias", bias=p["hy_b_in"][layer].reshape(1, 3 * d), tn=COL_TILE)
    cw, cb = p["hy_conv_w"][layer], p["hy_conv_b"][layer].reshape(1, 3 * d)
    skip = p["hy_skip"][layer].reshape(HYENA_ORDER, 1, d)
    a, u = _fft_s1(consts, z, 2, conv=(cw, cb))
    t = _fft_s2(a, kf, 0, consts)
    y1 = _fft_s3(consts, t, u, z, 0, cw, cb, skip, 0, BF16)
    a = _fft_s1(consts, y1, 0)
    t = _fft_s2(a, kf, 1, consts)
    y2 = _fft_s3(consts, t, y1, z, 1, cw, cb, skip, 1, F32)
    q = ROW_TILE // n1
    return _mm_res(
        [y2], [pl.BlockSpec((None, n1, q, d), lambda b, i, j: (b, 0, i, 0))], p["hy_w_out"], layer,
        p["hy_b_out"][layer].reshape(1, d), x, gate, ROW_TILE, a_mode="perm", perm=(n1, q), tn=COL_TILE)


def _attn_kernel(*refs, phases, ta, n_sub, to_classes):
    if to_classes:
        (pm_ref, q_ref, kp_ref, km_ref, kn_ref, vp_ref, vm_ref, vn_ref, o_ref, l_ref, kx_ref, vx_ref,
         on_ref, ot_ref, ln_ref, ls_ref, lt_ref) = refs
    else:
        q_ref, kp_ref, km_ref, kn_ref, vp_ref, vm_ref, vn_ref, o_ref, l_ref, kx_ref, vx_ref = refs
    i = pl.program_id(2)
    halo = ATT_BAND // phases
    qa = ATT_Q // phases
    ka = 2 * qa
    kx_ref[:, 0:halo] = kp_ref[...]
    kx_ref[:, halo:halo + ta] = km_ref[...]
    kx_ref[:, halo + ta:] = kn_ref[...]
    vx_ref[:, 0:halo] = vp_ref[...]
    vx_ref[:, halo:halo + ta] = vm_ref[...]
    vx_ref[:, halo + ta:] = vn_ref[...]
    row = lax.broadcasted_iota(jnp.int32, (ATT_Q, 2 * ATT_Q), 0)
    col = lax.broadcasted_iota(jnp.int32, (ATT_Q, 2 * ATT_Q), 1)
    cq, aq = row >> (qa.bit_length() - 1), row & (qa - 1)
    ck, ak = col >> (ka.bit_length() - 1), col & (ka - 1)
    delta = phases * (ak - aq) - ATT_BAND + ck - cq
    band = (delta >= -ATT_BAND) & (delta <= ATT_BAND)
    lane_head = lax.broadcasted_iota(jnp.int32, (ATT_Q, LANES), 1) >> (LSE_LANES.bit_length() - 1)
    for s in range(ta // qa):
        key_idx = phases * (i * ta + s * qa - halo + ak) + ck
        valid = band & (key_idx >= 0) & (key_idx < n_sub)
        lse_tile = jnp.zeros((ATT_Q, LANES), F32)
        for h in range(HEADS_PER_GROUP):
            cs = slice(h * HEAD_DIM, (h + 1) * HEAD_DIM)
            q = jnp.concatenate([q_ref[c, s * qa:(s + 1) * qa, cs] for c in range(phases)], axis=0)
            k = jnp.concatenate([kx_ref[c, s * qa:s * qa + ka, cs] for c in range(phases)], axis=0)
            v = jnp.concatenate([vx_ref[c, s * qa:s * qa + ka, cs] for c in range(phases)], axis=0)
            sc = lax.dot_general(q, k, (((1,), (1,)), ((), ())), preferred_element_type=F32)
            sc = jnp.where(valid, sc, NEG_BIG)
            m = jnp.max(sc, axis=-1, keepdims=True)
            pr = jnp.exp(sc - m)
            den = jnp.sum(pr, axis=-1, keepdims=True)
            o = (_dot(pr.astype(BF16), v) / den).astype(BF16)
            lse_tile = jnp.where(lane_head == h, m + jnp.log(den), lse_tile)
            if to_classes:
                on_ref[s * ATT_Q:(s + 1) * ATT_Q, cs] = o
            else:
                for c in range(phases):
                    o_ref[c, s * qa:(s + 1) * qa, cs] = o[c * qa:(c + 1) * qa]
        if to_classes:
            ln_ref[s * ATT_Q:(s + 1) * ATT_Q, :] = lse_tile
        else:
            for c in range(phases):
                l_ref[c, s * qa:(s + 1) * qa, :] = lse_tile[c * qa:(c + 1) * qa]
    if to_classes:
        na = ta // CLASSES
        _transpose_rows(ot_ref, on_ref, pm_ref, na, CLASSES)
        o_ref[...] = ot_ref[...].reshape(o_ref.shape)
        rest = ln_ref[...]
        total = jnp.zeros(rest.shape, F32)
        for _ in range(3):
            piece = rest.astype(BF16)
            rest = rest - piece.astype(F32)
            ls_ref[...] = piece
            _transpose_rows(lt_ref, ls_ref, pm_ref, na, CLASSES)
            total = total + lt_ref[...].astype(F32)
        l_ref[...] = total.reshape(l_ref.shape)


def _attn_group(qkv, col0, phases, n_sub, lead_grid, lead_block, lead_index, ta, *, to_classes=False,
                out_arr_shape=None, out_block=None, out_index=None):
    gw = GROUP_WIDTH
    halo = ATT_BAND // phases
    rows = qkv.shape[-2]
    per = ta // halo
    nblk = rows // halo
    cb = col0 // gw

    def spec(nrows, ridx, part):
        return pl.BlockSpec(tuple(lead_block) + (nrows, gw),
                            lambda b, rho, i: tuple(lead_index(b, rho)) + (ridx(i), cb + part))

    main = lambda part: spec(ta, lambda i: i, part)
    prev = lambda part: spec(halo, lambda i: jnp.maximum(i * per - 1, 0), part)
    nxt = lambda part: spec(halo, lambda i: jnp.minimum((i + 1) * per, nblk - 1), part)
    bsz = qkv.shape[0]
    in_specs = [main(0), prev(1), main(1), nxt(1), prev(2), main(2), nxt(2)]
    args = [qkv] * 7
    if to_classes:
        assert ta == PERM_ROWS
        out_specs = [pl.BlockSpec(out_block + (gw,), out_index), pl.BlockSpec(out_block + (LANES,), out_index)]
        scratch_extra = [pltpu.VMEM((ta, gw), BF16)] * 2 + [pltpu.VMEM((ta, LANES), F32)] + \
                        [pltpu.VMEM((ta, LANES), BF16)] * 2
        in_specs.insert(0, pl.BlockSpec((PERM_ROWS, PERM_ROWS), lambda b, rho, i: (0, 0)))
        args.insert(0, _perm_for(ta // CLASSES, CLASSES))
    else:
        out_arr_shape = qkv.shape[:-1]
        oidx = lambda b, rho, i: tuple(lead_index(b, rho)) + (i, 0)
        out_specs = [pl.BlockSpec(tuple(lead_block) + (ta, gw), oidx),
                     pl.BlockSpec(tuple(lead_block) + (ta, LANES), oidx)]
        scratch_extra = []
    kx_shape = (phases, ta + 2 * halo, gw)
    return pl.pallas_call(
        functools.partial(_attn_kernel, phases=phases, ta=ta, n_sub=n_sub, to_classes=to_classes),
        grid=(bsz, lead_grid, rows // ta),
        in_specs=in_specs,
        out_specs=out_specs,
        out_shape=[jax.ShapeDtypeStruct(tuple(out_arr_shape) + (gw,), BF16),
                   jax.ShapeDtypeStruct(tuple(out_arr_shape) + (LANES,), F32)],
        scratch_shapes=[pltpu.VMEM(kx_shape, BF16)] * 2 + scratch_extra,
        compiler_params=_params(("parallel", "parallel", "parallel")),
        name="attn_p%d" % phases + ("_cls" if to_classes else ""),
    )(*args)


def _rope_tables(pos):
    half = ROT_DIM // 2
    inv = ROPE_THETA ** (-jnp.arange(0, ROT_DIM, 2, dtype=F32) / ROT_DIM)
    ang = pos.astype(F32)[:, None] * inv[None, :]
    cos, sin = jnp.cos(ang), jnp.sin(ang)
    n = pos.shape[0]
    rest = HEAD_DIM - ROT_DIM
    c = jnp.concatenate([cos, cos, jnp.ones((n, rest), F32)], axis=1)
    s = jnp.concatenate([sin, sin, jnp.zeros((n, rest), F32)], axis=1)
    return c, s


def _rope_partner_matrix(width):
    half = ROT_DIM // 2
    m = np.zeros((width, width), np.float32)
    for base in range(0, width, HEAD_DIM):
        for k in range(half):
            m[base + k + half, base + k] = -1.0
            m[base + k, base + k + half] = 1.0
    return jnp.asarray(m, BF16)


def _attn_mixer(x, g, shift, scale, gate, p, layer):
    bsz, seq_len, d = x.shape
    gw = GROUP_WIDTH
    nc = seq_len // CLASSES
    ca = ROW_TILE // CLASSES
    w_in = p["at_w_in"]
    tabs = _rope_tables(jnp.arange(seq_len))
    qkv0 = _norm_mm(
        x, pl.BlockSpec((None, ROW_TILE, d), lambda b, i, j: (b, i, 0)), None, ROW_TILE, seq_len // ROW_TILE,
        g, shift, scale, [w_in], layer, 0, 3 * gw, jax.ShapeDtypeStruct((bsz, seq_len, 3 * gw), BF16),
        pl.BlockSpec((None, ROW_TILE, COL_TILE), lambda b, i, j: (b, i, j)), mode="rope",
        rope=(tabs, pl.BlockSpec((ROW_TILE, HEAD_DIM), lambda b, i, j: (i, 0))), tn=COL_TILE)
    pos_c = (jnp.arange(nc)[None, :] * CLASSES + jnp.arange(CLASSES)[:, None]).reshape(-1)
    tabs_c = [t.reshape(CLASSES, nc, HEAD_DIM) for t in _rope_tables(pos_c)]
    qkv12 = _norm_mm(
        x.reshape(bsz, nc, CLASSES, d), pl.BlockSpec((None, ca, CLASSES, d), lambda b, i, j: (b, i, 0, 0)),
        (ca, CLASSES), ROW_TILE, nc // ca, g, shift, scale, [w_in], layer, 3 * gw, 6 * gw,
        jax.ShapeDtypeStruct((bsz, CLASSES, nc, 6 * gw), BF16),
        pl.BlockSpec((None, CLASSES, ca, COL_TILE), lambda b, i, j: (b, 0, i, j)), mode="rope",
        rope=(tabs_c, pl.BlockSpec((CLASSES, ca, HEAD_DIM), lambda b, i, j: (0, i, 0))), tn=COL_TILE)
    cls_shape = (bsz, CLASSES, nc)
    ta0 = PERM_ROWS
    o0, l0 = _attn_group(
        qkv0.reshape(bsz, 1, seq_len, 3 * gw), 0, 1, seq_len, 1, (None, 1), lambda b, rho: (b, 0), ta0,
        to_classes=True, out_arr_shape=cls_shape, out_block=(None, CLASSES, ta0 // CLASSES),
        out_index=lambda b, rho, i: (b, 0, i, 0))
    dil1 = ATTN_PATTERNS[1][1]
    ph = CLASSES // dil1
    ta1 = min(64, nc)
    o1, l1 = _attn_group(
        qkv12.reshape(bsz, ph, dil1, nc, 6 * gw), 0, ph, seq_len // dil1, dil1, (None, ph, None),
        lambda b, rho: (b, 0, rho), ta1)
    ta2 = min(256, nc)
    o2, l2 = _attn_group(
        qkv12.reshape(bsz, CLASSES, 1, nc, 6 * gw), 3 * gw, 1, nc, CLASSES, (None, None, 1),
        lambda b, rho: (b, rho, 0), ta2)
    os_ = [o.reshape(cls_shape + (gw,)) for o in (o0, o1, o2)]
    ls = [l.reshape(cls_shape + (LANES,)) for l in (l0, l1, l2)]
    blk = lambda width: pl.BlockSpec((None, CLASSES, ca, width), lambda b, i, j: (b, 0, i, 0))
    return _mm_res(os_ + ls, [blk(gw)] * 3 + [blk(LANES)] * 3, p["at_w_out"], layer, jnp.zeros((1, d), F32),
                   x, gate, ROW_TILE, a_mode="merge", perm=(CLASSES, ca), tn=COL_TILE)


def _ffn(x, g, shift, scale, gate, p, layer):
    bsz, seq_len, d = x.shape
    dff = p["ffn_w_gate"].shape[-1]
    tiles = seq_len // ROW_TILE
    xs = pl.BlockSpec((None, ROW_TILE, d), lambda b, i, j: (b, i, 0))
    hs = pl.BlockSpec((None, ROW_TILE, 512), lambda b, i, j: (b, i, j))
    hmid = _norm_mm(x, xs, None, ROW_TILE, tiles, g, shift, scale, [p["ffn_w_gate"], p["ffn_w_up"]], layer, 0,
                    dff, jax.ShapeDtypeStruct((bsz, seq_len, dff), BF16), hs, mode="swiglu")
    return _mm_res([hmid], [pl.BlockSpec((None, ROW_TILE, dff), lambda b, i, j: (b, i, 0))],
                   p["ffn_w_down"], layer, jnp.zeros((1, d), F32), x, gate, ROW_TILE)


def _encoder(x, mods, final_mod, p):
    bsz, seq_len, d = x.shape
    n1 = 2 * seq_len // DFT_N2
    consts = _dft_consts(n1, DFT_N2)
    consts["d"] = d
    for i in range(DEPTH):
        sh_m, sc_m, g_m, sh_f, sc_f, g_f = [mods[i][:, None, k * d:(k + 1) * d] for k in range(6)]
        j = i // 2
        if i % 2 == 0:
            kf = _hyena_filter_spectra(seq_len, consts, p["hy_fw1"][j], p["hy_fb1"][j], p["hy_ffreq"][j],
                                       p["hy_fw2"][j], p["hy_fb2"][j], p["hy_fw3"][j], p["hy_decay"][j])
            x = _hyena_mixer(x, p["norm_mix"][i], sh_m, sc_m, g_m, p, j, kf, consts)
        else:
            x = _attn_mixer(x, p["norm_mix"][i], sh_m, sc_m, g_m, p, j)
        x = _ffn(x, p["norm_ffn"][i], sh_f, sc_f, g_f, p, i)
    sh, sc = final_mod[:, None, :d], final_mod[:, None, d:]
    return _final(x, p["final_norm"], sh, sc)


def kernel(x_prompt, x_sample, c_prompt, c_sample, ada_w, ada_b, norm_mix, norm_ffn, hy_w_in, hy_b_in, hy_conv_w, hy_conv_b, hy_fw1, hy_fb1, hy_ffreq, hy_fw2, hy_fb2, hy_fw3, hy_decay, hy_skip, hy_w_out, hy_b_out, at_w_in, at_w_out, ffn_w_gate, ffn_w_up, ffn_w_down, final_norm, final_ada_w, final_ada_b):
    d = x_prompt.shape[-1]
    bp, bs = c_prompt.shape[0], c_sample.shape[0]
    pad = -(bp + bs) % (2 * SUBLANES)
    c_all = jnp.concatenate([c_prompt, c_sample, jnp.zeros((pad, d), F32)], axis=0)
    mods = _ada(c_all, ada_w, ada_b)
    fmod = _ada(c_all, final_ada_w[None], final_ada_b[None])[0]
    p = dict(norm_mix=norm_mix, norm_ffn=norm_ffn,
             hy_w_in=hy_w_in.astype(BF16), hy_b_in=hy_b_in, hy_conv_w=hy_conv_w, hy_conv_b=hy_conv_b,
             hy_fw1=hy_fw1, hy_fb1=hy_fb1, hy_ffreq=hy_ffreq, hy_fw2=hy_fw2, hy_fb2=hy_fb2, hy_fw3=hy_fw3,
             hy_decay=hy_decay, hy_skip=hy_skip, hy_w_out=hy_w_out.astype(BF16), hy_b_out=hy_b_out,
             at_w_in=at_w_in.astype(BF16), at_w_out=at_w_out.astype(BF16),
             ffn_w_gate=ffn_w_gate.astype(BF16), ffn_w_up=ffn_w_up.astype(BF16),
             ffn_w_down=ffn_w_down.astype(BF16), final_norm=final_norm)
    y_prompt = _encoder(x_prompt, mods[:, :bp], fmod[:bp], p)
    y_sample = _encoder(x_sample, mods[:, bp:bp + bs], fmod[bp:bp + bs], p)
    return (y_prompt, y_sample)
```

```python
import functools
import math

import numpy as np
import jax
import jax.numpy as jnp
from jax import lax
from jax.experimental import pallas as pl
from jax.experimental.pallas import tpu as pltpu

F32 = jnp.float32
BF16 = jnp.bfloat16
EPS = 1e-6

DEPTH = 4
HYENA_ORDER = 2
N_DIRS = 2
FILTER_BANDS = 16
FILTER_EMB = 1 + 2 * FILTER_BANDS
ATTN_PATTERNS = ((128, 1), (512, 4), (2048, 16))
HEADS_PER_GROUP = 8
HEAD_DIM = 128
GROUP_WIDTH = HEADS_PER_GROUP * HEAD_DIM
ROT_DIM = HEAD_DIM // 4
ROPE_THETA = 500000.0

LANES = 128
SUBLANES = 8
VMEM_LIMIT_BYTES = 56 * 1024 * 1024

DFT_N2 = 256
SLAB_ROWS = DFT_N2 // 2
ATT_BAND = 64
ATT_Q = 2 * ATT_BAND
CLASSES = 16
LSE_LANES = LANES // HEADS_PER_GROUP
ROW_TILE = 1024
COL_TILE = 1024
NEG_BIG = -1e30


def _params(sem):
    return pltpu.CompilerParams(dimension_semantics=sem, vmem_limit_bytes=VMEM_LIMIT_BYTES)


def _dot(a, b):
    return jnp.dot(a, b, preferred_element_type=F32)


def _split(a):
    hi = a.astype(BF16)
    lo = (a - hi.astype(F32)).astype(BF16)
    return hi, lo


def _dot3(a, b):
    ah, al = _split(a)
    bh, bl = _split(b)
    return _dot(ah, bh) + _dot(al, bh) + _dot(ah, bl)


def _modnorm(x, g, shift, scale):
    ms = jnp.mean(x * x, axis=-1, keepdims=True)
    return (x * lax.rsqrt(ms + EPS)) * (g * (1.0 + scale)) + shift


PERM_ROWS = 256


def _perm_matrix(p, q):
    m = np.zeros((PERM_ROWS, PERM_ROWS), np.float32)
    pi, qi = np.meshgrid(np.arange(p), np.arange(q), indexing="ij")
    m[(qi * p + pi).ravel(), (pi * q + qi).ravel()] = 1.0
    return jnp.asarray(m, BF16)


def _perm_for(p, q):
    assert (q <= 16 and p % (PERM_ROWS // q) == 0) or (p <= 16 and q % (PERM_ROWS // p) == 0), (p, q)
    return _perm_matrix(PERM_ROWS // q, q) if q <= 16 else _perm_matrix(p, PERM_ROWS // p)


def _transpose_rows(dst_ref, src_ref, pm_ref, p, q):
    if q <= 16:
        pg = PERM_ROWS // q
        for grp in range(p // pg):
            t = _dot(pm_ref[...], src_ref[grp * PERM_ROWS:(grp + 1) * PERM_ROWS, :]).astype(BF16)
            for qi in range(q):
                dst_ref[qi * p + grp * pg:qi * p + (grp + 1) * pg, :] = t[qi * pg:(qi + 1) * pg]
    else:
        qg = PERM_ROWS // p
        for grp in range(q // qg):
            blk = jnp.concatenate([src_ref[pi * q + grp * qg:pi * q + (grp + 1) * qg, :] for pi in range(p)],
                                  axis=0)
            dst_ref[grp * PERM_ROWS:(grp + 1) * PERM_ROWS, :] = _dot(pm_ref[...], blk).astype(BF16)


def _ada_kernel(c_ref, w_ref, b_ref, o_ref):
    c = c_ref[...]
    cs = c * jax.nn.sigmoid(c)
    o_ref[...] = _dot3(cs, w_ref[...]) + b_ref[...]


def _ada(c_all, w, b, tn=1024):
    nl, d, no = w.shape
    r = c_all.shape[0]
    return pl.pallas_call(
        _ada_kernel,
        grid=(nl, no // tn),
        in_specs=[
            pl.BlockSpec((r, d), lambda l, j: (0, 0)),
            pl.BlockSpec((None, d, tn), lambda l, j: (l, 0, j)),
            pl.BlockSpec((None, 1, tn), lambda l, j: (l, 0, j)),
        ],
        out_specs=pl.BlockSpec((None, r, tn), lambda l, j: (l, 0, j)),
        out_shape=jax.ShapeDtypeStruct((nl, r, no), F32),
        compiler_params=_params(("parallel", "parallel")),
        name="ada_mod",
    )(c_all, w, b.reshape(nl, 1, no))


def _norm_mm_kernel(*refs, mode, tn, perm):
    if perm:
        pm_ref, refs = refs[0], refs[1:]
        h0_ref, refs = refs[-1], refs[:-1]
    if mode == "swiglu":
        x_ref, g_ref, sh_ref, sc_ref, wg_ref, wu_ref, o_ref, h_ref = refs
    elif mode == "rope":
        x_ref, g_ref, sh_ref, sc_ref, w_ref, c_ref, s_ref, rot_ref, o_ref, h_ref = refs
    else:
        x_ref, g_ref, sh_ref, sc_ref, w_ref, b_ref, o_ref, h_ref = refs
    j = pl.program_id(2)

    @pl.when(j == 0)
    def _():
        h = _modnorm(x_ref[...].reshape(h_ref.shape), g_ref[...], sh_ref[...], sc_ref[...]).astype(BF16)
        if perm:
            h0_ref[...] = h
            _transpose_rows(h_ref, h0_ref, pm_ref, *perm)
        else:
            h_ref[...] = h

    h = h_ref[...]
    if mode == "swiglu":
        a = _dot(h, wg_ref[...])
        u = _dot(h, wu_ref[...])
        o_ref[...] = (a * jax.nn.sigmoid(a) * u).astype(o_ref.dtype).reshape(o_ref.shape)
    elif mode == "rope":
        acc = _dot(h, w_ref[...])
        part = (j // (GROUP_WIDTH // tn)) % 3

        @pl.when(part == 2)
        def _():
            o_ref[...] = acc.astype(o_ref.dtype).reshape(o_ref.shape)

        @pl.when(part != 2)
        def _():
            reps = tn // HEAD_DIM
            tabs = [t[...].reshape(acc.shape[0], HEAD_DIM) for t in (c_ref, s_ref)]
            c, s = [jnp.concatenate([t] * reps, axis=1) for t in tabs]
            accb = acc.astype(BF16)
            rw = rot_ref.shape[0]
            partner = jnp.concatenate([_dot(accb[:, k * rw:(k + 1) * rw], rot_ref[...]) for k in range(tn // rw)],
                                      axis=1)
            qs = jnp.where(part == 0, HEAD_DIM ** -0.5, 1.0).astype(F32)
            o_ref[...] = ((acc * c + partner * s) * qs).astype(o_ref.dtype).reshape(o_ref.shape)
    else:
        o_ref[...] = (_dot(h, w_ref[...]) + b_ref[...]).astype(o_ref.dtype).reshape(o_ref.shape)


def _norm_mm(x, x_spec, perm, rows, grid_rows, g, shift, scale, ws, w_layer, col0, nout, out_shape,
             out_spec, *, mode, bias=None, rope=None, tn=512):
    d = x.shape[-1]
    bsz = x.shape[0]
    cb = col0 // tn
    vec = pl.BlockSpec((None, 1, d), lambda b, i, j: (b, 0, 0))
    in_specs = [x_spec, pl.BlockSpec((1, d), lambda b, i, j: (0, 0)), vec, vec]
    in_specs += [pl.BlockSpec((None, d, tn), lambda b, i, j: (w_layer, 0, cb + j)) for _ in ws]
    args = [x, g.reshape(1, d), shift, scale, *ws]
    scratch = [pltpu.VMEM((rows, d), BF16)]
    if perm:
        in_specs.insert(0, pl.BlockSpec((PERM_ROWS, PERM_ROWS), lambda b, i, j: (0, 0)))
        args.insert(0, _perm_for(*perm))
        scratch.append(pltpu.VMEM((rows, d), BF16))
    if mode == "rope":
        tabs, tab_spec = rope
        rw = 4 * HEAD_DIM
        in_specs += [tab_spec] * 2 + [pl.BlockSpec((rw, rw), lambda b, i, j: (0, 0))]
        args += list(tabs) + [_rope_partner_matrix(rw)]
    elif mode == "bias":
        in_specs.append(pl.BlockSpec((1, tn), lambda b, i, j: (0, cb + j)))
        args.append(bias)
    return pl.pallas_call(
        functools.partial(_norm_mm_kernel, mode=mode, tn=tn, perm=perm),
        grid=(bsz, grid_rows, nout // tn),
        in_specs=in_specs,
        out_specs=out_spec,
        out_shape=out_shape,
        scratch_shapes=scratch,
        compiler_params=_params(("parallel", "parallel", "arbitrary")),
        name="norm_mm_" + mode,
    )(*args)


def _mm_res_kernel(*refs, a_mode, perm):
    if a_mode == "merge":
        (pm_ref, o0, o1, o2, l0, l1, l2, e_ref, w_ref, b_ref, x_ref, gt_ref, out_ref, a_ref, a0_ref) = refs
    elif a_mode == "perm":
        pm_ref, a_in, w_ref, b_ref, x_ref, gt_ref, out_ref, a_ref, a0_ref = refs
    else:
        a_in, w_ref, b_ref, x_ref, gt_ref, out_ref = refs
    j = pl.program_id(2)

    if a_mode != "plain":
        @pl.when(j == 0)
        def _():
            k = a_ref.shape[1]
            if a_mode == "merge":
                ls = [l[...].reshape(-1, LANES) for l in (l0, l1, l2)]
                mx = jnp.maximum(jnp.maximum(ls[0], ls[1]), ls[2])
                ws = [jnp.exp(l - mx) for l in ls]
                inv = 1.0 / (ws[0] + ws[1] + ws[2])
                num = jnp.zeros((ls[0].shape[0], k), F32)
                for w, o in zip(ws, (o0, o1, o2)):
                    hi, lo = _split(w * inv)
                    wide = _dot(hi, e_ref[...]) + _dot(lo, e_ref[...])
                    num = num + wide * o[...].reshape(-1, k).astype(F32)
                a0_ref[...] = num.astype(BF16)
            else:
                a0_ref[...] = a_in[...].reshape(-1, k).astype(BF16)
            _transpose_rows(a_ref, a0_ref, pm_ref, *perm)

        a = a_ref[...]
    else:
        a = a_in[...]
    out_ref[...] = x_ref[...] + gt_ref[...] * (_dot(a, w_ref[...]) + b_ref[...])


def _head_spread_matrix(width):
    m = np.zeros((LANES, width), np.float32)
    for hd in range(width // HEAD_DIM):
        m[hd * LSE_LANES, hd * HEAD_DIM:(hd + 1) * HEAD_DIM] = 1.0
    return jnp.asarray(m, BF16)


def _mm_res(a_list, a_specs, w, w_layer, bias, x, gate, rows, *, a_mode="plain", perm=None, tn=512):
    bsz, seq_len, d = x.shape
    k = w.shape[1]
    blk = pl.BlockSpec((None, rows, tn), lambda b, i, j: (b, i, j))
    if a_mode == "merge":
        a_list = list(a_list) + [_head_spread_matrix(k)]
        a_specs = list(a_specs) + [pl.BlockSpec((LANES, k), lambda b, i, j: (0, 0))]
    in_specs = list(a_specs) + [
        pl.BlockSpec((None, k, tn), lambda b, i, j: (w_layer, 0, j)),
        pl.BlockSpec((1, tn), lambda b, i, j: (0, j)),
        blk,
        pl.BlockSpec((None, 1, tn), lambda b, i, j: (b, 0, j)),
    ]
    args = [*a_list, w, bias, x, gate]
    scratch = []
    if a_mode != "plain":
        in_specs.insert(0, pl.BlockSpec((PERM_ROWS, PERM_ROWS), lambda b, i, j: (0, 0)))
        args.insert(0, _perm_for(*perm))
        scratch = [pltpu.VMEM((rows, k), BF16)] * 2
    return pl.pallas_call(
        functools.partial(_mm_res_kernel, a_mode=a_mode, perm=perm),
        grid=(bsz, seq_len // rows, d // tn),
        in_specs=in_specs,
        out_specs=blk,
        out_shape=jax.ShapeDtypeStruct(x.shape, F32),
        scratch_shapes=scratch,
        compiler_params=_params(("parallel", "parallel", "arbitrary")),
        name="mm_res_" + a_mode,
    )(*args)


def _final_kernel(x_ref, g_ref, sh_ref, sc_ref, o_ref):
    o_ref[...] = _modnorm(x_ref[...], g_ref[...], sh_ref[...], sc_ref[...])


def _final(x, g, shift, scale, tm=512):
    bsz, seq_len, d = x.shape
    vec = pl.BlockSpec((None, 1, d), lambda b, i: (b, 0, 0))
    blk = pl.BlockSpec((None, tm, d), lambda b, i: (b, i, 0))
    return pl.pallas_call(
        _final_kernel,
        grid=(bsz, seq_len // tm),
        in_specs=[blk, pl.BlockSpec((1, d), lambda b, i: (0, 0)), vec, vec],
        out_specs=blk,
        out_shape=jax.ShapeDtypeStruct(x.shape, F32),
        compiler_params=_params(("parallel", "parallel")),
        name="final_norm",
    )(x, g.reshape(1, d), shift, scale)


def _filter_positions(seq_len, n1):
    n = 2 * seq_len
    h = SLAB_ROWS
    half = jnp.arange(2)[:, None, None]
    s = jnp.arange(n1)[None, :, None]
    r = jnp.arange(h)[None, None, :]
    idx = ((half * h + r) * n1 + s).reshape(n)
    pos = jnp.where(idx < seq_len, idx, n - idx).astype(F32)
    sign = jnp.where(idx < seq_len, 1.0, jnp.where(idx == seq_len, 0.0, -1.0)).astype(F32)
    t = pos / max(seq_len - 1, 1)
    bands = jnp.linspace(1e-4, FILTER_BANDS - 1, FILTER_BANDS, dtype=F32)
    ang = 2.0 * math.pi * pos[:, None] * bands[None, :] / seq_len
    z = jnp.concatenate([t[:, None], jnp.cos(ang), -jnp.sin(ang)], axis=-1)
    z = jnp.pad(z, ((0, 0), (0, LANES - FILTER_EMB - 1)))
    return jnp.concatenate([z, sign[:, None]], axis=-1)


def _filter_mlp_kernel(z_ref, w1_ref, b1_ref, f_ref, w2_ref, b2_ref, o_ref):
    f = f_ref[...]
    a = jnp.sin(f[0:1, :] * (_dot3(z_ref[...], w1_ref[...]) + b1_ref[...]))
    o_ref[...] = jnp.sin(f[1:2, :] * (_dot3(a, w2_ref[...]) + b2_ref[...]))


def _filter_mlp(zf, w1, b1, freq, w2, b2, tr=512):
    n = zf.shape[0]
    hid = w1.shape[1]
    w1p = jnp.pad(w1, ((0, LANES - w1.shape[0]), (0, 0)))
    full = lambda shape: pl.BlockSpec(shape, lambda i: (0,) * len(shape))
    return pl.pallas_call(
        _filter_mlp_kernel,
        grid=(n // tr,),
        in_specs=[pl.BlockSpec((tr, LANES), lambda i: (i, 0)), full((LANES, hid)), full((1, hid)),
                  full((2, hid)), full((hid, hid)), full((1, hid))],
        out_specs=pl.BlockSpec((tr, hid), lambda i: (i, 0)),
        out_shape=jax.ShapeDtypeStruct((n, hid), F32),
        compiler_params=_params(("parallel",)),
        name="filter_mlp",
    )(zf, w1p, b1.reshape(1, hid), freq, w2, b2.reshape(1, hid))


def _dft_consts(n1, n2):
    n = n1 * n2
    k2 = np.arange(n2 // 2)[:, None]
    nn2 = np.arange(n2)[None, :]
    ph = 2.0 * np.pi * nn2 * (k2 + 0.5) / n2
    f_s1 = np.concatenate([np.cos(ph), -np.sin(ph)], axis=0)
    m = np.arange(n2 // 2)[:, None]
    kk = np.arange(n2 // 2)[None, :]
    ph3 = 2.0 * np.pi * m * (kk + 0.5) / n2
    f_s3 = (2.0 / n) * np.concatenate([np.cos(ph3), -np.sin(ph3)], axis=1)
    a = 2.0 * np.pi * np.outer(np.arange(n1), np.arange(n1)) / n1
    c, s = np.cos(a), -np.sin(a)
    f_fwd = np.block([[c, -s], [s, c]])
    f_inv = np.block([[c, s], [-s, c]])
    th = 2.0 * np.pi * (np.arange(n2 // 2)[:, None] + 0.5) * np.arange(n1)[None, :] / n
    as_bf = lambda x: jnp.asarray(x, F32).astype(BF16)
    tc, ts = jnp.asarray(np.cos(th), F32), jnp.asarray(np.sin(th), F32)
    ff, fi = jnp.asarray(f_fwd, F32), jnp.asarray(f_inv, F32)
    fl, fr = ff[None, :, :n1], ff[None, :, n1:]
    g_fwd = jnp.concatenate([fl * tc[:, None, :] - fr * ts[:, None, :],
                             fl * ts[:, None, :] + fr * tc[:, None, :]], axis=2).astype(BF16)
    it, ib = fi[None, :n1, :], fi[None, n1:, :]
    g_inv = jnp.concatenate([tc[:, :, None] * it - ts[:, :, None] * ib,
                             ts[:, :, None] * it + tc[:, :, None] * ib], axis=1).astype(BF16)
    return dict(f_s1=as_bf(f_s1[:, :n2 // 2]), f_s1_hi=as_bf(f_s1[:, n2 // 2:]), f_s3=as_bf(f_s3),
                g_fwd=g_fwd, g_inv=g_inv, n1=n1)


def _slab_tile(n1):
    return min(n1, 8)


HI16 = -65536
HALF_ULP16 = 0x8000


def _pack_c(re, im):
    rb = lax.bitcast_convert_type(re, jnp.int32) + HALF_ULP16
    ib = lax.bitcast_convert_type(im, jnp.int32) + HALF_ULP16
    return (rb & HI16) | lax.shift_right_logical(ib, 16)


def _pack_bf16(re, im):
    rb = lax.bitcast_convert_type(re.astype(F32), jnp.int32)
    ib = lax.bitcast_convert_type(im.astype(F32), jnp.int32)
    return rb | lax.shift_right_logical(ib, 16)


def _unpack_c(p):
    re = lax.bitcast_convert_type(p & HI16, F32)
    im = lax.bitcast_convert_type(lax.shift_left(p, 16), F32)
    return re, im


def _shift_rows(x, down):
    rows = x.shape[0]
    row = lax.broadcasted_iota(jnp.int32, x.shape, 0)
    if down:
        return jnp.where(row == 0, 0.0, pltpu.roll(x, 1, 0))
    return jnp.where(row == rows - 1, 0.0, pltpu.roll(x, rows - 1, 0))


def _short_conv_slabs(main_ref, prev_ref, next_ref, w_ref, b_ref, first, last):
    n_slabs = main_ref.shape[0]
    prev = prev_ref[0].astype(F32)
    prev = jnp.where(first, _shift_rows(prev, True), prev)
    nxt = next_ref[0].astype(F32)
    nxt = jnp.where(last, _shift_rows(nxt, False), nxt)
    w = w_ref[...]
    out = []
    for s in range(n_slabs):
        up = prev if s == 0 else main_ref[s - 1].astype(F32)
        dn = nxt if s == n_slabs - 1 else main_ref[s + 1].astype(F32)
        out.append(up * w[0:1, :] + main_ref[s].astype(F32) * w[1:2, :] + dn * w[2:3, :] + b_ref[...])
    return out


def _fft_s1_kernel(*refs, short_conv):
    re_ref, im_ref, ret_ref, imt_ref = refs[-4:]
    if short_conv:
        pm_ref, f_ref, m_ref, p_ref, n_ref, w_ref, b_ref, o_ref, u_ref = refs[:-4]
        t = pl.program_id(1)
        slabs = _short_conv_slabs(m_ref, p_ref, n_ref, w_ref, b_ref, t == 0, t == pl.num_programs(1) - 1)
    else:
        pm_ref, f_ref, m_ref, o_ref = refs[:-4]
        slabs = [m_ref[s] for s in range(m_ref.shape[0])]
    half = o_ref.shape[0]
    for s, u in enumerate(slabs):
        if short_conv:
            u_ref[s] = u.astype(u_ref.dtype)
        r = _dot(f_ref[...], u.astype(BF16))
        re_ref[s * half:(s + 1) * half, :] = r[:half].astype(BF16)
        im_ref[s * half:(s + 1) * half, :] = r[half:].astype(BF16)
    _transpose_rows(ret_ref, re_ref, pm_ref, len(slabs), half)
    _transpose_rows(imt_ref, im_ref, pm_ref, len(slabs), half)
    o_ref[...] = _pack_bf16(ret_ref[...], imt_ref[...]).reshape(o_ref.shape)


def _fft_s1(consts, src, col_block, conv=None, ct=512):
    bsz, n1, h, c = src.shape
    d = consts["d"]
    st = _slab_tile(n1)
    cpb = d // ct
    f = consts["f_s1"]
    main = pl.BlockSpec((None, st, h, ct), lambda b, t, j: (b, t, 0, col_block * cpb + j))
    a_spec = pl.BlockSpec((None, h, st, ct), lambda b, t, j: (b, 0, t, j))
    a_shape = jax.ShapeDtypeStruct((bsz, h, n1, d), jnp.int32)
    fspec = pl.BlockSpec(f.shape, lambda b, t, j: (0, 0))
    pspec = pl.BlockSpec((PERM_ROWS, PERM_ROWS), lambda b, t, j: (0, 0))
    pm = _perm_for(st, h)
    scratch = [pltpu.VMEM((st * h, ct), BF16)] * 4
    if conv is None:
        return pl.pallas_call(
            functools.partial(_fft_s1_kernel, short_conv=False),
            grid=(bsz, n1 // st, cpb),
            in_specs=[pspec, fspec, main], out_specs=a_spec, out_shape=a_shape, scratch_shapes=scratch,
            compiler_params=_params(("parallel", "parallel", "parallel")), name="fft_s1",
        )(pm, f, src)
    w, b = conv
    prev = pl.BlockSpec((None, 1, h, ct), lambda b, t, j: (b, (t * st + n1 - 1) % n1, 0, col_block * cpb + j))
    nxt = pl.BlockSpec((None, 1, h, ct), lambda b, t, j: (b, ((t + 1) * st) % n1, 0, col_block * cpb + j))
    wspec = pl.BlockSpec((3, ct), lambda b, t, j: (0, col_block * cpb + j))
    bspec = pl.BlockSpec((1, ct), lambda b, t, j: (0, col_block * cpb + j))
    u_spec = pl.BlockSpec((None, st, h, ct), lambda b, t, j: (b, t, 0, j))
    return pl.pallas_call(
        functools.partial(_fft_s1_kernel, short_conv=True),
        grid=(bsz, n1 // st, cpb),
        in_specs=[pspec, fspec, main, prev, nxt, wspec, bspec],
        out_specs=[a_spec, u_spec],
        out_shape=[a_shape, jax.ShapeDtypeStruct((bsz, n1, h, d), BF16)],
        scratch_shapes=scratch,
        compiler_params=_params(("parallel", "parallel", "parallel")), name="fft_s1_conv",
    )(pm, f, src, src, src, w, b)


def _filter_s1_kernel(pm_ref, flo_ref, fhi_ref, alo_ref, ahi_ref, zlo_ref, zhi_ref, wf_ref, wb_ref, df_ref,
                      db_ref, o_ref, ss_ref, re_ref, im_ref, ret_ref, imt_ref):
    half = o_ref.shape[0]

    @pl.when(pl.program_id(2) == 0)
    def _():
        ss_ref[...] = jnp.zeros_like(ss_ref)

    n_slabs, rows = alo_ref.shape[0], alo_ref.shape[1]
    ct = o_ref.shape[2]
    ss = jnp.zeros(ss_ref.shape, F32)
    r = jnp.zeros((2 * half, n_slabs * ct), F32)
    for f_ref, a_ref, z_ref, w_ref, d_ref in ((flo_ref, alo_ref, zlo_ref, wf_ref, df_ref),
                                              (fhi_ref, ahi_ref, zhi_ref, wb_ref, db_ref)):
        z = z_ref[...].reshape(n_slabs * rows, LANES)
        t, sign = z[:, 0:1], z[:, LANES - 1:LANES]
        taps = _dot3(a_ref[...].reshape(n_slabs * rows, a_ref.shape[2]), w_ref[...])
        taps = taps * jnp.exp(-t * jnp.abs(d_ref[...])) * sign
        ss = ss + jnp.sum(taps * taps, axis=0, keepdims=True)
        tb = taps.astype(BF16)
        wide = jnp.concatenate([tb[s * rows:(s + 1) * rows, :] for s in range(n_slabs)], axis=1)
        r = r + _dot(f_ref[...], wide)
    for s in range(n_slabs):
        re_ref[s * half:(s + 1) * half, :] = r[:half, s * ct:(s + 1) * ct].astype(BF16)
        im_ref[s * half:(s + 1) * half, :] = r[half:, s * ct:(s + 1) * ct].astype(BF16)
    ss_ref[...] += ss
    _transpose_rows(ret_ref, re_ref, pm_ref, n_slabs, half)
    _transpose_rows(imt_ref, im_ref, pm_ref, n_slabs, half)
    o_ref[...] = _pack_bf16(ret_ref[...], imt_ref[...]).reshape(o_ref.shape)


def _filter_s1(consts, a2, zf, w3, decay, ct=256):
    n1, h = consts["n1"], SLAB_ROWS
    hid = a2.shape[1]
    d = decay.shape[-1]
    nct = d // ct
    st = _slab_tile(n1)
    flo, fhi = consts["f_s1"], consts["f_s1_hi"]
    dec = decay.reshape(1, HYENA_ORDER * N_DIRS * d)
    fspec = pl.BlockSpec(flo.shape, lambda o, j, t: (0, 0))
    rows = lambda half, width: pl.BlockSpec((None, st, h, width), lambda o, j, t: (half, t, 0, 0))
    wcol = lambda dirn, nrow: pl.BlockSpec((nrow, ct), lambda o, j, t: (0, (o * N_DIRS + dirn) * nct + j))
    return pl.pallas_call(
        _filter_s1_kernel,
        grid=(HYENA_ORDER, nct, n1 // st),
        in_specs=[pl.BlockSpec((PERM_ROWS, PERM_ROWS), lambda o, j, t: (0, 0)), fspec, fspec,
                  rows(0, hid), rows(1, hid), rows(0, LANES), rows(1, LANES),
                  wcol(0, hid), wcol(1, hid), wcol(0, 1), wcol(1, 1)],
        out_specs=[pl.BlockSpec((None, h, st, ct), lambda o, j, t: (o, 0, t, j)),
                   pl.BlockSpec((None, 1, ct), lambda o, j, t: (o, 0, j))],
        out_shape=[jax.ShapeDtypeStruct((HYENA_ORDER, h, n1, d), jnp.int32),
                   jax.ShapeDtypeStruct((HYENA_ORDER, 1, d), F32)],
        scratch_shapes=[pltpu.VMEM((st * h, ct), BF16)] * 4,
        compiler_params=_params(("parallel", "parallel", "arbitrary")), name="filter_s1",
    )(_perm_for(st, h), flo, fhi, a2.reshape(2, n1, h, hid), a2.reshape(2, n1, h, hid),
      zf.reshape(2, n1, h, LANES), zf.reshape(2, n1, h, LANES), w3, w3, dec, dec)


def _fft_s2f_kernel(a_ref, gf_ref, ss_ref, o_ref, *, kb):
    n1 = a_ref.shape[1]
    scale = lax.rsqrt(ss_ref[...] + EPS)

    def body(kk, carry):
        ar, ai = _unpack_c(a_ref[kk])
        x = _dot(gf_ref[kk], jnp.concatenate([ar, ai], axis=0).astype(BF16))
        o_ref[kk] = _pack_c(x[:n1] * scale, x[n1:] * scale)
        return carry

    lax.fori_loop(0, kb, body, 0, unroll=min(kb, 4))


def _fft_s2_kernel(a_ref, k_ref, gf_ref, gi_ref, o_ref, *, kb):
    nb, n1, ct = a_ref.shape[0], a_ref.shape[2], a_ref.shape[3]

    def body(kk, carry):
        parts = [_unpack_c(a_ref[b, kk]) for b in range(nb)]
        ar = jnp.concatenate([p[0] for p in parts], axis=1)
        ai = jnp.concatenate([p[1] for p in parts], axis=1)
        x = _dot(gf_ref[kk], jnp.concatenate([ar, ai], axis=0).astype(BF16))
        xr, xi = x[:n1], x[n1:]
        kr, ki = [jnp.concatenate([v] * nb, axis=1) for v in _unpack_c(k_ref[kk])]
        zr = xr * kr - xi * ki
        zi = xr * ki + xi * kr
        y = _dot(gi_ref[kk], jnp.concatenate([zr, zi], axis=0).astype(BF16))
        packed = _pack_c(y[:n1], y[n1:])
        for b in range(nb):
            o_ref[b, kk] = packed[:, b * ct:(b + 1) * ct]
        return carry

    lax.fori_loop(0, kb, body, 0, unroll=min(kb, 8))


def _s2_tiles(n1, d):
    kb = max(1, 1024 // n1)
    ct = min(d, 512)
    return kb, ct


def _fft_s2f(a, sumsq, consts):
    n_o, k2n, n1, d = a.shape
    kb, ct = _s2_tiles(n1, d)
    blk = pl.BlockSpec((None, kb, n1, ct), lambda k, j, o: (o, k, 0, j))
    return pl.pallas_call(
        functools.partial(_fft_s2f_kernel, kb=kb),
        grid=(k2n // kb, d // ct, n_o),
        in_specs=[blk, pl.BlockSpec((kb, 2 * n1, 2 * n1), lambda k, j, o: (k, 0, 0)),
                  pl.BlockSpec((None, 1, ct), lambda k, j, o: (o, 0, j))],
        out_specs=blk,
        out_shape=jax.ShapeDtypeStruct(a.shape, jnp.int32),
        compiler_params=_params(("parallel", "parallel", "parallel")),
        name="fft_s2_filter",
    )(a, consts["g_fwd"], sumsq)


def _fft_s2(a, kf, order, consts):
    bsz, k2n, n1, d = a.shape
    kb, ct = _s2_tiles(n1, d)
    nb = bsz if n1 * bsz <= SLAB_ROWS else 1
    blk = pl.BlockSpec((nb, kb, n1, ct), lambda k, j, b: (b, k, 0, j))
    mat = pl.BlockSpec((kb, 2 * n1, 2 * n1), lambda k, j, b: (k, 0, 0))
    return pl.pallas_call(
        functools.partial(_fft_s2_kernel, kb=kb),
        grid=(k2n // kb, d // ct, bsz // nb),
        in_specs=[blk, pl.BlockSpec((None, kb, n1, ct), lambda k, j, b: (order, k, 0, j)), mat, mat],
        out_specs=blk,
        out_shape=jax.ShapeDtypeStruct(a.shape, jnp.int32),
        compiler_params=_params(("parallel", "parallel", "parallel")),
        name="fft_s2",
    )(a, kf, consts["g_fwd"], consts["g_inv"])


def _fft_s3_kernel(pm_ref, f_ref, t_ref, u_ref, gm_ref, gp_ref, gn_ref, w_ref, b_ref, sk_ref, o_ref,
                   re_ref, im_ref, ret_ref, imt_ref):
    t_id = pl.program_id(1)
    half, st = t_ref.shape[0], t_ref.shape[1]
    re, im = _unpack_c(t_ref[...].reshape(half * st, t_ref.shape[2]))
    re_ref[...] = re.astype(BF16)
    im_ref[...] = im.astype(BF16)
    _transpose_rows(ret_ref, re_ref, pm_ref, half, st)
    _transpose_rows(imt_ref, im_ref, pm_ref, half, st)
    gates = _short_conv_slabs(gm_ref, gp_ref, gn_ref, w_ref, b_ref, t_id == 0, t_id == pl.num_programs(1) - 1)
    for s, gate in enumerate(gates):
        t = jnp.concatenate([ret_ref[s * half:(s + 1) * half, :], imt_ref[s * half:(s + 1) * half, :]], axis=0)
        y = _dot(f_ref[...], t)
        o_ref[s] = (gate * (y + u_ref[s].astype(F32) * sk_ref[...])).astype(o_ref.dtype)


def _fft_s3(consts, t, u, z, gate_block, conv_w, conv_b, skip, order, out_dtype, ct=512):
    bsz, h, n1, d = t.shape
    st = _slab_tile(n1)
    cpb = d // ct
    f = consts["f_s3"]
    gcol = lambda j: gate_block * cpb + j
    slab = lambda idx: pl.BlockSpec((None, 1, h, ct), lambda b, tt, j: (b, idx(tt), 0, gcol(j)))
    return pl.pallas_call(
        _fft_s3_kernel,
        grid=(bsz, n1 // st, cpb),
        scratch_shapes=[pltpu.VMEM((st * h, ct), BF16)] * 4,
        in_specs=[
            pl.BlockSpec((PERM_ROWS, PERM_ROWS), lambda b, tt, j: (0, 0)),
            pl.BlockSpec(f.shape, lambda b, tt, j: (0, 0)),
            pl.BlockSpec((None, h, st, ct), lambda b, tt, j: (b, 0, tt, j)),
            pl.BlockSpec((None, st, h, ct), lambda b, tt, j: (b, tt, 0, j)),
            pl.BlockSpec((None, st, h, ct), lambda b, tt, j: (b, tt, 0, gcol(j))),
            slab(lambda tt: (tt * st + n1 - 1) % n1),
            slab(lambda tt: ((tt + 1) * st) % n1),
            pl.BlockSpec((3, ct), lambda b, tt, j: (0, gcol(j))),
            pl.BlockSpec((1, ct), lambda b, tt, j: (0, gcol(j))),
            pl.BlockSpec((None, 1, ct), lambda b, tt, j: (order, 0, j)),
        ],
        out_specs=pl.BlockSpec((None, st, h, ct), lambda b, tt, j: (b, tt, 0, j)),
        out_shape=jax.ShapeDtypeStruct((bsz, n1, h, d), out_dtype),
        compiler_params=_params(("parallel", "parallel", "parallel")),
        name="fft_s3",
    )(_perm_for(h, st), f, t, u, z, z, z, conv_w, conv_b, skip)


def _hyena_filter_spectra(seq_len, consts, fw1, fb1, ffreq, fw2, fb2, fw3, decay):
    d = decay.shape[-1]
    n1 = consts["n1"]
    zf = _filter_positions(seq_len, n1)
    a2 = _filter_mlp(zf, fw1, fb1, ffreq, fw2, fb2)
    a, sumsq = _filter_s1(consts, a2, zf, fw3, decay)
    return _fft_s2f(a, sumsq, consts)


def _hyena_mixer(x, g, shift, scale, gate, p, layer, kf, consts):
    bsz, seq_len, d = x.shape
    n1 = consts["n1"]
    h = SLAB_ROWS
    st = _slab_tile(n1)
    xv = x.reshape(bsz, h, n1, d)
    z = _norm_mm(
        xv, pl.BlockSpec((None, h, st, d), lambda b, i, j: (b, 0, i, 0)), (h, st), st * h, n1 // st, g, shift, scale,
        [p["hy_w_in"]], layer, 0, 3 * d, jax.ShapeDtypeStruct((bsz, n1, h, 3 * d), BF16),
        pl.BlockSpec((None, st, h, COL_TILE), lambda b, i, j: (b, i, 0, j)),
        mode="bias", bias=p["hy_b_in"][layer].reshape(1, 3 * d), tn=COL_TILE)
    cw, cb = p["hy_conv_w"][layer], p["hy_conv_b"][layer].reshape(1, 3 * d)
    skip = p["hy_skip"][layer].reshape(HYENA_ORDER, 1, d)
    a, u = _fft_s1(consts, z, 2, conv=(cw, cb))
    t = _fft_s2(a, kf, 0, consts)
    y1 = _fft_s3(consts, t, u, z, 0, cw, cb, skip, 0, BF16)
    a = _fft_s1(consts, y1, 0)
    t = _fft_s2(a, kf, 1, consts)
    y2 = _fft_s3(consts, t, y1, z, 1, cw, cb, skip, 1, F32)
    q = ROW_TILE // n1
    return _mm_res(
        [y2], [pl.BlockSpec((None, n1, q, d), lambda b, i, j: (b, 0, i, 0))], p["hy_w_out"], layer,
        p["hy_b_out"][layer].reshape(1, d), x, gate, ROW_TILE, a_mode="perm", perm=(n1, q), tn=COL_TILE)


def _attn_kernel(*refs, phases, ta, n_sub, to_classes):
    if to_classes:
        (pm_ref, q_ref, kp_ref, km_ref, kn_ref, vp_ref, vm_ref, vn_ref, o_ref, l_ref, kx_ref, vx_ref,
         on_ref, ot_ref, ln_ref, ls_ref, lt_ref) = refs
    else:
        q_ref, kp_ref, km_ref, kn_ref, vp_ref, vm_ref, vn_ref, o_ref, l_ref, kx_ref, vx_ref = refs
    i = pl.program_id(2)
    halo = ATT_BAND // phases
    qa = ATT_Q // phases
    ka = 2 * qa
    kx_ref[:, 0:halo] = kp_ref[...]
    kx_ref[:, halo:halo + ta] = km_ref[...]
    kx_ref[:, halo + ta:] = kn_ref[...]
    vx_ref[:, 0:halo] = vp_ref[...]
    vx_ref[:, halo:halo + ta] = vm_ref[...]
    vx_ref[:, halo + ta:] = vn_ref[...]
    row = lax.broadcasted_iota(jnp.int32, (ATT_Q, 2 * ATT_Q), 0)
    col = lax.broadcasted_iota(jnp.int32, (ATT_Q, 2 * ATT_Q), 1)
    cq, aq = row >> (qa.bit_length() - 1), row & (qa - 1)
    ck, ak = col >> (ka.bit_length() - 1), col & (ka - 1)
    delta = phases * (ak - aq) - ATT_BAND + ck - cq
    band = (delta >= -ATT_BAND) & (delta <= ATT_BAND)
    lane_head = lax.broadcasted_iota(jnp.int32, (ATT_Q, LANES), 1) >> (LSE_LANES.bit_length() - 1)
    for s in range(ta // qa):
        key_idx = phases * (i * ta + s * qa - halo + ak) + ck
        valid = band & (key_idx >= 0) & (key_idx < n_sub)
        lse_tile = jnp.zeros((ATT_Q, LANES), F32)
        for h in range(HEADS_PER_GROUP):
            cs = slice(h * HEAD_DIM, (h + 1) * HEAD_DIM)
            q = jnp.concatenate([q_ref[c, s * qa:(s + 1) * qa, cs] for c in range(phases)], axis=0)
            k = jnp.concatenate([kx_ref[c, s * qa:s * qa + ka, cs] for c in range(phases)], axis=0)
            v = jnp.concatenate([vx_ref[c, s * qa:s * qa + ka, cs] for c in range(phases)], axis=0)
            sc = lax.dot_general(q, k, (((1,), (1,)), ((), ())), preferred_element_type=F32)
            sc = jnp.where(valid, sc, NEG_BIG)
            m = jnp.max(sc, axis=-1, keepdims=True)
            pr = jnp.exp(sc - m)
            den = jnp.sum(pr, axis=-1, keepdims=True)
            o = (_dot(pr.astype(BF16), v) / den).astype(BF16)
            lse_tile = jnp.where(lane_head == h, m + jnp.log(den), lse_tile)
            if to_classes:
                on_ref[s * ATT_Q:(s + 1) * ATT_Q, cs] = o
            else:
                for c in range(phases):
                    o_ref[c, s * qa:(s + 1) * qa, cs] = o[c * qa:(c + 1) * qa]
        if to_classes:
            ln_ref[s * ATT_Q:(s + 1) * ATT_Q, :] = lse_tile
        else:
            for c in range(phases):
                l_ref[c, s * qa:(s + 1) * qa, :] = lse_tile[c * qa:(c + 1) * qa]
    if to_classes:
        na = ta // CLASSES
        _transpose_rows(ot_ref, on_ref, pm_ref, na, CLASSES)
        o_ref[...] = ot_ref[...].reshape(o_ref.shape)
        rest = ln_ref[...]
        total = jnp.zeros(rest.shape, F32)
        for _ in range(3):
            piece = rest.astype(BF16)
            rest = rest - piece.astype(F32)
            ls_ref[...] = piece
            _transpose_rows(lt_ref, ls_ref, pm_ref, na, CLASSES)
            total = total + lt_ref[...].astype(F32)
        l_ref[...] = total.reshape(l_ref.shape)


def _attn_group(qkv, col0, phases, n_sub, lead_grid, lead_block, lead_index, ta, *, to_classes=False,
                out_arr_shape=None, out_block=None, out_index=None):
    gw = GROUP_WIDTH
    halo = ATT_BAND // phases
    rows = qkv.shape[-2]
    per = ta // halo
    nblk = rows // halo
    cb = col0 // gw

    def spec(nrows, ridx, part):
        return pl.BlockSpec(tuple(lead_block) + (nrows, gw),
                            lambda b, rho, i: tuple(lead_index(b, rho)) + (ridx(i), cb + part))

    main = lambda part: spec(ta, lambda i: i, part)
    prev = lambda part: spec(halo, lambda i: jnp.maximum(i * per - 1, 0), part)
    nxt = lambda part: spec(halo, lambda i: jnp.minimum((i + 1) * per, nblk - 1), part)
    bsz = qkv.shape[0]
    in_specs = [main(0), prev(1), main(1), nxt(1), prev(2), main(2), nxt(2)]
    args = [qkv] * 7
    if to_classes:
        assert ta == PERM_ROWS
        out_specs = [pl.BlockSpec(out_block + (gw,), out_index), pl.BlockSpec(out_block + (LANES,), out_index)]
        scratch_extra = [pltpu.VMEM((ta, gw), BF16)] * 2 + [pltpu.VMEM((ta, LANES), F32)] + \
                        [pltpu.VMEM((ta, LANES), BF16)] * 2
        in_specs.insert(0, pl.BlockSpec((PERM_ROWS, PERM_ROWS), lambda b, rho, i: (0, 0)))
        args.insert(0, _perm_for(ta // CLASSES, CLASSES))
    else:
        out_arr_shape = qkv.shape[:-1]
        oidx = lambda b, rho, i: tuple(lead_index(b, rho)) + (i, 0)
        out_specs = [pl.BlockSpec(tuple(lead_block) + (ta, gw), oidx),
                     pl.BlockSpec(tuple(lead_block) + (ta, LANES), oidx)]
        scratch_extra = []
    kx_shape = (phases, ta + 2 * halo, gw)
    return pl.pallas_call(
        functools.partial(_attn_kernel, phases=phases, ta=ta, n_sub=n_sub, to_classes=to_classes),
        grid=(bsz, lead_grid, rows // ta),
        in_specs=in_specs,
        out_specs=out_specs,
        out_shape=[jax.ShapeDtypeStruct(tuple(out_arr_shape) + (gw,), BF16),
                   jax.ShapeDtypeStruct(tuple(out_arr_shape) + (LANES,), F32)],
        scratch_shapes=[pltpu.VMEM(kx_shape, BF16)] * 2 + scratch_extra,
        compiler_params=_params(("parallel", "parallel", "parallel")),
        name="attn_p%d" % phases + ("_cls" if to_classes else ""),
    )(*args)


def _rope_tables(pos):
    half = ROT_DIM // 2
    inv = ROPE_THETA ** (-jnp.arange(0, ROT_DIM, 2, dtype=F32) / ROT_DIM)
    ang = pos.astype(F32)[:, None] * inv[None, :]
    cos, sin = jnp.cos(ang), jnp.sin(ang)
    n = pos.shape[0]
    rest = HEAD_DIM - ROT_DIM
    c = jnp.concatenate([cos, cos, jnp.ones((n, rest), F32)], axis=1)
    s = jnp.concatenate([sin, sin, jnp.zeros((n, rest), F32)], axis=1)
    return c, s


def _rope_partner_matrix(width):
    half = ROT_DIM // 2
    m = np.zeros((width, width), np.float32)
    for base in range(0, width, HEAD_DIM):
        for k in range(half):
            m[base + k + half, base + k] = -1.0
            m[base + k, base + k + half] = 1.0
    return jnp.asarray(m, BF16)


def _attn_mixer(x, g, shift, scale, gate, p, layer):
    bsz, seq_len, d = x.shape
    gw = GROUP_WIDTH
    nc = seq_len // CLASSES
    ca = ROW_TILE // CLASSES
    w_in = p["at_w_in"]
    tabs = _rope_tables(jnp.arange(seq_len))
    qkv0 = _norm_mm(
        x, pl.BlockSpec((None, ROW_TILE, d), lambda b, i, j: (b, i, 0)), None, ROW_TILE, seq_len // ROW_TILE,
        g, shift, scale, [w_in], layer, 0, 3 * gw, jax.ShapeDtypeStruct((bsz, seq_len, 3 * gw), BF16),
        pl.BlockSpec((None, ROW_TILE, COL_TILE), lambda b, i, j: (b, i, j)), mode="rope",
        rope=(tabs, pl.BlockSpec((ROW_TILE, HEAD_DIM), lambda b, i, j: (i, 0))), tn=COL_TILE)
    pos_c = (jnp.arange(nc)[None, :] * CLASSES + jnp.arange(CLASSES)[:, None]).reshape(-1)
    tabs_c = [t.reshape(CLASSES, nc, HEAD_DIM) for t in _rope_tables(pos_c)]
    qkv12 = _norm_mm(
        x.reshape(bsz, nc, CLASSES, d), pl.BlockSpec((None, ca, CLASSES, d), lambda b, i, j: (b, i, 0, 0)),
        (ca, CLASSES), ROW_TILE, nc // ca, g, shift, scale, [w_in], layer, 3 * gw, 6 * gw,
        jax.ShapeDtypeStruct((bsz, CLASSES, nc, 6 * gw), BF16),
        pl.BlockSpec((None, CLASSES, ca, COL_TILE), lambda b, i, j: (b, 0, i, j)), mode="rope",
        rope=(tabs_c, pl.BlockSpec((CLASSES, ca, HEAD_DIM), lambda b, i, j: (0, i, 0))), tn=COL_TILE)
    cls_shape = (bsz, CLASSES, nc)
    ta0 = PERM_ROWS
    o0, l0 = _attn_group(
        qkv0.reshape(bsz, 1, seq_len, 3 * gw), 0, 1, seq_len, 1, (None, 1), lambda b, rho: (b, 0), ta0,
        to_classes=True, out_arr_shape=cls_shape, out_block=(None, CLASSES, ta0 // CLASSES),
        out_index=lambda b, rho, i: (b, 0, i, 0))
    dil1 = ATTN_PATTERNS[1][1]
    ph = CLASSES // dil1
    ta1 = min(64, nc)
    o1, l1 = _attn_group(
        qkv12.reshape(bsz, ph, dil1, nc, 6 * gw), 0, ph, seq_len // dil1, dil1, (None, ph, None),
        lambda b, rho: (b, 0, rho), ta1)
    ta2 = min(256, nc)
    o2, l2 = _attn_group(
        qkv12.reshape(bsz, CLASSES, 1, nc, 6 * gw), 3 * gw, 1, nc, CLASSES, (None, None, 1),
        lambda b, rho: (b, rho, 0), ta2)
    os_ = [o.reshape(cls_shape + (gw,)) for o in (o0, o1, o2)]
    ls = [l.reshape(cls_shape + (LANES,)) for l in (l0, l1, l2)]
    blk = lambda width: pl.BlockSpec((None, CLASSES, ca, width), lambda b, i, j: (b, 0, i, 0))
    return _mm_res(os_ + ls, [blk(gw)] * 3 + [blk(LANES)] * 3, p["at_w_out"], layer, jnp.zeros((1, d), F32),
                   x, gate, ROW_TILE, a_mode="merge", perm=(CLASSES, ca), tn=COL_TILE)


def _ffn(x, g, shift, scale, gate, p, layer):
    bsz, seq_len, d = x.shape
    dff = p["ffn_w_gate"].shape[-1]
    tiles = seq_len // ROW_TILE
    xs = pl.BlockSpec((None, ROW_TILE, d), lambda b, i, j: (b, i, 0))
    hs = pl.BlockSpec((None, ROW_TILE, 512), lambda b, i, j: (b, i, j))
    hmid = _norm_mm(x, xs, None, ROW_TILE, tiles, g, shift, scale, [p["ffn_w_gate"], p["ffn_w_up"]], layer, 0,
                    dff, jax.ShapeDtypeStruct((bsz, seq_len, dff), BF16), hs, mode="swiglu")
    return _mm_res([hmid], [pl.BlockSpec((None, ROW_TILE, dff), lambda b, i, j: (b, i, 0))],
                   p["ffn_w_down"], layer, jnp.zeros((1, d), F32), x, gate, ROW_TILE)


def _encoder(x, mods, final_mod, p):
    bsz, seq_len, d = x.shape
    n1 = 2 * seq_len // DFT_N2
    consts = _dft_consts(n1, DFT_N2)
    consts["d"] = d
    for i in range(DEPTH):
        sh_m, sc_m, g_m, sh_f, sc_f, g_f = [mods[i][:, None, k * d:(k + 1) * d] for k in range(6)]
        j = i // 2
        if i % 2 == 0:
            kf = _hyena_filter_spectra(seq_len, consts, p["hy_fw1"][j], p["hy_fb1"][j], p["hy_ffreq"][j],
                                       p["hy_fw2"][j], p["hy_fb2"][j], p["hy_fw3"][j], p["hy_decay"][j])
            x = _hyena_mixer(x, p["norm_mix"][i], sh_m, sc_m, g_m, p, j, kf, consts)
        else:
            x = _attn_mixer(x, p["norm_mix"][i], sh_m, sc_m, g_m, p, j)
        x = _ffn(x, p["norm_ffn"][i], sh_f, sc_f, g_f, p, i)
    sh, sc = final_mod[:, None, :d], final_mod[:, None, d:]
    return _final(x, p["final_norm"], sh, sc)


def kernel(x_prompt, x_sample, c_prompt, c_sample, ada_w, ada_b, norm_mix, norm_ffn, hy_w_in, hy_b_in, hy_conv_w, hy_conv_b, hy_fw1, hy_fb1, hy_ffreq, hy_fw2, hy_fb2, hy_fw3, hy_decay, hy_skip, hy_w_out, hy_b_out, at_w_in, at_w_out, ffn_w_gate, ffn_w_up, ffn_w_down, final_norm, final_ada_w, final_ada_b):
    d = x_prompt.shape[-1]
    bp, bs = c_prompt.shape[0], c_sample.shape[0]
    pad = -(bp + bs) % (2 * SUBLANES)
    c_all = jnp.concatenate([c_prompt, c_sample, jnp.zeros((pad, d), F32)], axis=0)
    mods = _ada(c_all, ada_w, ada_b)
    fmod = _ada(c_all, final_ada_w[None], final_ada_b[None])[0]
    p = dict(norm_mix=norm_mix, norm_ffn=norm_ffn,
             hy_w_in=hy_w_in.astype(BF16), hy_b_in=hy_b_in, hy_conv_w=hy_conv_w, hy_conv_b=hy_conv_b,
             hy_fw1=hy_fw1, hy_fb1=hy_fb1, hy_ffreq=hy_ffreq, hy_fw2=hy_fw2, hy_fb2=hy_fb2, hy_fw3=hy_fw3,
             hy_decay=hy_decay, hy_skip=hy_skip, hy_w_out=hy_w_out.astype(BF16), hy_b_out=hy_b_out,
             at_w_in=at_w_in.astype(BF16), at_w_out=at_w_out.astype(BF16),
             ffn_w_gate=ffn_w_gate.astype(BF16), ffn_w_up=ffn_w_up.astype(BF16),
             ffn_w_down=ffn_w_down.astype(BF16), final_norm=final_norm)
    y_prompt = _encoder(x_prompt, mods[:, :bp], fmod[:bp], p)
    y_sample = _encoder(x_sample, mods[:, bp:bp + bs], fmod[bp:bp + bs], p)
    return (y_prompt, y_sample)
```

```python
import functools
import math

import numpy as np
import jax
import jax.numpy as jnp
from jax import lax
from jax.experimental import pallas as pl
from jax.experimental.pallas import tpu as pltpu

F32 = jnp.float32
BF16 = jnp.bfloat16
EPS = 1e-6

DEPTH = 4
HYENA_ORDER = 2
N_DIRS = 2
FILTER_BANDS = 16
FILTER_EMB = 1 + 2 * FILTER_BANDS
ATTN_PATTERNS = ((128, 1), (512, 4), (2048, 16))
HEADS_PER_GROUP = 8
HEAD_DIM = 128
GROUP_WIDTH = HEADS_PER_GROUP * HEAD_DIM
ROT_DIM = HEAD_DIM // 4
ROPE_THETA = 500000.0

LANES = 128
SUBLANES = 8
VMEM_LIMIT_BYTES = 56 * 1024 * 1024

DFT_N2 = 256
SLAB_ROWS = DFT_N2 // 2
ATT_BAND = 64
ATT_Q = 2 * ATT_BAND
CLASSES = 16
LSE_LANES = LANES // HEADS_PER_GROUP
ROW_TILE = 1024
COL_TILE = 1024
NEG_BIG = -1e30


def _params(sem):
    return pltpu.CompilerParams(dimension_semantics=sem, vmem_limit_bytes=VMEM_LIMIT_BYTES)


def _dot(a, b):
    return jnp.dot(a, b, preferred_element_type=F32)


def _split(a):
    hi = a.astype(BF16)
    lo = (a - hi.astype(F32)).astype(BF16)
    return hi, lo


def _dot3(a, b):
    ah, al = _split(a)
    bh, bl = _split(b)
    return _dot(ah, bh) + _dot(al, bh) + _dot(ah, bl)


def _modnorm(x, g, shift, scale):
    ms = jnp.mean(x * x, axis=-1, keepdims=True)
    return (x * lax.rsqrt(ms + EPS)) * (g * (1.0 + scale)) + shift


PERM_ROWS = 256


def _perm_matrix(p, q):
    m = np.zeros((PERM_ROWS, PERM_ROWS), np.float32)
    pi, qi = np.meshgrid(np.arange(p), np.arange(q), indexing="ij")
    m[(qi * p + pi).ravel(), (pi * q + qi).ravel()] = 1.0
    return jnp.asarray(m, BF16)


def _perm_for(p, q):
    assert (q <= 16 and p % (PERM_ROWS // q) == 0) or (p <= 16 and q % (PERM_ROWS // p) == 0), (p, q)
    return _perm_matrix(PERM_ROWS // q, q) if q <= 16 else _perm_matrix(p, PERM_ROWS // p)


def _transpose_rows(dst_ref, src_ref, pm_ref, p, q):
    if q <= 16:
        pg = PERM_ROWS // q
        for grp in range(p // pg):
            t = _dot(pm_ref[...], src_ref[grp * PERM_ROWS:(grp + 1) * PERM_ROWS, :]).astype(BF16)
            for qi in range(q):
                dst_ref[qi * p + grp * pg:qi * p + (grp + 1) * pg, :] = t[qi * pg:(qi + 1) * pg]
    else:
        qg = PERM_ROWS // p
        for grp in range(q // qg):
            blk = jnp.concatenate([src_ref[pi * q + grp * qg:pi * q + (grp + 1) * qg, :] for pi in range(p)],
                                  axis=0)
            dst_ref[grp * PERM_ROWS:(grp + 1) * PERM_ROWS, :] = _dot(pm_ref[...], blk).astype(BF16)


def _ada_kernel(c_ref, w_ref, b_ref, o_ref):
    c = c_ref[...]
    cs = c * jax.nn.sigmoid(c)
    o_ref[...] = _dot3(cs, w_ref[...]) + b_ref[...]


def _ada(c_all, w, b, tn=1024):
    nl, d, no = w.shape
    r = c_all.shape[0]
    return pl.pallas_call(
        _ada_kernel,
        grid=(nl, no // tn),
        in_specs=[
            pl.BlockSpec((r, d), lambda l, j: (0, 0)),
            pl.BlockSpec((None, d, tn), lambda l, j: (l, 0, j)),
            pl.BlockSpec((None, 1, tn), lambda l, j: (l, 0, j)),
        ],
        out_specs=pl.BlockSpec((None, r, tn), lambda l, j: (l, 0, j)),
        out_shape=jax.ShapeDtypeStruct((nl, r, no), F32),
        compiler_params=_params(("parallel", "parallel")),
        name="ada_mod",
    )(c_all, w, b.reshape(nl, 1, no))


def _norm_mm_kernel(*refs, mode, tn, perm):
    if perm:
        pm_ref, refs = refs[0], refs[1:]
        h0_ref, refs = refs[-1], refs[:-1]
    if mode == "swiglu":
        x_ref, g_ref, sh_ref, sc_ref, wg_ref, wu_ref, o_ref, h_ref = refs
    elif mode == "rope":
        x_ref, g_ref, sh_ref, sc_ref, w_ref, c_ref, s_ref, rot_ref, o_ref, h_ref = refs
    else:
        x_ref, g_ref, sh_ref, sc_ref, w_ref, b_ref, o_ref, h_ref = refs
    j = pl.program_id(2)

    @pl.when(j == 0)
    def _():
        h = _modnorm(x_ref[...].reshape(h_ref.shape), g_ref[...], sh_ref[...], sc_ref[...]).astype(BF16)
        if perm:
            h0_ref[...] = h
            _transpose_rows(h_ref, h0_ref, pm_ref, *perm)
        else:
            h_ref[...] = h

    h = h_ref[...]
    if mode == "swiglu":
        a = _dot(h, wg_ref[...])
        u = _dot(h, wu_ref[...])
        o_ref[...] = (a * jax.nn.sigmoid(a) * u).astype(o_ref.dtype).reshape(o_ref.shape)
    elif mode == "rope":
        acc = _dot(h, w_ref[...])
        part = (j // (GROUP_WIDTH // tn)) % 3

        @pl.when(part == 2)
        def _():
            o_ref[...] = acc.astype(o_ref.dtype).reshape(o_ref.shape)

        @pl.when(part != 2)
        def _():
            reps = tn // HEAD_DIM
            tabs = [t[...].reshape(acc.shape[0], HEAD_DIM) for t in (c_ref, s_ref)]
            c, s = [jnp.concatenate([t] * reps, axis=1) for t in tabs]
            accb = acc.astype(BF16)
            rw = rot_ref.shape[0]
            partner = jnp.concatenate([_dot(accb[:, k * rw:(k + 1) * rw], rot_ref[...]) for k in range(tn // rw)],
                                      axis=1)
            qs = jnp.where(part == 0, HEAD_DIM ** -0.5, 1.0).astype(F32)
            o_ref[...] = ((acc * c + partner * s) * qs).astype(o_ref.dtype).reshape(o_ref.shape)
    else:
        o_ref[...] = (_dot(h, w_ref[...]) + b_ref[...]).astype(o_ref.dtype).reshape(o_ref.shape)


def _norm_mm(x, x_spec, perm, rows, grid_rows, g, shift, scale, ws, w_layer, col0, nout, out_shape,
             out_spec, *, mode, bias=None, rope=None, tn=512):
    d = x.shape[-1]
    bsz = x.shape[0]
    cb = col0 // tn
    vec = pl.BlockSpec((None, 1, d), lambda b, i, j: (b, 0, 0))
    in_specs = [x_spec, pl.BlockSpec((1, d), lambda b, i, j: (0, 0)), vec, vec]
    in_specs += [pl.BlockSpec((None, d, tn), lambda b, i, j: (w_layer, 0, cb + j)) for _ in ws]
    args = [x, g.reshape(1, d), shift, scale, *ws]
    scratch = [pltpu.VMEM((rows, d), BF16)]
    if perm:
        in_specs.insert(0, pl.BlockSpec((PERM_ROWS, PERM_ROWS), lambda b, i, j: (0, 0)))
        args.insert(0, _perm_for(*perm))
        scratch.append(pltpu.VMEM((rows, d), BF16))
    if mode == "rope":
        tabs, tab_spec = rope
        rw = 4 * HEAD_DIM
        in_specs += [tab_spec] * 2 + [pl.BlockSpec((rw, rw), lambda b, i, j: (0, 0))]
        args += list(tabs) + [_rope_partner_matrix(rw)]
    elif mode == "bias":
        in_specs.append(pl.BlockSpec((1, tn), lambda b, i, j: (0, cb + j)))
        args.append(bias)
    return pl.pallas_call(
        functools.partial(_norm_mm_kernel, mode=mode, tn=tn, perm=perm),
        grid=(bsz, grid_rows, nout // tn),
        in_specs=in_specs,
        out_specs=out_spec,
        out_shape=out_shape,
        scratch_shapes=scratch,
        compiler_params=_params(("parallel", "parallel", "arbitrary")),
        name="norm_mm_" + mode,
    )(*args)


def _mm_res_kernel(*refs, a_mode, perm):
    if a_mode == "merge":
        (pm_ref, o0, o1, o2, l0, l1, l2, e_ref, w_ref, b_ref, x_ref, gt_ref, out_ref, a_ref, a0_ref) = refs
    elif a_mode == "perm":
        pm_ref, a_in, w_ref, b_ref, x_ref, gt_ref, out_ref, a_ref, a0_ref = refs
    else:
        a_in, w_ref, b_ref, x_ref, gt_ref, out_ref = refs
    j = pl.program_id(2)

    if a_mode != "plain":
        @pl.when(j == 0)
        def _():
            k = a_ref.shape[1]
            if a_mode == "merge":
                ls = [l[...].reshape(-1, LANES) for l in (l0, l1, l2)]
                mx = jnp.maximum(jnp.maximum(ls[0], ls[1]), ls[2])
                ws = [jnp.exp(l - mx) for l in ls]
                inv = 1.0 / (ws[0] + ws[1] + ws[2])
                num = jnp.zeros((ls[0].shape[0], k), F32)
                for w, o in zip(ws, (o0, o1, o2)):
                    hi, lo = _split(w * inv)
                    wide = _dot(hi, e_ref[...]) + _dot(lo, e_ref[...])
                    num = num + wide * o[...].reshape(-1, k).astype(F32)
                a0_ref[...] = num.astype(BF16)
            else:
                a0_ref[...] = a_in[...].reshape(-1, k).astype(BF16)
            _transpose_rows(a_ref, a0_ref, pm_ref, *perm)

        a = a_ref[...]
    else:
        a = a_in[...]
    out_ref[...] = x_ref[...] + gt_ref[...] * (_dot(a, w_ref[...]) + b_ref[...])


def _head_spread_matrix(width):
    m = np.zeros((LANES, width), np.float32)
    for hd in range(width // HEAD_DIM):
        m[hd * LSE_LANES, hd * HEAD_DIM:(hd + 1) * HEAD_DIM] = 1.0
    return jnp.asarray(m, BF16)


def _mm_res(a_list, a_specs, w, w_layer, bias, x, gate, rows, *, a_mode="plain", perm=None, tn=512):
    bsz, seq_len, d = x.shape
    k = w.shape[1]
    blk = pl.BlockSpec((None, rows, tn), lambda b, i, j: (b, i, j))
    if a_mode == "merge":
        a_list = list(a_list) + [_head_spread_matrix(k)]
        a_specs = list(a_specs) + [pl.BlockSpec((LANES, k), lambda b, i, j: (0, 0))]
    in_specs = list(a_specs) + [
        pl.BlockSpec((None, k, tn), lambda b, i, j: (w_layer, 0, j)),
        pl.BlockSpec((1, tn), lambda b, i, j: (0, j)),
        blk,
        pl.BlockSpec((None, 1, tn), lambda b, i, j: (b, 0, j)),
    ]
    args = [*a_list, w, bias, x, gate]
    scratch = []
    if a_mode != "plain":
        in_specs.insert(0, pl.BlockSpec((PERM_ROWS, PERM_ROWS), lambda b, i, j: (0, 0)))
        args.insert(0, _perm_for(*perm))
        scratch = [pltpu.VMEM((rows, k), BF16)] * 2
    return pl.pallas_call(
        functools.partial(_mm_res_kernel, a_mode=a_mode, perm=perm),
        grid=(bsz, seq_len // rows, d // tn),
        in_specs=in_specs,
        out_specs=blk,
        out_shape=jax.ShapeDtypeStruct(x.shape, F32),
        scratch_shapes=scratch,
        compiler_params=_params(("parallel", "parallel", "arbitrary")),
        name="mm_res_" + a_mode,
    )(*args)


def _final_kernel(x_ref, g_ref, sh_ref, sc_ref, o_ref):
    o_ref[...] = _modnorm(x_ref[...], g_ref[...], sh_ref[...], sc_ref[...])


def _final(x, g, shift, scale, tm=512):
    bsz, seq_len, d = x.shape
    vec = pl.BlockSpec((None, 1, d), lambda b, i: (b, 0, 0))
    blk = pl.BlockSpec((None, tm, d), lambda b, i: (b, i, 0))
    return pl.pallas_call(
        _final_kernel,
        grid=(bsz, seq_len // tm),
        in_specs=[blk, pl.BlockSpec((1, d), lambda b, i: (0, 0)), vec, vec],
        out_specs=blk,
        out_shape=jax.ShapeDtypeStruct(x.shape, F32),
        compiler_params=_params(("parallel", "parallel")),
        name="final_norm",
    )(x, g.reshape(1, d), shift, scale)


def _filter_positions(seq_len, n1):
    n = 2 * seq_len
    h = SLAB_ROWS
    half = jnp.arange(2)[:, None, None]
    s = jnp.arange(n1)[None, :, None]
    r = jnp.arange(h)[None, None, :]
    idx = ((half * h + r) * n1 + s).reshape(n)
    pos = jnp.where(idx < seq_len, idx, n - idx).astype(F32)
    sign = jnp.where(idx < seq_len, 1.0, jnp.where(idx == seq_len, 0.0, -1.0)).astype(F32)
    t = pos / max(seq_len - 1, 1)
    bands = jnp.linspace(1e-4, FILTER_BANDS - 1, FILTER_BANDS, dtype=F32)
    ang = 2.0 * math.pi * pos[:, None] * bands[None, :] / seq_len
    z = jnp.concatenate([t[:, None], jnp.cos(ang), -jnp.sin(ang)], axis=-1)
    z = jnp.pad(z, ((0, 0), (0, LANES - FILTER_EMB - 1)))
    return jnp.concatenate([z, sign[:, None]], axis=-1)


def _filter_mlp_kernel(z_ref, w1_ref, b1_ref, f_ref, w2_ref, b2_ref, o_ref):
    f = f_ref[...]
    a = jnp.sin(f[0:1, :] * (_dot3(z_ref[...], w1_ref[...]) + b1_ref[...]))
    o_ref[...] = jnp.sin(f[1:2, :] * (_dot3(a, w2_ref[...]) + b2_ref[...]))


def _filter_mlp(zf, w1, b1, freq, w2, b2, tr=512):
    n = zf.shape[0]
    hid = w1.shape[1]
    w1p = jnp.pad(w1, ((0, LANES - w1.shape[0]), (0, 0)))
    full = lambda shape: pl.BlockSpec(shape, lambda i: (0,) * len(shape))
    return pl.pallas_call(
        _filter_mlp_kernel,
        grid=(n // tr,),
        in_specs=[pl.BlockSpec((tr, LANES), lambda i: (i, 0)), full((LANES, hid)), full((1, hid)),
                  full((2, hid)), full((hid, hid)), full((1, hid))],
        out_specs=pl.BlockSpec((tr, hid), lambda i: (i, 0)),
        out_shape=jax.ShapeDtypeStruct((n, hid), F32),
        compiler_params=_params(("parallel",)),
        name="filter_mlp",
    )(zf, w1p, b1.reshape(1, hid), freq, w2, b2.reshape(1, hid))


def _dft_consts(n1, n2):
    n = n1 * n2
    k2 = np.arange(n2 // 2)[:, None]
    nn2 = np.arange(n2)[None, :]
    ph = 2.0 * np.pi * nn2 * (k2 + 0.5) / n2
    f_s1 = np.concatenate([np.cos(ph), -np.sin(ph)], axis=0)
    m = np.arange(n2 // 2)[:, None]
    kk = np.arange(n2 // 2)[None, :]
    ph3 = 2.0 * np.pi * m * (kk + 0.5) / n2
    f_s3 = (2.0 / n) * np.concatenate([np.cos(ph3), -np.sin(ph3)], axis=1)
    a = 2.0 * np.pi * np.outer(np.arange(n1), np.arange(n1)) / n1
    c, s = np.cos(a), -np.sin(a)
    f_fwd = np.block([[c, -s], [s, c]])
    f_inv = np.block([[c, s], [-s, c]])
    th = 2.0 * np.pi * (np.arange(n2 // 2)[:, None] + 0.5) * np.arange(n1)[None, :] / n
    as_bf = lambda x: jnp.asarray(x, F32).astype(BF16)
    tc, ts = jnp.asarray(np.cos(th), F32), jnp.asarray(np.sin(th), F32)
    ff, fi = jnp.asarray(f_fwd, F32), jnp.asarray(f_inv, F32)
    fl, fr = ff[None, :, :n1], ff[None, :, n1:]
    g_fwd = jnp.concatenate([fl * tc[:, None, :] - fr * ts[:, None, :],
                             fl * ts[:, None, :] + fr * tc[:, None, :]], axis=2).astype(BF16)
    it, ib = fi[None, :n1, :], fi[None, n1:, :]
    g_inv = jnp.concatenate([tc[:, :, None] * it - ts[:, :, None] * ib,
                             ts[:, :, None] * it + tc[:, :, None] * ib], axis=1).astype(BF16)
    return dict(f_s1=as_bf(f_s1[:, :n2 // 2]), f_s1_hi=as_bf(f_s1[:, n2 // 2:]), f_s3=as_bf(f_s3),
                g_fwd=g_fwd, g_inv=g_inv, n1=n1)


def _slab_tile(n1):
    return min(n1, 8)


HI16 = -65536
HALF_ULP16 = 0x8000


def _pack_c(re, im):
    rb = lax.bitcast_convert_type(re, jnp.int32) + HALF_ULP16
    ib = lax.bitcast_convert_type(im, jnp.int32) + HALF_ULP16
    return (rb & HI16) | lax.shift_right_logical(ib, 16)


def _pack_bf16(re, im):
    rb = lax.bitcast_convert_type(re.astype(F32), jnp.int32)
    ib = lax.bitcast_convert_type(im.astype(F32), jnp.int32)
    return rb | lax.shift_right_logical(ib, 16)


def _unpack_c(p):
    re = lax.bitcast_convert_type(p & HI16, F32)
    im = lax.bitcast_convert_type(lax.shift_left(p, 16), F32)
    return re, im


def _shift_rows(x, down):
    rows = x.shape[0]
    row = lax.broadcasted_iota(jnp.int32, x.shape, 0)
    if down:
        return jnp.where(row == 0, 0.0, pltpu.roll(x, 1, 0))
    return jnp.where(row == rows - 1, 0.0, pltpu.roll(x, rows - 1, 0))


def _short_conv_slabs(main_ref, prev_ref, next_ref, w_ref, b_ref, first, last):
    n_slabs = main_ref.shape[0]
    prev = prev_ref[0].astype(F32)
    prev = jnp.where(first, _shift_rows(prev, True), prev)
    nxt = next_ref[0].astype(F32)
    nxt = jnp.where(last, _shift_rows(nxt, False), nxt)
    w = w_ref[...]
    out = []
    for s in range(n_slabs):
        up = prev if s == 0 else main_ref[s - 1].astype(F32)
        dn = nxt if s == n_slabs - 1 else main_ref[s + 1].astype(F32)
        out.append(up * w[0:1, :] + main_ref[s].astype(F32) * w[1:2, :] + dn * w[2:3, :] + b_ref[...])
    return out


def _fft_s1_kernel(*refs, short_conv):
    re_ref, im_ref, ret_ref, imt_ref = refs[-4:]
    if short_conv:
        pm_ref, f_ref, m_ref, p_ref, n_ref, w_ref, b_ref, o_ref, u_ref = refs[:-4]
        t = pl.program_id(1)
        slabs = _short_conv_slabs(m_ref, p_ref, n_ref, w_ref, b_ref, t == 0, t == pl.num_programs(1) - 1)
    else:
        pm_ref, f_ref, m_ref, o_ref = refs[:-4]
        slabs = [m_ref[s] for s in range(m_ref.shape[0])]
    half = o_ref.shape[0]
    for s, u in enumerate(slabs):
        if short_conv:
            u_ref[s] = u.astype(u_ref.dtype)
        r = _dot(f_ref[...], u.astype(BF16))
        re_ref[s * half:(s + 1) * half, :] = r[:half].astype(BF16)
        im_ref[s * half:(s + 1) * half, :] = r[half:].astype(BF16)
    _transpose_rows(ret_ref, re_ref, pm_ref, len(slabs), half)
    _transpose_rows(imt_ref, im_ref, pm_ref, len(slabs), half)
    o_ref[...] = _pack_bf16(ret_ref[...], imt_ref[...]).reshape(o_ref.shape)


def _fft_s1(consts, src, col_block, conv=None, ct=512):
    bsz, n1, h, c = src.shape
    d = consts["d"]
    st = _slab_tile(n1)
    cpb = d // ct
    f = consts["f_s1"]
    main = pl.BlockSpec((None, st, h, ct), lambda b, t, j: (b, t, 0, col_block * cpb + j))
    a_spec = pl.BlockSpec((None, h, st, ct), lambda b, t, j: (b, 0, t, j))
    a_shape = jax.ShapeDtypeStruct((bsz, h, n1, d), jnp.int32)
    fspec = pl.BlockSpec(f.shape, lambda b, t, j: (0, 0))
    pspec = pl.BlockSpec((PERM_ROWS, PERM_ROWS), lambda b, t, j: (0, 0))
    pm = _perm_for(st, h)
    scratch = [pltpu.VMEM((st * h, ct), BF16)] * 4
    if conv is None:
        return pl.pallas_call(
            functools.partial(_fft_s1_kernel, short_conv=False),
            grid=(bsz, n1 // st, cpb),
            in_specs=[pspec, fspec, main], out_specs=a_spec, out_shape=a_shape, scratch_shapes=scratch,
            compiler_params=_params(("parallel", "parallel", "parallel")), name="fft_s1",
        )(pm, f, src)
    w, b = conv
    prev = pl.BlockSpec((None, 1, h, ct), lambda b, t, j: (b, (t * st + n1 - 1) % n1, 0, col_block * cpb + j))
    nxt = pl.BlockSpec((None, 1, h, ct), lambda b, t, j: (b, ((t + 1) * st) % n1, 0, col_block * cpb + j))
    wspec = pl.BlockSpec((3, ct), lambda b, t, j: (0, col_block * cpb + j))
    bspec = pl.BlockSpec((1, ct), lambda b, t, j: (0, col_block * cpb + j))
    u_spec = pl.BlockSpec((None, st, h, ct), lambda b, t, j: (b, t, 0, j))
    return pl.pallas_call(
        functools.partial(_fft_s1_kernel, short_conv=True),
        grid=(bsz, n1 // st, cpb),
        in_specs=[pspec, fspec, main, prev, nxt, wspec, bspec],
        out_specs=[a_spec, u_spec],
        out_shape=[a_shape, jax.ShapeDtypeStruct((bsz, n1, h, d), BF16)],
        scratch_shapes=scratch,
        compiler_params=_params(("parallel", "parallel", "parallel")), name="fft_s1_conv",
    )(pm, f, src, src, src, w, b)


def _filter_s1_kernel(pm_ref, flo_ref, fhi_ref, alo_ref, ahi_ref, zlo_ref, zhi_ref, wf_ref, wb_ref, df_ref,
                      db_ref, o_ref, ss_ref, re_ref, im_ref, ret_ref, imt_ref):
    half = o_ref.shape[0]

    @pl.when(pl.program_id(2) == 0)
    def _():
        ss_ref[...] = jnp.zeros_like(ss_ref)

    n_slabs, rows = alo_ref.shape[0], alo_ref.shape[1]
    ct = o_ref.shape[2]
    ss = jnp.zeros(ss_ref.shape, F32)
    r = jnp.zeros((2 * half, n_slabs * ct), F32)
    for f_ref, a_ref, z_ref, w_ref, d_ref in ((flo_ref, alo_ref, zlo_ref, wf_ref, df_ref),
                                              (fhi_ref, ahi_ref, zhi_ref, wb_ref, db_ref)):
        z = z_ref[...].reshape(n_slabs * rows, LANES)
        t, sign = z[:, 0:1], z[:, LANES - 1:LANES]
        taps = _dot3(a_ref[...].reshape(n_slabs * rows, a_ref.shape[2]), w_ref[...])
        taps = taps * jnp.exp(-t * jnp.abs(d_ref[...])) * sign
        ss = ss + jnp.sum(taps * taps, axis=0, keepdims=True)
        tb = taps.astype(BF16)
        wide = jnp.concatenate([tb[s * rows:(s + 1) * rows, :] for s in range(n_slabs)], axis=1)
        r = r + _dot(f_ref[...], wide)
    for s in range(n_slabs):
        re_ref[s * half:(s + 1) * half, :] = r[:half, s * ct:(s + 1) * ct].astype(BF16)
        im_ref[s * half:(s + 1) * half, :] = r[half:, s * ct:(s + 1) * ct].astype(BF16)
    ss_ref[...] += ss
    _transpose_rows(ret_ref, re_ref, pm_ref, n_slabs, half)
    _transpose_rows(imt_ref, im_ref, pm_ref, n_slabs, half)
    o_ref[...] = _pack_bf16(ret_ref[...], imt_ref[...]).reshape(o_ref.shape)


def _filter_s1(consts, a2, zf, w3, decay, ct=256):
    n1, h = consts["n1"], SLAB_ROWS
    hid = a2.shape[1]
    d = decay.shape[-1]
    nct = d // ct
    st = _slab_tile(n1)
    flo, fhi = consts["f_s1"], consts["f_s1_hi"]
    dec = decay.reshape(1, HYENA_ORDER * N_DIRS * d)
    fspec = pl.BlockSpec(flo.shape, lambda o, j, t: (0, 0))
    rows = lambda half, width: pl.BlockSpec((None, st, h, width), lambda o, j, t: (half, t, 0, 0))
    wcol = lambda dirn, nrow: pl.BlockSpec((nrow, ct), lambda o, j, t: (0, (o * N_DIRS + dirn) * nct + j))
    return pl.pallas_call(
        _filter_s1_kernel,
        grid=(HYENA_ORDER, nct, n1 // st),
        in_specs=[pl.BlockSpec((PERM_ROWS, PERM_ROWS), lambda o, j, t: (0, 0)), fspec, fspec,
                  rows(0, hid), rows(1, hid), rows(0, LANES), rows(1, LANES),
                  wcol(0, hid), wcol(1, hid), wcol(0, 1), wcol(1, 1)],
        out_specs=[pl.BlockSpec((None, h, st, ct), lambda o, j, t: (o, 0, t, j)),
                   pl.BlockSpec((None, 1, ct), lambda o, j, t: (o, 0, j))],
        out_shape=[jax.ShapeDtypeStruct((HYENA_ORDER, h, n1, d), jnp.int32),
                   jax.ShapeDtypeStruct((HYENA_ORDER, 1, d), F32)],
        scratch_shapes=[pltpu.VMEM((st * h, ct), BF16)] * 4,
        compiler_params=_params(("parallel", "parallel", "arbitrary")), name="filter_s1",
    )(_perm_for(st, h), flo, fhi, a2.reshape(2, n1, h, hid), a2.reshape(2, n1, h, hid),
      zf.reshape(2, n1, h, LANES), zf.reshape(2, n1, h, LANES), w3, w3, dec, dec)


def _fft_s2f_kernel(a_ref, gf_ref, ss_ref, o_ref, *, kb):
    n1 = a_ref.shape[1]
    scale = lax.rsqrt(ss_ref[...] + EPS)

    def body(kk, carry):
        ar, ai = _unpack_c(a_ref[kk])
        x = _dot(gf_ref[kk], jnp.concatenate([ar, ai], axis=0).astype(BF16))
        o_ref[kk] = _pack_c(x[:n1] * scale, x[n1:] * scale)
        return carry

    lax.fori_loop(0, kb, body, 0, unroll=min(kb, 8))


def _fft_s2_kernel(a_ref, k_ref, gf_ref, gi_ref, o_ref, *, kb):
    nb, n1, ct = a_ref.shape[0], a_ref.shape[2], a_ref.shape[3]

    def body(kk, carry):
        parts = [_unpack_c(a_ref[b, kk]) for b in range(nb)]
        ar = jnp.concatenate([p[0] for p in parts], axis=1)
        ai = jnp.concatenate([p[1] for p in parts], axis=1)
        x = _dot(gf_ref[kk], jnp.concatenate([ar, ai], axis=0).astype(BF16))
        xr, xi = x[:n1], x[n1:]
        kr, ki = [jnp.concatenate([v] * nb, axis=1) for v in _unpack_c(k_ref[kk])]
        zr = xr * kr - xi * ki
        zi = xr * ki + xi * kr
        y = _dot(gi_ref[kk], jnp.concatenate([zr, zi], axis=0).astype(BF16))
        packed = _pack_c(y[:n1], y[n1:])
        for b in range(nb):
            o_ref[b, kk] = packed[:, b * ct:(b + 1) * ct]
        return carry

    lax.fori_loop(0, kb, body, 0, unroll=min(kb, 8))


def _s2_tiles(n1, d):
    kb = max(1, 1024 // n1)
    ct = min(d, 512)
    return kb, ct


def _fft_s2f(a, sumsq, consts):
    n_o, k2n, n1, d = a.shape
    kb, ct = _s2_tiles(n1, d)
    blk = pl.BlockSpec((None, kb, n1, ct), lambda k, j, o: (o, k, 0, j))
    return pl.pallas_call(
        functools.partial(_fft_s2f_kernel, kb=kb),
        grid=(k2n // kb, d // ct, n_o),
        in_specs=[blk, pl.BlockSpec((kb, 2 * n1, 2 * n1), lambda k, j, o: (k, 0, 0)),
                  pl.BlockSpec((None, 1, ct), lambda k, j, o: (o, 0, j))],
        out_specs=blk,
        out_shape=jax.ShapeDtypeStruct(a.shape, jnp.int32),
        compiler_params=_params(("parallel", "parallel", "parallel")),
        name="fft_s2_filter",
    )(a, consts["g_fwd"], sumsq)


def _fft_s2(a, kf, order, consts):
    bsz, k2n, n1, d = a.shape
    kb, ct = _s2_tiles(n1, d)
    nb = bsz if n1 * bsz <= SLAB_ROWS else 1
    blk = pl.BlockSpec((nb, kb, n1, ct), lambda k, j, b: (b, k, 0, j))
    mat = pl.BlockSpec((kb, 2 * n1, 2 * n1), lambda k, j, b: (k, 0, 0))
    return pl.pallas_call(
        functools.partial(_fft_s2_kernel, kb=kb),
        grid=(k2n // kb, d // ct, bsz // nb),
        in_specs=[blk, pl.BlockSpec((None, kb, n1, ct), lambda k, j, b: (order, k, 0, j)), mat, mat],
        out_specs=blk,
        out_shape=jax.ShapeDtypeStruct(a.shape, jnp.int32),
        compiler_params=_params(("parallel", "parallel", "parallel")),
        name="fft_s2",
    )(a, kf, consts["g_fwd"], consts["g_inv"])


def _fft_s3_kernel(pm_ref, f_ref, t_ref, u_ref, gm_ref, gp_ref, gn_ref, w_ref, b_ref, sk_ref, o_ref,
                   re_ref, im_ref, ret_ref, imt_ref):
    t_id = pl.program_id(1)
    half, st = t_ref.shape[0], t_ref.shape[1]
    re, im = _unpack_c(t_ref[...].reshape(half * st, t_ref.shape[2]))
    re_ref[...] = re.astype(BF16)
    im_ref[...] = im.astype(BF16)
    _transpose_rows(ret_ref, re_ref, pm_ref, half, st)
    _transpose_rows(imt_ref, im_ref, pm_ref, half, st)
    gates = _short_conv_slabs(gm_ref, gp_ref, gn_ref, w_ref, b_ref, t_id == 0, t_id == pl.num_programs(1) - 1)
    for s, gate in enumerate(gates):
        t = jnp.concatenate([ret_ref[s * half:(s + 1) * half, :], imt_ref[s * half:(s + 1) * half, :]], axis=0)
        y = _dot(f_ref[...], t)
        o_ref[s] = (gate * (y + u_ref[s].astype(F32) * sk_ref[...])).astype(o_ref.dtype)


def _fft_s3(consts, t, u, z, gate_block, conv_w, conv_b, skip, order, out_dtype, ct=512):
    bsz, h, n1, d = t.shape
    st = _slab_tile(n1)
    cpb = d // ct
    f = consts["f_s3"]
    gcol = lambda j: gate_block * cpb + j
    slab = lambda idx: pl.BlockSpec((None, 1, h, ct), lambda b, tt, j: (b, idx(tt), 0, gcol(j)))
    return pl.pallas_call(
        _fft_s3_kernel,
        grid=(bsz, n1 // st, cpb),
        scratch_shapes=[pltpu.VMEM((st * h, ct), BF16)] * 4,
        in_specs=[
            pl.BlockSpec((PERM_ROWS, PERM_ROWS), lambda b, tt, j: (0, 0)),
            pl.BlockSpec(f.shape, lambda b, tt, j: (0, 0)),
            pl.BlockSpec((None, h, st, ct), lambda b, tt, j: (b, 0, tt, j)),
            pl.BlockSpec((None, st, h, ct), lambda b, tt, j: (b, tt, 0, j)),
            pl.BlockSpec((None, st, h, ct), lambda b, tt, j: (b, tt, 0, gcol(j))),
            slab(lambda tt: (tt * st + n1 - 1) % n1),
            slab(lambda tt: ((tt + 1) * st) % n1),
            pl.BlockSpec((3, ct), lambda b, tt, j: (0, gcol(j))),
            pl.BlockSpec((1, ct), lambda b, tt, j: (0, gcol(j))),
            pl.BlockSpec((None, 1, ct), lambda b, tt, j: (order, 0, j)),
        ],
        out_specs=pl.BlockSpec((None, st, h, ct), lambda b, tt, j: (b, tt, 0, j)),
        out_shape=jax.ShapeDtypeStruct((bsz, n1, h, d), out_dtype),
        compiler_params=_params(("parallel", "parallel", "parallel")),
        name="fft_s3",
    )(_perm_for(h, st), f, t, u, z, z, z, conv_w, conv_b, skip)


def _hyena_filter_spectra(seq_len, consts, fw1, fb1, ffreq, fw2, fb2, fw3, decay):
    d = decay.shape[-1]
    n1 = consts["n1"]
    zf = _filter_positions(seq_len, n1)
    a2 = _filter_mlp(zf, fw1, fb1, ffreq, fw2, fb2)
    a, sumsq = _filter_s1(consts, a2, zf, fw3, decay)
    return _fft_s2f(a, sumsq, consts)


def _hyena_mixer(x, g, shift, scale, gate, p, layer, kf, consts):
    bsz, seq_len, d = x.shape
    n1 = consts["n1"]
    h = SLAB_ROWS
    st = _slab_tile(n1)
    xv = x.reshape(bsz, h, n1, d)
    z = _norm_mm(
        xv, pl.BlockSpec((None, h, st, d), lambda b, i, j: (b, 0, i, 0)), (h, st), st * h, n1 // st, g, shift, scale,
        [p["hy_w_in"]], layer, 0, 3 * d, jax.ShapeDtypeStruct((bsz, n1, h, 3 * d), BF16),
        pl.BlockSpec((None, st, h, COL_TILE), lambda b, i, j: (b, i, 0, j)),
        mode="bias", bias=p["hy_b_in"][layer].reshape(1, 3 * d), tn=COL_TILE)
    cw, cb = p["hy_conv_w"][layer], p["hy_conv_b"][layer].reshape(1, 3 * d)
    skip = p["hy_skip"][layer].reshape(HYENA_ORDER, 1, d)
    a, u = _fft_s1(consts, z, 2, conv=(cw, cb))
    t = _fft_s2(a, kf, 0, consts)
    y1 = _fft_s3(consts, t, u, z, 0, cw, cb, skip, 0, BF16)
    a = _fft_s1(consts, y1, 0)
    t = _fft_s2(a, kf, 1, consts)
    y2 = _fft_s3(consts, t, y1, z, 1, cw, cb, skip, 1, F32)
    q = ROW_TILE // n1
    return _mm_res(
        [y2], [pl.BlockSpec((None, n1, q, d), lambda b, i, j: (b, 0, i, 0))], p["hy_w_out"], layer,
        p["hy_b_out"][layer].reshape(1, d), x, gate, ROW_TILE, a_mode="perm", perm=(n1, q), tn=COL_TILE)


def _attn_kernel(*refs, phases, ta, n_sub, to_classes):
    if to_classes:
        (pm_ref, q_ref, kp_ref, km_ref, kn_ref, vp_ref, vm_ref, vn_ref, o_ref, l_ref, kx_ref, vx_ref,
         on_ref, ot_ref, ln_ref, ls_ref, lt_ref) = refs
    else:
        q_ref, kp_ref, km_ref, kn_ref, vp_ref, vm_ref, vn_ref, o_ref, l_ref, kx_ref, vx_ref = refs
    i = pl.program_id(2)
    halo = ATT_BAND // phases
    qa = ATT_Q // phases
    ka = 2 * qa
    kx_ref[:, 0:halo] = kp_ref[...]
    kx_ref[:, halo:halo + ta] = km_ref[...]
    kx_ref[:, halo + ta:] = kn_ref[...]
    vx_ref[:, 0:halo] = vp_ref[...]
    vx_ref[:, halo:halo + ta] = vm_ref[...]
    vx_ref[:, halo + ta:] = vn_ref[...]
    row = lax.broadcasted_iota(jnp.int32, (ATT_Q, 2 * ATT_Q), 0)
    col = lax.broadcasted_iota(jnp.int32, (ATT_Q, 2 * ATT_Q), 1)
    cq, aq = row >> (qa.bit_length() - 1), row & (qa - 1)
    ck, ak = col >> (ka.bit_length() - 1), col & (ka - 1)
    delta = phases * (ak - aq) - ATT_BAND + ck - cq
    band = (delta >= -ATT_BAND) & (delta <= ATT_BAND)
    lane_head = lax.broadcasted_iota(jnp.int32, (ATT_Q, LANES), 1) >> (LSE_LANES.bit_length() - 1)
    for s in range(ta // qa):
        key_idx = phases * (i * ta + s * qa - halo + ak) + ck
        valid = band & (key_idx >= 0) & (key_idx < n_sub)
        lse_tile = jnp.zeros((ATT_Q, LANES), F32)
        for h in range(HEADS_PER_GROUP):
            cs = slice(h * HEAD_DIM, (h + 1) * HEAD_DIM)
            q = jnp.concatenate([q_ref[c, s * qa:(s + 1) * qa, cs] for c in range(phases)], axis=0)
            k = jnp.concatenate([kx_ref[c, s * qa:s * qa + ka, cs] for c in range(phases)], axis=0)
            v = jnp.concatenate([vx_ref[c, s * qa:s * qa + ka, cs] for c in range(phases)], axis=0)
            sc = lax.dot_general(q, k, (((1,), (1,)), ((), ())), preferred_element_type=F32)
            sc = jnp.where(valid, sc, NEG_BIG)
            m = jnp.max(sc, axis=-1, keepdims=True)
            pr = jnp.exp(sc - m)
            den = jnp.sum(pr, axis=-1, keepdims=True)
            o = (_dot(pr.astype(BF16), v) / den).astype(BF16)
            lse_tile = jnp.where(lane_head == h, m + jnp.log(den), lse_tile)
            if to_classes:
                on_ref[s * ATT_Q:(s + 1) * ATT_Q, cs] = o
            else:
                for c in range(phases):
                    o_ref[c, s * qa:(s + 1) * qa, cs] = o[c * qa:(c + 1) * qa]
        if to_classes:
            ln_ref[s * ATT_Q:(s + 1) * ATT_Q, :] = lse_tile
        else:
            for c in range(phases):
                l_ref[c, s * qa:(s + 1) * qa, :] = lse_tile[c * qa:(c + 1) * qa]
    if to_classes:
        na = ta // CLASSES
        _transpose_rows(ot_ref, on_ref, pm_ref, na, CLASSES)
        o_ref[...] = ot_ref[...].reshape(o_ref.shape)
        rest = ln_ref[...]
        total = jnp.zeros(rest.shape, F32)
        for _ in range(3):
            piece = rest.astype(BF16)
            rest = rest - piece.astype(F32)
            ls_ref[...] = piece
            _transpose_rows(lt_ref, ls_ref, pm_ref, na, CLASSES)
            total = total + lt_ref[...].astype(F32)
        l_ref[...] = total.reshape(l_ref.shape)


def _attn_group(qkv, col0, phases, n_sub, lead_grid, lead_block, lead_index, ta, *, to_classes=False,
                out_arr_shape=None, out_block=None, out_index=None):
    gw = GROUP_WIDTH
    halo = ATT_BAND // phases
    rows = qkv.shape[-2]
    per = ta // halo
    nblk = rows // halo
    cb = col0 // gw

    def spec(nrows, ridx, part):
        return pl.BlockSpec(tuple(lead_block) + (nrows, gw),
                            lambda b, rho, i: tuple(lead_index(b, rho)) + (ridx(i), cb + part))

    main = lambda part: spec(ta, lambda i: i, part)
    prev = lambda part: spec(halo, lambda i: jnp.maximum(i * per - 1, 0), part)
    nxt = lambda part: spec(halo, lambda i: jnp.minimum((i + 1) * per, nblk - 1), part)
    bsz = qkv.shape[0]
    in_specs = [main(0), prev(1), main(1), nxt(1), prev(2), main(2), nxt(2)]
    args = [qkv] * 7
    if to_classes:
        assert ta == PERM_ROWS
        out_specs = [pl.BlockSpec(out_block + (gw,), out_index), pl.BlockSpec(out_block + (LANES,), out_index)]
        scratch_extra = [pltpu.VMEM((ta, gw), BF16)] * 2 + [pltpu.VMEM((ta, LANES), F32)] + \
                        [pltpu.VMEM((ta, LANES), BF16)] * 2
        in_specs.insert(0, pl.BlockSpec((PERM_ROWS, PERM_ROWS), lambda b, rho, i: (0, 0)))
        args.insert(0, _perm_for(ta // CLASSES, CLASSES))
    else:
        out_arr_shape = qkv.shape[:-1]
        oidx = lambda b, rho, i: tuple(lead_index(b, rho)) + (i, 0)
        out_specs = [pl.BlockSpec(tuple(lead_block) + (ta, gw), oidx),
                     pl.BlockSpec(tuple(lead_block) + (ta, LANES), oidx)]
        scratch_extra = []
    kx_shape = (phases, ta + 2 * halo, gw)
    return pl.pallas_call(
        functools.partial(_attn_kernel, phases=phases, ta=ta, n_sub=n_sub, to_classes=to_classes),
        grid=(bsz, lead_grid, rows // ta),
        in_specs=in_specs,
        out_specs=out_specs,
        out_shape=[jax.ShapeDtypeStruct(tuple(out_arr_shape) + (gw,), BF16),
                   jax.ShapeDtypeStruct(tuple(out_arr_shape) + (LANES,), F32)],
        scratch_shapes=[pltpu.VMEM(kx_shape, BF16)] * 2 + scratch_extra,
        compiler_params=_params(("parallel", "parallel", "parallel")),
        name="attn_p%d" % phases + ("_cls" if to_classes else ""),
    )(*args)


def _rope_tables(pos):
    half = ROT_DIM // 2
    inv = ROPE_THETA ** (-jnp.arange(0, ROT_DIM, 2, dtype=F32) / ROT_DIM)
    ang = pos.astype(F32)[:, None] * inv[None, :]
    cos, sin = jnp.cos(ang), jnp.sin(ang)
    n = pos.shape[0]
    rest = HEAD_DIM - ROT_DIM
    c = jnp.concatenate([cos, cos, jnp.ones((n, rest), F32)], axis=1)
    s = jnp.concatenate([sin, sin, jnp.zeros((n, rest), F32)], axis=1)
    return c, s


def _rope_partner_matrix(width):
    half = ROT_DIM // 2
    m = np.zeros((width, width), np.float32)
    for base in range(0, width, HEAD_DIM):
        for k in range(half):
            m[base + k + half, base + k] = -1.0
            m[base + k, base + k + half] = 1.0
    return jnp.asarray(m, BF16)


def _attn_mixer(x, g, shift, scale, gate, p, layer):
    bsz, seq_len, d = x.shape
    gw = GROUP_WIDTH
    nc = seq_len // CLASSES
    ca = ROW_TILE // CLASSES
    w_in = p["at_w_in"]
    tabs = _rope_tables(jnp.arange(seq_len))
    qkv0 = _norm_mm(
        x, pl.BlockSpec((None, ROW_TILE, d), lambda b, i, j: (b, i, 0)), None, ROW_TILE, seq_len // ROW_TILE,
        g, shift, scale, [w_in], layer, 0, 3 * gw, jax.ShapeDtypeStruct((bsz, seq_len, 3 * gw), BF16),
        pl.BlockSpec((None, ROW_TILE, COL_TILE), lambda b, i, j: (b, i, j)), mode="rope",
        rope=(tabs, pl.BlockSpec((ROW_TILE, HEAD_DIM), lambda b, i, j: (i, 0))), tn=COL_TILE)
    pos_c = (jnp.arange(nc)[None, :] * CLASSES + jnp.arange(CLASSES)[:, None]).reshape(-1)
    tabs_c = [t.reshape(CLASSES, nc, HEAD_DIM) for t in _rope_tables(pos_c)]
    qkv12 = _norm_mm(
        x.reshape(bsz, nc, CLASSES, d), pl.BlockSpec((None, ca, CLASSES, d), lambda b, i, j: (b, i, 0, 0)),
        (ca, CLASSES), ROW_TILE, nc // ca, g, shift, scale, [w_in], layer, 3 * gw, 6 * gw,
        jax.ShapeDtypeStruct((bsz, CLASSES, nc, 6 * gw), BF16),
        pl.BlockSpec((None, CLASSES, ca, COL_TILE), lambda b, i, j: (b, 0, i, j)), mode="rope",
        rope=(tabs_c, pl.BlockSpec((CLASSES, ca, HEAD_DIM), lambda b, i, j: (0, i, 0))), tn=COL_TILE)
    cls_shape = (bsz, CLASSES, nc)
    ta0 = PERM_ROWS
    o0, l0 = _attn_group(
        qkv0.reshape(bsz, 1, seq_len, 3 * gw), 0, 1, seq_len, 1, (None, 1), lambda b, rho: (b, 0), ta0,
        to_classes=True, out_arr_shape=cls_shape, out_block=(None, CLASSES, ta0 // CLASSES),
        out_index=lambda b, rho, i: (b, 0, i, 0))
    dil1 = ATTN_PATTERNS[1][1]
    ph = CLASSES // dil1
    ta1 = min(128, nc)
    o1, l1 = _attn_group(
        qkv12.reshape(bsz, ph, dil1, nc, 6 * gw), 0, ph, seq_len // dil1, dil1, (None, ph, None),
        lambda b, rho: (b, 0, rho), ta1)
    ta2 = min(512, nc)
    o2, l2 = _attn_group(
        qkv12.reshape(bsz, CLASSES, 1, nc, 6 * gw), 3 * gw, 1, nc, CLASSES, (None, None, 1),
        lambda b, rho: (b, rho, 0), ta2)
    os_ = [o.reshape(cls_shape + (gw,)) for o in (o0, o1, o2)]
    ls = [l.reshape(cls_shape + (LANES,)) for l in (l0, l1, l2)]
    blk = lambda width: pl.BlockSpec((None, CLASSES, ca, width), lambda b, i, j: (b, 0, i, 0))
    return _mm_res(os_ + ls, [blk(gw)] * 3 + [blk(LANES)] * 3, p["at_w_out"], layer, jnp.zeros((1, d), F32),
                   x, gate, ROW_TILE, a_mode="merge", perm=(CLASSES, ca), tn=COL_TILE)


def _ffn(x, g, shift, scale, gate, p, layer):
    bsz, seq_len, d = x.shape
    dff = p["ffn_w_gate"].shape[-1]
    tiles = seq_len // ROW_TILE
    xs = pl.BlockSpec((None, ROW_TILE, d), lambda b, i, j: (b, i, 0))
    hs = pl.BlockSpec((None, ROW_TILE, 512), lambda b, i, j: (b, i, j))
    hmid = _norm_mm(x, xs, None, ROW_TILE, tiles, g, shift, scale, [p["ffn_w_gate"], p["ffn_w_up"]], layer, 0,
                    dff, jax.ShapeDtypeStruct((bsz, seq_len, dff), BF16), hs, mode="swiglu")
    return _mm_res([hmid], [pl.BlockSpec((None, ROW_TILE, dff), lambda b, i, j: (b, i, 0))],
                   p["ffn_w_down"], layer, jnp.zeros((1, d), F32), x, gate, ROW_TILE)


def _encoder(x, mods, final_mod, p):
    bsz, seq_len, d = x.shape
    n1 = 2 * seq_len // DFT_N2
    consts = _dft_consts(n1, DFT_N2)
    consts["d"] = d
    for i in range(DEPTH):
        sh_m, sc_m, g_m, sh_f, sc_f, g_f = [mods[i][:, None, k * d:(k + 1) * d] for k in range(6)]
        j = i // 2
        if i % 2 == 0:
            kf = _hyena_filter_spectra(seq_len, consts, p["hy_fw1"][j], p["hy_fb1"][j], p["hy_ffreq"][j],
                                       p["hy_fw2"][j], p["hy_fb2"][j], p["hy_fw3"][j], p["hy_decay"][j])
            x = _hyena_mixer(x, p["norm_mix"][i], sh_m, sc_m, g_m, p, j, kf, consts)
        else:
            x = _attn_mixer(x, p["norm_mix"][i], sh_m, sc_m, g_m, p, j)
        x = _ffn(x, p["norm_ffn"][i], sh_f, sc_f, g_f, p, i)
    sh, sc = final_mod[:, None, :d], final_mod[:, None, d:]
    return _final(x, p["final_norm"], sh, sc)


def kernel(x_prompt, x_sample, c_prompt, c_sample, ada_w, ada_b, norm_mix, norm_ffn, hy_w_in, hy_b_in, hy_conv_w, hy_conv_b, hy_fw1, hy_fb1, hy_ffreq, hy_fw2, hy_fb2, hy_fw3, hy_decay, hy_skip, hy_w_out, hy_b_out, at_w_in, at_w_out, ffn_w_gate, ffn_w_up, ffn_w_down, final_norm, final_ada_w, final_ada_b):
    d = x_prompt.shape[-1]
    bp, bs = c_prompt.shape[0], c_sample.shape[0]
    pad = -(bp + bs) % (2 * SUBLANES)
    c_all = jnp.concatenate([c_prompt, c_sample, jnp.zeros((pad, d), F32)], axis=0)
    mods = _ada(c_all, ada_w, ada_b)
    fmod = _ada(c_all, final_ada_w[None], final_ada_b[None])[0]
    p = dict(norm_mix=norm_mix, norm_ffn=norm_ffn,
             hy_w_in=hy_w_in.astype(BF16), hy_b_in=hy_b_in, hy_conv_w=hy_conv_w, hy_conv_b=hy_conv_b,
             hy_fw1=hy_fw1, hy_fb1=hy_fb1, hy_ffreq=hy_ffreq, hy_fw2=hy_fw2, hy_fb2=hy_fb2, hy_fw3=hy_fw3,
             hy_decay=hy_decay, hy_skip=hy_skip, hy_w_out=hy_w_out.astype(BF16), hy_b_out=hy_b_out,
             at_w_in=at_w_in.astype(BF16), at_w_out=at_w_out.astype(BF16),
             ffn_w_gate=ffn_w_gate.astype(BF16), ffn_w_up=ffn_w_up.astype(BF16),
             ffn_w_down=ffn_w_down.astype(BF16), final_norm=final_norm)
    y_prompt = _encoder(x_prompt, mods[:, :bp], fmod[:bp], p)
    y_sample = _encoder(x_sample, mods[:, bp:bp + bs], fmod[bp:bp + bs], p)
    return (y_prompt, y_sample)
```

```python
import functools
import math

import numpy as np
import jax
import jax.numpy as jnp
from jax import lax
from jax.experimental import pallas as pl
from jax.experimental.pallas import tpu as pltpu

F32 = jnp.float32
BF16 = jnp.bfloat16
EPS = 1e-6

DEPTH = 4
HYENA_ORDER = 2
N_DIRS = 2
FILTER_BANDS = 16
FILTER_EMB = 1 + 2 * FILTER_BANDS
ATTN_PATTERNS = ((128, 1), (512, 4), (2048, 16))
HEADS_PER_GROUP = 8
HEAD_DIM = 128
GROUP_WIDTH = HEADS_PER_GROUP * HEAD_DIM
ROT_DIM = HEAD_DIM // 4
ROPE_THETA = 500000.0

LANES = 128
SUBLANES = 8
VMEM_LIMIT_BYTES = 56 * 1024 * 1024

DFT_N2 = 256
SLAB_ROWS = DFT_N2 // 2
ATT_BAND = 64
ATT_Q = 2 * ATT_BAND
CLASSES = 16
LSE_LANES = LANES // HEADS_PER_GROUP
ROW_TILE = 1024
COL_TILE = 1024
NEG_BIG = -1e30


def _params(sem):
    return pltpu.CompilerParams(dimension_semantics=sem, vmem_limit_bytes=VMEM_LIMIT_BYTES)


def _dot(a, b):
    return jnp.dot(a, b, preferred_element_type=F32)


def _split(a):
    hi = a.astype(BF16)
    lo = (a - hi.astype(F32)).astype(BF16)
    return hi, lo


def _dot3(a, b):
    ah, al = _split(a)
    bh, bl = _split(b)
    return _dot(ah, bh) + _dot(al, bh) + _dot(ah, bl)


def _modnorm(x, g, shift, scale):
    ms = jnp.mean(x * x, axis=-1, keepdims=True)
    return (x * lax.rsqrt(ms + EPS)) * (g * (1.0 + scale)) + shift


PERM_ROWS = 256


def _perm_matrix(p, q):
    m = np.zeros((PERM_ROWS, PERM_ROWS), np.float32)
    pi, qi = np.meshgrid(np.arange(p), np.arange(q), indexing="ij")
    m[(qi * p + pi).ravel(), (pi * q + qi).ravel()] = 1.0
    return jnp.asarray(m, BF16)


def _perm_for(p, q):
    assert (q <= 16 and p % (PERM_ROWS // q) == 0) or (p <= 16 and q % (PERM_ROWS // p) == 0), (p, q)
    return _perm_matrix(PERM_ROWS // q, q) if q <= 16 else _perm_matrix(p, PERM_ROWS // p)


def _transpose_rows(dst_ref, src_ref, pm_ref, p, q):
    if q <= 16:
        pg = PERM_ROWS // q
        for grp in range(p // pg):
            t = _dot(pm_ref[...], src_ref[grp * PERM_ROWS:(grp + 1) * PERM_ROWS, :]).astype(BF16)
            for qi in range(q):
                dst_ref[qi * p + grp * pg:qi * p + (grp + 1) * pg, :] = t[qi * pg:(qi + 1) * pg]
    else:
        qg = PERM_ROWS // p
        for grp in range(q // qg):
            blk = jnp.concatenate([src_ref[pi * q + grp * qg:pi * q + (grp + 1) * qg, :] for pi in range(p)],
                                  axis=0)
            dst_ref[grp * PERM_ROWS:(grp + 1) * PERM_ROWS, :] = _dot(pm_ref[...], blk).astype(BF16)


def _ada_kernel(c_ref, w_ref, b_ref, o_ref):
    c = c_ref[...]
    cs = c * jax.nn.sigmoid(c)
    o_ref[...] = _dot3(cs, w_ref[...]) + b_ref[...]


def _ada(c_all, w, b, tn=1024):
    nl, d, no = w.shape
    r = c_all.shape[0]
    return pl.pallas_call(
        _ada_kernel,
        grid=(nl, no // tn),
        in_specs=[
            pl.BlockSpec((r, d), lambda l, j: (0, 0)),
            pl.BlockSpec((None, d, tn), lambda l, j: (l, 0, j)),
            pl.BlockSpec((None, 1, tn), lambda l, j: (l, 0, j)),
        ],
        out_specs=pl.BlockSpec((None, r, tn), lambda l, j: (l, 0, j)),
        out_shape=jax.ShapeDtypeStruct((nl, r, no), F32),
        compiler_params=_params(("parallel", "parallel")),
        name="ada_mod",
    )(c_all, w, b.reshape(nl, 1, no))


def _norm_mm_kernel(*refs, mode, tn, perm):
    if perm:
        pm_ref, refs = refs[0], refs[1:]
        h0_ref, refs = refs[-1], refs[:-1]
    if mode == "swiglu":
        x_ref, g_ref, sh_ref, sc_ref, wg_ref, wu_ref, o_ref, h_ref = refs
    elif mode == "rope":
        x_ref, g_ref, sh_ref, sc_ref, w_ref, c_ref, s_ref, rot_ref, o_ref, h_ref = refs
    else:
        x_ref, g_ref, sh_ref, sc_ref, w_ref, b_ref, o_ref, h_ref = refs
    j = pl.program_id(2)

    @pl.when(j == 0)
    def _():
        h = _modnorm(x_ref[...].reshape(h_ref.shape), g_ref[...], sh_ref[...], sc_ref[...]).astype(BF16)
        if perm:
            h0_ref[...] = h
            _transpose_rows(h_ref, h0_ref, pm_ref, *perm)
        else:
            h_ref[...] = h

    h = h_ref[...]
    if mode == "swiglu":
        a = _dot(h, wg_ref[...])
        u = _dot(h, wu_ref[...])
        o_ref[...] = (a * jax.nn.sigmoid(a) * u).astype(o_ref.dtype).reshape(o_ref.shape)
    elif mode == "rope":
        acc = _dot(h, w_ref[...])
        part = (j // (GROUP_WIDTH // tn)) % 3

        @pl.when(part == 2)
        def _():
            o_ref[...] = acc.astype(o_ref.dtype).reshape(o_ref.shape)

        @pl.when(part != 2)
        def _():
            reps = tn // HEAD_DIM
            tabs = [t[...].reshape(acc.shape[0], HEAD_DIM) for t in (c_ref, s_ref)]
            c, s = [jnp.concatenate([t] * reps, axis=1) for t in tabs]
            accb = acc.astype(BF16)
            rw = rot_ref.shape[0]
            partner = jnp.concatenate([_dot(accb[:, k * rw:(k + 1) * rw], rot_ref[...]) for k in range(tn // rw)],
                                      axis=1)
            qs = jnp.where(part == 0, HEAD_DIM ** -0.5, 1.0).astype(F32)
            o_ref[...] = ((acc * c + partner * s) * qs).astype(o_ref.dtype).reshape(o_ref.shape)
    else:
        o_ref[...] = (_dot(h, w_ref[...]) + b_ref[...]).astype(o_ref.dtype).reshape(o_ref.shape)


def _norm_mm(x, x_spec, perm, rows, grid_rows, g, shift, scale, ws, w_layer, col0, nout, out_shape,
             out_spec, *, mode, bias=None, rope=None, tn=512):
    d = x.shape[-1]
    bsz = x.shape[0]
    cb = col0 // tn
    vec = pl.BlockSpec((None, 1, d), lambda b, i, j: (b, 0, 0))
    in_specs = [x_spec, pl.BlockSpec((1, d), lambda b, i, j: (0, 0)), vec, vec]
    in_specs += [pl.BlockSpec((None, d, tn), lambda b, i, j: (w_layer, 0, cb + j)) for _ in ws]
    args = [x, g.reshape(1, d), shift, scale, *ws]
    scratch = [pltpu.VMEM((rows, d), BF16)]
    if perm:
        in_specs.insert(0, pl.BlockSpec((PERM_ROWS, PERM_ROWS), lambda b, i, j: (0, 0)))
        args.insert(0, _perm_for(*perm))
        scratch.append(pltpu.VMEM((rows, d), BF16))
    if mode == "rope":
        tabs, tab_spec = rope
        rw = 4 * HEAD_DIM
        in_specs += [tab_spec] * 2 + [pl.BlockSpec((rw, rw), lambda b, i, j: (0, 0))]
        args += list(tabs) + [_rope_partner_matrix(rw)]
    elif mode == "bias":
        in_specs.append(pl.BlockSpec((1, tn), lambda b, i, j: (0, cb + j)))
        args.append(bias)
    return pl.pallas_call(
        functools.partial(_norm_mm_kernel, mode=mode, tn=tn, perm=perm),
        grid=(bsz, grid_rows, nout // tn),
        in_specs=in_specs,
        out_specs=out_spec,
        out_shape=out_shape,
        scratch_shapes=scratch,
        compiler_params=_params(("parallel", "parallel", "arbitrary")),
        name="norm_mm_" + mode,
    )(*args)


def _mm_res_kernel(*refs, a_mode, perm):
    if a_mode == "merge":
        (pm_ref, o0, o1, o2, l0, l1, l2, e_ref, w_ref, b_ref, x_ref, gt_ref, out_ref, a_ref, a0_ref) = refs
    elif a_mode == "perm":
        pm_ref, a_in, w_ref, b_ref, x_ref, gt_ref, out_ref, a_ref, a0_ref = refs
    else:
        a_in, w_ref, b_ref, x_ref, gt_ref, out_ref = refs
    j = pl.program_id(2)

    if a_mode != "plain":
        @pl.when(j == 0)
        def _():
            k = a_ref.shape[1]
            if a_mode == "merge":
                ls = [l[...].reshape(-1, LANES) for l in (l0, l1, l2)]
                mx = jnp.maximum(jnp.maximum(ls[0], ls[1]), ls[2])
                ws = [jnp.exp(l - mx) for l in ls]
                inv = 1.0 / (ws[0] + ws[1] + ws[2])
                num = jnp.zeros((ls[0].shape[0], k), F32)
                for w, o in zip(ws, (o0, o1, o2)):
                    hi, lo = _split(w * inv)
                    wide = _dot(hi, e_ref[...]) + _dot(lo, e_ref[...])
                    num = num + wide * o[...].reshape(-1, k).astype(F32)
                a0_ref[...] = num.astype(BF16)
            else:
                a0_ref[...] = a_in[...].reshape(-1, k).astype(BF16)
            _transpose_rows(a_ref, a0_ref, pm_ref, *perm)

        a = a_ref[...]
    else:
        a = a_in[...]
    out_ref[...] = x_ref[...] + gt_ref[...] * (_dot(a, w_ref[...]) + b_ref[...])


def _head_spread_matrix(width):
    m = np.zeros((LANES, width), np.float32)
    for hd in range(width // HEAD_DIM):
        m[hd * LSE_LANES, hd * HEAD_DIM:(hd + 1) * HEAD_DIM] = 1.0
    return jnp.asarray(m, BF16)


def _mm_res(a_list, a_specs, w, w_layer, bias, x, gate, rows, *, a_mode="plain", perm=None, tn=512):
    bsz, seq_len, d = x.shape
    k = w.shape[1]
    blk = pl.BlockSpec((None, rows, tn), lambda b, i, j: (b, i, j))
    if a_mode == "merge":
        a_list = list(a_list) + [_head_spread_matrix(k)]
        a_specs = list(a_specs) + [pl.BlockSpec((LANES, k), lambda b, i, j: (0, 0))]
    in_specs = list(a_specs) + [
        pl.BlockSpec((None, k, tn), lambda b, i, j: (w_layer, 0, j)),
        pl.BlockSpec((1, tn), lambda b, i, j: (0, j)),
        blk,
        pl.BlockSpec((None, 1, tn), lambda b, i, j: (b, 0, j)),
    ]
    args = [*a_list, w, bias, x, gate]
    scratch = []
    if a_mode != "plain":
        in_specs.insert(0, pl.BlockSpec((PERM_ROWS, PERM_ROWS), lambda b, i, j: (0, 0)))
        args.insert(0, _perm_for(*perm))
        scratch = [pltpu.VMEM((rows, k), BF16)] * 2
    return pl.pallas_call(
        functools.partial(_mm_res_kernel, a_mode=a_mode, perm=perm),
        grid=(bsz, seq_len // rows, d // tn),
        in_specs=in_specs,
        out_specs=blk,
        out_shape=jax.ShapeDtypeStruct(x.shape, F32),
        scratch_shapes=scratch,
        compiler_params=_params(("parallel", "parallel", "arbitrary")),
        name="mm_res_" + a_mode,
    )(*args)


def _final_kernel(x_ref, g_ref, sh_ref, sc_ref, o_ref):
    o_ref[...] = _modnorm(x_ref[...], g_ref[...], sh_ref[...], sc_ref[...])


def _final(x, g, shift, scale, tm=512):
    bsz, seq_len, d = x.shape
    vec = pl.BlockSpec((None, 1, d), lambda b, i: (b, 0, 0))
    blk = pl.BlockSpec((None, tm, d), lambda b, i: (b, i, 0))
    return pl.pallas_call(
        _final_kernel,
        grid=(bsz, seq_len // tm),
        in_specs=[blk, pl.BlockSpec((1, d), lambda b, i: (0, 0)), vec, vec],
        out_specs=blk,
        out_shape=jax.ShapeDtypeStruct(x.shape, F32),
        compiler_params=_params(("parallel", "parallel")),
        name="final_norm",
    )(x, g.reshape(1, d), shift, scale)


def _filter_positions(seq_len, n1):
    n = 2 * seq_len
    h = SLAB_ROWS
    half = jnp.arange(2)[:, None, None]
    s = jnp.arange(n1)[None, :, None]
    r = jnp.arange(h)[None, None, :]
    idx = ((half * h + r) * n1 + s).reshape(n)
    pos = jnp.where(idx < seq_len, idx, n - idx).astype(F32)
    sign = jnp.where(idx < seq_len, 1.0, jnp.where(idx == seq_len, 0.0, -1.0)).astype(F32)
    t = pos / max(seq_len - 1, 1)
    bands = jnp.linspace(1e-4, FILTER_BANDS - 1, FILTER_BANDS, dtype=F32)
    ang = 2.0 * math.pi * pos[:, None] * bands[None, :] / seq_len
    z = jnp.concatenate([t[:, None], jnp.cos(ang), -jnp.sin(ang)], axis=-1)
    z = jnp.pad(z, ((0, 0), (0, LANES - FILTER_EMB - 1)))
    return jnp.concatenate([z, sign[:, None]], axis=-1)


def _filter_mlp_kernel(z_ref, w1_ref, b1_ref, f_ref, w2_ref, b2_ref, o_ref):
    f = f_ref[...]
    a = jnp.sin(f[0:1, :] * (_dot3(z_ref[...], w1_ref[...]) + b1_ref[...]))
    o_ref[...] = jnp.sin(f[1:2, :] * (_dot3(a, w2_ref[...]) + b2_ref[...]))


def _filter_mlp(zf, w1, b1, freq, w2, b2, tr=512):
    n = zf.shape[0]
    hid = w1.shape[1]
    w1p = jnp.pad(w1, ((0, LANES - w1.shape[0]), (0, 0)))
    full = lambda shape: pl.BlockSpec(shape, lambda i: (0,) * len(shape))
    return pl.pallas_call(
        _filter_mlp_kernel,
        grid=(n // tr,),
        in_specs=[pl.BlockSpec((tr, LANES), lambda i: (i, 0)), full((LANES, hid)), full((1, hid)),
                  full((2, hid)), full((hid, hid)), full((1, hid))],
        out_specs=pl.BlockSpec((tr, hid), lambda i: (i, 0)),
        out_shape=jax.ShapeDtypeStruct((n, hid), F32),
        compiler_params=_params(("parallel",)),
        name="filter_mlp",
    )(zf, w1p, b1.reshape(1, hid), freq, w2, b2.reshape(1, hid))


def _dft_consts(n1, n2):
    n = n1 * n2
    k2 = np.arange(n2 // 2)[:, None]
    nn2 = np.arange(n2)[None, :]
    ph = 2.0 * np.pi * nn2 * (k2 + 0.5) / n2
    f_s1 = np.concatenate([np.cos(ph), -np.sin(ph)], axis=0)
    m = np.arange(n2 // 2)[:, None]
    kk = np.arange(n2 // 2)[None, :]
    ph3 = 2.0 * np.pi * m * (kk + 0.5) / n2
    f_s3 = (2.0 / n) * np.concatenate([np.cos(ph3), -np.sin(ph3)], axis=1)
    a = 2.0 * np.pi * np.outer(np.arange(n1), np.arange(n1)) / n1
    c, s = np.cos(a), -np.sin(a)
    f_fwd = np.block([[c, -s], [s, c]])
    f_inv = np.block([[c, s], [-s, c]])
    th = 2.0 * np.pi * (np.arange(n2 // 2)[:, None] + 0.5) * np.arange(n1)[None, :] / n
    as_bf = lambda x: jnp.asarray(x, F32).astype(BF16)
    tc, ts = jnp.asarray(np.cos(th), F32), jnp.asarray(np.sin(th), F32)
    ff, fi = jnp.asarray(f_fwd, F32), jnp.asarray(f_inv, F32)
    fl, fr = ff[None, :, :n1], ff[None, :, n1:]
    g_fwd = jnp.concatenate([fl * tc[:, None, :] - fr * ts[:, None, :],
                             fl * ts[:, None, :] + fr * tc[:, None, :]], axis=2).astype(BF16)
    it, ib = fi[None, :n1, :], fi[None, n1:, :]
    g_inv = jnp.concatenate([tc[:, :, None] * it - ts[:, :, None] * ib,
                             ts[:, :, None] * it + tc[:, :, None] * ib], axis=1).astype(BF16)
    return dict(f_s1=as_bf(f_s1[:, :n2 // 2]), f_s1_hi=as_bf(f_s1[:, n2 // 2:]), f_s3=as_bf(f_s3),
                g_fwd=g_fwd, g_inv=g_inv, n1=n1)


def _slab_tile(n1):
    return min(n1, 8)


def _dft_slab_tile(n1):
    return min(n1, 16)


HI16 = -65536
HALF_ULP16 = 0x8000


def _pack_c(re, im):
    rb = lax.bitcast_convert_type(re, jnp.int32) + HALF_ULP16
    ib = lax.bitcast_convert_type(im, jnp.int32) + HALF_ULP16
    return (rb & HI16) | lax.shift_right_logical(ib, 16)


def _pack_bf16(re, im):
    rb = lax.bitcast_convert_type(re.astype(F32), jnp.int32)
    ib = lax.bitcast_convert_type(im.astype(F32), jnp.int32)
    return rb | lax.shift_right_logical(ib, 16)


def _unpack_c(p):
    re = lax.bitcast_convert_type(p & HI16, F32)
    im = lax.bitcast_convert_type(lax.shift_left(p, 16), F32)
    return re, im


def _shift_rows(x, down):
    rows = x.shape[0]
    row = lax.broadcasted_iota(jnp.int32, x.shape, 0)
    if down:
        return jnp.where(row == 0, 0.0, pltpu.roll(x, 1, 0))
    return jnp.where(row == rows - 1, 0.0, pltpu.roll(x, rows - 1, 0))


def _short_conv_slabs(main_ref, prev_ref, next_ref, w_ref, b_ref, first, last):
    n_slabs = main_ref.shape[0]
    prev = prev_ref[0].astype(F32)
    prev = jnp.where(first, _shift_rows(prev, True), prev)
    nxt = next_ref[0].astype(F32)
    nxt = jnp.where(last, _shift_rows(nxt, False), nxt)
    w = w_ref[...]
    out = []
    for s in range(n_slabs):
        up = prev if s == 0 else main_ref[s - 1].astype(F32)
        dn = nxt if s == n_slabs - 1 else main_ref[s + 1].astype(F32)
        out.append(up * w[0:1, :] + main_ref[s].astype(F32) * w[1:2, :] + dn * w[2:3, :] + b_ref[...])
    return out


def _fft_s1_kernel(*refs, short_conv):
    re_ref, im_ref, ret_ref, imt_ref = refs[-4:]
    if short_conv:
        pm_ref, f_ref, m_ref, p_ref, n_ref, w_ref, b_ref, o_ref, u_ref = refs[:-4]
        t = pl.program_id(1)
        slabs = _short_conv_slabs(m_ref, p_ref, n_ref, w_ref, b_ref, t == 0, t == pl.num_programs(1) - 1)
    else:
        pm_ref, f_ref, m_ref, o_ref = refs[:-4]
        slabs = [m_ref[s] for s in range(m_ref.shape[0])]
    half = o_ref.shape[0]
    for s, u in enumerate(slabs):
        if short_conv:
            u_ref[s] = u.astype(u_ref.dtype)
        r = _dot(f_ref[...], u.astype(BF16))
        re_ref[s * half:(s + 1) * half, :] = r[:half].astype(BF16)
        im_ref[s * half:(s + 1) * half, :] = r[half:].astype(BF16)
    _transpose_rows(ret_ref, re_ref, pm_ref, len(slabs), half)
    _transpose_rows(imt_ref, im_ref, pm_ref, len(slabs), half)
    o_ref[...] = _pack_bf16(ret_ref[...], imt_ref[...]).reshape(o_ref.shape)


def _fft_s1(consts, src, col_block, conv=None, ct=512):
    bsz, n1, h, c = src.shape
    d = consts["d"]
    st = _dft_slab_tile(n1)
    cpb = d // ct
    f = consts["f_s1"]
    main = pl.BlockSpec((None, st, h, ct), lambda b, t, j: (b, t, 0, col_block * cpb + j))
    a_spec = pl.BlockSpec((None, h, st, ct), lambda b, t, j: (b, 0, t, j))
    a_shape = jax.ShapeDtypeStruct((bsz, h, n1, d), jnp.int32)
    fspec = pl.BlockSpec(f.shape, lambda b, t, j: (0, 0))
    pspec = pl.BlockSpec((PERM_ROWS, PERM_ROWS), lambda b, t, j: (0, 0))
    pm = _perm_for(st, h)
    scratch = [pltpu.VMEM((st * h, ct), BF16)] * 4
    if conv is None:
        return pl.pallas_call(
            functools.partial(_fft_s1_kernel, short_conv=False),
            grid=(bsz, n1 // st, cpb),
            in_specs=[pspec, fspec, main], out_specs=a_spec, out_shape=a_shape, scratch_shapes=scratch,
            compiler_params=_params(("parallel", "parallel", "parallel")), name="fft_s1",
        )(pm, f, src)
    w, b = conv
    prev = pl.BlockSpec((None, 1, h, ct), lambda b, t, j: (b, (t * st + n1 - 1) % n1, 0, col_block * cpb + j))
    nxt = pl.BlockSpec((None, 1, h, ct), lambda b, t, j: (b, ((t + 1) * st) % n1, 0, col_block * cpb + j))
    wspec = pl.BlockSpec((3, ct), lambda b, t, j: (0, col_block * cpb + j))
    bspec = pl.BlockSpec((1, ct), lambda b, t, j: (0, col_block * cpb + j))
    u_spec = pl.BlockSpec((None, st, h, ct), lambda b, t, j: (b, t, 0, j))
    return pl.pallas_call(
        functools.partial(_fft_s1_kernel, short_conv=True),
        grid=(bsz, n1 // st, cpb),
        in_specs=[pspec, fspec, main, prev, nxt, wspec, bspec],
        out_specs=[a_spec, u_spec],
        out_shape=[a_shape, jax.ShapeDtypeStruct((bsz, n1, h, d), BF16)],
        scratch_shapes=scratch,
        compiler_params=_params(("parallel", "parallel", "parallel")), name="fft_s1_conv",
    )(pm, f, src, src, src, w, b)


def _filter_s1_kernel(pm_ref, flo_ref, fhi_ref, alo_ref, ahi_ref, zlo_ref, zhi_ref, wf_ref, wb_ref, df_ref,
                      db_ref, o_ref, ss_ref, re_ref, im_ref, ret_ref, imt_ref):
    half = o_ref.shape[0]

    @pl.when(pl.program_id(2) == 0)
    def _():
        ss_ref[...] = jnp.zeros_like(ss_ref)

    n_slabs, rows = alo_ref.shape[0], alo_ref.shape[1]
    ct = o_ref.shape[2]
    ss = jnp.zeros(ss_ref.shape, F32)
    r = jnp.zeros((2 * half, n_slabs * ct), F32)
    for f_ref, a_ref, z_ref, w_ref, d_ref in ((flo_ref, alo_ref, zlo_ref, wf_ref, df_ref),
                                              (fhi_ref, ahi_ref, zhi_ref, wb_ref, db_ref)):
        z = z_ref[...].reshape(n_slabs * rows, LANES)
        t, sign = z[:, 0:1], z[:, LANES - 1:LANES]
        taps = _dot3(a_ref[...].reshape(n_slabs * rows, a_ref.shape[2]), w_ref[...])
        taps = taps * jnp.exp(-t * jnp.abs(d_ref[...])) * sign
        ss = ss + jnp.sum(taps * taps, axis=0, keepdims=True)
        tb = taps.astype(BF16)
        wide = jnp.concatenate([tb[s * rows:(s + 1) * rows, :] for s in range(n_slabs)], axis=1)
        r = r + _dot(f_ref[...], wide)
    for s in range(n_slabs):
        re_ref[s * half:(s + 1) * half, :] = r[:half, s * ct:(s + 1) * ct].astype(BF16)
        im_ref[s * half:(s + 1) * half, :] = r[half:, s * ct:(s + 1) * ct].astype(BF16)
    ss_ref[...] += ss
    _transpose_rows(ret_ref, re_ref, pm_ref, n_slabs, half)
    _transpose_rows(imt_ref, im_ref, pm_ref, n_slabs, half)
    o_ref[...] = _pack_bf16(ret_ref[...], imt_ref[...]).reshape(o_ref.shape)


def _filter_s1(consts, a2, zf, w3, decay, ct=256):
    n1, h = consts["n1"], SLAB_ROWS
    hid = a2.shape[1]
    d = decay.shape[-1]
    nct = d // ct
    st = _slab_tile(n1)
    flo, fhi = consts["f_s1"], consts["f_s1_hi"]
    dec = decay.reshape(1, HYENA_ORDER * N_DIRS * d)
    fspec = pl.BlockSpec(flo.shape, lambda o, j, t: (0, 0))
    rows = lambda half, width: pl.BlockSpec((None, st, h, width), lambda o, j, t: (half, t, 0, 0))
    wcol = lambda dirn, nrow: pl.BlockSpec((nrow, ct), lambda o, j, t: (0, (o * N_DIRS + dirn) * nct + j))
    return pl.pallas_call(
        _filter_s1_kernel,
        grid=(HYENA_ORDER, nct, n1 // st),
        in_specs=[pl.BlockSpec((PERM_ROWS, PERM_ROWS), lambda o, j, t: (0, 0)), fspec, fspec,
                  rows(0, hid), rows(1, hid), rows(0, LANES), rows(1, LANES),
                  wcol(0, hid), wcol(1, hid), wcol(0, 1), wcol(1, 1)],
        out_specs=[pl.BlockSpec((None, h, st, ct), lambda o, j, t: (o, 0, t, j)),
                   pl.BlockSpec((None, 1, ct), lambda o, j, t: (o, 0, j))],
        out_shape=[jax.ShapeDtypeStruct((HYENA_ORDER, h, n1, d), jnp.int32),
                   jax.ShapeDtypeStruct((HYENA_ORDER, 1, d), F32)],
        scratch_shapes=[pltpu.VMEM((st * h, ct), BF16)] * 4,
        compiler_params=_params(("parallel", "parallel", "arbitrary")), name="filter_s1",
    )(_perm_for(st, h), flo, fhi, a2.reshape(2, n1, h, hid), a2.reshape(2, n1, h, hid),
      zf.reshape(2, n1, h, LANES), zf.reshape(2, n1, h, LANES), w3, w3, dec, dec)


def _fft_s2f_kernel(a_ref, gf_ref, ss_ref, o_ref, *, kb):
    n1 = a_ref.shape[1]
    scale = lax.rsqrt(ss_ref[...] + EPS)

    def body(kk, carry):
        ar, ai = _unpack_c(a_ref[kk])
        x = _dot(gf_ref[kk], jnp.concatenate([ar, ai], axis=0).astype(BF16))
        o_ref[kk] = _pack_c(x[:n1] * scale, x[n1:] * scale)
        return carry

    lax.fori_loop(0, kb, body, 0, unroll=min(kb, 8))


def _fft_s2_kernel(a_ref, k_ref, gf_ref, gi_ref, o_ref, *, kb):
    nb, n1, ct = a_ref.shape[0], a_ref.shape[2], a_ref.shape[3]

    def body(kk, carry):
        parts = [_unpack_c(a_ref[b, kk]) for b in range(nb)]
        ar = jnp.concatenate([p[0] for p in parts], axis=1)
        ai = jnp.concatenate([p[1] for p in parts], axis=1)
        x = _dot(gf_ref[kk], jnp.concatenate([ar, ai], axis=0).astype(BF16))
        xr, xi = x[:n1], x[n1:]
        kr, ki = [jnp.concatenate([v] * nb, axis=1) for v in _unpack_c(k_ref[kk])]
        zr = xr * kr - xi * ki
        zi = xr * ki + xi * kr
        y = _dot(gi_ref[kk], jnp.concatenate([zr, zi], axis=0).astype(BF16))
        packed = _pack_c(y[:n1], y[n1:])
        for b in range(nb):
            o_ref[b, kk] = packed[:, b * ct:(b + 1) * ct]
        return carry

    lax.fori_loop(0, kb, body, 0, unroll=min(kb, 8))


def _s2_tiles(n1, d):
    kb = max(1, 1024 // n1)
    ct = min(d, 512)
    return kb, ct


def _fft_s2f(a, sumsq, consts):
    n_o, k2n, n1, d = a.shape
    kb, ct = _s2_tiles(n1, d)
    blk = pl.BlockSpec((None, kb, n1, ct), lambda k, j, o: (o, k, 0, j))
    return pl.pallas_call(
        functools.partial(_fft_s2f_kernel, kb=kb),
        grid=(k2n // kb, d // ct, n_o),
        in_specs=[blk, pl.BlockSpec((kb, 2 * n1, 2 * n1), lambda k, j, o: (k, 0, 0)),
                  pl.BlockSpec((None, 1, ct), lambda k, j, o: (o, 0, j))],
        out_specs=blk,
        out_shape=jax.ShapeDtypeStruct(a.shape, jnp.int32),
        compiler_params=_params(("parallel", "parallel", "parallel")),
        name="fft_s2_filter",
    )(a, consts["g_fwd"], sumsq)


def _fft_s2(a, kf, order, consts):
    bsz, k2n, n1, d = a.shape
    kb, ct = _s2_tiles(n1, d)
    nb = bsz if n1 * bsz <= SLAB_ROWS else 1
    blk = pl.BlockSpec((nb, kb, n1, ct), lambda k, j, b: (b, k, 0, j))
    mat = pl.BlockSpec((kb, 2 * n1, 2 * n1), lambda k, j, b: (k, 0, 0))
    return pl.pallas_call(
        functools.partial(_fft_s2_kernel, kb=kb),
        grid=(k2n // kb, d // ct, bsz // nb),
        in_specs=[blk, pl.BlockSpec((None, kb, n1, ct), lambda k, j, b: (order, k, 0, j)), mat, mat],
        out_specs=blk,
        out_shape=jax.ShapeDtypeStruct(a.shape, jnp.int32),
        compiler_params=_params(("parallel", "parallel", "parallel")),
        name="fft_s2",
    )(a, kf, consts["g_fwd"], consts["g_inv"])


def _fft_s3_kernel(pm_ref, f_ref, t_ref, u_ref, gm_ref, gp_ref, gn_ref, w_ref, b_ref, sk_ref, o_ref,
                   re_ref, im_ref, ret_ref, imt_ref):
    t_id = pl.program_id(1)
    half, st = t_ref.shape[0], t_ref.shape[1]
    re, im = _unpack_c(t_ref[...].reshape(half * st, t_ref.shape[2]))
    re_ref[...] = re.astype(BF16)
    im_ref[...] = im.astype(BF16)
    _transpose_rows(ret_ref, re_ref, pm_ref, half, st)
    _transpose_rows(imt_ref, im_ref, pm_ref, half, st)
    gates = _short_conv_slabs(gm_ref, gp_ref, gn_ref, w_ref, b_ref, t_id == 0, t_id == pl.num_programs(1) - 1)
    for s, gate in enumerate(gates):
        t = jnp.concatenate([ret_ref[s * half:(s + 1) * half, :], imt_ref[s * half:(s + 1) * half, :]], axis=0)
        y = _dot(f_ref[...], t)
        o_ref[s] = (gate * (y + u_ref[s].astype(F32) * sk_ref[...])).astype(o_ref.dtype)


def _fft_s3(consts, t, u, z, gate_block, conv_w, conv_b, skip, order, out_dtype, ct=512):
    bsz, h, n1, d = t.shape
    st = _dft_slab_tile(n1)
    cpb = d // ct
    f = consts["f_s3"]
    gcol = lambda j: gate_block * cpb + j
    slab = lambda idx: pl.BlockSpec((None, 1, h, ct), lambda b, tt, j: (b, idx(tt), 0, gcol(j)))
    return pl.pallas_call(
        _fft_s3_kernel,
        grid=(bsz, n1 // st, cpb),
        scratch_shapes=[pltpu.VMEM((st * h, ct), BF16)] * 4,
        in_specs=[
            pl.BlockSpec((PERM_ROWS, PERM_ROWS), lambda b, tt, j: (0, 0)),
            pl.BlockSpec(f.shape, lambda b, tt, j: (0, 0)),
            pl.BlockSpec((None, h, st, ct), lambda b, tt, j: (b, 0, tt, j)),
            pl.BlockSpec((None, st, h, ct), lambda b, tt, j: (b, tt, 0, j)),
            pl.BlockSpec((None, st, h, ct), lambda b, tt, j: (b, tt, 0, gcol(j))),
            slab(lambda tt: (tt * st + n1 - 1) % n1),
            slab(lambda tt: ((tt + 1) * st) % n1),
            pl.BlockSpec((3, ct), lambda b, tt, j: (0, gcol(j))),
            pl.BlockSpec((1, ct), lambda b, tt, j: (0, gcol(j))),
            pl.BlockSpec((None, 1, ct), lambda b, tt, j: (order, 0, j)),
        ],
        out_specs=pl.BlockSpec((None, st, h, ct), lambda b, tt, j: (b, tt, 0, j)),
        out_shape=jax.ShapeDtypeStruct((bsz, n1, h, d), out_dtype),
        compiler_params=_params(("parallel", "parallel", "parallel")),
        name="fft_s3",
    )(_perm_for(h, st), f, t, u, z, z, z, conv_w, conv_b, skip)


def _hyena_filter_spectra(seq_len, consts, fw1, fb1, ffreq, fw2, fb2, fw3, decay):
    d = decay.shape[-1]
    n1 = consts["n1"]
    zf = _filter_positions(seq_len, n1)
    a2 = _filter_mlp(zf, fw1, fb1, ffreq, fw2, fb2)
    a, sumsq = _filter_s1(consts, a2, zf, fw3, decay)
    return _fft_s2f(a, sumsq, consts)


def _hyena_mixer(x, g, shift, scale, gate, p, layer, kf, consts):
    bsz, seq_len, d = x.shape
    n1 = consts["n1"]
    h = SLAB_ROWS
    st = _slab_tile(n1)
    xv = x.reshape(bsz, h, n1, d)
    z = _norm_mm(
        xv, pl.BlockSpec((None, h, st, d), lambda b, i, j: (b, 0, i, 0)), (h, st), st * h, n1 // st, g, shift, scale,
        [p["hy_w_in"]], layer, 0, 3 * d, jax.ShapeDtypeStruct((bsz, n1, h, 3 * d), BF16),
        pl.BlockSpec((None, st, h, COL_TILE), lambda b, i, j: (b, i, 0, j)),
        mode="bias", bias=p["hy_b_in"][layer].reshape(1, 3 * d), tn=COL_TILE)
    cw, cb = p["hy_conv_w"][layer], p["hy_conv_b"][layer].reshape(1, 3 * d)
    skip = p["hy_skip"][layer].reshape(HYENA_ORDER, 1, d)
    a, u = _fft_s1(consts, z, 2, conv=(cw, cb))
    t = _fft_s2(a, kf, 0, consts)
    y1 = _fft_s3(consts, t, u, z, 0, cw, cb, skip, 0, BF16)
    a = _fft_s1(consts, y1, 0)
    t = _fft_s2(a, kf, 1, consts)
    y2 = _fft_s3(consts, t, y1, z, 1, cw, cb, skip, 1, F32)
    q = ROW_TILE // n1
    return _mm_res(
        [y2], [pl.BlockSpec((None, n1, q, d), lambda b, i, j: (b, 0, i, 0))], p["hy_w_out"], layer,
        p["hy_b_out"][layer].reshape(1, d), x, gate, ROW_TILE, a_mode="perm", perm=(n1, q), tn=COL_TILE)


def _attn_kernel(*refs, phases, ta, n_sub, to_classes):
    if to_classes:
        (pm_ref, q_ref, kp_ref, km_ref, kn_ref, vp_ref, vm_ref, vn_ref, o_ref, l_ref, kx_ref, vx_ref,
         on_ref, ot_ref, ln_ref, ls_ref, lt_ref) = refs
    else:
        q_ref, kp_ref, km_ref, kn_ref, vp_ref, vm_ref, vn_ref, o_ref, l_ref, kx_ref, vx_ref = refs
    i = pl.program_id(2)
    halo = ATT_BAND // phases
    qa = ATT_Q // phases
    ka = 2 * qa
    kx_ref[:, 0:halo] = kp_ref[...]
    kx_ref[:, halo:halo + ta] = km_ref[...]
    kx_ref[:, halo + ta:] = kn_ref[...]
    vx_ref[:, 0:halo] = vp_ref[...]
    vx_ref[:, halo:halo + ta] = vm_ref[...]
    vx_ref[:, halo + ta:] = vn_ref[...]
    row = lax.broadcasted_iota(jnp.int32, (ATT_Q, 2 * ATT_Q), 0)
    col = lax.broadcasted_iota(jnp.int32, (ATT_Q, 2 * ATT_Q), 1)
    cq, aq = row >> (qa.bit_length() - 1), row & (qa - 1)
    ck, ak = col >> (ka.bit_length() - 1), col & (ka - 1)
    delta = phases * (ak - aq) - ATT_BAND + ck - cq
    band = (delta >= -ATT_BAND) & (delta <= ATT_BAND)
    lane_head = lax.broadcasted_iota(jnp.int32, (ATT_Q, LANES), 1) >> (LSE_LANES.bit_length() - 1)
    for s in range(ta // qa):
        key_idx = phases * (i * ta + s * qa - halo + ak) + ck
        valid = band & (key_idx >= 0) & (key_idx < n_sub)
        lse_tile = jnp.zeros((ATT_Q, LANES), F32)
        for h in range(HEADS_PER_GROUP):
            cs = slice(h * HEAD_DIM, (h + 1) * HEAD_DIM)
            q = jnp.concatenate([q_ref[c, s * qa:(s + 1) * qa, cs] for c in range(phases)], axis=0)
            k = jnp.concatenate([kx_ref[c, s * qa:s * qa + ka, cs] for c in range(phases)], axis=0)
            v = jnp.concatenate([vx_ref[c, s * qa:s * qa + ka, cs] for c in range(phases)], axis=0)
            sc = lax.dot_general(q, k, (((1,), (1,)), ((), ())), preferred_element_type=F32)
            sc = jnp.where(valid, sc, NEG_BIG)
            m = jnp.max(sc, axis=-1, keepdims=True)
            pr = jnp.exp(sc - m)
            den = jnp.sum(pr, axis=-1, keepdims=True)
            o = (_dot(pr.astype(BF16), v) / den).astype(BF16)
            lse_tile = jnp.where(lane_head == h, m + jnp.log(den), lse_tile)
            if to_classes:
                on_ref[s * ATT_Q:(s + 1) * ATT_Q, cs] = o
            else:
                for c in range(phases):
                    o_ref[c, s * qa:(s + 1) * qa, cs] = o[c * qa:(c + 1) * qa]
        if to_classes:
            ln_ref[s * ATT_Q:(s + 1) * ATT_Q, :] = lse_tile
        else:
            for c in range(phases):
                l_ref[c, s * qa:(s + 1) * qa, :] = lse_tile[c * qa:(c + 1) * qa]
    if to_classes:
        na = ta // CLASSES
        _transpose_rows(ot_ref, on_ref, pm_ref, na, CLASSES)
        o_ref[...] = ot_ref[...].reshape(o_ref.shape)
        rest = ln_ref[...]
        total = jnp.zeros(rest.shape, F32)
        for _ in range(3):
            piece = rest.astype(BF16)
            rest = rest - piece.astype(F32)
            ls_ref[...] = piece
            _transpose_rows(lt_ref, ls_ref, pm_ref, na, CLASSES)
            total = total + lt_ref[...].astype(F32)
        l_ref[...] = total.reshape(l_ref.shape)


def _attn_group(qkv, col0, phases, n_sub, lead_grid, lead_block, lead_index, ta, *, to_classes=False,
                out_arr_shape=None, out_block=None, out_index=None):
    gw = GROUP_WIDTH
    halo = ATT_BAND // phases
    rows = qkv.shape[-2]
    per = ta // halo
    nblk = rows // halo
    cb = col0 // gw

    def spec(nrows, ridx, part):
        return pl.BlockSpec(tuple(lead_block) + (nrows, gw),
                            lambda b, rho, i: tuple(lead_index(b, rho)) + (ridx(i), cb + part))

    main = lambda part: spec(ta, lambda i: i, part)
    prev = lambda part: spec(halo, lambda i: jnp.maximum(i * per - 1, 0), part)
    nxt = lambda part: spec(halo, lambda i: jnp.minimum((i + 1) * per, nblk - 1), part)
    bsz = qkv.shape[0]
    in_specs = [main(0), prev(1), main(1), nxt(1), prev(2), main(2), nxt(2)]
    args = [qkv] * 7
    if to_classes:
        assert ta == PERM_ROWS
        out_specs = [pl.BlockSpec(out_block + (gw,), out_index), pl.BlockSpec(out_block + (LANES,), out_index)]
        scratch_extra = [pltpu.VMEM((ta, gw), BF16)] * 2 + [pltpu.VMEM((ta, LANES), F32)] + \
                        [pltpu.VMEM((ta, LANES), BF16)] * 2
        in_specs.insert(0, pl.BlockSpec((PERM_ROWS, PERM_ROWS), lambda b, rho, i: (0, 0)))
        args.insert(0, _perm_for(ta // CLASSES, CLASSES))
    else:
        out_arr_shape = qkv.shape[:-1]
        oidx = lambda b, rho, i: tuple(lead_index(b, rho)) + (i, 0)
        out_specs = [pl.BlockSpec(tuple(lead_block) + (ta, gw), oidx),
                     pl.BlockSpec(tuple(lead_block) + (ta, LANES), oidx)]
        scratch_extra = []
    kx_shape = (phases, ta + 2 * halo, gw)
    return pl.pallas_call(
        functools.partial(_attn_kernel, phases=phases, ta=ta, n_sub=n_sub, to_classes=to_classes),
        grid=(bsz, lead_grid, rows // ta),
        in_specs=in_specs,
        out_specs=out_specs,
        out_shape=[jax.ShapeDtypeStruct(tuple(out_arr_shape) + (gw,), BF16),
                   jax.ShapeDtypeStruct(tuple(out_arr_shape) + (LANES,), F32)],
        scratch_shapes=[pltpu.VMEM(kx_shape, BF16)] * 2 + scratch_extra,
        compiler_params=_params(("parallel", "parallel", "parallel")),
        name="attn_p%d" % phases + ("_cls" if to_classes else ""),
    )(*args)


def _rope_tables(pos):
    half = ROT_DIM // 2
    inv = ROPE_THETA ** (-jnp.arange(0, ROT_DIM, 2, dtype=F32) / ROT_DIM)
    ang = pos.astype(F32)[:, None] * inv[None, :]
    cos, sin = jnp.cos(ang), jnp.sin(ang)
    n = pos.shape[0]
    rest = HEAD_DIM - ROT_DIM
    c = jnp.concatenate([cos, cos, jnp.ones((n, rest), F32)], axis=1)
    s = jnp.concatenate([sin, sin, jnp.zeros((n, rest), F32)], axis=1)
    return c, s


def _rope_partner_matrix(width):
    half = ROT_DIM // 2
    m = np.zeros((width, width), np.float32)
    for base in range(0, width, HEAD_DIM):
        for k in range(half):
            m[base + k + half, base + k] = -1.0
            m[base + k, base + k + half] = 1.0
    return jnp.asarray(m, BF16)


def _attn_mixer(x, g, shift, scale, gate, p, layer):
    bsz, seq_len, d = x.shape
    gw = GROUP_WIDTH
    nc = seq_len // CLASSES
    ca = ROW_TILE // CLASSES
    w_in = p["at_w_in"]
    tabs = _rope_tables(jnp.arange(seq_len))
    qkv0 = _norm_mm(
        x, pl.BlockSpec((None, ROW_TILE, d), lambda b, i, j: (b, i, 0)), None, ROW_TILE, seq_len // ROW_TILE,
        g, shift, scale, [w_in], layer, 0, 3 * gw, jax.ShapeDtypeStruct((bsz, seq_len, 3 * gw), BF16),
        pl.BlockSpec((None, ROW_TILE, COL_TILE), lambda b, i, j: (b, i, j)), mode="rope",
        rope=(tabs, pl.BlockSpec((ROW_TILE, HEAD_DIM), lambda b, i, j: (i, 0))), tn=COL_TILE)
    pos_c = (jnp.arange(nc)[None, :] * CLASSES + jnp.arange(CLASSES)[:, None]).reshape(-1)
    tabs_c = [t.reshape(CLASSES, nc, HEAD_DIM) for t in _rope_tables(pos_c)]
    qkv12 = _norm_mm(
        x.reshape(bsz, nc, CLASSES, d), pl.BlockSpec((None, ca, CLASSES, d), lambda b, i, j: (b, i, 0, 0)),
        (ca, CLASSES), ROW_TILE, nc // ca, g, shift, scale, [w_in], layer, 3 * gw, 6 * gw,
        jax.ShapeDtypeStruct((bsz, CLASSES, nc, 6 * gw), BF16),
        pl.BlockSpec((None, CLASSES, ca, COL_TILE), lambda b, i, j: (b, 0, i, j)), mode="rope",
        rope=(tabs_c, pl.BlockSpec((CLASSES, ca, HEAD_DIM), lambda b, i, j: (0, i, 0))), tn=COL_TILE)
    cls_shape = (bsz, CLASSES, nc)
    ta0 = PERM_ROWS
    o0, l0 = _attn_group(
        qkv0.reshape(bsz, 1, seq_len, 3 * gw), 0, 1, seq_len, 1, (None, 1), lambda b, rho: (b, 0), ta0,
        to_classes=True, out_arr_shape=cls_shape, out_block=(None, CLASSES, ta0 // CLASSES),
        out_index=lambda b, rho, i: (b, 0, i, 0))
    dil1 = ATTN_PATTERNS[1][1]
    ph = CLASSES // dil1
    ta1 = min(128, nc)
    o1, l1 = _attn_group(
        qkv12.reshape(bsz, ph, dil1, nc, 6 * gw), 0, ph, seq_len // dil1, dil1, (None, ph, None),
        lambda b, rho: (b, 0, rho), ta1)
    ta2 = min(512, nc)
    o2, l2 = _attn_group(
        qkv12.reshape(bsz, CLASSES, 1, nc, 6 * gw), 3 * gw, 1, nc, CLASSES, (None, None, 1),
        lambda b, rho: (b, rho, 0), ta2)
    os_ = [o.reshape(cls_shape + (gw,)) for o in (o0, o1, o2)]
    ls = [l.reshape(cls_shape + (LANES,)) for l in (l0, l1, l2)]
    blk = lambda width: pl.BlockSpec((None, CLASSES, ca, width), lambda b, i, j: (b, 0, i, 0))
    return _mm_res(os_ + ls, [blk(gw)] * 3 + [blk(LANES)] * 3, p["at_w_out"], layer, jnp.zeros((1, d), F32),
                   x, gate, ROW_TILE, a_mode="merge", perm=(CLASSES, ca), tn=COL_TILE)


def _ffn(x, g, shift, scale, gate, p, layer):
    bsz, seq_len, d = x.shape
    dff = p["ffn_w_gate"].shape[-1]
    tiles = seq_len // ROW_TILE
    xs = pl.BlockSpec((None, ROW_TILE, d), lambda b, i, j: (b, i, 0))
    hs = pl.BlockSpec((None, ROW_TILE, 512), lambda b, i, j: (b, i, j))
    hmid = _norm_mm(x, xs, None, ROW_TILE, tiles, g, shift, scale, [p["ffn_w_gate"], p["ffn_w_up"]], layer, 0,
                    dff, jax.ShapeDtypeStruct((bsz, seq_len, dff), BF16), hs, mode="swiglu")
    return _mm_res([hmid], [pl.BlockSpec((None, ROW_TILE, dff), lambda b, i, j: (b, i, 0))],
                   p["ffn_w_down"], layer, jnp.zeros((1, d), F32), x, gate, ROW_TILE)


def _encoder(x, mods, final_mod, p):
    bsz, seq_len, d = x.shape
    n1 = 2 * seq_len // DFT_N2
    consts = _dft_consts(n1, DFT_N2)
    consts["d"] = d
    for i in range(DEPTH):
        sh_m, sc_m, g_m, sh_f, sc_f, g_f = [mods[i][:, None, k * d:(k + 1) * d] for k in range(6)]
        j = i // 2
        if i % 2 == 0:
            kf = _hyena_filter_spectra(seq_len, consts, p["hy_fw1"][j], p["hy_fb1"][j], p["hy_ffreq"][j],
                                       p["hy_fw2"][j], p["hy_fb2"][j], p["hy_fw3"][j], p["hy_decay"][j])
            x = _hyena_mixer(x, p["norm_mix"][i], sh_m, sc_m, g_m, p, j, kf, consts)
        else:
            x = _attn_mixer(x, p["norm_mix"][i], sh_m, sc_m, g_m, p, j)
        x = _ffn(x, p["norm_ffn"][i], sh_f, sc_f, g_f, p, i)
    sh, sc = final_mod[:, None, :d], final_mod[:, None, d:]
    return _final(x, p["final_norm"], sh, sc)


def kernel(x_prompt, x_sample, c_prompt, c_sample, ada_w, ada_b, norm_mix, norm_ffn, hy_w_in, hy_b_in, hy_conv_w, hy_conv_b, hy_fw1, hy_fb1, hy_ffreq, hy_fw2, hy_fb2, hy_fw3, hy_decay, hy_skip, hy_w_out, hy_b_out, at_w_in, at_w_out, ffn_w_gate, ffn_w_up, ffn_w_down, final_norm, final_ada_w, final_ada_b):
    d = x_prompt.shape[-1]
    bp, bs = c_prompt.shape[0], c_sample.shape[0]
    pad = -(bp + bs) % (2 * SUBLANES)
    c_all = jnp.concatenate([c_prompt, c_sample, jnp.zeros((pad, d), F32)], axis=0)
    mods = _ada(c_all, ada_w, ada_b)
    fmod = _ada(c_all, final_ada_w[None], final_ada_b[None])[0]
    p = dict(norm_mix=norm_mix, norm_ffn=norm_ffn,
             hy_w_in=hy_w_in.astype(BF16), hy_b_in=hy_b_in, hy_conv_w=hy_conv_w, hy_conv_b=hy_conv_b,
             hy_fw1=hy_fw1, hy_fb1=hy_fb1, hy_ffreq=hy_ffreq, hy_fw2=hy_fw2, hy_fb2=hy_fb2, hy_fw3=hy_fw3,
             hy_decay=hy_decay, hy_skip=hy_skip, hy_w_out=hy_w_out.astype(BF16), hy_b_out=hy_b_out,
             at_w_in=at_w_in.astype(BF16), at_w_out=at_w_out.astype(BF16),
             ffn_w_gate=ffn_w_gate.astype(BF16), ffn_w_up=ffn_w_up.astype(BF16),
             ffn_w_down=ffn_w_down.astype(BF16), final_norm=final_norm)
    y_prompt = _encoder(x_prompt, mods[:, :bp], fmod[:bp], p)
    y_sample = _encoder(x_sample, mods[:, bp:bp + bs], fmod[bp:bp + bs], p)
    return (y_prompt, y_sample)
```

```python
import functools
import math

import numpy as np
import jax
import jax.numpy as jnp
from jax import lax
from jax.experimental import pallas as pl
from jax.experimental.pallas import tpu as pltpu

F32 = jnp.float32
BF16 = jnp.bfloat16
EPS = 1e-6

DEPTH = 4
HYENA_ORDER = 2
N_DIRS = 2
FILTER_BANDS = 16
FILTER_EMB = 1 + 2 * FILTER_BANDS
ATTN_PATTERNS = ((128, 1), (512, 4), (2048, 16))
HEADS_PER_GROUP = 8
HEAD_DIM = 128
GROUP_WIDTH = HEADS_PER_GROUP * HEAD_DIM
ROT_DIM = HEAD_DIM // 4
ROPE_THETA = 500000.0

LANES = 128
SUBLANES = 8
VMEM_LIMIT_BYTES = 56 * 1024 * 1024

DFT_N2 = 256
SLAB_ROWS = DFT_N2 // 2
ATT_BAND = 64
ATT_Q = 2 * ATT_BAND
CLASSES = 16
LSE_LANES = LANES // HEADS_PER_GROUP
ROW_TILE = 1024
COL_TILE = 1024
NEG_BIG = -1e30


def _params(sem):
    return pltpu.CompilerParams(dimension_semantics=sem, vmem_limit_bytes=VMEM_LIMIT_BYTES)


def _dot(a, b):
    return jnp.dot(a, b, preferred_element_type=F32)


def _split(a):
    hi = a.astype(BF16)
    lo = (a - hi.astype(F32)).astype(BF16)
    return hi, lo


def _dot3(a, b):
    ah, al = _split(a)
    bh, bl = _split(b)
    return _dot(ah, bh) + _dot(al, bh) + _dot(ah, bl)


def _modnorm(x, g, shift, scale):
    ms = jnp.mean(x * x, axis=-1, keepdims=True)
    return (x * lax.rsqrt(ms + EPS)) * (g * (1.0 + scale)) + shift


PERM_ROWS = 256


def _perm_matrix(p, q):
    m = np.zeros((PERM_ROWS, PERM_ROWS), np.float32)
    pi, qi = np.meshgrid(np.arange(p), np.arange(q), indexing="ij")
    m[(qi * p + pi).ravel(), (pi * q + qi).ravel()] = 1.0
    return jnp.asarray(m, BF16)


def _perm_for(p, q):
    assert (q <= 16 and p % (PERM_ROWS // q) == 0) or (p <= 16 and q % (PERM_ROWS // p) == 0), (p, q)
    return _perm_matrix(PERM_ROWS // q, q) if q <= 16 else _perm_matrix(p, PERM_ROWS // p)


def _transpose_rows(dst_ref, src_ref, pm_ref, p, q):
    if q <= 16:
        pg = PERM_ROWS // q
        for grp in range(p // pg):
            t = _dot(pm_ref[...], src_ref[grp * PERM_ROWS:(grp + 1) * PERM_ROWS, :]).astype(BF16)
            for qi in range(q):
                dst_ref[qi * p + grp * pg:qi * p + (grp + 1) * pg, :] = t[qi * pg:(qi + 1) * pg]
    else:
        qg = PERM_ROWS // p
        for grp in range(q // qg):
            blk = jnp.concatenate([src_ref[pi * q + grp * qg:pi * q + (grp + 1) * qg, :] for pi in range(p)],
                                  axis=0)
            dst_ref[grp * PERM_ROWS:(grp + 1) * PERM_ROWS, :] = _dot(pm_ref[...], blk).astype(BF16)


def _ada_kernel(c_ref, w_ref, b_ref, o_ref):
    c = c_ref[...]
    cs = c * jax.nn.sigmoid(c)
    o_ref[...] = _dot3(cs, w_ref[...]) + b_ref[...]


def _ada(c_all, w, b, tn=1024):
    nl, d, no = w.shape
    r = c_all.shape[0]
    return pl.pallas_call(
        _ada_kernel,
        grid=(nl, no // tn),
        in_specs=[
            pl.BlockSpec((r, d), lambda l, j: (0, 0)),
            pl.BlockSpec((None, d, tn), lambda l, j: (l, 0, j)),
            pl.BlockSpec((None, 1, tn), lambda l, j: (l, 0, j)),
        ],
        out_specs=pl.BlockSpec((None, r, tn), lambda l, j: (l, 0, j)),
        out_shape=jax.ShapeDtypeStruct((nl, r, no), F32),
        compiler_params=_params(("parallel", "parallel")),
        name="ada_mod",
    )(c_all, w, b.reshape(nl, 1, no))


def _norm_mm_kernel(*refs, mode, tn, perm):
    if perm:
        pm_ref, refs = refs[0], refs[1:]
        h0_ref, refs = refs[-1], refs[:-1]
    if mode == "swiglu":
        x_ref, g_ref, sh_ref, sc_ref, wg_ref, wu_ref, o_ref, h_ref = refs
    elif mode == "rope":
        x_ref, g_ref, sh_ref, sc_ref, w_ref, c_ref, s_ref, rot_ref, o_ref, h_ref = refs
    else:
        x_ref, g_ref, sh_ref, sc_ref, w_ref, b_ref, o_ref, h_ref = refs
    j = pl.program_id(2)

    @pl.when(j == 0)
    def _():
        h = _modnorm(x_ref[...].reshape(h_ref.shape), g_ref[...], sh_ref[...], sc_ref[...]).astype(BF16)
        if perm:
            h0_ref[...] = h
            _transpose_rows(h_ref, h0_ref, pm_ref, *perm)
        else:
            h_ref[...] = h

    h = h_ref[...]
    if mode == "swiglu":
        a = _dot(h, wg_ref[...])
        u = _dot(h, wu_ref[...])
        o_ref[...] = (a * jax.nn.sigmoid(a) * u).astype(o_ref.dtype).reshape(o_ref.shape)
    elif mode == "rope":
        acc = _dot(h, w_ref[...])
        part = (j // (GROUP_WIDTH // tn)) % 3

        @pl.when(part == 2)
        def _():
            o_ref[...] = acc.astype(o_ref.dtype).reshape(o_ref.shape)

        @pl.when(part != 2)
        def _():
            reps = tn // HEAD_DIM
            tabs = [t[...].reshape(acc.shape[0], HEAD_DIM) for t in (c_ref, s_ref)]
            c, s = [jnp.concatenate([t] * reps, axis=1) for t in tabs]
            accb = acc.astype(BF16)
            rw = rot_ref.shape[0]
            partner = jnp.concatenate([_dot(accb[:, k * rw:(k + 1) * rw], rot_ref[...]) for k in range(tn // rw)],
                                      axis=1)
            qs = jnp.where(part == 0, HEAD_DIM ** -0.5, 1.0).astype(F32)
            o_ref[...] = ((acc * c + partner * s) * qs).astype(o_ref.dtype).reshape(o_ref.shape)
    else:
        o_ref[...] = (_dot(h, w_ref[...]) + b_ref[...]).astype(o_ref.dtype).reshape(o_ref.shape)


def _norm_mm(x, x_spec, perm, rows, grid_rows, g, shift, scale, ws, w_layer, col0, nout, out_shape,
             out_spec, *, mode, bias=None, rope=None, tn=512):
    d = x.shape[-1]
    bsz = x.shape[0]
    cb = col0 // tn
    vec = pl.BlockSpec((None, 1, d), lambda b, i, j: (b, 0, 0))
    in_specs = [x_spec, pl.BlockSpec((1, d), lambda b, i, j: (0, 0)), vec, vec]
    in_specs += [pl.BlockSpec((None, d, tn), lambda b, i, j: (w_layer, 0, cb + j)) for _ in ws]
    args = [x, g.reshape(1, d), shift, scale, *ws]
    scratch = [pltpu.VMEM((rows, d), BF16)]
    if perm:
        in_specs.insert(0, pl.BlockSpec((PERM_ROWS, PERM_ROWS), lambda b, i, j: (0, 0)))
        args.insert(0, _perm_for(*perm))
        scratch.append(pltpu.VMEM((rows, d), BF16))
    if mode == "rope":
        tabs, tab_spec = rope
        rw = 4 * HEAD_DIM
        in_specs += [tab_spec] * 2 + [pl.BlockSpec((rw, rw), lambda b, i, j: (0, 0))]
        args += list(tabs) + [_rope_partner_matrix(rw)]
    elif mode == "bias":
        in_specs.append(pl.BlockSpec((1, tn), lambda b, i, j: (0, cb + j)))
        args.append(bias)
    return pl.pallas_call(
        functools.partial(_norm_mm_kernel, mode=mode, tn=tn, perm=perm),
        grid=(bsz, grid_rows, nout // tn),
        in_specs=in_specs,
        out_specs=out_spec,
        out_shape=out_shape,
        scratch_shapes=scratch,
        compiler_params=_params(("parallel", "parallel", "arbitrary")),
        name="norm_mm_" + mode,
    )(*args)


def _mm_res_kernel(*refs, a_mode, perm):
    if a_mode == "merge":
        (pm_ref, o0, o1, o2, l0, l1, l2, e_ref, w_ref, b_ref, x_ref, gt_ref, out_ref, a_ref, a0_ref) = refs
    elif a_mode == "perm":
        pm_ref, a_in, w_ref, b_ref, x_ref, gt_ref, out_ref, a_ref, a0_ref = refs
    else:
        a_in, w_ref, b_ref, x_ref, gt_ref, out_ref = refs
    j = pl.program_id(2)

    if a_mode != "plain":
        @pl.when(j == 0)
        def _():
            k = a_ref.shape[1]
            if a_mode == "merge":
                ls = [l[...].reshape(-1, LANES) for l in (l0, l1, l2)]
                mx = jnp.maximum(jnp.maximum(ls[0], ls[1]), ls[2])
                ws = [jnp.exp(l - mx) for l in ls]
                inv = 1.0 / (ws[0] + ws[1] + ws[2])
                num = jnp.zeros((ls[0].shape[0], k), F32)
                for w, o in zip(ws, (o0, o1, o2)):
                    hi, lo = _split(w * inv)
                    wide = _dot(hi, e_ref[...]) + _dot(lo, e_ref[...])
                    num = num + wide * o[...].reshape(-1, k).astype(F32)
                a0_ref[...] = num.astype(BF16)
            else:
                a0_ref[...] = a_in[...].reshape(-1, k).astype(BF16)
            _transpose_rows(a_ref, a0_ref, pm_ref, *perm)

        a = a_ref[...]
    else:
        a = a_in[...]
    out_ref[...] = x_ref[...] + gt_ref[...] * (_dot(a, w_ref[...]) + b_ref[...])


def _head_spread_matrix(width):
    m = np.zeros((LANES, width), np.float32)
    for hd in range(width // HEAD_DIM):
        m[hd * LSE_LANES, hd * HEAD_DIM:(hd + 1) * HEAD_DIM] = 1.0
    return jnp.asarray(m, BF16)


def _mm_res(a_list, a_specs, w, w_layer, bias, x, gate, rows, *, a_mode="plain", perm=None, tn=512):
    bsz, seq_len, d = x.shape
    k = w.shape[1]
    blk = pl.BlockSpec((None, rows, tn), lambda b, i, j: (b, i, j))
    if a_mode == "merge":
        a_list = list(a_list) + [_head_spread_matrix(k)]
        a_specs = list(a_specs) + [pl.BlockSpec((LANES, k), lambda b, i, j: (0, 0))]
    in_specs = list(a_specs) + [
        pl.BlockSpec((None, k, tn), lambda b, i, j: (w_layer, 0, j)),
        pl.BlockSpec((1, tn), lambda b, i, j: (0, j)),
        blk,
        pl.BlockSpec((None, 1, tn), lambda b, i, j: (b, 0, j)),
    ]
    args = [*a_list, w, bias, x, gate]
    scratch = []
    if a_mode != "plain":
        in_specs.insert(0, pl.BlockSpec((PERM_ROWS, PERM_ROWS), lambda b, i, j: (0, 0)))
        args.insert(0, _perm_for(*perm))
        scratch = [pltpu.VMEM((rows, k), BF16)] * 2
    return pl.pallas_call(
        functools.partial(_mm_res_kernel, a_mode=a_mode, perm=perm),
        grid=(bsz, seq_len // rows, d // tn),
        in_specs=in_specs,
        out_specs=blk,
        out_shape=jax.ShapeDtypeStruct(x.shape, F32),
        scratch_shapes=scratch,
        compiler_params=_params(("parallel", "parallel", "arbitrary")),
        name="mm_res_" + a_mode,
    )(*args)


def _final_kernel(x_ref, g_ref, sh_ref, sc_ref, o_ref):
    o_ref[...] = _modnorm(x_ref[...], g_ref[...], sh_ref[...], sc_ref[...])


def _final(x, g, shift, scale, tm=ROW_TILE):
    bsz, seq_len, d = x.shape
    vec = pl.BlockSpec((None, 1, d), lambda b, i: (b, 0, 0))
    blk = pl.BlockSpec((None, tm, d), lambda b, i: (b, i, 0))
    return pl.pallas_call(
        _final_kernel,
        grid=(bsz, seq_len // tm),
        in_specs=[blk, pl.BlockSpec((1, d), lambda b, i: (0, 0)), vec, vec],
        out_specs=blk,
        out_shape=jax.ShapeDtypeStruct(x.shape, F32),
        compiler_params=_params(("parallel", "parallel")),
        name="final_norm",
    )(x, g.reshape(1, d), shift, scale)


def _filter_positions(seq_len, n1):
    n = 2 * seq_len
    h = SLAB_ROWS
    half = jnp.arange(2)[:, None, None]
    s = jnp.arange(n1)[None, :, None]
    r = jnp.arange(h)[None, None, :]
    idx = ((half * h + r) * n1 + s).reshape(n)
    pos = jnp.where(idx < seq_len, idx, n - idx).astype(F32)
    sign = jnp.where(idx < seq_len, 1.0, jnp.where(idx == seq_len, 0.0, -1.0)).astype(F32)
    t = pos / max(seq_len - 1, 1)
    bands = jnp.linspace(1e-4, FILTER_BANDS - 1, FILTER_BANDS, dtype=F32)
    ang = 2.0 * math.pi * pos[:, None] * bands[None, :] / seq_len
    z = jnp.concatenate([t[:, None], jnp.cos(ang), -jnp.sin(ang)], axis=-1)
    z = jnp.pad(z, ((0, 0), (0, LANES - FILTER_EMB - 1)))
    return jnp.concatenate([z, sign[:, None]], axis=-1)


def _filter_mlp_kernel(z_ref, w1_ref, b1_ref, f_ref, w2_ref, b2_ref, o_ref):
    f = f_ref[...]
    a = jnp.sin(f[0:1, :] * (_dot3(z_ref[...], w1_ref[...]) + b1_ref[...]))
    o_ref[...] = jnp.sin(f[1:2, :] * (_dot3(a, w2_ref[...]) + b2_ref[...]))


def _filter_mlp(zf, w1, b1, freq, w2, b2, tr=512):
    n = zf.shape[0]
    hid = w1.shape[1]
    w1p = jnp.pad(w1, ((0, LANES - w1.shape[0]), (0, 0)))
    full = lambda shape: pl.BlockSpec(shape, lambda i: (0,) * len(shape))
    return pl.pallas_call(
        _filter_mlp_kernel,
        grid=(n // tr,),
        in_specs=[pl.BlockSpec((tr, LANES), lambda i: (i, 0)), full((LANES, hid)), full((1, hid)),
                  full((2, hid)), full((hid, hid)), full((1, hid))],
        out_specs=pl.BlockSpec((tr, hid), lambda i: (i, 0)),
        out_shape=jax.ShapeDtypeStruct((n, hid), F32),
        compiler_params=_params(("parallel",)),
        name="filter_mlp",
    )(zf, w1p, b1.reshape(1, hid), freq, w2, b2.reshape(1, hid))


def _dft_consts(n1, n2):
    n = n1 * n2
    k2 = np.arange(n2 // 2)[:, None]
    nn2 = np.arange(n2)[None, :]
    ph = 2.0 * np.pi * nn2 * (k2 + 0.5) / n2
    f_s1 = np.concatenate([np.cos(ph), -np.sin(ph)], axis=0)
    m = np.arange(n2 // 2)[:, None]
    kk = np.arange(n2 // 2)[None, :]
    ph3 = 2.0 * np.pi * m * (kk + 0.5) / n2
    f_s3 = (2.0 / n) * np.concatenate([np.cos(ph3), -np.sin(ph3)], axis=1)
    a = 2.0 * np.pi * np.outer(np.arange(n1), np.arange(n1)) / n1
    c, s = np.cos(a), -np.sin(a)
    f_fwd = np.block([[c, -s], [s, c]])
    f_inv = np.block([[c, s], [-s, c]])
    th = 2.0 * np.pi * (np.arange(n2 // 2)[:, None] + 0.5) * np.arange(n1)[None, :] / n
    as_bf = lambda x: jnp.asarray(x, F32).astype(BF16)
    tc, ts = jnp.asarray(np.cos(th), F32), jnp.asarray(np.sin(th), F32)
    ff, fi = jnp.asarray(f_fwd, F32), jnp.asarray(f_inv, F32)
    fl, fr = ff[None, :, :n1], ff[None, :, n1:]
    g_fwd = jnp.concatenate([fl * tc[:, None, :] - fr * ts[:, None, :],
                             fl * ts[:, None, :] + fr * tc[:, None, :]], axis=2).astype(BF16)
    it, ib = fi[None, :n1, :], fi[None, n1:, :]
    g_inv = jnp.concatenate([tc[:, :, None] * it - ts[:, :, None] * ib,
                             ts[:, :, None] * it + tc[:, :, None] * ib], axis=1).astype(BF16)
    return dict(f_s1=as_bf(f_s1[:, :n2 // 2]), f_s1_hi=as_bf(f_s1[:, n2 // 2:]), f_s3=as_bf(f_s3),
                g_fwd=g_fwd, g_inv=g_inv, n1=n1)


def _slab_tile(n1):
    return min(n1, 8)


def _dft_slab_tile(n1):
    return min(n1, 16)


HI16 = -65536
HALF_ULP16 = 0x8000


def _pack_c(re, im):
    rb = lax.bitcast_convert_type(re, jnp.int32) + HALF_ULP16
    ib = lax.bitcast_convert_type(im, jnp.int32) + HALF_ULP16
    return (rb & HI16) | lax.shift_right_logical(ib, 16)


def _pack_bf16(re, im):
    rb = lax.bitcast_convert_type(re.astype(F32), jnp.int32)
    ib = lax.bitcast_convert_type(im.astype(F32), jnp.int32)
    return rb | lax.shift_right_logical(ib, 16)


def _unpack_c(p):
    re = lax.bitcast_convert_type(p & HI16, F32)
    im = lax.bitcast_convert_type(lax.shift_left(p, 16), F32)
    return re, im


def _shift_rows(x, down):
    rows = x.shape[0]
    row = lax.broadcasted_iota(jnp.int32, x.shape, 0)
    if down:
        return jnp.where(row == 0, 0.0, pltpu.roll(x, 1, 0))
    return jnp.where(row == rows - 1, 0.0, pltpu.roll(x, rows - 1, 0))


def _short_conv_slabs(main_ref, prev_ref, next_ref, w_ref, b_ref, first, last):
    n_slabs = main_ref.shape[0]
    prev = prev_ref[0].astype(F32)
    prev = jnp.where(first, _shift_rows(prev, True), prev)
    nxt = next_ref[0].astype(F32)
    nxt = jnp.where(last, _shift_rows(nxt, False), nxt)
    w = w_ref[...]
    out = []
    for s in range(n_slabs):
        up = prev if s == 0 else main_ref[s - 1].astype(F32)
        dn = nxt if s == n_slabs - 1 else main_ref[s + 1].astype(F32)
        out.append(up * w[0:1, :] + main_ref[s].astype(F32) * w[1:2, :] + dn * w[2:3, :] + b_ref[...])
    return out


def _fft_s1_kernel(*refs, short_conv):
    re_ref, im_ref, ret_ref, imt_ref = refs[-4:]
    if short_conv:
        pm_ref, f_ref, m_ref, p_ref, n_ref, w_ref, b_ref, o_ref, u_ref = refs[:-4]
        t = pl.program_id(1)
        slabs = _short_conv_slabs(m_ref, p_ref, n_ref, w_ref, b_ref, t == 0, t == pl.num_programs(1) - 1)
    else:
        pm_ref, f_ref, m_ref, o_ref = refs[:-4]
        slabs = [m_ref[s] for s in range(m_ref.shape[0])]
    half = o_ref.shape[0]
    for s, u in enumerate(slabs):
        if short_conv:
            u_ref[s] = u.astype(u_ref.dtype)
        r = _dot(f_ref[...], u.astype(BF16))
        re_ref[s * half:(s + 1) * half, :] = r[:half].astype(BF16)
        im_ref[s * half:(s + 1) * half, :] = r[half:].astype(BF16)
    _transpose_rows(ret_ref, re_ref, pm_ref, len(slabs), half)
    _transpose_rows(imt_ref, im_ref, pm_ref, len(slabs), half)
    o_ref[...] = _pack_bf16(ret_ref[...], imt_ref[...]).reshape(o_ref.shape)


def _fft_s1(consts, src, col_block, conv=None, ct=512):
    bsz, n1, h, c = src.shape
    d = consts["d"]
    st = _dft_slab_tile(n1)
    cpb = d // ct
    f = consts["f_s1"]
    main = pl.BlockSpec((None, st, h, ct), lambda b, t, j: (b, t, 0, col_block * cpb + j))
    a_spec = pl.BlockSpec((None, h, st, ct), lambda b, t, j: (b, 0, t, j))
    a_shape = jax.ShapeDtypeStruct((bsz, h, n1, d), jnp.int32)
    fspec = pl.BlockSpec(f.shape, lambda b, t, j: (0, 0))
    pspec = pl.BlockSpec((PERM_ROWS, PERM_ROWS), lambda b, t, j: (0, 0))
    pm = _perm_for(st, h)
    scratch = [pltpu.VMEM((st * h, ct), BF16)] * 4
    if conv is None:
        return pl.pallas_call(
            functools.partial(_fft_s1_kernel, short_conv=False),
            grid=(bsz, n1 // st, cpb),
            in_specs=[pspec, fspec, main], out_specs=a_spec, out_shape=a_shape, scratch_shapes=scratch,
            compiler_params=_params(("parallel", "parallel", "parallel")), name="fft_s1",
        )(pm, f, src)
    w, b = conv
    prev = pl.BlockSpec((None, 1, h, ct), lambda b, t, j: (b, (t * st + n1 - 1) % n1, 0, col_block * cpb + j))
    nxt = pl.BlockSpec((None, 1, h, ct), lambda b, t, j: (b, ((t + 1) * st) % n1, 0, col_block * cpb + j))
    wspec = pl.BlockSpec((3, ct), lambda b, t, j: (0, col_block * cpb + j))
    bspec = pl.BlockSpec((1, ct), lambda b, t, j: (0, col_block * cpb + j))
    u_spec = pl.BlockSpec((None, st, h, ct), lambda b, t, j: (b, t, 0, j))
    return pl.pallas_call(
        functools.partial(_fft_s1_kernel, short_conv=True),
        grid=(bsz, n1 // st, cpb),
        in_specs=[pspec, fspec, main, prev, nxt, wspec, bspec],
        out_specs=[a_spec, u_spec],
        out_shape=[a_shape, jax.ShapeDtypeStruct((bsz, n1, h, d), BF16)],
        scratch_shapes=scratch,
        compiler_params=_params(("parallel", "parallel", "parallel")), name="fft_s1_conv",
    )(pm, f, src, src, src, w, b)


def _filter_s1_kernel(pm_ref, flo_ref, fhi_ref, alo_ref, ahi_ref, zlo_ref, zhi_ref, wf_ref, wb_ref, df_ref,
                      db_ref, o_ref, ss_ref, re_ref, im_ref, ret_ref, imt_ref):
    half = o_ref.shape[0]

    @pl.when(pl.program_id(2) == 0)
    def _():
        ss_ref[...] = jnp.zeros_like(ss_ref)

    n_slabs, rows = alo_ref.shape[0], alo_ref.shape[1]
    ct = o_ref.shape[2]
    ss = jnp.zeros(ss_ref.shape, F32)
    r = jnp.zeros((2 * half, n_slabs * ct), F32)
    for f_ref, a_ref, z_ref, w_ref, d_ref in ((flo_ref, alo_ref, zlo_ref, wf_ref, df_ref),
                                              (fhi_ref, ahi_ref, zhi_ref, wb_ref, db_ref)):
        z = z_ref[...].reshape(n_slabs * rows, LANES)
        t, sign = z[:, 0:1], z[:, LANES - 1:LANES]
        taps = _dot3(a_ref[...].reshape(n_slabs * rows, a_ref.shape[2]), w_ref[...])
        taps = taps * jnp.exp(-t * jnp.abs(d_ref[...])) * sign
        ss = ss + jnp.sum(taps * taps, axis=0, keepdims=True)
        tb = taps.astype(BF16)
        wide = jnp.concatenate([tb[s * rows:(s + 1) * rows, :] for s in range(n_slabs)], axis=1)
        r = r + _dot(f_ref[...], wide)
    for s in range(n_slabs):
        re_ref[s * half:(s + 1) * half, :] = r[:half, s * ct:(s + 1) * ct].astype(BF16)
        im_ref[s * half:(s + 1) * half, :] = r[half:, s * ct:(s + 1) * ct].astype(BF16)
    ss_ref[...] += ss
    _transpose_rows(ret_ref, re_ref, pm_ref, n_slabs, half)
    _transpose_rows(imt_ref, im_ref, pm_ref, n_slabs, half)
    o_ref[...] = _pack_bf16(ret_ref[...], imt_ref[...]).reshape(o_ref.shape)


def _filter_s1(consts, a2, zf, w3, decay, ct=512):
    n1, h = consts["n1"], SLAB_ROWS
    hid = a2.shape[1]
    d = decay.shape[-1]
    nct = d // ct
    st = _slab_tile(n1)
    flo, fhi = consts["f_s1"], consts["f_s1_hi"]
    dec = decay.reshape(1, HYENA_ORDER * N_DIRS * d)
    fspec = pl.BlockSpec(flo.shape, lambda o, j, t: (0, 0))
    rows = lambda half, width: pl.BlockSpec((None, st, h, width), lambda o, j, t: (half, t, 0, 0))
    wcol = lambda dirn, nrow: pl.BlockSpec((nrow, ct), lambda o, j, t: (0, (o * N_DIRS + dirn) * nct + j))
    return pl.pallas_call(
        _filter_s1_kernel,
        grid=(HYENA_ORDER, nct, n1 // st),
        in_specs=[pl.BlockSpec((PERM_ROWS, PERM_ROWS), lambda o, j, t: (0, 0)), fspec, fspec,
                  rows(0, hid), rows(1, hid), rows(0, LANES), rows(1, LANES),
                  wcol(0, hid), wcol(1, hid), wcol(0, 1), wcol(1, 1)],
        out_specs=[pl.BlockSpec((None, h, st, ct), lambda o, j, t: (o, 0, t, j)),
                   pl.BlockSpec((None, 1, ct), lambda o, j, t: (o, 0, j))],
        out_shape=[jax.ShapeDtypeStruct((HYENA_ORDER, h, n1, d), jnp.int32),
                   jax.ShapeDtypeStruct((HYENA_ORDER, 1, d), F32)],
        scratch_shapes=[pltpu.VMEM((st * h, ct), BF16)] * 4,
        compiler_params=_params(("parallel", "parallel", "arbitrary")), name="filter_s1",
    )(_perm_for(st, h), flo, fhi, a2.reshape(2, n1, h, hid), a2.reshape(2, n1, h, hid),
      zf.reshape(2, n1, h, LANES), zf.reshape(2, n1, h, LANES), w3, w3, dec, dec)


def _fft_s2f_kernel(a_ref, gf_ref, ss_ref, o_ref, *, kb):
    n1 = a_ref.shape[1]
    scale = lax.rsqrt(ss_ref[...] + EPS)

    def body(kk, carry):
        ar, ai = _unpack_c(a_ref[kk])
        x = _dot(gf_ref[kk], jnp.concatenate([ar, ai], axis=0).astype(BF16))
        o_ref[kk] = _pack_c(x[:n1] * scale, x[n1:] * scale)
        return carry

    lax.fori_loop(0, kb, body, 0, unroll=min(kb, 8))


def _fft_s2_kernel(a_ref, k_ref, gf_ref, gi_ref, o_ref, *, kb):
    nb, n1, ct = a_ref.shape[0], a_ref.shape[2], a_ref.shape[3]

    def body(kk, carry):
        parts = [_unpack_c(a_ref[b, kk]) for b in range(nb)]
        ar = jnp.concatenate([p[0] for p in parts], axis=1)
        ai = jnp.concatenate([p[1] for p in parts], axis=1)
        x = _dot(gf_ref[kk], jnp.concatenate([ar, ai], axis=0).astype(BF16))
        xr, xi = x[:n1], x[n1:]
        kr, ki = [jnp.concatenate([v] * nb, axis=1) for v in _unpack_c(k_ref[kk])]
        zr = xr * kr - xi * ki
        zi = xr * ki + xi * kr
        y = _dot(gi_ref[kk], jnp.concatenate([zr, zi], axis=0).astype(BF16))
        packed = _pack_c(y[:n1], y[n1:])
        for b in range(nb):
            o_ref[b, kk] = packed[:, b * ct:(b + 1) * ct]
        return carry

    lax.fori_loop(0, kb, body, 0, unroll=min(kb, 8))


def _s2_tiles(n1, d):
    kb = max(1, 1024 // n1)
    ct = min(d, 512)
    return kb, ct


def _fft_s2f(a, sumsq, consts):
    n_o, k2n, n1, d = a.shape
    kb, ct = _s2_tiles(n1, d)
    blk = pl.BlockSpec((None, kb, n1, ct), lambda k, j, o: (o, k, 0, j))
    return pl.pallas_call(
        functools.partial(_fft_s2f_kernel, kb=kb),
        grid=(k2n // kb, d // ct, n_o),
        in_specs=[blk, pl.BlockSpec((kb, 2 * n1, 2 * n1), lambda k, j, o: (k, 0, 0)),
                  pl.BlockSpec((None, 1, ct), lambda k, j, o: (o, 0, j))],
        out_specs=blk,
        out_shape=jax.ShapeDtypeStruct(a.shape, jnp.int32),
        compiler_params=_params(("parallel", "parallel", "parallel")),
        name="fft_s2_filter",
    )(a, consts["g_fwd"], sumsq)


def _fft_s2(a, kf, order, consts):
    bsz, k2n, n1, d = a.shape
    kb, ct = _s2_tiles(n1, d)
    nb = bsz if n1 * bsz <= SLAB_ROWS else 1
    blk = pl.BlockSpec((nb, kb, n1, ct), lambda k, j, b: (b, k, 0, j))
    mat = pl.BlockSpec((kb, 2 * n1, 2 * n1), lambda k, j, b: (k, 0, 0))
    return pl.pallas_call(
        functools.partial(_fft_s2_kernel, kb=kb),
        grid=(k2n // kb, d // ct, bsz // nb),
        in_specs=[blk, pl.BlockSpec((None, kb, n1, ct), lambda k, j, b: (order, k, 0, j)), mat, mat],
        out_specs=blk,
        out_shape=jax.ShapeDtypeStruct(a.shape, jnp.int32),
        compiler_params=_params(("parallel", "parallel", "parallel")),
        name="fft_s2",
    )(a, kf, consts["g_fwd"], consts["g_inv"])


def _fft_s3_kernel(pm_ref, f_ref, t_ref, u_ref, gm_ref, gp_ref, gn_ref, w_ref, b_ref, sk_ref, o_ref,
                   re_ref, im_ref, ret_ref, imt_ref):
    t_id = pl.program_id(1)
    half, st = t_ref.shape[0], t_ref.shape[1]
    re, im = _unpack_c(t_ref[...].reshape(half * st, t_ref.shape[2]))
    re_ref[...] = re.astype(BF16)
    im_ref[...] = im.astype(BF16)
    _transpose_rows(ret_ref, re_ref, pm_ref, half, st)
    _transpose_rows(imt_ref, im_ref, pm_ref, half, st)
    gates = _short_conv_slabs(gm_ref, gp_ref, gn_ref, w_ref, b_ref, t_id == 0, t_id == pl.num_programs(1) - 1)
    for s, gate in enumerate(gates):
        t = jnp.concatenate([ret_ref[s * half:(s + 1) * half, :], imt_ref[s * half:(s + 1) * half, :]], axis=0)
        y = _dot(f_ref[...], t)
        o_ref[s] = (gate * (y + u_ref[s].astype(F32) * sk_ref[...])).astype(o_ref.dtype)


def _fft_s3(consts, t, u, z, gate_block, conv_w, conv_b, skip, order, out_dtype, ct=512):
    bsz, h, n1, d = t.shape
    st = _dft_slab_tile(n1)
    cpb = d // ct
    f = consts["f_s3"]
    gcol = lambda j: gate_block * cpb + j
    slab = lambda idx: pl.BlockSpec((None, 1, h, ct), lambda b, tt, j: (b, idx(tt), 0, gcol(j)))
    return pl.pallas_call(
        _fft_s3_kernel,
        grid=(bsz, n1 // st, cpb),
        scratch_shapes=[pltpu.VMEM((st * h, ct), BF16)] * 4,
        in_specs=[
            pl.BlockSpec((PERM_ROWS, PERM_ROWS), lambda b, tt, j: (0, 0)),
            pl.BlockSpec(f.shape, lambda b, tt, j: (0, 0)),
            pl.BlockSpec((None, h, st, ct), lambda b, tt, j: (b, 0, tt, j)),
            pl.BlockSpec((None, st, h, ct), lambda b, tt, j: (b, tt, 0, j)),
            pl.BlockSpec((None, st, h, ct), lambda b, tt, j: (b, tt, 0, gcol(j))),
            slab(lambda tt: (tt * st + n1 - 1) % n1),
            slab(lambda tt: ((tt + 1) * st) % n1),
            pl.BlockSpec((3, ct), lambda b, tt, j: (0, gcol(j))),
            pl.BlockSpec((1, ct), lambda b, tt, j: (0, gcol(j))),
            pl.BlockSpec((None, 1, ct), lambda b, tt, j: (order, 0, j)),
        ],
        out_specs=pl.BlockSpec((None, st, h, ct), lambda b, tt, j: (b, tt, 0, j)),
        out_shape=jax.ShapeDtypeStruct((bsz, n1, h, d), out_dtype),
        compiler_params=_params(("parallel", "parallel", "parallel")),
        name="fft_s3",
    )(_perm_for(h, st), f, t, u, z, z, z, conv_w, conv_b, skip)


def _hyena_filter_spectra(seq_len, consts, fw1, fb1, ffreq, fw2, fb2, fw3, decay):
    d = decay.shape[-1]
    n1 = consts["n1"]
    zf = _filter_positions(seq_len, n1)
    a2 = _filter_mlp(zf, fw1, fb1, ffreq, fw2, fb2)
    a, sumsq = _filter_s1(consts, a2, zf, fw3, decay)
    return _fft_s2f(a, sumsq, consts)


def _hyena_mixer(x, g, shift, scale, gate, p, layer, kf, consts):
    bsz, seq_len, d = x.shape
    n1 = consts["n1"]
    h = SLAB_ROWS
    st = _slab_tile(n1)
    xv = x.reshape(bsz, h, n1, d)
    z = _norm_mm(
        xv, pl.BlockSpec((None, h, st, d), lambda b, i, j: (b, 0, i, 0)), (h, st), st * h, n1 // st, g, shift, scale,
        [p["hy_w_in"]], layer, 0, 3 * d, jax.ShapeDtypeStruct((bsz, n1, h, 3 * d), BF16),
        pl.BlockSpec((None, st, h, COL_TILE), lambda b, i, j: (b, i, 0, j)),
        mode="bias", bias=p["hy_b_in"][layer].reshape(1, 3 * d), tn=COL_TILE)
    cw, cb = p["hy_conv_w"][layer], p["hy_conv_b"][layer].reshape(1, 3 * d)
    skip = p["hy_skip"][layer].reshape(HYENA_ORDER, 1, d)
    a, u = _fft_s1(consts, z, 2, conv=(cw, cb))
    t = _fft_s2(a, kf, 0, consts)
    y1 = _fft_s3(consts, t, u, z, 0, cw, cb, skip, 0, BF16)
    a = _fft_s1(consts, y1, 0)
    t = _fft_s2(a, kf, 1, consts)
    y2 = _fft_s3(consts, t, y1, z, 1, cw, cb, skip, 1, F32)
    q = ROW_TILE // n1
    return _mm_res(
        [y2], [pl.BlockSpec((None, n1, q, d), lambda b, i, j: (b, 0, i, 0))], p["hy_w_out"], layer,
        p["hy_b_out"][layer].reshape(1, d), x, gate, ROW_TILE, a_mode="perm", perm=(n1, q), tn=COL_TILE)


def _attn_kernel(*refs, phases, ta, n_sub, to_classes):
    if to_classes:
        (pm_ref, q_ref, kp_ref, km_ref, kn_ref, vp_ref, vm_ref, vn_ref, o_ref, l_ref, kx_ref, vx_ref,
         on_ref, ot_ref, ln_ref, ls_ref, lt_ref) = refs
    else:
        q_ref, kp_ref, km_ref, kn_ref, vp_ref, vm_ref, vn_ref, o_ref, l_ref, kx_ref, vx_ref = refs
    i = pl.program_id(2)
    halo = ATT_BAND // phases
    qa = ATT_Q // phases
    ka = 2 * qa
    kx_ref[:, 0:halo] = kp_ref[...]
    kx_ref[:, halo:halo + ta] = km_ref[...]
    kx_ref[:, halo + ta:] = kn_ref[...]
    vx_ref[:, 0:halo] = vp_ref[...]
    vx_ref[:, halo:halo + ta] = vm_ref[...]
    vx_ref[:, halo + ta:] = vn_ref[...]
    row = lax.broadcasted_iota(jnp.int32, (ATT_Q, 2 * ATT_Q), 0)
    col = lax.broadcasted_iota(jnp.int32, (ATT_Q, 2 * ATT_Q), 1)
    cq, aq = row >> (qa.bit_length() - 1), row & (qa - 1)
    ck, ak = col >> (ka.bit_length() - 1), col & (ka - 1)
    delta = phases * (ak - aq) - ATT_BAND + ck - cq
    band = (delta >= -ATT_BAND) & (delta <= ATT_BAND)
    lane_head = lax.broadcasted_iota(jnp.int32, (ATT_Q, LANES), 1) >> (LSE_LANES.bit_length() - 1)
    for s in range(ta // qa):
        key_idx = phases * (i * ta + s * qa - halo + ak) + ck
        valid = band & (key_idx >= 0) & (key_idx < n_sub)
        lse_tile = jnp.zeros((ATT_Q, LANES), F32)
        for h in range(HEADS_PER_GROUP):
            cs = slice(h * HEAD_DIM, (h + 1) * HEAD_DIM)
            q = jnp.concatenate([q_ref[c, s * qa:(s + 1) * qa, cs] for c in range(phases)], axis=0)
            k = jnp.concatenate([kx_ref[c, s * qa:s * qa + ka, cs] for c in range(phases)], axis=0)
            v = jnp.concatenate([vx_ref[c, s * qa:s * qa + ka, cs] for c in range(phases)], axis=0)
            sc = lax.dot_general(q, k, (((1,), (1,)), ((), ())), preferred_element_type=F32)
            sc = jnp.where(valid, sc, NEG_BIG)
            m = jnp.max(sc, axis=-1, keepdims=True)
            pr = jnp.exp(sc - m)
            den = jnp.sum(pr, axis=-1, keepdims=True)
            o = (_dot(pr.astype(BF16), v) / den).astype(BF16)
            lse_tile = jnp.where(lane_head == h, m + jnp.log(den), lse_tile)
            if to_classes:
                on_ref[s * ATT_Q:(s + 1) * ATT_Q, cs] = o
            else:
                for c in range(phases):
                    o_ref[c, s * qa:(s + 1) * qa, cs] = o[c * qa:(c + 1) * qa]
        if to_classes:
            ln_ref[s * ATT_Q:(s + 1) * ATT_Q, :] = lse_tile
        else:
            for c in range(phases):
                l_ref[c, s * qa:(s + 1) * qa, :] = lse_tile[c * qa:(c + 1) * qa]
    if to_classes:
        na = ta // CLASSES
        _transpose_rows(ot_ref, on_ref, pm_ref, na, CLASSES)
        o_ref[...] = ot_ref[...].reshape(o_ref.shape)
        rest = ln_ref[...]
        total = jnp.zeros(rest.shape, F32)
        for _ in range(3):
            piece = rest.astype(BF16)
            rest = rest - piece.astype(F32)
            ls_ref[...] = piece
            _transpose_rows(lt_ref, ls_ref, pm_ref, na, CLASSES)
            total = total + lt_ref[...].astype(F32)
        l_ref[...] = total.reshape(l_ref.shape)


def _attn_group(qkv, col0, phases, n_sub, lead_grid, lead_block, lead_index, ta, *, to_classes=False,
                out_arr_shape=None, out_block=None, out_index=None):
    gw = GROUP_WIDTH
    halo = ATT_BAND // phases
    rows = qkv.shape[-2]
    per = ta // halo
    nblk = rows // halo
    cb = col0 // gw

    def spec(nrows, ridx, part):
        return pl.BlockSpec(tuple(lead_block) + (nrows, gw),
                            lambda b, rho, i: tuple(lead_index(b, rho)) + (ridx(i), cb + part))

    main = lambda part: spec(ta, lambda i: i, part)
    prev = lambda part: spec(halo, lambda i: jnp.maximum(i * per - 1, 0), part)
    nxt = lambda part: spec(halo, lambda i: jnp.minimum((i + 1) * per, nblk - 1), part)
    bsz = qkv.shape[0]
    in_specs = [main(0), prev(1), main(1), nxt(1), prev(2), main(2), nxt(2)]
    args = [qkv] * 7
    if to_classes:
        assert ta == PERM_ROWS
        out_specs = [pl.BlockSpec(out_block + (gw,), out_index), pl.BlockSpec(out_block + (LANES,), out_index)]
        scratch_extra = [pltpu.VMEM((ta, gw), BF16)] * 2 + [pltpu.VMEM((ta, LANES), F32)] + \
                        [pltpu.VMEM((ta, LANES), BF16)] * 2
        in_specs.insert(0, pl.BlockSpec((PERM_ROWS, PERM_ROWS), lambda b, rho, i: (0, 0)))
        args.insert(0, _perm_for(ta // CLASSES, CLASSES))
    else:
        out_arr_shape = qkv.shape[:-1]
        oidx = lambda b, rho, i: tuple(lead_index(b, rho)) + (i, 0)
        out_specs = [pl.BlockSpec(tuple(lead_block) + (ta, gw), oidx),
                     pl.BlockSpec(tuple(lead_block) + (ta, LANES), oidx)]
        scratch_extra = []
    kx_shape = (phases, ta + 2 * halo, gw)
    return pl.pallas_call(
        functools.partial(_attn_kernel, phases=phases, ta=ta, n_sub=n_sub, to_classes=to_classes),
        grid=(bsz, lead_grid, rows // ta),
        in_specs=in_specs,
        out_specs=out_specs,
        out_shape=[jax.ShapeDtypeStruct(tuple(out_arr_shape) + (gw,), BF16),
                   jax.ShapeDtypeStruct(tuple(out_arr_shape) + (LANES,), F32)],
        scratch_shapes=[pltpu.VMEM(kx_shape, BF16)] * 2 + scratch_extra,
        compiler_params=_params(("parallel", "parallel", "parallel")),
        name="attn_p%d" % phases + ("_cls" if to_classes else ""),
    )(*args)


def _rope_tables(pos):
    half = ROT_DIM // 2
    inv = ROPE_THETA ** (-jnp.arange(0, ROT_DIM, 2, dtype=F32) / ROT_DIM)
    ang = pos.astype(F32)[:, None] * inv[None, :]
    cos, sin = jnp.cos(ang), jnp.sin(ang)
    n = pos.shape[0]
    rest = HEAD_DIM - ROT_DIM
    c = jnp.concatenate([cos, cos, jnp.ones((n, rest), F32)], axis=1)
    s = jnp.concatenate([sin, sin, jnp.zeros((n, rest), F32)], axis=1)
    return c, s


def _rope_partner_matrix(width):
    half = ROT_DIM // 2
    m = np.zeros((width, width), np.float32)
    for base in range(0, width, HEAD_DIM):
        for k in range(half):
            m[base + k + half, base + k] = -1.0
            m[base + k, base + k + half] = 1.0
    return jnp.asarray(m, BF16)


def _attn_mixer(x, g, shift, scale, gate, p, layer):
    bsz, seq_len, d = x.shape
    gw = GROUP_WIDTH
    nc = seq_len // CLASSES
    ca = ROW_TILE // CLASSES
    w_in = p["at_w_in"]
    tabs = _rope_tables(jnp.arange(seq_len))
    qkv0 = _norm_mm(
        x, pl.BlockSpec((None, ROW_TILE, d), lambda b, i, j: (b, i, 0)), None, ROW_TILE, seq_len // ROW_TILE,
        g, shift, scale, [w_in], layer, 0, 3 * gw, jax.ShapeDtypeStruct((bsz, seq_len, 3 * gw), BF16),
        pl.BlockSpec((None, ROW_TILE, COL_TILE), lambda b, i, j: (b, i, j)), mode="rope",
        rope=(tabs, pl.BlockSpec((ROW_TILE, HEAD_DIM), lambda b, i, j: (i, 0))), tn=COL_TILE)
    pos_c = (jnp.arange(nc)[None, :] * CLASSES + jnp.arange(CLASSES)[:, None]).reshape(-1)
    tabs_c = [t.reshape(CLASSES, nc, HEAD_DIM) for t in _rope_tables(pos_c)]
    qkv12 = _norm_mm(
        x.reshape(bsz, nc, CLASSES, d), pl.BlockSpec((None, ca, CLASSES, d), lambda b, i, j: (b, i, 0, 0)),
        (ca, CLASSES), ROW_TILE, nc // ca, g, shift, scale, [w_in], layer, 3 * gw, 6 * gw,
        jax.ShapeDtypeStruct((bsz, CLASSES, nc, 6 * gw), BF16),
        pl.BlockSpec((None, CLASSES, ca, COL_TILE), lambda b, i, j: (b, 0, i, j)), mode="rope",
        rope=(tabs_c, pl.BlockSpec((CLASSES, ca, HEAD_DIM), lambda b, i, j: (0, i, 0))), tn=COL_TILE)
    cls_shape = (bsz, CLASSES, nc)
    ta0 = PERM_ROWS
    o0, l0 = _attn_group(
        qkv0.reshape(bsz, 1, seq_len, 3 * gw), 0, 1, seq_len, 1, (None, 1), lambda b, rho: (b, 0), ta0,
        to_classes=True, out_arr_shape=cls_shape, out_block=(None, CLASSES, ta0 // CLASSES),
        out_index=lambda b, rho, i: (b, 0, i, 0))
    dil1 = ATTN_PATTERNS[1][1]
    ph = CLASSES // dil1
    ta1 = min(128, nc)
    o1, l1 = _attn_group(
        qkv12.reshape(bsz, ph, dil1, nc, 6 * gw), 0, ph, seq_len // dil1, dil1, (None, ph, None),
        lambda b, rho: (b, 0, rho), ta1)
    ta2 = min(512, nc)
    o2, l2 = _attn_group(
        qkv12.reshape(bsz, CLASSES, 1, nc, 6 * gw), 3 * gw, 1, nc, CLASSES, (None, None, 1),
        lambda b, rho: (b, rho, 0), ta2)
    os_ = [o.reshape(cls_shape + (gw,)) for o in (o0, o1, o2)]
    ls = [l.reshape(cls_shape + (LANES,)) for l in (l0, l1, l2)]
    blk = lambda width: pl.BlockSpec((None, CLASSES, ca, width), lambda b, i, j: (b, 0, i, 0))
    return _mm_res(os_ + ls, [blk(gw)] * 3 + [blk(LANES)] * 3, p["at_w_out"], layer, jnp.zeros((1, d), F32),
                   x, gate, ROW_TILE, a_mode="merge", perm=(CLASSES, ca), tn=COL_TILE)


def _ffn(x, g, shift, scale, gate, p, layer):
    bsz, seq_len, d = x.shape
    dff = p["ffn_w_gate"].shape[-1]
    tiles = seq_len // ROW_TILE
    xs = pl.BlockSpec((None, ROW_TILE, d), lambda b, i, j: (b, i, 0))
    hs = pl.BlockSpec((None, ROW_TILE, 512), lambda b, i, j: (b, i, j))
    hmid = _norm_mm(x, xs, None, ROW_TILE, tiles, g, shift, scale, [p["ffn_w_gate"], p["ffn_w_up"]], layer, 0,
                    dff, jax.ShapeDtypeStruct((bsz, seq_len, dff), BF16), hs, mode="swiglu")
    return _mm_res([hmid], [pl.BlockSpec((None, ROW_TILE, dff), lambda b, i, j: (b, i, 0))],
                   p["ffn_w_down"], layer, jnp.zeros((1, d), F32), x, gate, ROW_TILE)


def _encoder(x, mods, final_mod, p):
    bsz, seq_len, d = x.shape
    n1 = 2 * seq_len // DFT_N2
    consts = _dft_consts(n1, DFT_N2)
    consts["d"] = d
    for i in range(DEPTH):
        sh_m, sc_m, g_m, sh_f, sc_f, g_f = [mods[i][:, None, k * d:(k + 1) * d] for k in range(6)]
        j = i // 2
        if i % 2 == 0:
            kf = _hyena_filter_spectra(seq_len, consts, p["hy_fw1"][j], p["hy_fb1"][j], p["hy_ffreq"][j],
                                       p["hy_fw2"][j], p["hy_fb2"][j], p["hy_fw3"][j], p["hy_decay"][j])
            x = _hyena_mixer(x, p["norm_mix"][i], sh_m, sc_m, g_m, p, j, kf, consts)
        else:
            x = _attn_mixer(x, p["norm_mix"][i], sh_m, sc_m, g_m, p, j)
        x = _ffn(x, p["norm_ffn"][i], sh_f, sc_f, g_f, p, i)
    sh, sc = final_mod[:, None, :d], final_mod[:, None, d:]
    return _final(x, p["final_norm"], sh, sc)


def kernel(x_prompt, x_sample, c_prompt, c_sample, ada_w, ada_b, norm_mix, norm_ffn, hy_w_in, hy_b_in, hy_conv_w, hy_conv_b, hy_fw1, hy_fb1, hy_ffreq, hy_fw2, hy_fb2, hy_fw3, hy_decay, hy_skip, hy_w_out, hy_b_out, at_w_in, at_w_out, ffn_w_gate, ffn_w_up, ffn_w_down, final_norm, final_ada_w, final_ada_b):
    d = x_prompt.shape[-1]
    bp, bs = c_prompt.shape[0], c_sample.shape[0]
    pad = -(bp + bs) % (2 * SUBLANES)
    c_all = jnp.concatenate([c_prompt, c_sample, jnp.zeros((pad, d), F32)], axis=0)
    mods = _ada(c_all, ada_w, ada_b)
    fmod = _ada(c_all, final_ada_w[None], final_ada_b[None])[0]
    p = dict(norm_mix=norm_mix, norm_ffn=norm_ffn,
             hy_w_in=hy_w_in.astype(BF16), hy_b_in=hy_b_in, hy_conv_w=hy_conv_w, hy_conv_b=hy_conv_b,
             hy_fw1=hy_fw1, hy_fb1=hy_fb1, hy_ffreq=hy_ffreq, hy_fw2=hy_fw2, hy_fb2=hy_fb2, hy_fw3=hy_fw3,
             hy_decay=hy_decay, hy_skip=hy_skip, hy_w_out=hy_w_out.astype(BF16), hy_b_out=hy_b_out,
             at_w_in=at_w_in.astype(BF16), at_w_out=at_w_out.astype(BF16),
             ffn_w_gate=ffn_w_gate.astype(BF16), ffn_w_up=ffn_w_up.astype(BF16),
             ffn_w_down=ffn_w_down.astype(BF16), final_norm=final_norm)
    y_prompt = _encoder(x_prompt, mods[:, :bp], fmod[:bp], p)
    y_sample = _encoder(x_sample, mods[:, bp:bp + bs], fmod[bp:bp + bs], p)
    return (y_prompt, y_sample)
```
